```python
import jax
import jax.numpy as jnp
from jax import lax
import numpy as np

D_MODEL = 1024
BATCH = 2
SEQ = 8192
DEPTH = 1

GRID_W = 64
CTX_LEN = 256
RET_HEADS = 4
RET_DK = 128
RET_DV = 128
RET_CHUNK = 128
GLA_HEADS = 4
GLA_DK = 64
GLA_DV = 128
GLA_RANK = 16
GLA_TAU = 16.0
GLA_CHUNK = 64
RET_WIDTH = RET_HEADS * RET_DV
GLA_WIDTH = GLA_HEADS * GLA_DV
D_MIX = RET_WIDTH + GLA_WIDTH
D_FF = 2816
N_MOD = 9
ROPE_BASE = 10000.0
EPS = 1e-6
IN_WIDTHS = (RET_HEADS * RET_DK, RET_HEADS * RET_DK, RET_WIDTH, RET_WIDTH,
             GLA_HEADS * GLA_DK, GLA_HEADS * GLA_DK, GLA_WIDTH, GLA_WIDTH, GLA_RANK, GLA_RANK)
IN_COLS = sum(IN_WIDTHS)
IN_SPLITS = tuple(int(v) for v in np.cumsum(IN_WIDTHS)[:-1])

kernel_name = "hybrid_retention_gla_macaron_dit"


def rms_norm(x, w):
    xf = x.astype(jnp.float32)
    y = xf * lax.rsqrt(jnp.mean(xf * xf, axis=-1, keepdims=True) + EPS)
    return (y * w.astype(jnp.float32)).astype(x.dtype)


def modulate(h, shift, scale):
    return h * (1.0 + scale[:, None, :]) + shift[:, None, :]


def swiglu(h, w1, w3, w2):
    return (jax.nn.silu(h @ w1) * (h @ w3)) @ w2


def split_heads(t, n_heads):
    b, n, _ = t.shape
    return t.reshape(b, n, n_heads, -1).transpose(0, 2, 1, 3).astype(jnp.float32)


def merge_heads(t):
    b, h, n, d = t.shape
    return t.transpose(0, 2, 1, 3).reshape(b, n, h * d)


def rotate(x, ang):
    x1, x2 = jnp.split(x, 2, axis=-1)
    cos, sin = jnp.cos(ang), jnp.sin(ang)
    return jnp.concatenate([x1 * cos - x2 * sin, x1 * sin + x2 * cos], axis=-1)


def grid_rope(x, n_tok):
    rows = n_tok // GRID_W
    row = jnp.repeat(jnp.arange(rows, dtype=jnp.float32), GRID_W)
    col = jnp.tile(jnp.arange(GRID_W, dtype=jnp.float32), rows)
    n_freq = x.shape[-1] // 4
    freqs = ROPE_BASE ** (-jnp.arange(n_freq, dtype=jnp.float32) / n_freq)
    xr, xc = jnp.split(x, 2, axis=-1)
    return jnp.concatenate([rotate(xr, row[:, None] * freqs), rotate(xc, col[:, None] * freqs)], axis=-1)


def to_chunks(t, c):
    b, h, n, d = t.shape
    return t.reshape(b, h, n // c, c, d).transpose(2, 0, 1, 3, 4)


def from_chunks(t):
    nc, b, h, c, d = t.shape
    return t.transpose(1, 2, 0, 3, 4).reshape(b, h, nc * c, d)


def retention_chunked(q, k, v, s0, log_gamma):
    c = RET_CHUNK
    pos = jnp.arange(c, dtype=jnp.float32)
    diff = pos[:, None] - pos[None, :]
    lg = log_gamma[:, None, None]
    decay = jnp.exp(jnp.where(diff >= 0, diff * lg, -jnp.inf))
    q_dec = jnp.exp((pos + 1.0) * log_gamma[:, None])[..., None]
    k_dec = jnp.exp((c - 1.0 - pos) * log_gamma[:, None])[..., None]
    chunk_dec = jnp.exp(c * log_gamma)[:, None, None]

    def step(s, xs):
        qc, kc, vc = xs
        scores = jnp.einsum('bhid,bhjd->bhij', qc, kc) * decay
        o = jnp.einsum('bhij,bhje->bhie', scores, vc) + jnp.einsum('bhid,bhde->bhie', qc * q_dec, s)
        s = chunk_dec * s + jnp.einsum('bhjd,bhje->bhde', kc * k_dec, vc)
        return s, o

    s_fin, o = lax.scan(step, s0, (to_chunks(q, c), to_chunks(k, c), to_chunks(v, c)))
    return from_chunks(o), s_fin


def gla_chunked(q, k, v, g, s0):
    c = GLA_CHUNK
    tri = jnp.tril(jnp.ones((c, c), dtype=bool))

    def step(s, xs):
        qc, kc, vc, gc = xs
        b = jnp.cumsum(gc, axis=-2)
        rel = b[:, :, :, None, :] - b[:, :, None, :, :]
        rel = jnp.exp(jnp.where(tri[:, :, None], rel, -jnp.inf))
        scores = jnp.einsum('bhid,bhjd,bhijd->bhij', qc, kc, rel)
        o = jnp.einsum('bhij,bhje->bhie', scores, vc) + jnp.einsum('bhid,bhde->bhie', qc * jnp.exp(b), s)
        b_last = b[:, :, -1:, :]
        s = jnp.exp(b_last[:, :, 0, :])[..., None] * s + jnp.einsum('bhjd,bhje->bhde', kc * jnp.exp(b_last - b), vc)
        return s, o

    s_fin, o = lax.scan(step, s0, (to_chunks(q, c), to_chunks(k, c), to_chunks(v, c), to_chunks(g, c)))
    return from_chunks(o), s_fin


def bidir_prefix(scan_f, scan_b, lat_f, lat_b, ctx_f, ctx_b):
    flip = lambda ts: tuple(t[:, :, ::-1] for t in ts)
    q0, v0 = ctx_f[0], ctx_f[2]
    s0 = jnp.zeros(q0.shape[:2] + (q0.shape[-1], v0.shape[-1]), jnp.float32)
    o_cf, s_f = scan_f(*ctx_f, s0)
    o_cb, s_b = scan_b(*flip(ctx_b), s0)
    o_lf, _ = scan_f(*lat_f, s_f)
    o_lb, _ = scan_b(*flip(lat_b), s_b)
    return o_lf + o_lb[:, :, ::-1], o_cf + o_cb[:, :, ::-1]


def retention_inputs(p, n_tok):
    q = split_heads(p[0], RET_HEADS) * (RET_DK ** -0.5)
    k = split_heads(p[1], RET_HEADS)
    if n_tok is not None:
        q, k = grid_rope(q, n_tok), grid_rope(k, n_tok)
    return (q, k, split_heads(p[2], RET_HEADS))


def gla_inputs(p, w_f, b_f, w_b, b_b):
    q = split_heads(p[4], GLA_HEADS) * (GLA_DK ** -0.5)
    k = split_heads(p[5], GLA_HEADS)
    v = split_heads(p[6], GLA_HEADS)
    g_f = split_heads(jax.nn.log_sigmoid((p[8] @ w_f + b_f).astype(jnp.float32)), GLA_HEADS) / GLA_TAU
    g_b = split_heads(jax.nn.log_sigmoid((p[9] @ w_b + b_b).astype(jnp.float32)), GLA_HEADS) / GLA_TAU
    return (q, k, v, g_f), (q, k, v, g_b)


def merge_outputs(o_ret, o_gla, p, ret_norm_w, gla_norm_w, w_out, dtype):
    mu = jnp.mean(o_ret, axis=-1, keepdims=True)
    var = jnp.mean(jnp.square(o_ret - mu), axis=-1, keepdims=True)
    r = merge_heads((o_ret - mu) * lax.rsqrt(var + EPS)) * ret_norm_w * jax.nn.silu(p[3].astype(jnp.float32))
    gl = merge_heads(o_gla * lax.rsqrt(jnp.mean(o_gla * o_gla, axis=-1, keepdims=True) + EPS))
    gl = gl * gla_norm_w * jax.nn.silu(p[7].astype(jnp.float32))
    return jnp.concatenate([r, gl], axis=-1).astype(dtype) @ w_out


def token_mixing(h_lat, h_ctx, w_in, ret_decay_f, ret_decay_b, ret_norm_w,
                 gla_gate_w_f, gla_gate_b_f, gla_gate_w_b, gla_gate_b_b, gla_norm_w, w_out, with_ctx):
    n_lat = h_lat.shape[1]
    p_lat = jnp.split(h_lat @ w_in, IN_SPLITS, axis=-1)
    p_ctx = jnp.split(h_ctx @ w_in, IN_SPLITS, axis=-1)
    lg_f = jax.nn.log_sigmoid(ret_decay_f.astype(jnp.float32))
    lg_b = jax.nn.log_sigmoid(ret_decay_b.astype(jnp.float32))
    ret_lat = retention_inputs(p_lat, n_lat)
    ret_ctx = retention_inputs(p_ctx, None)
    o_ret_l, o_ret_c = bidir_prefix(lambda q, k, v, s: retention_chunked(q, k, v, s, lg_f),
                                    lambda q, k, v, s: retention_chunked(q, k, v, s, lg_b),
                                    ret_lat, ret_lat, ret_ctx, ret_ctx)
    gla_lat_f, gla_lat_b = gla_inputs(p_lat, gla_gate_w_f, gla_gate_b_f, gla_gate_w_b, gla_gate_b_b)
    gla_ctx_f, gla_ctx_b = gla_inputs(p_ctx, gla_gate_w_f, gla_gate_b_f, gla_gate_w_b, gla_gate_b_b)
    o_gla_l, o_gla_c = bidir_prefix(gla_chunked, gla_chunked, gla_lat_f, gla_lat_b, gla_ctx_f, gla_ctx_b)
    y_lat = merge_outputs(o_ret_l, o_gla_l, p_lat, ret_norm_w, gla_norm_w, w_out, h_lat.dtype)
    y_ctx = None
    if with_ctx:
        y_ctx = merge_outputs(o_ret_c, o_gla_c, p_ctx, ret_norm_w, gla_norm_w, w_out, h_ctx.dtype)
    return y_lat, y_ctx


def setup_inputs(seed: int = 0) -> dict:
    key = jax.random.key(seed)
    ks = jax.random.split(key, 32)
    f32 = jnp.float32

    def nrm(k, shape, scale):
        return jax.random.normal(k, shape, f32) * scale

    def gain(k, shape):
        return 1.0 + 0.05 * jax.random.normal(k, shape, f32)

    decay_logit = jnp.log(2.0 ** (5.0 + jnp.arange(RET_HEADS, dtype=f32)) - 1.0)
    return {
        "x": nrm(ks[0], (BATCH, SEQ, D_MODEL), 1.0),
        "c": nrm(ks[1], (BATCH, D_MODEL), 1.0),
        "ctx": nrm(ks[2], (BATCH, CTX_LEN, D_MODEL), 1.0),
        "c_ctx": nrm(ks[3], (D_MODEL,), 1.0),
        "ada_w": nrm(ks[4], (DEPTH, D_MODEL, N_MOD * D_MODEL), 0.5 * D_MODEL ** -0.5),
        "ada_b": nrm(ks[5], (DEPTH, N_MOD * D_MODEL), 0.02),
        "norm1_w": gain(ks[6], (DEPTH, D_MODEL)),
        "ffn1_w1": nrm(ks[7], (DEPTH, D_MODEL, D_FF), D_MODEL ** -0.5),
        "ffn1_w3": nrm(ks[8], (DEPTH, D_MODEL, D_FF), D_MODEL ** -0.5),
        "ffn1_w2": nrm(ks[9], (DEPTH, D_FF, D_MODEL), D_FF ** -0.5),
        "norm2_w": gain(ks[10], (DEPTH, D_MODEL)),
        "w_in": nrm(ks[11], (DEPTH, D_MODEL, IN_COLS), D_MODEL ** -0.5),
        "ret_decay_f": decay_logit + 0.1 * jax.random.normal(ks[12], (DEPTH, RET_HEADS), f32),
        "ret_decay_b": decay_logit + 0.1 * jax.random.normal(ks[13], (DEPTH, RET_HEADS), f32),
        "ret_norm_w": gain(ks[14], (DEPTH, RET_WIDTH)),
        "gla_gate_w_f": nrm(ks[15], (DEPTH, GLA_RANK, GLA_HEADS * GLA_DK), GLA_RANK ** -0.5),
        "gla_gate_b_f": 1.0 + 0.5 * jax.random.normal(ks[16], (DEPTH, GLA_HEADS * GLA_DK), f32),
        "gla_gate_w_b": nrm(ks[17], (DEPTH, GLA_RANK, GLA_HEADS * GLA_DK), GLA_RANK ** -0.5),
        "gla_gate_b_b": 1.0 + 0.5 * jax.random.normal(ks[18], (DEPTH, GLA_HEADS * GLA_DK), f32),
        "gla_norm_w": gain(ks[19], (DEPTH, GLA_WIDTH)),
        "w_out": nrm(ks[20], (DEPTH, D_MIX, D_MODEL), D_MIX ** -0.5),
        "norm3_w": gain(ks[21], (DEPTH, D_MODEL)),
        "ffn2_w1": nrm(ks[22], (DEPTH, D_MODEL, D_FF), D_MODEL ** -0.5),
        "ffn2_w3": nrm(ks[23], (DEPTH, D_MODEL, D_FF), D_MODEL ** -0.5),
        "ffn2_w2": nrm(ks[24], (DEPTH, D_FF, D_MODEL), D_FF ** -0.5),
        "final_norm_w": gain(ks[25], (D_MODEL,)),
    }


def reference(x, c, ctx, c_ctx, ada_w, ada_b, norm1_w, ffn1_w1, ffn1_w3, ffn1_w2, norm2_w, w_in,
              ret_decay_f, ret_decay_b, ret_norm_w, gla_gate_w_f, gla_gate_b_f, gla_gate_w_b, gla_gate_b_b,
              gla_norm_w, w_out, norm3_w, ffn2_w1, ffn2_w3, ffn2_w2, final_norm_w):
    cond_lat = jax.nn.silu(c)
    cond_ctx = jax.nn.silu(c_ctx)[None, :]
    for i in range(DEPTH):
        update_ctx = i < DEPTH - 1
        m_l = jnp.split(cond_lat @ ada_w[i] + ada_b[i], N_MOD, axis=-1)
        m_c = jnp.split(cond_ctx @ ada_w[i] + ada_b[i], N_MOD, axis=-1)
        x = x + 0.5 * m_l[2][:, None, :] * swiglu(modulate(rms_norm(x, norm1_w[i]), m_l[0], m_l[1]),
                                                    ffn1_w1[i], ffn1_w3[i], ffn1_w2[i])
        ctx = ctx + 0.5 * m_c[2][:, None, :] * swiglu(modulate(rms_norm(ctx, norm1_w[i]), m_c[0], m_c[1]),
                                                        ffn1_w1[i], ffn1_w3[i], ffn1_w2[i])
        y_l, y_c = token_mixing(modulate(rms_norm(x, norm2_w[i]), m_l[3], m_l[4]),
                                modulate(rms_norm(ctx, norm2_w[i]), m_c[3], m_c[4]),
                                w_in[i], ret_decay_f[i], ret_decay_b[i], ret_norm_w[i],
                                gla_gate_w_f[i], gla_gate_b_f[i], gla_gate_w_b[i], gla_gate_b_b[i],
                                gla_norm_w[i], w_out[i], update_ctx)
        x = x + m_l[5][:, None, :] * y_l
        x = x + 0.5 * m_l[8][:, None, :] * swiglu(modulate(rms_norm(x, norm3_w[i]), m_l[6], m_l[7]),
                                                    ffn2_w1[i], ffn2_w3[i], ffn2_w2[i])
        if update_ctx:
            ctx = ctx + m_c[5][:, None, :] * y_c
            ctx = ctx + 0.5 * m_c[8][:, None, :] * swiglu(modulate(rms_norm(ctx, norm3_w[i]), m_c[6], m_c[7]),
                                                            ffn2_w1[i], ffn2_w3[i], ffn2_w2[i])
    return rms_norm(x, final_norm_w)
```

```python
import functools

import jax
import jax.numpy as jnp
from jax import lax
from jax.experimental import pallas as pl
from jax.experimental.pallas import tpu as pltpu

F32 = jnp.float32
BF16 = jnp.bfloat16

EPS = 1e-6
N_MOD = 9
GRID_W = 64
ROPE_BASE = 10000.0
RET_HEADS = 4
RET_DK = 128
RET_DV = 128
GLA_HEADS = 4
GLA_DK = 64
GLA_DV = 128
GLA_RANK = 16
GLA_TAU = 16.0
RET_W = RET_HEADS * RET_DV
GLA_W = GLA_HEADS * GLA_DV
GLA_QK = GLA_HEADS * GLA_DK

LANE = 128
CHUNK = 256
FFN_TILE = 512
FF_CHUNK = 256
VMEM_LIMIT = 60 * 1024 * 1024

C_RQ, C_RK, C_RV, C_RG = 0, 512, 1024, 1536
C_GQ, C_GK, C_GV, C_GG = 2048, 2304, 2560, 3072
C_LOW = 3584
IN_PACKED = C_LOW + LANE


def _silu(x):
    return x * (1.0 / (1.0 + jnp.exp(-x)))


def _log_sigmoid(z):
    return jnp.minimum(z, 0.0) - jnp.log(1.0 + jnp.exp(-jnp.abs(z)))


def _rms(x, w):
    return x * lax.rsqrt(jnp.mean(x * x, axis=-1, keepdims=True) + EPS) * w


def _dot(a, b):
    return jnp.dot(a, b, preferred_element_type=F32)


def _dot_nt(a, b):
    return lax.dot_general(a, b, (((1,), (1,)), ((), ())), preferred_element_type=F32)


def _dot_tn(a, b):
    return lax.dot_general(a, b, (((0,), (0,)), ((), ())), preferred_element_type=F32)


def _split3(x):
    hi = x.astype(BF16)
    r1 = x - hi.astype(F32)
    mid = r1.astype(BF16)
    lo = (r1 - mid.astype(F32)).astype(BF16)
    return hi, mid, lo


def _dot_exact_lhs(sel, x):
    hi, mid, lo = _split3(x)
    return _dot(sel, hi) + _dot(sel, mid) + _dot(sel, lo)


def _iota(shape, dim):
    return lax.broadcasted_iota(jnp.int32, shape, dim)


def _resident(shape):
    nd = len(shape)
    return pl.BlockSpec(shape, lambda *_: (0,) * nd, pipeline_mode=pl.Buffered(1))


def _mod_kernel(c_ref, w_ref, b_ref, o_ref):
    cond = _silu(c_ref[...])
    o_ref[...] = jnp.dot(cond, w_ref[...], preferred_element_type=F32,
                         precision=lax.Precision.HIGHEST) + b_ref[...]


def _modulation(cvec, ada_w, ada_b):
    d, n = ada_w.shape
    bn = d
    return pl.pallas_call(
        _mod_kernel,
        grid=(n // bn,),
        in_specs=[pl.BlockSpec((8, d), lambda j: (0, 0)),
                  pl.BlockSpec((d, bn), lambda j: (0, j)),
                  pl.BlockSpec((1, bn), lambda j: (0, j))],
        out_specs=pl.BlockSpec((8, bn), lambda j: (0, j)),
        out_shape=jax.ShapeDtypeStruct((8, n), F32),
        compiler_params=pltpu.CompilerParams(dimension_semantics=("arbitrary",)),
        name="mod",
    )(cvec, ada_w, ada_b.reshape(1, n))


def _swap32(x):
    lane = _iota(x.shape, 1)
    return jnp.where((lane & 63) < 32, pltpu.roll(x, 96, 1), pltpu.roll(x, 32, 1))


def _ffn_kernel(*refs, mod_off, proj, rope, final, d, f):
    it = iter(refs)
    x_ref, m_ref, nw_ref, w1_ref, w3_ref, w2_ref = (next(it) for _ in range(6))
    if proj:
        n2w_ref, win_ref, gw_ref, gbias_ref = (next(it) for _ in range(4))
        if rope:
            cos_ref, sin_ref = next(it), next(it)
    if final:
        fnw_ref = next(it)
    xo_ref = next(it)
    if proj:
        (rq_ref, rk_ref, rv_ref, rg_ref, gq_ref, gk_ref, gv_ref, gg_ref,
         gf_ref, gb_ref) = (next(it) for _ in range(10))
    u_ref = next(it)

    def mod(i):
        return m_ref[:, (mod_off + i) * d:(mod_off + i + 1) * d]

    x = x_ref[...]
    h = (_rms(x, nw_ref[...]) * (1.0 + mod(1)) + mod(0)).astype(BF16)
    for k in range(f // FF_CHUNK):
        sl = slice(k * FF_CHUNK, (k + 1) * FF_CHUNK)
        a = _dot(h, w1_ref[:, sl])
        g = _dot(h, w3_ref[:, sl])
        u_ref[:, sl] = (_silu(a) * g).astype(BF16)
    y = _dot(u_ref[...], w2_ref[...])
    x1 = x + (0.5 * mod(2)) * y

    if final:
        xo_ref[...] = _rms(x1, fnw_ref[...])
    else:
        xo_ref[...] = x1

    if proj:
        h2 = (_rms(x1, n2w_ref[...]) * (1.0 + mod(4)) + mod(3)).astype(BF16)

        def p(lo, hi):
            return _dot(h2, win_ref[:, lo:hi])

        for base, scale, o_ref in ((C_RQ, RET_DK ** -0.5, rq_ref), (C_RK, 1.0, rk_ref)):
            t = p(base, base + RET_W)
            for hd in range(RET_HEADS):
                th = t[:, hd * LANE:(hd + 1) * LANE] * scale
                if rope:
                    th = th * cos_ref[...] + _swap32(th) * sin_ref[...]
                o_ref[:, hd * LANE:(hd + 1) * LANE] = th.astype(BF16)
        rv_ref[...] = p(C_RV, C_RV + RET_W).astype(BF16)
        rg_ref[...] = _silu(p(C_RG, C_RG + RET_W)).astype(BF16)
        gq_ref[...] = (p(C_GQ, C_GQ + GLA_QK) * GLA_DK ** -0.5).astype(BF16)
        gk_ref[...] = p(C_GK, C_GK + GLA_QK).astype(BF16)
        gv_ref[...] = p(C_GV, C_GV + GLA_W).astype(BF16)
        gg_ref[...] = _silu(p(C_GG, C_GG + GLA_W)).astype(BF16)
        low = p(C_LOW, C_LOW + LANE).astype(BF16)
        z = _dot(low, gw_ref[...]) + gbias_ref[...]
        ls = _log_sigmoid(z) * (1.0 / GLA_TAU)
        gf_ref[...] = ls[:, :GLA_QK]
        gb_ref[...] = ls[:, GLA_QK:]


def _ffn_call(x, m3, row_of_tile, nw, w1, w3, w2, *, mod_off, tm, proj=None, rope=None,
              final_w=None, name):
    t, d = x.shape
    f = w1.shape[1]
    grid = (t // tm,)
    tile = lambda w: pl.BlockSpec((tm, w), lambda i: (i, 0))
    in_specs = [tile(d),
                pl.BlockSpec((None, 1, N_MOD * d), lambda i: (row_of_tile(i), 0, 0)),
                _resident((1, d)), _resident((d, f)), _resident((d, f)), _resident((f, d))]
    args = [x, m3, nw.reshape(1, d), w1, w3, w2]
    if proj is not None:
        n2w, win, gw, gbias = proj
        in_specs += [_resident((1, d)), _resident(win.shape), _resident(gw.shape), _resident(gbias.shape)]
        args += [n2w.reshape(1, d), win, gw, gbias]
        if rope is not None:
            cos, sin, tiles_per_seq = rope
            tab = pl.BlockSpec((tm, LANE), lambda i: (i % tiles_per_seq, 0))
            in_specs += [tab, tab]
            args += [cos, sin]
    if final_w is not None:
        in_specs.append(_resident((1, d)))
        args.append(final_w.reshape(1, d))
    out_specs = [tile(d)]
    out_shape = [jax.ShapeDtypeStruct((t, d), F32)]
    if proj is not None:
        for w, dt in ((RET_W, BF16),) * 4 + ((GLA_QK, BF16),) * 2 + ((GLA_W, BF16),) * 2 + ((GLA_QK, F32),) * 2:
            out_specs.append(tile(w))
            out_shape.append(jax.ShapeDtypeStruct((t, w), dt))
    kern = functools.partial(_ffn_kernel, mod_off=mod_off, proj=proj is not None,
                             rope=rope is not None, final=final_w is not None, d=d, f=f)
    return pl.pallas_call(
        kern, grid=grid, in_specs=in_specs, out_specs=out_specs, out_shape=out_shape,
        scratch_shapes=[pltpu.VMEM((tm, f), BF16)],
        compiler_params=pltpu.CompilerParams(dimension_semantics=("arbitrary",),
                                             vmem_limit_bytes=VMEM_LIMIT),
        name=name,
    )(*args)


def _head_lanes(shape, hd):
    lane = _iota(shape, len(shape) - 1)
    return (lane >= 64) if hd % 2 else (lane < 64)


def _state_kernel(rkf_ref, rvf_ref, gkf_ref, gvf_ref, gf_ref,
                  rkb_ref, rvb_ref, gkb_ref, gvb_ref, gb_ref, dec_ref,
                  irf_ref, irb_ref, igf_ref, igb_ref,
                  orf_ref, orb_ref, ogf_ref, ogb_ref,
                  frf_ref, frb_ref, fgf_ref, fgb_ref,
                  srf, srb, sgf, sgb):
    c = pl.program_id(1)
    n = CHUNK

    @pl.when(c == 0)
    def _():
        srf[...] = irf_ref[...]
        srb[...] = irb_ref[...]
        sgf[...] = igf_ref[...]
        sgb[...] = igb_ref[...]

    orf_ref[...] = srf[...].astype(BF16)
    orb_ref[...] = srb[...].astype(BF16)
    ogf_ref[...] = sgf[...].astype(BF16)
    ogb_ref[...] = sgb[...].astype(BF16)

    row = _iota((n, LANE), 0).astype(F32)
    lg = _log_sigmoid(dec_ref[...])
    ri = _iota((n, n), 0)
    ci = _iota((n, n), 1)
    after = (ci > ri).astype(BF16)
    before = (ci < ri).astype(BF16)

    for hd in range(RET_HEADS):
        sl = slice(hd * LANE, (hd + 1) * LANE)
        lgf = lg[0:1, sl]
        lgb = lg[1:2, sl]
        kf = (rkf_ref[:, sl].astype(F32) * jnp.exp((n - 1.0 - row) * lgf)).astype(BF16)
        srf[hd] = srf[hd] * jnp.exp(n * lgf) + _dot_tn(kf, rvf_ref[:, sl])
        kb = (rkb_ref[:, sl].astype(F32) * jnp.exp(row * lgb)).astype(BF16)
        srb[hd] = srb[hd] * jnp.exp(n * lgb) + _dot_tn(kb, rvb_ref[:, sl])

    def gla(g_ref, k_ref, v_ref, tri, edge, st):
        g = g_ref[...]
        e = _dot_exact_lhs(tri, g)
        tot = e[edge:edge + 1, :] + g[edge:edge + 1, :]
        kd = (k_ref[...].astype(F32) * jnp.exp(e)).astype(BF16)
        for hd in range(GLA_HEADS):
            pr = slice((hd // 2) * LANE, (hd // 2 + 1) * LANE)
            upd = _dot_tn(v_ref[:, hd * LANE:(hd + 1) * LANE], kd[:, pr])
            upd = jnp.where(_head_lanes(upd.shape, hd), upd, 0.0)
            st[hd] = st[hd] * jnp.exp(tot[:, pr]) + upd

    gla(gf_ref, gkf_ref, gvf_ref, after, 0, sgf)
    gla(gb_ref, gkb_ref, gvb_ref, before, n - 1, sgb)

    frf_ref[...] = srf[...]
    frb_ref[...] = srb[...]
    fgf_ref[...] = sgf[...]
    fgb_ref[...] = sgb[...]


def _states(rk, rv, gk, gv, gf, gb, dec, init, batch):
    t = rk.shape[0]
    nc = t // (batch * CHUNK)
    fwd = lambda w: pl.BlockSpec((CHUNK, w), lambda b, c: (b * nc + c, 0))
    bwd = lambda w: pl.BlockSpec((CHUNK, w), lambda b, c: (b * nc + nc - 1 - c, 0))
    st_shape = (RET_HEADS, LANE, LANE)
    init_spec = pl.BlockSpec((None,) + st_shape, lambda b, c: (b, 0, 0, 0))
    of_spec = pl.BlockSpec((None, None) + st_shape, lambda b, c: (b, c, 0, 0, 0))
    ob_spec = pl.BlockSpec((None, None) + st_shape, lambda b, c: (b, nc - 1 - c, 0, 0, 0))
    per_chunk = jax.ShapeDtypeStruct((batch, nc) + st_shape, BF16)
    final = jax.ShapeDtypeStruct((batch,) + st_shape, F32)
    return pl.pallas_call(
        _state_kernel,
        grid=(batch, nc),
        in_specs=[fwd(RET_W), fwd(RET_W), fwd(GLA_QK), fwd(GLA_W), fwd(GLA_QK),
                  bwd(RET_W), bwd(RET_W), bwd(GLA_QK), bwd(GLA_W), bwd(GLA_QK),
                  pl.BlockSpec(dec.shape, lambda b, c: (0, 0))] + [init_spec] * 4,
        out_specs=[of_spec, ob_spec, of_spec, ob_spec] + [init_spec] * 4,
        out_shape=[per_chunk] * 4 + [final] * 4,
        scratch_shapes=[pltpu.VMEM(st_shape, F32)] * 4,
        compiler_params=pltpu.CompilerParams(dimension_semantics=("arbitrary", "arbitrary"),
                                             vmem_limit_bytes=VMEM_LIMIT),
        name="states",
    )(rk, rv, gk, gv, gf, rk, rv, gk, gv, gb, dec, *init)


def _block_row(x, parent, r):
    n, w = x.shape
    if parent == n:
        return jnp.broadcast_to(x[r:r + 1, :], (n, w))
    x3 = x.reshape(n // parent, parent, w)
    return jnp.broadcast_to(x3[:, r:r + 1, :], x3.shape).reshape(n, w)


def _gla_tiers(q, k, gf, gb, bf, bb):
    n = q.shape[0]
    row = _iota(q.shape, 0)
    tiers = []
    s = n // 2
    while s >= 4:
        second = (row & (2 * s - 1)) >= s
        ef = jnp.exp(-jnp.abs(bf - _block_row(bf, 2 * s, s - 1)))
        eb = jnp.exp(-jnp.abs(bb - _block_row(bb, 2 * s, s)))
        tiers.append((2 * s, [(jnp.where(second, q * ef, 0.0), jnp.where(second, 0.0, k * ef)),
                              (jnp.where(second, 0.0, q * eb), jnp.where(second, k * eb, 0.0))]))
        s //= 2
    up = lambda x, j: pltpu.roll(x, n - j, 0)
    dn = lambda x, j: pltpu.roll(x, j, 0)
    r4 = row & 3
    ef, eb = jnp.exp(gf), jnp.exp(gb)
    uf = jnp.where(r4 == 2, q * ef, jnp.where(r4 == 3, q * jnp.exp(gf + dn(gf, 1)), 0.0))
    wf = jnp.where(r4 == 1, k, jnp.where(r4 == 0, k * jnp.exp(up(gf, 1)), 0.0))
    ub = jnp.where(r4 == 1, q * eb, jnp.where(r4 == 0, q * jnp.exp(gb + up(gb, 1)), 0.0))
    wb = jnp.where(r4 == 2, k, jnp.where(r4 == 3, k * jnp.exp(dn(gb, 1)), 0.0))
    tiers.append((4, [(uf, wf), (ub, wb)]))
    odd = (row & 1) == 1
    tiers.append((2, [(jnp.where(odd, q * ef, 2.0 * q), jnp.where(odd, 0.0, k)),
                      (jnp.where(odd, 2.0 * q, q * eb), jnp.where(odd, k, 0.0))]))
    return tiers


def _mix_kernel(rq_ref, rk_ref, rv_ref, rg_ref, gq_ref, gk_ref, gv_ref, gg_ref, gf_ref, gb_ref,
                srf_ref, srb_ref, sgf_ref, sgb_ref, dec_ref, rnw_ref, gnw_ref,
                x_ref, m_ref, wout_ref, o_ref, mix_ref, *, d):
    n = CHUNK
    ri = _iota((n, n), 0)
    ci = _iota((n, n), 1)
    dist = (ri - ci).astype(F32)
    row = _iota((n, LANE), 0).astype(F32)
    lg = _log_sigmoid(dec_ref[...])

    for hd in range(RET_HEADS):
        sl = slice(hd * LANE, (hd + 1) * LANE)
        lgf = lg[0:1, sl]
        lgb = lg[1:2, sl]
        q = rq_ref[:, sl]
        v = rv_ref[:, sl]
        decay = jnp.exp(jnp.where(dist > 0, dist * lgf[:, 0:1],
                                  jnp.where(dist < 0, -dist * lgb[:, 0:1], jnp.log(2.0))))
        p = (_dot_nt(q, rk_ref[:, sl]) * decay).astype(BF16)
        qf = q.astype(F32)
        qs = jnp.concatenate([(qf * jnp.exp((row + 1.0) * lgf)).astype(BF16),
                              (qf * jnp.exp((n - row) * lgb)).astype(BF16)], axis=1)
        st = jnp.concatenate([srf_ref[hd], srb_ref[hd]], axis=0)
        o = _dot(p, v) + _dot(qs, st)
        mu = jnp.mean(o, axis=-1, keepdims=True)
        oc = o - mu
        var = jnp.mean(oc * oc, axis=-1, keepdims=True)
        r = oc * lax.rsqrt(var + EPS) * rnw_ref[:, sl] * rg_ref[:, sl].astype(F32)
        mix_ref[:, sl] = r.astype(BF16)

    gf = gf_ref[...]
    gb = gb_ref[...]
    bf = _dot_exact_lhs((ci <= ri).astype(BF16), gf)
    bb = _dot_exact_lhs((ci >= ri).astype(BF16), gb)
    q = gq_ref[...].astype(F32)
    k = gk_ref[...].astype(F32)
    tiers = _gla_tiers(q, k, gf, gb, bf, bb)
    qs = [(q * jnp.exp(bf)).astype(BF16), (q * jnp.exp(bb)).astype(BF16)]
    for hd in range(GLA_HEADS):
        pr = slice((hd // 2) * LANE, (hd // 2 + 1) * LANE)
        sl = slice(hd * LANE, (hd + 1) * LANE)
        mine = _head_lanes((n, LANE), hd)
        p = None
        for size, pairs in tiers:
            u = jnp.concatenate([a[:, pr].astype(BF16) for a, _ in pairs], axis=1)
            w = jnp.concatenate([jnp.where(mine, b[:, pr], 0.0).astype(BF16) for _, b in pairs], axis=1)
            t = _dot_nt(u, w)
            if size < n:
                t = jnp.where((ri ^ ci) < size, t, 0.0)
            p = t if p is None else p + t
        qcat = jnp.concatenate([a[:, pr] for a in qs], axis=1)
        st = jnp.concatenate([sgf_ref[hd], sgb_ref[hd]], axis=1)
        o = _dot(p.astype(BF16), gv_ref[:, sl]) + _dot_nt(qcat, st)
        ms = jnp.mean(o * o, axis=-1, keepdims=True)
        r = o * lax.rsqrt(ms + EPS) * gnw_ref[:, sl] * gg_ref[:, sl].astype(F32)
        mix_ref[:, RET_W + hd * LANE:RET_W + (hd + 1) * LANE] = r.astype(BF16)

    y = _dot(mix_ref[...], wout_ref[...])
    o_ref[...] = x_ref[...] + m_ref[:, 5 * d:6 * d] * y


def _mix_call(mix_in, states, dec, rnw, gnw, x1, m3, wout, batch):
    t, d = x1.shape
    nc = t // (batch * CHUNK)
    tile = lambda w: pl.BlockSpec((CHUNK, w), lambda b, c: (b * nc + c, 0))
    st_spec = pl.BlockSpec((None, None, RET_HEADS, LANE, LANE), lambda b, c: (b, c, 0, 0, 0))
    widths = (RET_W,) * 4 + (GLA_QK,) * 2 + (GLA_W,) * 2 + (GLA_QK,) * 2
    return pl.pallas_call(
        functools.partial(_mix_kernel, d=d),
        grid=(batch, nc),
        in_specs=[tile(w) for w in widths] + [st_spec] * 4
        + [pl.BlockSpec(dec.shape, lambda b, c: (0, 0)),
           pl.BlockSpec((1, RET_W), lambda b, c: (0, 0)),
           pl.BlockSpec((1, GLA_W), lambda b, c: (0, 0)),
           tile(d),
           pl.BlockSpec((None, 1, N_MOD * d), lambda b, c: (b, 0, 0)),
           pl.BlockSpec(wout.shape, lambda b, c: (0, 0))],
        out_specs=tile(d),
        out_shape=jax.ShapeDtypeStruct((t, d), F32),
        scratch_shapes=[pltpu.VMEM((CHUNK, RET_W + GLA_W), BF16)],
        compiler_params=pltpu.CompilerParams(dimension_semantics=("arbitrary", "arbitrary"),
                                             vmem_limit_bytes=VMEM_LIMIT),
        name="mix",
    )(*mix_in, *states, dec, rnw.reshape(1, RET_W), gnw.reshape(1, GLA_W), x1, m3, wout)


def _rope_tables(n_tok):
    pos = jnp.arange(n_tok)
    freqs = ROPE_BASE ** (-jnp.arange(RET_DK // 4, dtype=F32) / (RET_DK // 4))
    ar = (pos // GRID_W).astype(F32)[:, None] * freqs
    ac = (pos % GRID_W).astype(F32)[:, None] * freqs
    cos = jnp.concatenate([jnp.cos(ar)] * 2 + [jnp.cos(ac)] * 2, axis=-1)
    sin = jnp.concatenate([-jnp.sin(ar), jnp.sin(ar), -jnp.sin(ac), jnp.sin(ac)], axis=-1)
    return cos, sin


def _pack_w_in(w_in):
    d = w_in.shape[0]
    pad = jnp.zeros((d, IN_PACKED - w_in.shape[1]), w_in.dtype)
    return jnp.concatenate([w_in, pad], axis=1).astype(BF16)


def _pack_gate(w_f, b_f, w_b, b_b):
    gw = jnp.zeros((LANE, 2 * GLA_QK), F32)
    gw = gw.at[:GLA_RANK, :GLA_QK].set(w_f).at[GLA_RANK:2 * GLA_RANK, GLA_QK:].set(w_b)
    return gw.astype(BF16), jnp.concatenate([b_f, b_b]).reshape(1, 2 * GLA_QK)


def kernel(x, c, ctx, c_ctx, ada_w, ada_b, norm1_w, ffn1_w1, ffn1_w3, ffn1_w2, norm2_w, w_in,
           ret_decay_f, ret_decay_b, ret_norm_w, gla_gate_w_f, gla_gate_b_f, gla_gate_w_b, gla_gate_b_b,
           gla_norm_w, w_out, norm3_w, ffn2_w1, ffn2_w3, ffn2_w2, final_norm_w):
    batch, n_tok, d = x.shape
    n_ctx = ctx.shape[1]
    depth = ada_w.shape[0]
    assert depth == 1 and batch + 1 <= 8
    assert n_tok % FFN_TILE == 0 and n_tok % CHUNK == 0 and n_ctx % CHUNK == 0

    cvec = jnp.zeros((8, d), F32).at[:batch].set(c).at[batch].set(c_ctx)
    m3 = _modulation(cvec, ada_w[0], ada_b[0]).reshape(8, 1, N_MOD * d)

    bf = lambda w: w.astype(BF16)
    w_in_p = _pack_w_in(w_in[0])
    gw, gbias = _pack_gate(gla_gate_w_f[0], gla_gate_b_f[0], gla_gate_w_b[0], gla_gate_b_b[0])
    proj = (norm2_w[0], w_in_p, gw, gbias)
    f1 = (norm1_w[0], bf(ffn1_w1[0]), bf(ffn1_w3[0]), bf(ffn1_w2[0]))
    cos, sin = _rope_tables(n_tok)
    dec = jnp.zeros((8, RET_W), F32)
    dec = dec.at[0].set(jnp.repeat(ret_decay_f[0], LANE)).at[1].set(jnp.repeat(ret_decay_b[0], LANE))

    tm_c = min(FFN_TILE, batch * n_ctx)
    ctx_out = _ffn_call(ctx.reshape(batch * n_ctx, d), m3, lambda i: batch, *f1, mod_off=0, tm=tm_c,
                        proj=proj, name="ffn_ctx")
    zero = jnp.zeros((batch, RET_HEADS, LANE, LANE), F32)
    ctx_states = _states(ctx_out[2], ctx_out[3], ctx_out[6], ctx_out[7], ctx_out[9], ctx_out[10],
                         dec, (zero,) * 4, batch)[4:]

    tiles_per_seq = n_tok // FFN_TILE
    lat = _ffn_call(x.reshape(batch * n_tok, d), m3, lambda i: i // tiles_per_seq, *f1, mod_off=0,
                    tm=FFN_TILE, proj=proj, rope=(cos, sin, tiles_per_seq), name="ffn_in")
    x1, mix_in = lat[0], lat[1:]
    states = _states(mix_in[1], mix_in[2], mix_in[5], mix_in[6], mix_in[8], mix_in[9],
                     dec, ctx_states, batch)[:4]
    x2 = _mix_call(mix_in, states, dec, ret_norm_w[0], gla_norm_w[0], x1, m3, bf(w_out[0]), batch)
    out = _ffn_call(x2, m3, lambda i: i // tiles_per_seq, norm3_w[0], bf(ffn2_w1[0]), bf(ffn2_w3[0]),
                    bf(ffn2_w2[0]), mod_off=6, tm=FFN_TILE, final_w=final_norm_w, name="ffn_out")[0]
    return out.reshape(batch, n_tok, d)
```

```python
import functools

import jax
import jax.numpy as jnp
from jax import lax
from jax.experimental import pallas as pl
from jax.experimental.pallas import tpu as pltpu

F32 = jnp.float32
BF16 = jnp.bfloat16

EPS = 1e-6
N_MOD = 9
GRID_W = 64
ROPE_BASE = 10000.0
RET_HEADS = 4
RET_DK = 128
RET_DV = 128
GLA_HEADS = 4
GLA_DK = 64
GLA_DV = 128
GLA_RANK = 16
GLA_TAU = 16.0
RET_W = RET_HEADS * RET_DV
GLA_W = GLA_HEADS * GLA_DV
GLA_QK = GLA_HEADS * GLA_DK

LANE = 128
CHUNK = 256
FFN_TILE = 512
FF_CHUNK = 256
VMEM_LIMIT = 60 * 1024 * 1024

C_RQ, C_RK, C_RV, C_RG = 0, 512, 1024, 1536
C_GQ, C_GK, C_GV, C_GG = 2048, 2304, 2560, 3072
C_LOW = 3584
IN_PACKED = C_LOW + LANE


def _silu(x):
    return x * (1.0 / (1.0 + jnp.exp(-x)))


def _log_sigmoid(z):
    return jnp.minimum(z, 0.0) - jnp.log(1.0 + jnp.exp(-jnp.abs(z)))


def _rms(x, w):
    return x * lax.rsqrt(jnp.mean(x * x, axis=-1, keepdims=True) + EPS) * w


def _dot(a, b):
    return jnp.dot(a, b, preferred_element_type=F32)


def _dot_nt(a, b):
    return lax.dot_general(a, b, (((1,), (1,)), ((), ())), preferred_element_type=F32)


def _dot_tn(a, b):
    return lax.dot_general(a, b, (((0,), (0,)), ((), ())), preferred_element_type=F32)


def _split3(x):
    hi = x.astype(BF16)
    r1 = x - hi.astype(F32)
    mid = r1.astype(BF16)
    lo = (r1 - mid.astype(F32)).astype(BF16)
    return hi, mid, lo


def _dot_exact_lhs(sel, x):
    hi, mid, lo = _split3(x)
    return _dot(sel, hi) + _dot(sel, mid) + _dot(sel, lo)


def _iota(shape, dim):
    return lax.broadcasted_iota(jnp.int32, shape, dim)


def _resident(shape):
    nd = len(shape)
    return pl.BlockSpec(shape, lambda *_: (0,) * nd, pipeline_mode=pl.Buffered(1))


def _mod_kernel(c_ref, w_ref, b_ref, o_ref):
    cond = _silu(c_ref[...])
    o_ref[...] = jnp.dot(cond, w_ref[...], preferred_element_type=F32,
                         precision=lax.Precision.HIGHEST) + b_ref[...]


def _modulation(cvec, ada_w, ada_b):
    d, n = ada_w.shape
    bn = d
    return pl.pallas_call(
        _mod_kernel,
        grid=(n // bn,),
        in_specs=[pl.BlockSpec((8, d), lambda j: (0, 0)),
                  pl.BlockSpec((d, bn), lambda j: (0, j)),
                  pl.BlockSpec((1, bn), lambda j: (0, j))],
        out_specs=pl.BlockSpec((8, bn), lambda j: (0, j)),
        out_shape=jax.ShapeDtypeStruct((8, n), F32),
        compiler_params=pltpu.CompilerParams(dimension_semantics=("arbitrary",)),
        name="mod",
    )(cvec, ada_w, ada_b.reshape(1, n))


def _swap32(x):
    lane = _iota(x.shape, 1)
    return jnp.where((lane & 63) < 32, pltpu.roll(x, 96, 1), pltpu.roll(x, 32, 1))


def _ffn_kernel(*refs, mod_off, proj, rope, final, d, f):
    it = iter(refs)
    x_ref, m_ref, nw_ref, w1_ref, w3_ref, w2_ref = (next(it) for _ in range(6))
    if proj:
        n2w_ref, win_ref, gw_ref, gbias_ref = (next(it) for _ in range(4))
        if rope:
            rowtab_ref, coltab_ref = next(it), next(it)
    if final:
        fnw_ref = next(it)
    xo_ref = next(it)
    if proj:
        (rq_ref, rk_ref, rv_ref, rg_ref, gq_ref, gk_ref, gv_ref, gg_ref,
         gf_ref, gb_ref) = (next(it) for _ in range(10))
    u_ref = next(it)

    def mod(i):
        return m_ref[:, (mod_off + i) * d:(mod_off + i + 1) * d]

    x = x_ref[...]
    h = (_rms(x, nw_ref[...]) * (1.0 + mod(1)) + mod(0)).astype(BF16)
    for k in range(f // FF_CHUNK):
        sl = slice(k * FF_CHUNK, (k + 1) * FF_CHUNK)
        a = _dot(h, w1_ref[:, sl])
        g = _dot(h, w3_ref[:, sl])
        u_ref[:, sl] = (_silu(a) * g).astype(BF16)
    y = _dot(u_ref[...], w2_ref[...])
    x1 = x + (0.5 * mod(2)) * y

    if final:
        xo_ref[...] = _rms(x1, fnw_ref[...])
    else:
        xo_ref[...] = x1

    if proj:
        h2 = (_rms(x1, n2w_ref[...]) * (1.0 + mod(4)) + mod(3)).astype(BF16)

        def p(lo, hi):
            return _dot(h2, win_ref[:, lo:hi])

        if rope:
            by_row = _iota((GRID_W, 2 * LANE), 1) % LANE < LANE // 2
            tab = jnp.concatenate(
                [jnp.where(by_row, jnp.broadcast_to(rowtab_ref[g:g + 1, :], (GRID_W, 2 * LANE)), coltab_ref[...])
                 for g in range(x.shape[0] // GRID_W)], axis=0)
            cos, sin = tab[:, :LANE], tab[:, LANE:]

        for base, scale, o_ref in ((C_RQ, RET_DK ** -0.5, rq_ref), (C_RK, 1.0, rk_ref)):
            t = p(base, base + RET_W)
            for hd in range(RET_HEADS):
                th = t[:, hd * LANE:(hd + 1) * LANE] * scale
                if rope:
                    th = th * cos + _swap32(th) * sin
                o_ref[:, hd * LANE:(hd + 1) * LANE] = th.astype(BF16)
        rv_ref[...] = p(C_RV, C_RV + RET_W).astype(BF16)
        rg_ref[...] = _silu(p(C_RG, C_RG + RET_W)).astype(BF16)
        gq_ref[...] = (p(C_GQ, C_GQ + GLA_QK) * GLA_DK ** -0.5).astype(BF16)
        gk_ref[...] = p(C_GK, C_GK + GLA_QK).astype(BF16)
        gv_ref[...] = p(C_GV, C_GV + GLA_W).astype(BF16)
        gg_ref[...] = _silu(p(C_GG, C_GG + GLA_W)).astype(BF16)
        low = p(C_LOW, C_LOW + LANE).astype(BF16)
        z = _dot(low, gw_ref[...]) + gbias_ref[...]
        ls = _log_sigmoid(z) * (1.0 / GLA_TAU)
        gf_ref[...] = ls[:, :GLA_QK]
        gb_ref[...] = ls[:, GLA_QK:]


def _ffn_call(x, m3, row_of_tile, nw, w1, w3, w2, *, mod_off, tm, proj=None, rope=None,
              final_w=None, name):
    t, d = x.shape
    f = w1.shape[1]
    grid = (t // tm,)
    tile = lambda w: pl.BlockSpec((tm, w), lambda i: (i, 0))
    in_specs = [tile(d),
                pl.BlockSpec((None, 1, N_MOD * d), lambda i: (row_of_tile(i), 0, 0)),
                _resident((1, d)), _resident((d, f)), _resident((d, f)), _resident((f, d))]
    args = [x, m3, nw.reshape(1, d), w1, w3, w2]
    if proj is not None:
        n2w, win, gw, gbias = proj
        in_specs += [_resident((1, d)), _resident(win.shape), _resident(gw.shape), _resident(gbias.shape)]
        args += [n2w.reshape(1, d), win, gw, gbias]
        if rope is not None:
            rowtab, coltab, tiles_per_seq = rope
            in_specs += [pl.BlockSpec((tm // GRID_W, 2 * LANE), lambda i: (i % tiles_per_seq, 0)),
                         _resident(coltab.shape)]
            args += [rowtab, coltab]
    if final_w is not None:
        in_specs.append(_resident((1, d)))
        args.append(final_w.reshape(1, d))
    out_specs = [tile(d)]
    out_shape = [jax.ShapeDtypeStruct((t, d), F32)]
    if proj is not None:
        for w, dt in ((RET_W, BF16),) * 4 + ((GLA_QK, BF16),) * 2 + ((GLA_W, BF16),) * 2 + ((GLA_QK, F32),) * 2:
            out_specs.append(tile(w))
            out_shape.append(jax.ShapeDtypeStruct((t, w), dt))
    kern = functools.partial(_ffn_kernel, mod_off=mod_off, proj=proj is not None,
                             rope=rope is not None, final=final_w is not None, d=d, f=f)
    return pl.pallas_call(
        kern, grid=grid, in_specs=in_specs, out_specs=out_specs, out_shape=out_shape,
        scratch_shapes=[pltpu.VMEM((tm, f), BF16)],
        compiler_params=pltpu.CompilerParams(dimension_semantics=("arbitrary",),
                                             vmem_limit_bytes=VMEM_LIMIT),
        name=name,
    )(*args)


def _head_lanes(shape, hd):
    lane = _iota(shape, len(shape) - 1)
    return (lane >= 64) if hd % 2 else (lane < 64)


def _state_kernel(rkf_ref, rvf_ref, gkf_ref, gvf_ref, gf_ref,
                  rkb_ref, rvb_ref, gkb_ref, gvb_ref, gb_ref, dec_ref,
                  irf_ref, irb_ref, igf_ref, igb_ref,
                  orf_ref, orb_ref, ogf_ref, ogb_ref,
                  frf_ref, frb_ref, fgf_ref, fgb_ref,
                  srf, srb, sgf, sgb):
    n = CHUNK

    @pl.when(pl.program_id(0) == 0)
    def _():
        srf[...] = irf_ref[...]
        srb[...] = irb_ref[...]
        sgf[...] = igf_ref[...]
        sgb[...] = igb_ref[...]

    orf_ref[...] = srf[...].astype(BF16)
    orb_ref[...] = srb[...].astype(BF16)
    ogf_ref[...] = sgf[...].astype(BF16)
    ogb_ref[...] = sgb[...].astype(BF16)

    row = _iota((n, LANE), 0).astype(F32)
    lg = _log_sigmoid(dec_ref[...])
    ri = _iota((n, n), 0)
    ci = _iota((n, n), 1)
    after = (ci > ri).astype(BF16)
    before = (ci < ri).astype(BF16)

    def gla(b, g_ref, k_ref, v_ref, tri, edge, st):
        g = g_ref[b]
        e = _dot_exact_lhs(tri, g)
        tot = e[edge:edge + 1, :] + g[edge:edge + 1, :]
        kd = (k_ref[b].astype(F32) * jnp.exp(e)).astype(BF16)
        for hd in range(GLA_HEADS):
            pr = slice((hd // 2) * LANE, (hd // 2 + 1) * LANE)
            upd = _dot_tn(v_ref[b, :, hd * LANE:(hd + 1) * LANE], kd[:, pr])
            upd = jnp.where(_head_lanes(upd.shape, hd), upd, 0.0)
            st[b, hd] = st[b, hd] * jnp.exp(tot[:, pr]) + upd

    for b in range(srf.shape[0]):
        for hd in range(RET_HEADS):
            sl = slice(hd * LANE, (hd + 1) * LANE)
            lgf = lg[0:1, sl]
            lgb = lg[1:2, sl]
            kf = (rkf_ref[b, :, sl].astype(F32) * jnp.exp((n - 1.0 - row) * lgf)).astype(BF16)
            srf[b, hd] = srf[b, hd] * jnp.exp(n * lgf) + _dot_tn(kf, rvf_ref[b, :, sl])
            kb = (rkb_ref[b, :, sl].astype(F32) * jnp.exp(row * lgb)).astype(BF16)
            srb[b, hd] = srb[b, hd] * jnp.exp(n * lgb) + _dot_tn(kb, rvb_ref[b, :, sl])
        gla(b, gf_ref, gkf_ref, gvf_ref, after, 0, sgf)
        gla(b, gb_ref, gkb_ref, gvb_ref, before, n - 1, sgb)

    frf_ref[...] = srf[...]
    frb_ref[...] = srb[...]
    fgf_ref[...] = sgf[...]
    fgb_ref[...] = sgb[...]


def _states(rk, rv, gk, gv, gf, gb, dec, init, batch):
    nc = rk.shape[0] // (batch * CHUNK)
    seq = lambda a: a.reshape(batch, nc * CHUNK, a.shape[-1])
    fwd = lambda w: pl.BlockSpec((batch, CHUNK, w), lambda c: (0, c, 0))
    bwd = lambda w: pl.BlockSpec((batch, CHUNK, w), lambda c: (0, nc - 1 - c, 0))
    st_shape = (batch, RET_HEADS, LANE, LANE)
    init_spec = pl.BlockSpec(st_shape, lambda c: (0, 0, 0, 0))
    chunk_shape = (batch, None, RET_HEADS, LANE, LANE)
    of_spec = pl.BlockSpec(chunk_shape, lambda c: (0, c, 0, 0, 0))
    ob_spec = pl.BlockSpec(chunk_shape, lambda c: (0, nc - 1 - c, 0, 0, 0))
    per_chunk = jax.ShapeDtypeStruct((batch, nc, RET_HEADS, LANE, LANE), BF16)
    final = jax.ShapeDtypeStruct(st_shape, F32)
    rk, rv, gk, gv, gf, gb = (seq(a) for a in (rk, rv, gk, gv, gf, gb))
    return pl.pallas_call(
        _state_kernel,
        grid=(nc,),
        in_specs=[fwd(RET_W), fwd(RET_W), fwd(GLA_QK), fwd(GLA_W), fwd(GLA_QK),
                  bwd(RET_W), bwd(RET_W), bwd(GLA_QK), bwd(GLA_W), bwd(GLA_QK),
                  pl.BlockSpec(dec.shape, lambda c: (0, 0))] + [init_spec] * 4,
        out_specs=[of_spec, ob_spec, of_spec, ob_spec] + [init_spec] * 4,
        out_shape=[per_chunk] * 4 + [final] * 4,
        scratch_shapes=[pltpu.VMEM(st_shape, F32)] * 4,
        compiler_params=pltpu.CompilerParams(dimension_semantics=("arbitrary",),
                                             vmem_limit_bytes=VMEM_LIMIT),
        name="states",
    )(rk, rv, gk, gv, gf, rk, rv, gk, gv, gb, dec, *init)


def _block_row(x, parent, r):
    n, w = x.shape
    if parent == n:
        return jnp.broadcast_to(x[r:r + 1, :], (n, w))
    x3 = x.reshape(n // parent, parent, w)
    return jnp.broadcast_to(x3[:, r:r + 1, :], x3.shape).reshape(n, w)


def _gla_tiers(q, k, gf, gb, bf, bb):
    n = q.shape[0]
    row = _iota(q.shape, 0)
    tiers = []
    s = n // 2
    while s >= 4:
        second = (row & (2 * s - 1)) >= s
        ef = jnp.exp(-jnp.abs(bf - _block_row(bf, 2 * s, s - 1)))
        eb = jnp.exp(-jnp.abs(bb - _block_row(bb, 2 * s, s)))
        tiers.append((2 * s, [(jnp.where(second, q * ef, 0.0), jnp.where(second, 0.0, k * ef)),
                              (jnp.where(second, 0.0, q * eb), jnp.where(second, k * eb, 0.0))]))
        s //= 2
    up = lambda x, j: pltpu.roll(x, n - j, 0)
    dn = lambda x, j: pltpu.roll(x, j, 0)
    r4 = row & 3
    ef, eb = jnp.exp(gf), jnp.exp(gb)
    uf = jnp.where(r4 == 2, q * ef, jnp.where(r4 == 3, q * jnp.exp(gf + dn(gf, 1)), 0.0))
    wf = jnp.where(r4 == 1, k, jnp.where(r4 == 0, k * jnp.exp(up(gf, 1)), 0.0))
    ub = jnp.where(r4 == 1, q * eb, jnp.where(r4 == 0, q * jnp.exp(gb + up(gb, 1)), 0.0))
    wb = jnp.where(r4 == 2, k, jnp.where(r4 == 3, k * jnp.exp(dn(gb, 1)), 0.0))
    tiers.append((4, [(uf, wf), (ub, wb)]))
    odd = (row & 1) == 1
    tiers.append((2, [(jnp.where(odd, q * ef, 2.0 * q), jnp.where(odd, 0.0, k)),
                      (jnp.where(odd, 2.0 * q, q * eb), jnp.where(odd, k, 0.0))]))
    return tiers


def _mix_kernel(rq_ref, rk_ref, rv_ref, rg_ref, gq_ref, gk_ref, gv_ref, gg_ref, gf_ref, gb_ref,
                srf_ref, srb_ref, sgf_ref, sgb_ref, dec_ref, rnw_ref, gnw_ref,
                x_ref, m_ref, wout_ref, o_ref, mix_ref, decay_ref, *, d):
    n = CHUNK
    ri = _iota((n, n), 0)
    ci = _iota((n, n), 1)
    row = _iota((n, LANE), 0).astype(F32)
    lg = _log_sigmoid(dec_ref[...])

    @pl.when((pl.program_id(0) == 0) & (pl.program_id(1) == 0))
    def _():
        dist = (ri - ci).astype(F32)
        for hd in range(RET_HEADS):
            lgf = lg[0:1, hd * LANE:hd * LANE + 1]
            lgb = lg[1:2, hd * LANE:hd * LANE + 1]
            decay_ref[hd] = jnp.exp(jnp.where(dist > 0, dist * lgf,
                                              jnp.where(dist < 0, -dist * lgb, jnp.log(2.0))))

    for hd in range(RET_HEADS):
        sl = slice(hd * LANE, (hd + 1) * LANE)
        lgf = lg[0:1, sl]
        lgb = lg[1:2, sl]
        q = rq_ref[:, sl]
        v = rv_ref[:, sl]
        p = (_dot_nt(q, rk_ref[:, sl]) * decay_ref[hd]).astype(BF16)
        qf = q.astype(F32)
        qs = jnp.concatenate([(qf * jnp.exp((row + 1.0) * lgf)).astype(BF16),
                              (qf * jnp.exp((n - row) * lgb)).astype(BF16)], axis=1)
        st = jnp.concatenate([srf_ref[hd], srb_ref[hd]], axis=0)
        o = _dot(p, v) + _dot(qs, st)
        mu = jnp.mean(o, axis=-1, keepdims=True)
        oc = o - mu
        var = jnp.mean(oc * oc, axis=-1, keepdims=True)
        r = oc * lax.rsqrt(var + EPS) * rnw_ref[:, sl] * rg_ref[:, sl].astype(F32)
        mix_ref[:, sl] = r.astype(BF16)

    gf = gf_ref[...]
    gb = gb_ref[...]
    bf = _dot_exact_lhs((ci <= ri).astype(BF16), gf)
    bb = _dot_exact_lhs((ci >= ri).astype(BF16), gb)
    q = gq_ref[...].astype(F32)
    k = gk_ref[...].astype(F32)
    tiers = _gla_tiers(q, k, gf, gb, bf, bb)
    qs = [(q * jnp.exp(bf)).astype(BF16), (q * jnp.exp(bb)).astype(BF16)]
    for hd in range(GLA_HEADS):
        pr = slice((hd // 2) * LANE, (hd // 2 + 1) * LANE)
        sl = slice(hd * LANE, (hd + 1) * LANE)
        mine = _head_lanes((n, LANE), hd)
        p = None
        for size, pairs in tiers:
            u = jnp.concatenate([a[:, pr].astype(BF16) for a, _ in pairs], axis=1)
            w = jnp.concatenate([jnp.where(mine, b[:, pr], 0.0).astype(BF16) for _, b in pairs], axis=1)
            t = _dot_nt(u, w)
            if size < n:
                t = jnp.where((ri ^ ci) < size, t, 0.0)
            p = t if p is None else p + t
        qcat = jnp.concatenate([a[:, pr] for a in qs], axis=1)
        st = jnp.concatenate([sgf_ref[hd], sgb_ref[hd]], axis=1)
        o = _dot(p.astype(BF16), gv_ref[:, sl]) + _dot_nt(qcat, st)
        ms = jnp.mean(o * o, axis=-1, keepdims=True)
        r = o * lax.rsqrt(ms + EPS) * gnw_ref[:, sl] * gg_ref[:, sl].astype(F32)
        mix_ref[:, RET_W + hd * LANE:RET_W + (hd + 1) * LANE] = r.astype(BF16)

    y = _dot(mix_ref[...], wout_ref[...])
    o_ref[...] = x_ref[...] + m_ref[:, 5 * d:6 * d] * y


def _mix_call(mix_in, states, dec, rnw, gnw, x1, m3, wout, batch):
    t, d = x1.shape
    nc = t // (batch * CHUNK)
    tile = lambda w: pl.BlockSpec((CHUNK, w), lambda b, c: (b * nc + c, 0))
    st_spec = pl.BlockSpec((None, None, RET_HEADS, LANE, LANE), lambda b, c: (b, c, 0, 0, 0))
    widths = (RET_W,) * 4 + (GLA_QK,) * 2 + (GLA_W,) * 2 + (GLA_QK,) * 2
    return pl.pallas_call(
        functools.partial(_mix_kernel, d=d),
        grid=(batch, nc),
        in_specs=[tile(w) for w in widths] + [st_spec] * 4
        + [pl.BlockSpec(dec.shape, lambda b, c: (0, 0)),
           pl.BlockSpec((1, RET_W), lambda b, c: (0, 0)),
           pl.BlockSpec((1, GLA_W), lambda b, c: (0, 0)),
           tile(d),
           pl.BlockSpec((None, 1, N_MOD * d), lambda b, c: (b, 0, 0)),
           pl.BlockSpec(wout.shape, lambda b, c: (0, 0))],
        out_specs=tile(d),
        out_shape=jax.ShapeDtypeStruct((t, d), F32),
        scratch_shapes=[pltpu.VMEM((CHUNK, RET_W + GLA_W), BF16),
                        pltpu.VMEM((RET_HEADS, CHUNK, CHUNK), F32)],
        compiler_params=pltpu.CompilerParams(dimension_semantics=("arbitrary", "arbitrary"),
                                             vmem_limit_bytes=VMEM_LIMIT),
        name="mix",
    )(*mix_in, *states, dec, rnw.reshape(1, RET_W), gnw.reshape(1, GLA_W), x1, m3, wout)


def _rope_tables(n_tok):
    freqs = ROPE_BASE ** (-jnp.arange(RET_DK // 4, dtype=F32) / (RET_DK // 4))

    def table(n_pos, first_half):
        ang = jnp.arange(n_pos, dtype=F32)[:, None] * freqs
        zero = jnp.zeros((n_pos, LANE // 2), F32)
        cos = jnp.concatenate([jnp.cos(ang)] * 2, axis=-1)
        sin = jnp.concatenate([-jnp.sin(ang), jnp.sin(ang)], axis=-1)
        halves = (cos, zero, sin, zero) if first_half else (zero, cos, zero, sin)
        return jnp.concatenate(halves, axis=-1)

    return table(n_tok // GRID_W, True), table(GRID_W, False)


def _pack_w_in(w_in):
    d = w_in.shape[0]
    pad = jnp.zeros((d, IN_PACKED - w_in.shape[1]), w_in.dtype)
    return jnp.concatenate([w_in, pad], axis=1).astype(BF16)


def _pack_gate(w_f, b_f, w_b, b_b):
    gw = jnp.zeros((LANE, 2 * GLA_QK), F32)
    gw = gw.at[:GLA_RANK, :GLA_QK].set(w_f).at[GLA_RANK:2 * GLA_RANK, GLA_QK:].set(w_b)
    return gw.astype(BF16), jnp.concatenate([b_f, b_b]).reshape(1, 2 * GLA_QK)


def kernel(x, c, ctx, c_ctx, ada_w, ada_b, norm1_w, ffn1_w1, ffn1_w3, ffn1_w2, norm2_w, w_in,
           ret_decay_f, ret_decay_b, ret_norm_w, gla_gate_w_f, gla_gate_b_f, gla_gate_w_b, gla_gate_b_b,
           gla_norm_w, w_out, norm3_w, ffn2_w1, ffn2_w3, ffn2_w2, final_norm_w):
    batch, n_tok, d = x.shape
    n_ctx = ctx.shape[1]
    depth = ada_w.shape[0]
    assert depth == 1 and batch + 1 <= 8
    assert n_tok % FFN_TILE == 0 and n_tok % CHUNK == 0 and n_ctx % CHUNK == 0

    cvec = jnp.zeros((8, d), F32).at[:batch].set(c).at[batch].set(c_ctx)
    m3 = _modulation(cvec, ada_w[0], ada_b[0]).reshape(8, 1, N_MOD * d)

    bf = lambda w: w.astype(BF16)
    w_in_p = _pack_w_in(w_in[0])
    gw, gbias = _pack_gate(gla_gate_w_f[0], gla_gate_b_f[0], gla_gate_w_b[0], gla_gate_b_b[0])
    proj = (norm2_w[0], w_in_p, gw, gbias)
    f1 = (norm1_w[0], bf(ffn1_w1[0]), bf(ffn1_w3[0]), bf(ffn1_w2[0]))
    rowtab, coltab = _rope_tables(n_tok)
    dec = jnp.zeros((8, RET_W), F32)
    dec = dec.at[0].set(jnp.repeat(ret_decay_f[0], LANE)).at[1].set(jnp.repeat(ret_decay_b[0], LANE))

    tm_c = min(FFN_TILE, batch * n_ctx)
    ctx_out = _ffn_call(ctx.reshape(batch * n_ctx, d), m3, lambda i: batch, *f1, mod_off=0, tm=tm_c,
                        proj=proj, name="ffn_ctx")
    zero = jnp.zeros((batch, RET_HEADS, LANE, LANE), F32)
    ctx_states = _states(ctx_out[2], ctx_out[3], ctx_out[6], ctx_out[7], ctx_out[9], ctx_out[10],
                         dec, (zero,) * 4, batch)[4:]

    tiles_per_seq = n_tok // FFN_TILE
    lat = _ffn_call(x.reshape(batch * n_tok, d), m3, lambda i: i // tiles_per_seq, *f1, mod_off=0,
                    tm=FFN_TILE, proj=proj, rope=(rowtab, coltab, tiles_per_seq), name="ffn_in")
    x1, mix_in = lat[0], lat[1:]
    states = _states(mix_in[1], mix_in[2], mix_in[5], mix_in[6], mix_in[8], mix_in[9],
                     dec, ctx_states, batch)[:4]
    x2 = _mix_call(mix_in, states, dec, ret_norm_w[0], gla_norm_w[0], x1, m3, bf(w_out[0]), batch)
    out = _ffn_call(x2, m3, lambda i: i // tiles_per_seq, norm3_w[0], bf(ffn2_w1[0]), bf(ffn2_w3[0]),
                    bf(ffn2_w2[0]), mod_off=6, tm=FFN_TILE, final_w=final_norm_w, name="ffn_out")[0]
    return out.reshape(batch, n_tok, d)
```

```python
import functools

import jax
import jax.numpy as jnp
from jax import lax
from jax.experimental import pallas as pl
from jax.experimental.pallas import tpu as pltpu

F32 = jnp.float32
BF16 = jnp.bfloat16

EPS = 1e-6
LOG2E = 1.4426950408889634
N_MOD = 9
GRID_W = 64
ROPE_BASE = 10000.0
RET_HEADS = 4
RET_DK = 128
RET_DV = 128
GLA_HEADS = 4
GLA_DK = 64
GLA_DV = 128
GLA_RANK = 16
GLA_TAU = 16.0
RET_W = RET_HEADS * RET_DV
GLA_W = GLA_HEADS * GLA_DV
GLA_QK = GLA_HEADS * GLA_DK

LANE = 128
CHUNK = 256
FFN_TILE = 512
FF_CHUNK = 256
VMEM_LIMIT = 60 * 1024 * 1024

C_RQ, C_RK, C_RV, C_RG = 0, 512, 1024, 1536
C_GQ, C_GK, C_GV, C_GG = 2048, 2304, 2560, 3072
C_LOW = 3584
IN_PACKED = C_LOW + LANE


def _silu(x):
    return x * (1.0 / (1.0 + jnp.exp(-x)))


def _log_sigmoid(z):
    return jnp.minimum(z, 0.0) - jnp.log(1.0 + jnp.exp(-jnp.abs(z)))


def _rms(x, w):
    return x * lax.rsqrt(jnp.mean(x * x, axis=-1, keepdims=True) + EPS) * w


def _dot(a, b):
    return jnp.dot(a, b, preferred_element_type=F32)


def _dot_nt(a, b):
    return lax.dot_general(a, b, (((1,), (1,)), ((), ())), preferred_element_type=F32)


def _dot_tn(a, b):
    return lax.dot_general(a, b, (((0,), (0,)), ((), ())), preferred_element_type=F32)


def _dot_select(sel, x):
    hi = x.astype(BF16)
    lo = (x - hi.astype(F32)).astype(BF16)
    return _dot(sel, hi) + _dot(sel, lo)


def _iota(shape, dim):
    return lax.broadcasted_iota(jnp.int32, shape, dim)


def _resident(shape):
    nd = len(shape)
    return pl.BlockSpec(shape, lambda *_: (0,) * nd, pipeline_mode=pl.Buffered(1))


def _mod_kernel(c_ref, w_ref, b_ref, o_ref):
    cond = _silu(c_ref[...])
    o_ref[...] = jnp.dot(cond, w_ref[...], preferred_element_type=F32,
                         precision=lax.Precision.HIGHEST) + b_ref[...]


def _modulation(cvec, ada_w, ada_b):
    d, n = ada_w.shape
    bn = d
    return pl.pallas_call(
        _mod_kernel,
        grid=(n // bn,),
        in_specs=[pl.BlockSpec((8, d), lambda j: (0, 0)),
                  pl.BlockSpec((d, bn), lambda j: (0, j)),
                  pl.BlockSpec((1, bn), lambda j: (0, j))],
        out_specs=pl.BlockSpec((8, bn), lambda j: (0, j)),
        out_shape=jax.ShapeDtypeStruct((8, n), F32),
        compiler_params=pltpu.CompilerParams(dimension_semantics=("arbitrary",)),
        name="mod",
    )(cvec, ada_w, ada_b.reshape(1, n))


def _swap32(x):
    lane = _iota(x.shape, 1)
    return jnp.where((lane & 63) < 32, pltpu.roll(x, 96, 1), pltpu.roll(x, 32, 1))


def _ffn_kernel(*refs, mod_off, proj, rope, final, d, f):
    it = iter(refs)
    x_ref, m_ref, nw_ref, w1_ref, w3_ref, w2_ref = (next(it) for _ in range(6))
    if proj:
        n2w_ref, win_ref, gw_ref, gbias_ref = (next(it) for _ in range(4))
        if rope:
            rowtab_ref, coltab_ref = next(it), next(it)
    if final:
        fnw_ref = next(it)
    xo_ref = next(it)
    if proj:
        (rq_ref, rk_ref, rv_ref, rg_ref, gq_ref, gk_ref, gv_ref, gg_ref,
         gf_ref, gb_ref) = (next(it) for _ in range(10))
    u_ref = next(it)

    def mod(i):
        return m_ref[:, (mod_off + i) * d:(mod_off + i + 1) * d]

    x = x_ref[...]
    h = (_rms(x, nw_ref[...]) * (1.0 + mod(1)) + mod(0)).astype(BF16)
    for k in range(f // FF_CHUNK):
        sl = slice(k * FF_CHUNK, (k + 1) * FF_CHUNK)
        a = _dot(h, w1_ref[:, sl])
        g = _dot(h, w3_ref[:, sl])
        u_ref[:, sl] = (_silu(a) * g).astype(BF16)
    y = _dot(u_ref[...], w2_ref[...])
    x1 = x + (0.5 * mod(2)) * y

    if final:
        xo_ref[...] = _rms(x1, fnw_ref[...])
    else:
        xo_ref[...] = x1

    if proj:
        h2 = (_rms(x1, n2w_ref[...]) * (1.0 + mod(4)) + mod(3)).astype(BF16)

        def p(lo, hi):
            return _dot(h2, win_ref[:, lo:hi])

        if rope:
            by_row = _iota((GRID_W, 2 * LANE), 1) % LANE < LANE // 2
            tab = jnp.concatenate(
                [jnp.where(by_row, jnp.broadcast_to(rowtab_ref[g:g + 1, :], (GRID_W, 2 * LANE)), coltab_ref[...])
                 for g in range(x.shape[0] // GRID_W)], axis=0)
            cos, sin = tab[:, :LANE], tab[:, LANE:]

        for base, scale, o_ref in ((C_RQ, RET_DK ** -0.5, rq_ref), (C_RK, 1.0, rk_ref)):
            t = p(base, base + RET_W)
            for hd in range(RET_HEADS):
                th = t[:, hd * LANE:(hd + 1) * LANE] * scale
                if rope:
                    th = th * cos + _swap32(th) * sin
                o_ref[:, hd * LANE:(hd + 1) * LANE] = th.astype(BF16)
        rv_ref[...] = p(C_RV, C_RV + RET_W).astype(BF16)
        rg_ref[...] = _silu(p(C_RG, C_RG + RET_W)).astype(BF16)
        gq_ref[...] = (p(C_GQ, C_GQ + GLA_QK) * GLA_DK ** -0.5).astype(BF16)
        gk_ref[...] = p(C_GK, C_GK + GLA_QK).astype(BF16)
        gv_ref[...] = p(C_GV, C_GV + GLA_W).astype(BF16)
        gg_ref[...] = _silu(p(C_GG, C_GG + GLA_W)).astype(BF16)
        low = p(C_LOW, C_LOW + LANE).astype(BF16)
        z = _dot(low, gw_ref[...]) + gbias_ref[...]
        ls = _log_sigmoid(z) * (1.0 / GLA_TAU)
        gf_ref[...] = ls[:, :GLA_QK]
        gb_ref[...] = ls[:, GLA_QK:]


def _ffn_call(x, m3, row_of_tile, nw, w1, w3, w2, *, mod_off, tm, proj=None, rope=None,
              final_w=None, name):
    t, d = x.shape
    f = w1.shape[1]
    grid = (t // tm,)
    tile = lambda w: pl.BlockSpec((tm, w), lambda i: (i, 0))
    in_specs = [tile(d),
                pl.BlockSpec((None, 1, N_MOD * d), lambda i: (row_of_tile(i), 0, 0)),
                _resident((1, d)), _resident((d, f)), _resident((d, f)), _resident((f, d))]
    args = [x, m3, nw.reshape(1, d), w1, w3, w2]
    if proj is not None:
        n2w, win, gw, gbias = proj
        in_specs += [_resident((1, d)), _resident(win.shape), _resident(gw.shape), _resident(gbias.shape)]
        args += [n2w.reshape(1, d), win, gw, gbias]
        if rope is not None:
            rowtab, coltab, tiles_per_seq = rope
            in_specs += [pl.BlockSpec((tm // GRID_W, 2 * LANE), lambda i: (i % tiles_per_seq, 0)),
                         _resident(coltab.shape)]
            args += [rowtab, coltab]
    if final_w is not None:
        in_specs.append(_resident((1, d)))
        args.append(final_w.reshape(1, d))
    out_specs = [tile(d)]
    out_shape = [jax.ShapeDtypeStruct((t, d), F32)]
    if proj is not None:
        for w, dt in ((RET_W, BF16),) * 4 + ((GLA_QK, BF16),) * 2 + ((GLA_W, BF16),) * 2 + ((GLA_QK, F32),) * 2:
            out_specs.append(tile(w))
            out_shape.append(jax.ShapeDtypeStruct((t, w), dt))
    kern = functools.partial(_ffn_kernel, mod_off=mod_off, proj=proj is not None,
                             rope=rope is not None, final=final_w is not None, d=d, f=f)
    return pl.pallas_call(
        kern, grid=grid, in_specs=in_specs, out_specs=out_specs, out_shape=out_shape,
        scratch_shapes=[pltpu.VMEM((tm, f), BF16)],
        compiler_params=pltpu.CompilerParams(dimension_semantics=("arbitrary",),
                                             vmem_limit_bytes=VMEM_LIMIT),
        name=name,
    )(*args)


def _head_lanes(shape, hd):
    lane = _iota(shape, len(shape) - 1)
    return (lane >= 64) if hd % 2 else (lane < 64)


def _state_kernel(rkf_ref, rvf_ref, gkf_ref, gvf_ref, gf_ref,
                  rkb_ref, rvb_ref, gkb_ref, gvb_ref, gb_ref, dec_ref,
                  irf_ref, irb_ref, igf_ref, igb_ref,
                  orf_ref, orb_ref, ogf_ref, ogb_ref,
                  frf_ref, frb_ref, fgf_ref, fgb_ref,
                  srf, srb, sgf, sgb):
    n = CHUNK

    @pl.when(pl.program_id(0) == 0)
    def _():
        srf[...] = irf_ref[...]
        srb[...] = irb_ref[...]
        sgf[...] = igf_ref[...]
        sgb[...] = igb_ref[...]

    orf_ref[...] = srf[...].astype(BF16)
    orb_ref[...] = srb[...].astype(BF16)
    ogf_ref[...] = sgf[...].astype(BF16)
    ogb_ref[...] = sgb[...].astype(BF16)

    row = _iota((n, LANE), 0).astype(F32)
    lg = _log_sigmoid(dec_ref[...])
    ri = _iota((n, n), 0)
    ci = _iota((n, n), 1)
    after = (ci > ri).astype(BF16)
    before = (ci < ri).astype(BF16)

    def gla(b, g_ref, k_ref, v_ref, tri, edge, st):
        g = g_ref[b]
        e = _dot_select(tri, g)
        tot = e[edge:edge + 1, :] + g[edge:edge + 1, :]
        kd = (k_ref[b].astype(F32) * jnp.exp(e)).astype(BF16)
        for hd in range(GLA_HEADS):
            pr = slice((hd // 2) * LANE, (hd // 2 + 1) * LANE)
            upd = _dot_tn(v_ref[b, :, hd * LANE:(hd + 1) * LANE], kd[:, pr])
            upd = jnp.where(_head_lanes(upd.shape, hd), upd, 0.0)
            st[b, hd] = st[b, hd] * jnp.exp(tot[:, pr]) + upd

    for b in range(srf.shape[0]):
        for hd in range(RET_HEADS):
            sl = slice(hd * LANE, (hd + 1) * LANE)
            lgf = lg[0:1, sl]
            lgb = lg[1:2, sl]
            kf = (rkf_ref[b, :, sl].astype(F32) * jnp.exp((n - 1.0 - row) * lgf)).astype(BF16)
            srf[b, hd] = srf[b, hd] * jnp.exp(n * lgf) + _dot_tn(kf, rvf_ref[b, :, sl])
            kb = (rkb_ref[b, :, sl].astype(F32) * jnp.exp(row * lgb)).astype(BF16)
            srb[b, hd] = srb[b, hd] * jnp.exp(n * lgb) + _dot_tn(kb, rvb_ref[b, :, sl])
        gla(b, gf_ref, gkf_ref, gvf_ref, after, 0, sgf)
        gla(b, gb_ref, gkb_ref, gvb_ref, before, n - 1, sgb)

    frf_ref[...] = srf[...]
    frb_ref[...] = srb[...]
    fgf_ref[...] = sgf[...]
    fgb_ref[...] = sgb[...]


def _states(rk, rv, gk, gv, gf, gb, dec, init, batch):
    nc = rk.shape[0] // (batch * CHUNK)
    seq = lambda a: a.reshape(batch, nc * CHUNK, a.shape[-1])
    fwd = lambda w: pl.BlockSpec((batch, CHUNK, w), lambda c: (0, c, 0))
    bwd = lambda w: pl.BlockSpec((batch, CHUNK, w), lambda c: (0, nc - 1 - c, 0))
    st_shape = (batch, RET_HEADS, LANE, LANE)
    init_spec = pl.BlockSpec(st_shape, lambda c: (0, 0, 0, 0))
    chunk_shape = (batch, None, RET_HEADS, LANE, LANE)
    of_spec = pl.BlockSpec(chunk_shape, lambda c: (0, c, 0, 0, 0))
    ob_spec = pl.BlockSpec(chunk_shape, lambda c: (0, nc - 1 - c, 0, 0, 0))
    per_chunk = jax.ShapeDtypeStruct((batch, nc, RET_HEADS, LANE, LANE), BF16)
    final = jax.ShapeDtypeStruct(st_shape, F32)
    rk, rv, gk, gv, gf, gb = (seq(a) for a in (rk, rv, gk, gv, gf, gb))
    return pl.pallas_call(
        _state_kernel,
        grid=(nc,),
        in_specs=[fwd(RET_W), fwd(RET_W), fwd(GLA_QK), fwd(GLA_W), fwd(GLA_QK),
                  bwd(RET_W), bwd(RET_W), bwd(GLA_QK), bwd(GLA_W), bwd(GLA_QK),
                  pl.BlockSpec(dec.shape, lambda c: (0, 0))] + [init_spec] * 4,
        out_specs=[of_spec, ob_spec, of_spec, ob_spec] + [init_spec] * 4,
        out_shape=[per_chunk] * 4 + [final] * 4,
        scratch_shapes=[pltpu.VMEM(st_shape, F32)] * 4,
        compiler_params=pltpu.CompilerParams(dimension_semantics=("arbitrary",),
                                             vmem_limit_bytes=VMEM_LIMIT),
        name="states",
    )(rk, rv, gk, gv, gf, rk, rv, gk, gv, gb, dec, *init)


def _block_row(x, parent, r):
    n, w = x.shape
    if parent == n:
        return jnp.broadcast_to(x[r:r + 1, :], (n, w))
    x3 = x.reshape(n // parent, parent, w)
    return jnp.broadcast_to(x3[:, r:r + 1, :], x3.shape).reshape(n, w)


def _gla_levels(q, k_even, k_odd, gf, gb, bf, bb):
    n = q.shape[0]
    row = _iota(q.shape, 0)

    def factors(u_exp, w_exp):
        ew = jnp.exp2(w_exp).astype(BF16)
        return (q * jnp.exp2(u_exp).astype(BF16), k_even * ew, k_odd * ew)

    levels = []
    s = n // 2
    while s >= 2:
        if s >= 4:
            last_of_first = _block_row(bf, 2 * s, s - 1)
            first_of_second = _block_row(bb, 2 * s, s)
        else:
            upper = (row & 4) != 0
            last_of_first = jnp.where(upper, _block_row(bf, 8, 5), _block_row(bf, 8, 1))
            first_of_second = jnp.where(upper, _block_row(bb, 8, 6), _block_row(bb, 8, 2))
        df = bf - last_of_first
        db = bb - first_of_second
        levels.append((2 * s, [factors(jnp.minimum(df, db), -jnp.maximum(df, db))]))
        s //= 2
    odd = (row & 1) == 1
    qa = q * jnp.where(odd, jnp.exp2(gf), 2.0).astype(BF16)
    qb = q * jnp.where(odd, 2.0, jnp.exp2(gb)).astype(BF16)
    zero = jnp.zeros_like(k_even)
    ev = lambda x: jnp.where(odd, zero, x)
    od = lambda x: jnp.where(odd, x, zero)
    levels.append((2, [(qa, ev(k_even), ev(k_odd)), (qb, od(k_even), od(k_odd))]))
    return levels


def _mix_kernel(rq_ref, rk_ref, rv_ref, rg_ref, gq_ref, gk_ref, gv_ref, gg_ref, gf_ref, gb_ref,
                srf_ref, srb_ref, sgf_ref, sgb_ref, dec_ref, rnw_ref, gnw_ref,
                x_ref, m_ref, wout_ref, o_ref, mix_ref, decay_ref, qdec_ref, *, d):
    n = CHUNK
    half = n // 2
    ri = _iota((n, n), 0)
    ci = _iota((n, n), 1)

    @pl.when((pl.program_id(0) == 0) & (pl.program_id(1) == 0))
    def _():
        lg = _log_sigmoid(dec_ref[...])
        dist = (ri - ci).astype(F32)
        row = _iota((n, LANE), 0).astype(F32)
        for hd in range(RET_HEADS):
            sl = slice(hd * LANE, (hd + 1) * LANE)
            lgf = lg[0:1, sl]
            lgb = lg[1:2, sl]
            decay_ref[hd] = jnp.exp(jnp.where(dist > 0, dist * lgf[:, 0:1],
                                              jnp.where(dist < 0, -dist * lgb[:, 0:1], jnp.log(2.0))))
            qdec_ref[hd] = jnp.exp((row + 1.0) * lgf)
            qdec_ref[RET_HEADS + hd] = jnp.exp((n - row) * lgb)

    for hd in range(RET_HEADS):
        sl = slice(hd * LANE, (hd + 1) * LANE)
        q = rq_ref[:, sl]
        p = (_dot_nt(q, rk_ref[:, sl]) * decay_ref[hd]).astype(BF16)
        qf = q.astype(F32)
        qs = jnp.concatenate([(qf * qdec_ref[hd]).astype(BF16),
                              (qf * qdec_ref[RET_HEADS + hd]).astype(BF16)], axis=1)
        st = jnp.concatenate([srf_ref[hd], srb_ref[hd]], axis=0)
        o = _dot(p, rv_ref[:, sl]) + _dot(qs, st)
        mu = jnp.mean(o, axis=-1, keepdims=True)
        oc = o - mu
        var = jnp.mean(oc * oc, axis=-1, keepdims=True)
        r = oc * lax.rsqrt(var + EPS) * rnw_ref[:, sl] * rg_ref[:, sl].astype(F32)
        mix_ref[:, sl] = r.astype(BF16)

    gf = gf_ref[...] * LOG2E
    gb = gb_ref[...] * LOG2E
    bf = _dot_select((ci <= ri).astype(BF16), gf)
    bb = _dot_select((ci >= ri).astype(BF16), gb)
    q = gq_ref[...]
    k = gk_ref[...]
    even_head = (_iota(k.shape, 1) & GLA_DK) == 0
    no_k = jnp.zeros_like(k)
    levels = _gla_levels(q, jnp.where(even_head, k, no_k), jnp.where(even_head, no_k, k), gf, gb, bf, bb)
    qsf = q * jnp.exp2(bf).astype(BF16)
    qsb = q * jnp.exp2(bb).astype(BF16)
    code = _iota((half, n), 0) ^ (_iota((half, n), 1) & (half - 1))
    for pair in range(GLA_HEADS // 2):
        pr = slice(pair * LANE, (pair + 1) * LANE)
        near = [None, None]
        far = [None, None]
        for size, blocks in levels:
            for rb in range(2):
                kb = rb if size < n else 1 - rb
                rows = slice(rb * half, (rb + 1) * half)
                keys = slice(kb * half, (kb + 1) * half)
                u = jnp.concatenate([blk[0][rows, pr] for blk in blocks], axis=1)
                w = jnp.concatenate([jnp.concatenate([blk[j][keys, pr] for blk in blocks], axis=1)
                                     for j in (1, 2)], axis=0)
                t = _dot_nt(u, w)
                if size == n:
                    far[rb] = t
                else:
                    near[rb] = t if near[rb] is None else jnp.where(code < size, t, near[rb])
        qcat = jnp.concatenate([qsf[:, pr], qsb[:, pr]], axis=1)
        for j in range(2):
            hd = 2 * pair + j
            sl = slice(hd * LANE, (hd + 1) * LANE)
            mine = slice(j * half, (j + 1) * half)
            p = jnp.concatenate([jnp.concatenate([near[0][:, mine], far[0][:, mine]], axis=1),
                                 jnp.concatenate([far[1][:, mine], near[1][:, mine]], axis=1)], axis=0)
            st = jnp.concatenate([sgf_ref[hd], sgb_ref[hd]], axis=1)
            o = _dot(p.astype(BF16), gv_ref[:, sl]) + _dot_nt(qcat, st)
            ms = jnp.mean(o * o, axis=-1, keepdims=True)
            r = o * lax.rsqrt(ms + EPS) * gnw_ref[:, sl] * gg_ref[:, sl].astype(F32)
            mix_ref[:, RET_W + hd * LANE:RET_W + (hd + 1) * LANE] = r.astype(BF16)

    y = _dot(mix_ref[...], wout_ref[...])
    o_ref[...] = x_ref[...] + m_ref[:, 5 * d:6 * d] * y


def _mix_call(mix_in, states, dec, rnw, gnw, x1, m3, wout, batch):
    t, d = x1.shape
    nc = t // (batch * CHUNK)
    tile = lambda w: pl.BlockSpec((CHUNK, w), lambda b, c: (b * nc + c, 0))
    st_spec = pl.BlockSpec((None, None, RET_HEADS, LANE, LANE), lambda b, c: (b, c, 0, 0, 0))
    widths = (RET_W,) * 4 + (GLA_QK,) * 2 + (GLA_W,) * 2 + (GLA_QK,) * 2
    return pl.pallas_call(
        functools.partial(_mix_kernel, d=d),
        grid=(batch, nc),
        in_specs=[tile(w) for w in widths] + [st_spec] * 4
        + [pl.BlockSpec(dec.shape, lambda b, c: (0, 0)),
           pl.BlockSpec((1, RET_W), lambda b, c: (0, 0)),
           pl.BlockSpec((1, GLA_W), lambda b, c: (0, 0)),
           tile(d),
           pl.BlockSpec((None, 1, N_MOD * d), lambda b, c: (b, 0, 0)),
           pl.BlockSpec(wout.shape, lambda b, c: (0, 0))],
        out_specs=tile(d),
        out_shape=jax.ShapeDtypeStruct((t, d), F32),
        scratch_shapes=[pltpu.VMEM((CHUNK, RET_W + GLA_W), BF16),
                        pltpu.VMEM((RET_HEADS, CHUNK, CHUNK), F32),
                        pltpu.VMEM((2 * RET_HEADS, CHUNK, LANE), F32)],
        compiler_params=pltpu.CompilerParams(dimension_semantics=("arbitrary", "arbitrary"),
                                             vmem_limit_bytes=VMEM_LIMIT),
        name="mix",
    )(*mix_in, *states, dec, rnw.reshape(1, RET_W), gnw.reshape(1, GLA_W), x1, m3, wout)


def _rope_tables(n_tok):
    freqs = ROPE_BASE ** (-jnp.arange(RET_DK // 4, dtype=F32) / (RET_DK // 4))

    def table(n_pos, first_half):
        ang = jnp.arange(n_pos, dtype=F32)[:, None] * freqs
        zero = jnp.zeros((n_pos, LANE // 2), F32)
        cos = jnp.concatenate([jnp.cos(ang)] * 2, axis=-1)
        sin = jnp.concatenate([-jnp.sin(ang), jnp.sin(ang)], axis=-1)
        halves = (cos, zero, sin, zero) if first_half else (zero, cos, zero, sin)
        return jnp.concatenate(halves, axis=-1)

    return table(n_tok // GRID_W, True), table(GRID_W, False)


def _pack_w_in(w_in):
    d = w_in.shape[0]
    pad = jnp.zeros((d, IN_PACKED - w_in.shape[1]), w_in.dtype)
    return jnp.concatenate([w_in, pad], axis=1).astype(BF16)


def _pack_gate(w_f, b_f, w_b, b_b):
    gw = jnp.zeros((LANE, 2 * GLA_QK), F32)
    gw = gw.at[:GLA_RANK, :GLA_QK].set(w_f).at[GLA_RANK:2 * GLA_RANK, GLA_QK:].set(w_b)
    return gw.astype(BF16), jnp.concatenate([b_f, b_b]).reshape(1, 2 * GLA_QK)


def kernel(x, c, ctx, c_ctx, ada_w, ada_b, norm1_w, ffn1_w1, ffn1_w3, ffn1_w2, norm2_w, w_in,
           ret_decay_f, ret_decay_b, ret_norm_w, gla_gate_w_f, gla_gate_b_f, gla_gate_w_b, gla_gate_b_b,
           gla_norm_w, w_out, norm3_w, ffn2_w1, ffn2_w3, ffn2_w2, final_norm_w):
    batch, n_tok, d = x.shape
    n_ctx = ctx.shape[1]
    depth = ada_w.shape[0]
    assert depth == 1 and batch + 1 <= 8
    assert n_tok % FFN_TILE == 0 and n_tok % CHUNK == 0 and n_ctx % CHUNK == 0

    cvec = jnp.zeros((8, d), F32).at[:batch].set(c).at[batch].set(c_ctx)
    m3 = _modulation(cvec, ada_w[0], ada_b[0]).reshape(8, 1, N_MOD * d)

    bf = lambda w: w.astype(BF16)
    w_in_p = _pack_w_in(w_in[0])
    gw, gbias = _pack_gate(gla_gate_w_f[0], gla_gate_b_f[0], gla_gate_w_b[0], gla_gate_b_b[0])
    proj = (norm2_w[0], w_in_p, gw, gbias)
    f1 = (norm1_w[0], bf(ffn1_w1[0]), bf(ffn1_w3[0]), bf(ffn1_w2[0]))
    rowtab, coltab = _rope_tables(n_tok)
    dec = jnp.zeros((8, RET_W), F32)
    dec = dec.at[0].set(jnp.repeat(ret_decay_f[0], LANE)).at[1].set(jnp.repeat(ret_decay_b[0], LANE))

    tm_c = min(FFN_TILE, batch * n_ctx)
    ctx_out = _ffn_call(ctx.reshape(batch * n_ctx, d), m3, lambda i: batch, *f1, mod_off=0, tm=tm_c,
                        proj=proj, name="ffn_ctx")
    zero = jnp.zeros((batch, RET_HEADS, LANE, LANE), F32)
    ctx_states = _states(ctx_out[2], ctx_out[3], ctx_out[6], ctx_out[7], ctx_out[9], ctx_out[10],
                         dec, (zero,) * 4, batch)[4:]

    tiles_per_seq = n_tok // FFN_TILE
    lat = _ffn_call(x.reshape(batch * n_tok, d), m3, lambda i: i // tiles_per_seq, *f1, mod_off=0,
                    tm=FFN_TILE, proj=proj, rope=(rowtab, coltab, tiles_per_seq), name="ffn_in")
    x1, mix_in = lat[0], lat[1:]
    states = _states(mix_in[1], mix_in[2], mix_in[5], mix_in[6], mix_in[8], mix_in[9],
                     dec, ctx_states, batch)[:4]
    x2 = _mix_call(mix_in, states, dec, ret_norm_w[0], gla_norm_w[0], x1, m3, bf(w_out[0]), batch)
    out = _ffn_call(x2, m3, lambda i: i // tiles_per_seq, norm3_w[0], bf(ffn2_w1[0]), bf(ffn2_w3[0]),
                    bf(ffn2_w2[0]), mod_off=6, tm=FFN_TILE, final_w=final_norm_w, name="ffn_out")[0]
    return out.reshape(batch, n_tok, d)
```

```python
import functools

import jax
import jax.numpy as jnp
from jax import lax
from jax.experimental import pallas as pl
from jax.experimental.pallas import tpu as pltpu

F32 = jnp.float32
BF16 = jnp.bfloat16

EPS = 1e-6
LOG2E = 1.4426950408889634
N_MOD = 9
GRID_W = 64
ROPE_BASE = 10000.0
RET_HEADS = 4
RET_DK = 128
RET_DV = 128
GLA_HEADS = 4
GLA_DK = 64
GLA_DV = 128
GLA_RANK = 16
GLA_TAU = 16.0
RET_W = RET_HEADS * RET_DV
GLA_W = GLA_HEADS * GLA_DV
GLA_QK = GLA_HEADS * GLA_DK

LANE = 128
CHUNK = 256
FFN_TILE = 512
FF_CHUNK = 256
VMEM_LIMIT = 60 * 1024 * 1024

C_RQ, C_RK, C_RV, C_RG = 0, 512, 1024, 1536
C_GQ, C_GK, C_GV, C_GG = 2048, 2304, 2560, 3072
C_LOW = 3584


def _silu(x):
    return x * (1.0 / (1.0 + jnp.exp(-x)))


def _log_sigmoid(z):
    return jnp.minimum(z, 0.0) - jnp.log(1.0 + jnp.exp(-jnp.abs(z)))


def _rms(x, w):
    return x * lax.rsqrt(jnp.mean(x * x, axis=-1, keepdims=True) + EPS) * w


def _dot(a, b):
    return jnp.dot(a, b, preferred_element_type=F32)


def _dot_nt(a, b):
    return lax.dot_general(a, b, (((1,), (1,)), ((), ())), preferred_element_type=F32)


def _dot_tn(a, b):
    return lax.dot_general(a, b, (((0,), (0,)), ((), ())), preferred_element_type=F32)


def _dot_select(sel, x):
    hi = x.astype(BF16)
    lo = (x - hi.astype(F32)).astype(BF16)
    return _dot(sel, hi) + _dot(sel, lo)


def _iota(shape, dim):
    return lax.broadcasted_iota(jnp.int32, shape, dim)


def _resident(shape):
    nd = len(shape)
    return pl.BlockSpec(shape, lambda *_: (0,) * nd, pipeline_mode=pl.Buffered(1))


def _mod_kernel(c_ref, w_ref, b_ref, o_ref):
    cond = _silu(c_ref[...]).astype(BF16)
    o_ref[...] = _dot(cond, w_ref[...].astype(BF16)) + b_ref[...]


def _modulation(cvec, ada_w, ada_b):
    d, n = ada_w.shape
    bn = n // 4 if n % (4 * LANE) == 0 else d
    return pl.pallas_call(
        _mod_kernel,
        grid=(n // bn,),
        in_specs=[pl.BlockSpec((8, d), lambda j: (0, 0)),
                  pl.BlockSpec((d, bn), lambda j: (0, j)),
                  pl.BlockSpec((1, bn), lambda j: (0, j))],
        out_specs=pl.BlockSpec((8, bn), lambda j: (0, j)),
        out_shape=jax.ShapeDtypeStruct((8, n), F32),
        compiler_params=pltpu.CompilerParams(dimension_semantics=("arbitrary",),
                                             vmem_limit_bytes=VMEM_LIMIT),
        name="mod",
    )(cvec, ada_w, ada_b.reshape(1, n))


def _swap32(x):
    lane = _iota(x.shape, 1)
    return jnp.where((lane & 63) < 32, pltpu.roll(x, 96, 1), pltpu.roll(x, 32, 1))


def _ffn_kernel(*refs, mod_off, proj, rope, final, n_main, d, f):
    it = iter(refs)
    x_ref = next(it)
    if n_main is not None:
        xc_ref = next(it)
        is_ctx = pl.program_id(0) >= n_main
    m_ref, nw_ref, w1_ref, w3_ref, w2_ref = (next(it) for _ in range(5))
    if proj:
        n2w_ref, win_ref, wlow_ref, gw_ref, gbias_ref = (next(it) for _ in range(5))
        if rope:
            rowtab_ref, coltab_ref = next(it), next(it)
    if final:
        fnw_ref = next(it)
    xo_ref = next(it)
    if proj:
        (rq_ref, rk_ref, rv_ref, rg_ref, gq_ref, gk_ref, gv_ref, gg_ref,
         gf_ref, gb_ref) = (next(it) for _ in range(10))
    u_ref = next(it)

    def mod(i):
        return m_ref[:, (mod_off + i) * d:(mod_off + i + 1) * d]

    x = x_ref[...]
    if n_main is not None:
        x = jnp.where(is_ctx, xc_ref[...], x)
    h = (_rms(x, nw_ref[...]) * (1.0 + mod(1)) + mod(0)).astype(BF16)
    for k in range(f // FF_CHUNK):
        sl = slice(k * FF_CHUNK, (k + 1) * FF_CHUNK)
        a = _dot(h, w1_ref[:, sl])
        g = _dot(h, w3_ref[:, sl])
        u_ref[:, sl] = (_silu(a) * g).astype(BF16)
    y = _dot(u_ref[...], w2_ref[...])
    x1 = x + (0.5 * mod(2)) * y

    if final:
        xo_ref[...] = _rms(x1, fnw_ref[...])
    else:
        xo_ref[...] = x1

    if proj:
        h2 = (_rms(x1, n2w_ref[...]) * (1.0 + mod(4)) + mod(3)).astype(BF16)

        def p(lo, hi):
            return _dot(h2, win_ref[:, lo:hi])

        if rope:
            by_row = _iota((GRID_W, 2 * LANE), 1) % LANE < LANE // 2
            tab = jnp.concatenate(
                [jnp.where(by_row, jnp.broadcast_to(rowtab_ref[g:g + 1, :], (GRID_W, 2 * LANE)), coltab_ref[...])
                 for g in range(x.shape[0] // GRID_W)], axis=0)
            cos, sin = tab[:, :LANE], tab[:, LANE:]
            if n_main is not None:
                cos = jnp.where(is_ctx, 1.0, cos)
                sin = jnp.where(is_ctx, 0.0, sin)

        for base, scale, o_ref in ((C_RQ, RET_DK ** -0.5, rq_ref), (C_RK, 1.0, rk_ref)):
            t = p(base, base + RET_W)
            for hd in range(RET_HEADS):
                th = t[:, hd * LANE:(hd + 1) * LANE] * scale
                if rope:
                    th = th * cos + _swap32(th) * sin
                o_ref[:, hd * LANE:(hd + 1) * LANE] = th.astype(BF16)
        rv_ref[...] = p(C_RV, C_RV + RET_W).astype(BF16)
        rg_ref[...] = _silu(p(C_RG, C_RG + RET_W)).astype(BF16)
        gq_ref[...] = (p(C_GQ, C_GQ + GLA_QK) * GLA_DK ** -0.5).astype(BF16)
        gk_ref[...] = p(C_GK, C_GK + GLA_QK).astype(BF16)
        gv_ref[...] = p(C_GV, C_GV + GLA_W).astype(BF16)
        gg_ref[...] = _silu(p(C_GG, C_GG + GLA_W)).astype(BF16)
        low = _dot(h2, wlow_ref[...]).astype(BF16)
        z = _dot(low, gw_ref[...]) + gbias_ref[...]
        ls = _log_sigmoid(z) * (1.0 / GLA_TAU)
        gf_ref[...] = ls[:, :GLA_QK]
        gb_ref[...] = ls[:, GLA_QK:]


def _ffn_call(x, m3, tiles_per_seq, nw, w1, w3, w2, *, mod_off, tm, ctx=None, ctx_row=None, proj=None,
              rope=None, final_w=None, name):
    t, d = x.shape
    f = w1.shape[1]
    n_main = t // tm
    n_tiles = n_main + (0 if ctx is None else ctx.shape[0] // tm)
    t = n_tiles * tm
    main = lambda i: jnp.minimum(i, n_main - 1)
    tile = lambda w: pl.BlockSpec((tm, w), lambda i: (i, 0))
    in_specs = [pl.BlockSpec((tm, d), lambda i: (main(i), 0))]
    args = [x]
    if ctx is None:
        row_of_tile = lambda i: i // tiles_per_seq
    else:
        row_of_tile = lambda i: jnp.where(i >= n_main, ctx_row, i // tiles_per_seq)
        in_specs.append(pl.BlockSpec((tm, d), lambda i: (jnp.maximum(i - n_main, 0), 0)))
        args.append(ctx)
    in_specs += [pl.BlockSpec((None, 1, N_MOD * d), lambda i: (row_of_tile(i), 0, 0)),
                 _resident((1, d)), _resident((d, f)), _resident((d, f)), _resident((f, d))]
    args += [m3, nw.reshape(1, d), w1, w3, w2]
    if proj is not None:
        n2w, win, wlow, gw, gbias = proj
        in_specs += [_resident((1, d))] + [_resident(a.shape) for a in (win, wlow, gw, gbias)]
        args += [n2w.reshape(1, d), win, wlow, gw, gbias]
        if rope is not None:
            rowtab, coltab = rope
            in_specs += [pl.BlockSpec((tm // GRID_W, 2 * LANE), lambda i: (main(i) % tiles_per_seq, 0)),
                         _resident(coltab.shape)]
            args += [rowtab, coltab]
    if final_w is not None:
        in_specs.append(_resident((1, d)))
        args.append(final_w.reshape(1, d))
    out_specs = [tile(d)]
    out_shape = [jax.ShapeDtypeStruct((t, d), F32)]
    if proj is not None:
        for w, dt in ((RET_W, BF16),) * 4 + ((GLA_QK, BF16),) * 2 + ((GLA_W, BF16),) * 2 + ((GLA_QK, F32),) * 2:
            out_specs.append(tile(w))
            out_shape.append(jax.ShapeDtypeStruct((t, w), dt))
    kern = functools.partial(_ffn_kernel, mod_off=mod_off, proj=proj is not None,
                             rope=rope is not None, final=final_w is not None,
                             n_main=None if ctx is None else n_main, d=d, f=f)
    return pl.pallas_call(
        kern, grid=(n_tiles,), in_specs=in_specs, out_specs=out_specs, out_shape=out_shape,
        scratch_shapes=[pltpu.VMEM((tm, f), BF16)],
        compiler_params=pltpu.CompilerParams(dimension_semantics=("arbitrary",),
                                             vmem_limit_bytes=VMEM_LIMIT),
        name=name,
    )(*args)


def _head_lanes(shape, hd):
    lane = _iota(shape, len(shape) - 1)
    return (lane >= 64) if hd % 2 else (lane < 64)


def _state_kernel(*refs, batch):
    it = iter(refs)
    per_batch = lambda: [next(it) for _ in range(batch)]
    rkf_ref, rvf_ref, gkf_ref, gvf_ref, gf_ref = (per_batch() for _ in range(5))
    rkb_ref, rvb_ref, gkb_ref, gvb_ref, gb_ref = (per_batch() for _ in range(5))
    dec_ref = next(it)
    irf_ref, irb_ref, igf_ref, igb_ref = (next(it) for _ in range(4))
    orf_ref, orb_ref, ogf_ref, ogb_ref = (next(it) for _ in range(4))
    frf_ref, frb_ref, fgf_ref, fgb_ref = (next(it) for _ in range(4))
    srf, srb, sgf, sgb = (next(it) for _ in range(4))
    n = CHUNK

    @pl.when(pl.program_id(0) == 0)
    def _():
        srf[...] = irf_ref[...]
        srb[...] = irb_ref[...]
        sgf[...] = igf_ref[...]
        sgb[...] = igb_ref[...]

    orf_ref[...] = srf[...].astype(BF16)
    orb_ref[...] = srb[...].astype(BF16)
    ogf_ref[...] = sgf[...].astype(BF16)
    ogb_ref[...] = sgb[...].astype(BF16)

    row = _iota((n, LANE), 0).astype(F32)
    lg = _log_sigmoid(dec_ref[...])
    ri = _iota((n, n), 0)
    ci = _iota((n, n), 1)
    after = (ci > ri).astype(BF16)
    before = (ci < ri).astype(BF16)

    def gla(b, g_ref, k_ref, v_ref, tri, edge, st):
        g = g_ref[b][...]
        e = _dot_select(tri, g)
        tot = e[edge:edge + 1, :] + g[edge:edge + 1, :]
        kd = (k_ref[b][...].astype(F32) * jnp.exp(e)).astype(BF16)
        for hd in range(GLA_HEADS):
            pr = slice((hd // 2) * LANE, (hd // 2 + 1) * LANE)
            upd = _dot_tn(v_ref[b][:, hd * LANE:(hd + 1) * LANE], kd[:, pr])
            upd = jnp.where(_head_lanes(upd.shape, hd), upd, 0.0)
            st[b, hd] = st[b, hd] * jnp.exp(tot[:, pr]) + upd

    for b in range(batch):
        for hd in range(RET_HEADS):
            sl = slice(hd * LANE, (hd + 1) * LANE)
            lgf = lg[0:1, sl]
            lgb = lg[1:2, sl]
            kf = (rkf_ref[b][:, sl].astype(F32) * jnp.exp((n - 1.0 - row) * lgf)).astype(BF16)
            srf[b, hd] = srf[b, hd] * jnp.exp(n * lgf) + _dot_tn(kf, rvf_ref[b][:, sl])
            kb = (rkb_ref[b][:, sl].astype(F32) * jnp.exp(row * lgb)).astype(BF16)
            srb[b, hd] = srb[b, hd] * jnp.exp(n * lgb) + _dot_tn(kb, rvb_ref[b][:, sl])
        gla(b, gf_ref, gkf_ref, gvf_ref, after, 0, sgf)
        gla(b, gb_ref, gkb_ref, gvb_ref, before, n - 1, sgb)

    frf_ref[...] = srf[...]
    frb_ref[...] = srb[...]
    fgf_ref[...] = sgf[...]
    fgb_ref[...] = sgb[...]


def _states(rk, rv, gk, gv, gf, gb, dec, init, batch, first_chunk, nc):
    def per_batch(w, backward):
        at = (lambda c: nc - 1 - c) if backward else (lambda c: c)
        return [pl.BlockSpec((CHUNK, w), lambda c, b=b: (first_chunk + b * nc + at(c), 0)) for b in range(batch)]

    fwd = lambda w: per_batch(w, False)
    bwd = lambda w: per_batch(w, True)
    st_shape = (batch, RET_HEADS, LANE, LANE)
    init_spec = pl.BlockSpec(st_shape, lambda c: (0, 0, 0, 0))
    chunk_shape = (batch, None, RET_HEADS, LANE, LANE)
    of_spec = pl.BlockSpec(chunk_shape, lambda c: (0, c, 0, 0, 0))
    ob_spec = pl.BlockSpec(chunk_shape, lambda c: (0, nc - 1 - c, 0, 0, 0))
    per_chunk = jax.ShapeDtypeStruct((batch, nc, RET_HEADS, LANE, LANE), BF16)
    final = jax.ShapeDtypeStruct(st_shape, F32)
    rep = lambda a: [a] * batch
    return pl.pallas_call(
        functools.partial(_state_kernel, batch=batch),
        grid=(nc,),
        in_specs=fwd(RET_W) + fwd(RET_W) + fwd(GLA_QK) + fwd(GLA_W) + fwd(GLA_QK)
        + bwd(RET_W) + bwd(RET_W) + bwd(GLA_QK) + bwd(GLA_W) + bwd(GLA_QK)
        + [pl.BlockSpec(dec.shape, lambda c: (0, 0))] + [init_spec] * 4,
        out_specs=[of_spec, ob_spec, of_spec, ob_spec] + [init_spec] * 4,
        out_shape=[per_chunk] * 4 + [final] * 4,
        scratch_shapes=[pltpu.VMEM(st_shape, F32)] * 4,
        compiler_params=pltpu.CompilerParams(dimension_semantics=("arbitrary",),
                                             vmem_limit_bytes=VMEM_LIMIT),
        name="states",
    )(*rep(rk), *rep(rv), *rep(gk), *rep(gv), *rep(gf), *rep(rk), *rep(rv), *rep(gk), *rep(gv), *rep(gb),
      dec, *init)


def _block_row(x, parent, r):
    n, w = x.shape
    if parent == n:
        return jnp.broadcast_to(x[r:r + 1, :], (n, w))
    x3 = x.reshape(n // parent, parent, w)
    return jnp.broadcast_to(x3[:, r:r + 1, :], x3.shape).reshape(n, w)


def _gla_levels(q, k_even, k_odd, gf, gb, bf, bb):
    n = q.shape[0]
    row = _iota(q.shape, 0)

    def factors(u_exp, w_exp):
        ew = jnp.exp2(w_exp).astype(BF16)
        return (q * jnp.exp2(u_exp).astype(BF16), k_even * ew, k_odd * ew)

    levels = []
    s = n // 2
    while s >= 2:
        if s >= 4:
            last_of_first = _block_row(bf, 2 * s, s - 1)
            first_of_second = _block_row(bb, 2 * s, s)
        else:
            upper = (row & 4) != 0
            last_of_first = jnp.where(upper, _block_row(bf, 8, 5), _block_row(bf, 8, 1))
            first_of_second = jnp.where(upper, _block_row(bb, 8, 6), _block_row(bb, 8, 2))
        df = bf - last_of_first
        db = bb - first_of_second
        levels.append((2 * s, [factors(jnp.minimum(df, db), -jnp.maximum(df, db))]))
        s //= 2
    odd = (row & 1) == 1
    qa = q * jnp.where(odd, jnp.exp2(gf), 2.0).astype(BF16)
    qb = q * jnp.where(odd, 2.0, jnp.exp2(gb)).astype(BF16)
    zero = jnp.zeros_like(k_even)
    ev = lambda x: jnp.where(odd, zero, x)
    od = lambda x: jnp.where(odd, x, zero)
    levels.append((2, [(qa, ev(k_even), ev(k_odd)), (qb, od(k_even), od(k_odd))]))
    return levels


def _mix_kernel(rq_ref, rk_ref, rv_ref, rg_ref, gq_ref, gk_ref, gv_ref, gg_ref, gf_ref, gb_ref,
                srf_ref, srb_ref, sgf_ref, sgb_ref, dec_ref, rnw_ref, gnw_ref,
                x_ref, m_ref, wout_ref, o_ref, mix_ref, decay_ref, qdec_ref, *, d):
    n = CHUNK
    half = n // 2
    ri = _iota((n, n), 0)
    ci = _iota((n, n), 1)

    @pl.when((pl.program_id(0) == 0) & (pl.program_id(1) == 0))
    def _():
        lg = _log_sigmoid(dec_ref[...])
        dist = (ri - ci).astype(F32)
        row = _iota((n, LANE), 0).astype(F32)
        for hd in range(RET_HEADS):
            sl = slice(hd * LANE, (hd + 1) * LANE)
            lgf = lg[0:1, sl]
            lgb = lg[1:2, sl]
            decay_ref[hd] = jnp.exp(jnp.where(dist > 0, dist * lgf[:, 0:1],
                                              jnp.where(dist < 0, -dist * lgb[:, 0:1], jnp.log(2.0))))
            qdec_ref[hd] = jnp.exp((row + 1.0) * lgf)
            qdec_ref[RET_HEADS + hd] = jnp.exp((n - row) * lgb)

    for hd in range(RET_HEADS):
        sl = slice(hd * LANE, (hd + 1) * LANE)
        q = rq_ref[:, sl]
        p = (_dot_nt(q, rk_ref[:, sl]) * decay_ref[hd]).astype(BF16)
        qf = q.astype(F32)
        qs = jnp.concatenate([(qf * qdec_ref[hd]).astype(BF16),
                              (qf * qdec_ref[RET_HEADS + hd]).astype(BF16)], axis=1)
        st = jnp.concatenate([srf_ref[hd], srb_ref[hd]], axis=0)
        o = _dot(p, rv_ref[:, sl]) + _dot(qs, st)
        mu = jnp.mean(o, axis=-1, keepdims=True)
        oc = o - mu
        var = jnp.mean(oc * oc, axis=-1, keepdims=True)
        r = oc * lax.rsqrt(var + EPS) * rnw_ref[:, sl] * rg_ref[:, sl].astype(F32)
        mix_ref[:, sl] = r.astype(BF16)

    gf = gf_ref[...] * LOG2E
    gb = gb_ref[...] * LOG2E
    bf = _dot_select((ci <= ri).astype(BF16), gf)
    bb = _dot_select((ci >= ri).astype(BF16), gb)
    q = gq_ref[...]
    k = gk_ref[...]
    even_head = (_iota(k.shape, 1) & GLA_DK) == 0
    no_k = jnp.zeros_like(k)
    levels = _gla_levels(q, jnp.where(even_head, k, no_k), jnp.where(even_head, no_k, k), gf, gb, bf, bb)
    qsf = q * jnp.exp2(bf).astype(BF16)
    qsb = q * jnp.exp2(bb).astype(BF16)
    code = _iota((half, n), 0) ^ (_iota((half, n), 1) & (half - 1))
    for pair in range(GLA_HEADS // 2):
        pr = slice(pair * LANE, (pair + 1) * LANE)
        near = [None, None]
        far = [None, None]
        for size, blocks in levels:
            for rb in range(2):
                kb = rb if size < n else 1 - rb
                rows = slice(rb * half, (rb + 1) * half)
                keys = slice(kb * half, (kb + 1) * half)
                u = jnp.concatenate([blk[0][rows, pr] for blk in blocks], axis=1)
                w = jnp.concatenate([jnp.concatenate([blk[j][keys, pr] for blk in blocks], axis=1)
                                     for j in (1, 2)], axis=0)
                t = _dot_nt(u, w)
                if size == n:
                    far[rb] = t
                else:
                    near[rb] = t if near[rb] is None else jnp.where(code < size, t, near[rb])
        qcat = jnp.concatenate([qsf[:, pr], qsb[:, pr]], axis=1)
        for j in range(2):
            hd = 2 * pair + j
            sl = slice(hd * LANE, (hd + 1) * LANE)
            mine = slice(j * half, (j + 1) * half)
            p = jnp.concatenate([jnp.concatenate([near[0][:, mine], far[0][:, mine]], axis=1),
                                 jnp.concatenate([far[1][:, mine], near[1][:, mine]], axis=1)], axis=0)
            st = jnp.concatenate([sgf_ref[hd], sgb_ref[hd]], axis=1)
            o = _dot(p.astype(BF16), gv_ref[:, sl]) + _dot_nt(qcat, st)
            ms = jnp.mean(o * o, axis=-1, keepdims=True)
            r = o * lax.rsqrt(ms + EPS) * gnw_ref[:, sl] * gg_ref[:, sl].astype(F32)
            mix_ref[:, RET_W + hd * LANE:RET_W + (hd + 1) * LANE] = r.astype(BF16)

    y = _dot(mix_ref[...], wout_ref[...])
    o_ref[...] = x_ref[...] + m_ref[:, 5 * d:6 * d] * y


def _mix_call(mix_in, states, dec, rnw, gnw, x1, m3, wout, batch, nc):
    t, d = batch * nc * CHUNK, x1.shape[1]
    tile = lambda w: pl.BlockSpec((CHUNK, w), lambda b, c: (b * nc + c, 0))
    st_spec = pl.BlockSpec((None, None, RET_HEADS, LANE, LANE), lambda b, c: (b, c, 0, 0, 0))
    widths = (RET_W,) * 4 + (GLA_QK,) * 2 + (GLA_W,) * 2 + (GLA_QK,) * 2
    return pl.pallas_call(
        functools.partial(_mix_kernel, d=d),
        grid=(batch, nc),
        in_specs=[tile(w) for w in widths] + [st_spec] * 4
        + [pl.BlockSpec(dec.shape, lambda b, c: (0, 0)),
           pl.BlockSpec((1, RET_W), lambda b, c: (0, 0)),
           pl.BlockSpec((1, GLA_W), lambda b, c: (0, 0)),
           tile(d),
           pl.BlockSpec((None, 1, N_MOD * d), lambda b, c: (b, 0, 0)),
           pl.BlockSpec(wout.shape, lambda b, c: (0, 0))],
        out_specs=tile(d),
        out_shape=jax.ShapeDtypeStruct((t, d), F32),
        scratch_shapes=[pltpu.VMEM((CHUNK, RET_W + GLA_W), BF16),
                        pltpu.VMEM((RET_HEADS, CHUNK, CHUNK), F32),
                        pltpu.VMEM((2 * RET_HEADS, CHUNK, LANE), F32)],
        compiler_params=pltpu.CompilerParams(dimension_semantics=("arbitrary", "arbitrary"),
                                             vmem_limit_bytes=VMEM_LIMIT),
        name="mix",
    )(*mix_in, *states, dec, rnw.reshape(1, RET_W), gnw.reshape(1, GLA_W), x1, m3, wout)


def _rope_tables(n_tok):
    freqs = ROPE_BASE ** (-jnp.arange(RET_DK // 4, dtype=F32) / (RET_DK // 4))

    def table(n_pos, first_half):
        ang = jnp.arange(n_pos, dtype=F32)[:, None] * freqs
        zero = jnp.zeros((n_pos, LANE // 2), F32)
        cos = jnp.concatenate([jnp.cos(ang)] * 2, axis=-1)
        sin = jnp.concatenate([-jnp.sin(ang), jnp.sin(ang)], axis=-1)
        halves = (cos, zero, sin, zero) if first_half else (zero, cos, zero, sin)
        return jnp.concatenate(halves, axis=-1)

    return table(n_tok // GRID_W, True), table(GRID_W, False)


def _pack_w_in(w_in):
    low = w_in[:, C_LOW:]
    pad = jnp.zeros((w_in.shape[0], LANE - low.shape[1]), w_in.dtype)
    return w_in[:, :C_LOW].astype(BF16), jnp.concatenate([low, pad], axis=1).astype(BF16)


def _pack_gate(w_f, b_f, w_b, b_b):
    gw = jnp.zeros((LANE, 2 * GLA_QK), F32)
    gw = gw.at[:GLA_RANK, :GLA_QK].set(w_f).at[GLA_RANK:2 * GLA_RANK, GLA_QK:].set(w_b)
    return gw.astype(BF16), jnp.concatenate([b_f, b_b]).reshape(1, 2 * GLA_QK)


def kernel(x, c, ctx, c_ctx, ada_w, ada_b, norm1_w, ffn1_w1, ffn1_w3, ffn1_w2, norm2_w, w_in,
           ret_decay_f, ret_decay_b, ret_norm_w, gla_gate_w_f, gla_gate_b_f, gla_gate_w_b, gla_gate_b_b,
           gla_norm_w, w_out, norm3_w, ffn2_w1, ffn2_w3, ffn2_w2, final_norm_w):
    batch, n_tok, d = x.shape
    n_ctx = ctx.shape[1]
    depth = ada_w.shape[0]
    assert depth == 1 and batch + 1 <= 8
    assert n_tok % FFN_TILE == 0 and n_tok % CHUNK == 0 and n_ctx % CHUNK == 0

    cvec = jnp.zeros((8, d), F32).at[:batch].set(c).at[batch].set(c_ctx)
    m3 = _modulation(cvec, ada_w[0], ada_b[0]).reshape(8, 1, N_MOD * d)

    bf = lambda w: w.astype(BF16)
    gw, gbias = _pack_gate(gla_gate_w_f[0], gla_gate_b_f[0], gla_gate_w_b[0], gla_gate_b_b[0])
    proj = (norm2_w[0], *_pack_w_in(w_in[0]), gw, gbias)
    f1 = (norm1_w[0], bf(ffn1_w1[0]), bf(ffn1_w3[0]), bf(ffn1_w2[0]))
    rowtab, coltab = _rope_tables(n_tok)
    dec = jnp.zeros((8, RET_W), F32)
    dec = dec.at[0].set(jnp.repeat(ret_decay_f[0], LANE)).at[1].set(jnp.repeat(ret_decay_b[0], LANE))

    tiles_per_seq = n_tok // FFN_TILE
    first = _ffn_call(x.reshape(batch * n_tok, d), m3, tiles_per_seq, *f1, mod_off=0, tm=FFN_TILE,
                      ctx=ctx.reshape(batch * n_ctx, d), ctx_row=batch, proj=proj, rope=(rowtab, coltab),
                      name="ffn_in")
    x1, mix_in = first[0], first[1:]
    scan_in = (mix_in[1], mix_in[2], mix_in[5], mix_in[6], mix_in[8], mix_in[9])
    zero = jnp.zeros((batch, RET_HEADS, LANE, LANE), F32)
    nc, nc_ctx = n_tok // CHUNK, n_ctx // CHUNK
    ctx_states = _states(*scan_in, dec, (zero,) * 4, batch, batch * nc, nc_ctx)[4:]
    states = _states(*scan_in, dec, ctx_states, batch, 0, nc)[:4]
    x2 = _mix_call(mix_in, states, dec, ret_norm_w[0], gla_norm_w[0], x1, m3, bf(w_out[0]), batch, nc)
    out = _ffn_call(x2, m3, tiles_per_seq, norm3_w[0], bf(ffn2_w1[0]), bf(ffn2_w3[0]), bf(ffn2_w2[0]),
                    mod_off=6, tm=FFN_TILE, final_w=final_norm_w, name="ffn_out")[0]
    return out.reshape(batch, n_tok, d)
```

```python
import functools

import jax
import jax.numpy as jnp
from jax import lax
from jax.experimental import pallas as pl
from jax.experimental.pallas import tpu as pltpu

F32 = jnp.float32
BF16 = jnp.bfloat16

EPS = 1e-6
LOG2E = 1.4426950408889634
N_MOD = 9
GRID_W = 64
ROPE_BASE = 10000.0
RET_HEADS = 4
RET_DK = 128
RET_DV = 128
GLA_HEADS = 4
GLA_DK = 64
GLA_DV = 128
GLA_RANK = 16
GLA_TAU = 16.0
RET_W = RET_HEADS * RET_DV
GLA_W = GLA_HEADS * GLA_DV
GLA_QK = GLA_HEADS * GLA_DK

LANE = 128
CHUNK = 256
FFN_TILE = 512
ROW_BLOCK = 256
FF_CHUNK = 256
VMEM_LIMIT = 60 * 1024 * 1024

C_RQ, C_RK, C_RV, C_RG = 0, 512, 1024, 1536
C_GQ, C_GK, C_GV, C_GG = 2048, 2304, 2560, 3072
C_LOW = 3584


def _silu(x):
    return x * (1.0 / (1.0 + jnp.exp(-x)))


def _log_sigmoid(z):
    return jnp.minimum(z, 0.0) - jnp.log(1.0 + jnp.exp(-jnp.abs(z)))


def _rms(x, w):
    return x * lax.rsqrt(jnp.mean(x * x, axis=-1, keepdims=True) + EPS) * w


def _dot(a, b):
    return jnp.dot(a, b, preferred_element_type=F32)


def _dot_nt(a, b):
    return lax.dot_general(a, b, (((1,), (1,)), ((), ())), preferred_element_type=F32)


def _dot_tn(a, b):
    return lax.dot_general(a, b, (((0,), (0,)), ((), ())), preferred_element_type=F32)


def _dot_select(sel, x):
    hi = x.astype(BF16)
    lo = (x - hi.astype(F32)).astype(BF16)
    return _dot(sel, hi) + _dot(sel, lo)


def _iota(shape, dim):
    return lax.broadcasted_iota(jnp.int32, shape, dim)


def _resident(shape):
    nd = len(shape)
    return pl.BlockSpec(shape, lambda *_: (0,) * nd, pipeline_mode=pl.Buffered(1))


def _mod_kernel(c_ref, w_ref, b_ref, o_ref):
    cond = _silu(c_ref[...]).astype(BF16)
    o_ref[...] = _dot(cond, w_ref[...].astype(BF16)) + b_ref[...]


def _modulation(cvec, ada_w, ada_b):
    d, n = ada_w.shape
    bn = n // 4 if n % (4 * LANE) == 0 else d
    return pl.pallas_call(
        _mod_kernel,
        grid=(n // bn,),
        in_specs=[pl.BlockSpec((8, d), lambda j: (0, 0)),
                  pl.BlockSpec((d, bn), lambda j: (0, j)),
                  pl.BlockSpec((1, bn), lambda j: (0, j))],
        out_specs=pl.BlockSpec((8, bn), lambda j: (0, j)),
        out_shape=jax.ShapeDtypeStruct((8, n), F32),
        compiler_params=pltpu.CompilerParams(dimension_semantics=("arbitrary",),
                                             vmem_limit_bytes=VMEM_LIMIT),
        name="mod",
    )(cvec, ada_w, ada_b.reshape(1, n))


def _swap32(x):
    lane = _iota(x.shape, 1)
    return jnp.where((lane & 63) < 32, pltpu.roll(x, 96, 1), pltpu.roll(x, 32, 1))


def _ffn_kernel(*refs, mod_off, proj, rope, final, n_main, d, f):
    it = iter(refs)
    x_ref = next(it)
    if n_main is not None:
        xc_ref = next(it)
        is_ctx = pl.program_id(0) >= n_main
    m_ref, nw_ref, w1_ref, w3_ref, w2_ref = (next(it) for _ in range(5))
    if proj:
        n2w_ref, win_ref, wlow_ref, gw_ref, gbias_ref = (next(it) for _ in range(5))
        if rope:
            rowtab_ref, coltab_ref = next(it), next(it)
    if final:
        fnw_ref = next(it)
    xo_ref = next(it)
    if proj:
        (rq_ref, rk_ref, rv_ref, rg_ref, gq_ref, gk_ref, gv_ref, gg_ref,
         gf_ref, gb_ref) = (next(it) for _ in range(10))
    u_ref = next(it)

    def mod(i):
        return m_ref[:, (mod_off + i) * d:(mod_off + i + 1) * d]

    for rb in range(x_ref.shape[0] // ROW_BLOCK):
        rows = slice(rb * ROW_BLOCK, (rb + 1) * ROW_BLOCK)
        x = x_ref[rows, :]
        if n_main is not None:
            x = jnp.where(is_ctx, xc_ref[rows, :], x)
        h = (_rms(x, nw_ref[...]) * (1.0 + mod(1)) + mod(0)).astype(BF16)
        for k in range(f // FF_CHUNK):
            sl = slice(k * FF_CHUNK, (k + 1) * FF_CHUNK)
            a = _dot(h, w1_ref[:, sl])
            g = _dot(h, w3_ref[:, sl])
            u_ref[rows, sl] = (_silu(a) * g).astype(BF16)
        y = _dot(u_ref[rows, :], w2_ref[...])
        x1 = x + (0.5 * mod(2)) * y

        if final:
            xo_ref[rows, :] = _rms(x1, fnw_ref[...])
        else:
            xo_ref[rows, :] = x1

        if proj:
            h2 = (_rms(x1, n2w_ref[...]) * (1.0 + mod(4)) + mod(3)).astype(BF16)

            def p(lo, hi):
                return _dot(h2, win_ref[:, lo:hi])

            if rope:
                by_row = _iota((GRID_W, 2 * LANE), 1) % LANE < LANE // 2
                g0 = rb * (ROW_BLOCK // GRID_W)
                tab = jnp.concatenate(
                    [jnp.where(by_row, jnp.broadcast_to(rowtab_ref[g0 + g:g0 + g + 1, :], (GRID_W, 2 * LANE)),
                               coltab_ref[...]) for g in range(ROW_BLOCK // GRID_W)], axis=0)
                cos, sin = tab[:, :LANE], tab[:, LANE:]
                if n_main is not None:
                    cos = jnp.where(is_ctx, 1.0, cos)
                    sin = jnp.where(is_ctx, 0.0, sin)

            for base, scale, o_ref in ((C_RQ, RET_DK ** -0.5, rq_ref), (C_RK, 1.0, rk_ref)):
                t = p(base, base + RET_W)
                for hd in range(RET_HEADS):
                    th = t[:, hd * LANE:(hd + 1) * LANE] * scale
                    if rope:
                        th = th * cos + _swap32(th) * sin
                    o_ref[rows, hd * LANE:(hd + 1) * LANE] = th.astype(BF16)
            rv_ref[rows, :] = p(C_RV, C_RV + RET_W).astype(BF16)
            rg_ref[rows, :] = _silu(p(C_RG, C_RG + RET_W)).astype(BF16)
            gq_ref[rows, :] = (p(C_GQ, C_GQ + GLA_QK) * GLA_DK ** -0.5).astype(BF16)
            gk_ref[rows, :] = p(C_GK, C_GK + GLA_QK).astype(BF16)
            gv_ref[rows, :] = p(C_GV, C_GV + GLA_W).astype(BF16)
            gg_ref[rows, :] = _silu(p(C_GG, C_GG + GLA_W)).astype(BF16)
            low = _dot(h2, wlow_ref[...]).astype(BF16)
            z = _dot(low, gw_ref[...]) + gbias_ref[...]
            ls = _log_sigmoid(z) * (1.0 / GLA_TAU)
            gf_ref[rows, :] = ls[:, :GLA_QK]
            gb_ref[rows, :] = ls[:, GLA_QK:]


def _ffn_call(x, m3, tiles_per_seq, nw, w1, w3, w2, *, mod_off, tm, ctx=None, ctx_row=None, proj=None,
              rope=None, final_w=None, name):
    t, d = x.shape
    f = w1.shape[1]
    n_main = t // tm
    n_tiles = n_main + (0 if ctx is None else ctx.shape[0] // tm)
    t = n_tiles * tm
    main = lambda i: jnp.minimum(i, n_main - 1)
    tile = lambda w: pl.BlockSpec((tm, w), lambda i: (i, 0))
    in_specs = [pl.BlockSpec((tm, d), lambda i: (main(i), 0))]
    args = [x]
    if ctx is None:
        row_of_tile = lambda i: i // tiles_per_seq
    else:
        row_of_tile = lambda i: jnp.where(i >= n_main, ctx_row, i // tiles_per_seq)
        in_specs.append(pl.BlockSpec((tm, d), lambda i: (jnp.maximum(i - n_main, 0), 0)))
        args.append(ctx)
    in_specs += [pl.BlockSpec((None, 1, N_MOD * d), lambda i: (row_of_tile(i), 0, 0)),
                 _resident((1, d)), _resident((d, f)), _resident((d, f)), _resident((f, d))]
    args += [m3, nw.reshape(1, d), w1, w3, w2]
    if proj is not None:
        n2w, win, wlow, gw, gbias = proj
        in_specs += [_resident((1, d))] + [_resident(a.shape) for a in (win, wlow, gw, gbias)]
        args += [n2w.reshape(1, d), win, wlow, gw, gbias]
        if rope is not None:
            rowtab, coltab = rope
            in_specs += [pl.BlockSpec((tm // GRID_W, 2 * LANE), lambda i: (main(i) % tiles_per_seq, 0)),
                         _resident(coltab.shape)]
            args += [rowtab, coltab]
    if final_w is not None:
        in_specs.append(_resident((1, d)))
        args.append(final_w.reshape(1, d))
    out_specs = [tile(d)]
    out_shape = [jax.ShapeDtypeStruct((t, d), F32)]
    if proj is not None:
        for w, dt in ((RET_W, BF16),) * 4 + ((GLA_QK, BF16),) * 2 + ((GLA_W, BF16),) * 2 + ((GLA_QK, F32),) * 2:
            out_specs.append(tile(w))
            out_shape.append(jax.ShapeDtypeStruct((t, w), dt))
    kern = functools.partial(_ffn_kernel, mod_off=mod_off, proj=proj is not None,
                             rope=rope is not None, final=final_w is not None,
                             n_main=None if ctx is None else n_main, d=d, f=f)
    return pl.pallas_call(
        kern, grid=(n_tiles,), in_specs=in_specs, out_specs=out_specs, out_shape=out_shape,
        scratch_shapes=[pltpu.VMEM((tm, f), BF16)],
        compiler_params=pltpu.CompilerParams(dimension_semantics=("arbitrary",),
                                             vmem_limit_bytes=VMEM_LIMIT),
        name=name,
    )(*args)


def _head_lanes(shape, hd):
    lane = _iota(shape, len(shape) - 1)
    return (lane >= 64) if hd % 2 else (lane < 64)


def _state_kernel(*refs, batch):
    it = iter(refs)
    per_batch = lambda: [next(it) for _ in range(batch)]
    rkf_ref, rvf_ref, gkf_ref, gvf_ref, gf_ref = (per_batch() for _ in range(5))
    rkb_ref, rvb_ref, gkb_ref, gvb_ref, gb_ref = (per_batch() for _ in range(5))
    dec_ref = next(it)
    irf_ref, irb_ref, igf_ref, igb_ref = (next(it) for _ in range(4))
    orf_ref, orb_ref, ogf_ref, ogb_ref = (next(it) for _ in range(4))
    frf_ref, frb_ref, fgf_ref, fgb_ref = (next(it) for _ in range(4))
    srf, srb, sgf, sgb = (next(it) for _ in range(4))
    n = CHUNK

    @pl.when(pl.program_id(0) == 0)
    def _():
        srf[...] = irf_ref[...]
        srb[...] = irb_ref[...]
        sgf[...] = igf_ref[...]
        sgb[...] = igb_ref[...]

    orf_ref[...] = srf[...].astype(BF16)
    orb_ref[...] = srb[...].astype(BF16)
    ogf_ref[...] = sgf[...].astype(BF16)
    ogb_ref[...] = sgb[...].astype(BF16)

    row = _iota((n, LANE), 0).astype(F32)
    lg = _log_sigmoid(dec_ref[...])
    ri = _iota((n, n), 0)
    ci = _iota((n, n), 1)
    after = (ci > ri).astype(BF16)
    before = (ci < ri).astype(BF16)

    def gla(b, g_ref, k_ref, v_ref, tri, edge, st):
        g = g_ref[b][...]
        e = _dot_select(tri, g)
        tot = e[edge:edge + 1, :] + g[edge:edge + 1, :]
        kd = (k_ref[b][...].astype(F32) * jnp.exp(e)).astype(BF16)
        for hd in range(GLA_HEADS):
            pr = slice((hd // 2) * LANE, (hd // 2 + 1) * LANE)
            upd = _dot_tn(v_ref[b][:, hd * LANE:(hd + 1) * LANE], kd[:, pr])
            upd = jnp.where(_head_lanes(upd.shape, hd), upd, 0.0)
            st[b, hd] = st[b, hd] * jnp.exp(tot[:, pr]) + upd

    for b in range(batch):
        for hd in range(RET_HEADS):
            sl = slice(hd * LANE, (hd + 1) * LANE)
            lgf = lg[0:1, sl]
            lgb = lg[1:2, sl]
            kf = (rkf_ref[b][:, sl].astype(F32) * jnp.exp((n - 1.0 - row) * lgf)).astype(BF16)
            srf[b, hd] = srf[b, hd] * jnp.exp(n * lgf) + _dot_tn(kf, rvf_ref[b][:, sl])
            kb = (rkb_ref[b][:, sl].astype(F32) * jnp.exp(row * lgb)).astype(BF16)
            srb[b, hd] = srb[b, hd] * jnp.exp(n * lgb) + _dot_tn(kb, rvb_ref[b][:, sl])
        gla(b, gf_ref, gkf_ref, gvf_ref, after, 0, sgf)
        gla(b, gb_ref, gkb_ref, gvb_ref, before, n - 1, sgb)

    frf_ref[...] = srf[...]
    frb_ref[...] = srb[...]
    fgf_ref[...] = sgf[...]
    fgb_ref[...] = sgb[...]


def _states(rk, rv, gk, gv, gf, gb, dec, init, batch, first_chunk, nc):
    def per_batch(w, backward):
        at = (lambda c: nc - 1 - c) if backward else (lambda c: c)
        return [pl.BlockSpec((CHUNK, w), lambda c, b=b: (first_chunk + b * nc + at(c), 0)) for b in range(batch)]

    fwd = lambda w: per_batch(w, False)
    bwd = lambda w: per_batch(w, True)
    st_shape = (batch, RET_HEADS, LANE, LANE)
    init_spec = pl.BlockSpec(st_shape, lambda c: (0, 0, 0, 0))
    chunk_shape = (batch, None, RET_HEADS, LANE, LANE)
    of_spec = pl.BlockSpec(chunk_shape, lambda c: (0, c, 0, 0, 0))
    ob_spec = pl.BlockSpec(chunk_shape, lambda c: (0, nc - 1 - c, 0, 0, 0))
    per_chunk = jax.ShapeDtypeStruct((batch, nc, RET_HEADS, LANE, LANE), BF16)
    final = jax.ShapeDtypeStruct(st_shape, F32)
    rep = lambda a: [a] * batch
    return pl.pallas_call(
        functools.partial(_state_kernel, batch=batch),
        grid=(nc,),
        in_specs=fwd(RET_W) + fwd(RET_W) + fwd(GLA_QK) + fwd(GLA_W) + fwd(GLA_QK)
        + bwd(RET_W) + bwd(RET_W) + bwd(GLA_QK) + bwd(GLA_W) + bwd(GLA_QK)
        + [pl.BlockSpec(dec.shape, lambda c: (0, 0))] + [init_spec] * 4,
        out_specs=[of_spec, ob_spec, of_spec, ob_spec] + [init_spec] * 4,
        out_shape=[per_chunk] * 4 + [final] * 4,
        scratch_shapes=[pltpu.VMEM(st_shape, F32)] * 4,
        compiler_params=pltpu.CompilerParams(dimension_semantics=("arbitrary",),
                                             vmem_limit_bytes=VMEM_LIMIT),
        name="states",
    )(*rep(rk), *rep(rv), *rep(gk), *rep(gv), *rep(gf), *rep(rk), *rep(rv), *rep(gk), *rep(gv), *rep(gb),
      dec, *init)


def _block_row(x, parent, r):
    n, w = x.shape
    if parent == n:
        return jnp.broadcast_to(x[r:r + 1, :], (n, w))
    x3 = x.reshape(n // parent, parent, w)
    return jnp.broadcast_to(x3[:, r:r + 1, :], x3.shape).reshape(n, w)


def _gla_levels(q, k_even, k_odd, gf, gb, bf, bb):
    n = q.shape[0]
    row = _iota(q.shape, 0)

    def factors(u_exp, w_exp):
        ew = jnp.exp2(w_exp).astype(BF16)
        return (q * jnp.exp2(u_exp).astype(BF16), k_even * ew, k_odd * ew)

    levels = []
    s = n // 2
    while s >= 2:
        if s >= 4:
            last_of_first = _block_row(bf, 2 * s, s - 1)
            first_of_second = _block_row(bb, 2 * s, s)
        else:
            upper = (row & 4) != 0
            last_of_first = jnp.where(upper, _block_row(bf, 8, 5), _block_row(bf, 8, 1))
            first_of_second = jnp.where(upper, _block_row(bb, 8, 6), _block_row(bb, 8, 2))
        df = bf - last_of_first
        db = bb - first_of_second
        levels.append((2 * s, [factors(jnp.minimum(df, db), -jnp.maximum(df, db))]))
        s //= 2
    odd = (row & 1) == 1
    qa = q * jnp.where(odd, jnp.exp2(gf), 2.0).astype(BF16)
    qb = q * jnp.where(odd, 2.0, jnp.exp2(gb)).astype(BF16)
    zero = jnp.zeros_like(k_even)
    ev = lambda x: jnp.where(odd, zero, x)
    od = lambda x: jnp.where(odd, x, zero)
    levels.append((2, [(qa, ev(k_even), ev(k_odd)), (qb, od(k_even), od(k_odd))]))
    return levels


def _mix_kernel(rq_ref, rk_ref, rv_ref, rg_ref, gq_ref, gk_ref, gv_ref, gg_ref, gf_ref, gb_ref,
                srf_ref, srb_ref, sgf_ref, sgb_ref, dec_ref, rnw_ref, gnw_ref,
                x_ref, m_ref, wout_ref, o_ref, mix_ref, decay_ref, qdec_ref, *, d):
    n = CHUNK
    half = n // 2
    ri = _iota((n, n), 0)
    ci = _iota((n, n), 1)
    step = pl.program_id(0)
    cur = step % 2

    @pl.when(step == 0)
    def _():
        mix_ref[1] = jnp.zeros(mix_ref.shape[1:], BF16)
        lg = _log_sigmoid(dec_ref[...])
        dist = (ri - ci).astype(F32)
        row = _iota((n, LANE), 0).astype(F32)
        for hd in range(RET_HEADS):
            sl = slice(hd * LANE, (hd + 1) * LANE)
            lgf = lg[0:1, sl]
            lgb = lg[1:2, sl]
            decay_ref[hd] = jnp.exp(jnp.where(dist > 0, dist * lgf[:, 0:1],
                                              jnp.where(dist < 0, -dist * lgb[:, 0:1], jnp.log(2.0))))
            qdec_ref[hd] = jnp.exp((row + 1.0) * lgf)
            qdec_ref[RET_HEADS + hd] = jnp.exp((n - row) * lgb)

    for hd in range(RET_HEADS):
        sl = slice(hd * LANE, (hd + 1) * LANE)
        q = rq_ref[:, sl]
        p = (_dot_nt(q, rk_ref[:, sl]) * decay_ref[hd]).astype(BF16)
        qf = q.astype(F32)
        qs = jnp.concatenate([(qf * qdec_ref[hd]).astype(BF16),
                              (qf * qdec_ref[RET_HEADS + hd]).astype(BF16)], axis=1)
        st = jnp.concatenate([srf_ref[hd], srb_ref[hd]], axis=0)
        o = _dot(p, rv_ref[:, sl]) + _dot(qs, st)
        mu = jnp.mean(o, axis=-1, keepdims=True)
        oc = o - mu
        var = jnp.mean(oc * oc, axis=-1, keepdims=True)
        r = oc * lax.rsqrt(var + EPS) * rnw_ref[:, sl] * rg_ref[:, sl].astype(F32)
        mix_ref[cur, :, sl] = r.astype(BF16)

    gf = gf_ref[...] * LOG2E
    gb = gb_ref[...] * LOG2E
    bf = _dot_select((ci <= ri).astype(BF16), gf)
    bb = _dot_select((ci >= ri).astype(BF16), gb)
    q = gq_ref[...]
    k = gk_ref[...]
    even_head = (_iota(k.shape, 1) & GLA_DK) == 0
    no_k = jnp.zeros_like(k)
    o_ref[...] = x_ref[...] + m_ref[:, 5 * d:6 * d] * _dot(mix_ref[1 - cur], wout_ref[...])
    levels = _gla_levels(q, jnp.where(even_head, k, no_k), jnp.where(even_head, no_k, k), gf, gb, bf, bb)
    qsf = q * jnp.exp2(bf).astype(BF16)
    qsb = q * jnp.exp2(bb).astype(BF16)
    code = _iota((half, n), 0) ^ (_iota((half, n), 1) & (half - 1))
    for pair in range(GLA_HEADS // 2):
        pr = slice(pair * LANE, (pair + 1) * LANE)
        near = [None, None]
        far = [None, None]
        for size, blocks in levels:
            for rb in range(2):
                kb = rb if size < n else 1 - rb
                rows = slice(rb * half, (rb + 1) * half)
                keys = slice(kb * half, (kb + 1) * half)
                u = jnp.concatenate([blk[0][rows, pr] for blk in blocks], axis=1)
                w = jnp.concatenate([jnp.concatenate([blk[j][keys, pr] for blk in blocks], axis=1)
                                     for j in (1, 2)], axis=0)
                t = _dot_nt(u, w)
                if size == n:
                    far[rb] = t
                else:
                    near[rb] = t if near[rb] is None else jnp.where(code < size, t, near[rb])
        qcat = jnp.concatenate([qsf[:, pr], qsb[:, pr]], axis=1)
        for j in range(2):
            hd = 2 * pair + j
            sl = slice(hd * LANE, (hd + 1) * LANE)
            mine = slice(j * half, (j + 1) * half)
            p = jnp.concatenate([jnp.concatenate([near[0][:, mine], far[0][:, mine]], axis=1),
                                 jnp.concatenate([far[1][:, mine], near[1][:, mine]], axis=1)], axis=0)
            st = jnp.concatenate([sgf_ref[hd], sgb_ref[hd]], axis=1)
            o = _dot(p.astype(BF16), gv_ref[:, sl]) + _dot_nt(qcat, st)
            ms = jnp.mean(o * o, axis=-1, keepdims=True)
            r = o * lax.rsqrt(ms + EPS) * gnw_ref[:, sl] * gg_ref[:, sl].astype(F32)
            mix_ref[cur, :, RET_W + hd * LANE:RET_W + (hd + 1) * LANE] = r.astype(BF16)


def _mix_call(mix_in, states, dec, rnw, gnw, x1, m3, wout, batch, nc):
    n_chunks = batch * nc
    t, d = n_chunks * CHUNK, x1.shape[1]
    mixed = lambda i: jnp.minimum(i, n_chunks - 1)
    done = lambda i: jnp.maximum(i - 1, 0)
    tile = lambda w: pl.BlockSpec((CHUNK, w), lambda i: (mixed(i), 0))
    st_spec = pl.BlockSpec((None, None, RET_HEADS, LANE, LANE),
                           lambda i: (mixed(i) // nc, mixed(i) % nc, 0, 0, 0))
    const = lambda shape: pl.BlockSpec(shape, lambda i: (0,) * len(shape))
    widths = (RET_W,) * 4 + (GLA_QK,) * 2 + (GLA_W,) * 2 + (GLA_QK,) * 2
    return pl.pallas_call(
        functools.partial(_mix_kernel, d=d),
        grid=(n_chunks + 1,),
        in_specs=[tile(w) for w in widths] + [st_spec] * 4
        + [const(dec.shape), const((1, RET_W)), const((1, GLA_W)),
           pl.BlockSpec((CHUNK, d), lambda i: (done(i), 0)),
           pl.BlockSpec((None, 1, N_MOD * d), lambda i: (done(i) // nc, 0, 0)),
           const(wout.shape)],
        out_specs=pl.BlockSpec((CHUNK, d), lambda i: (done(i), 0)),
        out_shape=jax.ShapeDtypeStruct((t, d), F32),
        scratch_shapes=[pltpu.VMEM((2, CHUNK, RET_W + GLA_W), BF16),
                        pltpu.VMEM((RET_HEADS, CHUNK, CHUNK), F32),
                        pltpu.VMEM((2 * RET_HEADS, CHUNK, LANE), F32)],
        compiler_params=pltpu.CompilerParams(dimension_semantics=("arbitrary",),
                                             vmem_limit_bytes=VMEM_LIMIT),
        name="mix",
    )(*mix_in, *states, dec, rnw.reshape(1, RET_W), gnw.reshape(1, GLA_W), x1, m3, wout)


def _rope_tables(n_tok):
    freqs = ROPE_BASE ** (-jnp.arange(RET_DK // 4, dtype=F32) / (RET_DK // 4))

    def table(n_pos, first_half):
        ang = jnp.arange(n_pos, dtype=F32)[:, None] * freqs
        zero = jnp.zeros((n_pos, LANE // 2), F32)
        cos = jnp.concatenate([jnp.cos(ang)] * 2, axis=-1)
        sin = jnp.concatenate([-jnp.sin(ang), jnp.sin(ang)], axis=-1)
        halves = (cos, zero, sin, zero) if first_half else (zero, cos, zero, sin)
        return jnp.concatenate(halves, axis=-1)

    return table(n_tok // GRID_W, True), table(GRID_W, False)


def _pack_w_in(w_in):
    low = w_in[:, C_LOW:]
    pad = jnp.zeros((w_in.shape[0], LANE - low.shape[1]), w_in.dtype)
    return w_in[:, :C_LOW].astype(BF16), jnp.concatenate([low, pad], axis=1).astype(BF16)


def _pack_gate(w_f, b_f, w_b, b_b):
    gw = jnp.zeros((LANE, 2 * GLA_QK), F32)
    gw = gw.at[:GLA_RANK, :GLA_QK].set(w_f).at[GLA_RANK:2 * GLA_RANK, GLA_QK:].set(w_b)
    return gw.astype(BF16), jnp.concatenate([b_f, b_b]).reshape(1, 2 * GLA_QK)


def kernel(x, c, ctx, c_ctx, ada_w, ada_b, norm1_w, ffn1_w1, ffn1_w3, ffn1_w2, norm2_w, w_in,
           ret_decay_f, ret_decay_b, ret_norm_w, gla_gate_w_f, gla_gate_b_f, gla_gate_w_b, gla_gate_b_b,
           gla_norm_w, w_out, norm3_w, ffn2_w1, ffn2_w3, ffn2_w2, final_norm_w):
    batch, n_tok, d = x.shape
    n_ctx = ctx.shape[1]
    depth = ada_w.shape[0]
    assert depth == 1 and batch + 1 <= 8
    assert n_tok % FFN_TILE == 0 and n_tok % CHUNK == 0 and n_ctx % CHUNK == 0

    cvec = jnp.zeros((8, d), F32).at[:batch].set(c).at[batch].set(c_ctx)
    m3 = _modulation(cvec, ada_w[0], ada_b[0]).reshape(8, 1, N_MOD * d)

    bf = lambda w: w.astype(BF16)
    gw, gbias = _pack_gate(gla_gate_w_f[0], gla_gate_b_f[0], gla_gate_w_b[0], gla_gate_b_b[0])
    proj = (norm2_w[0], *_pack_w_in(w_in[0]), gw, gbias)
    f1 = (norm1_w[0], bf(ffn1_w1[0]), bf(ffn1_w3[0]), bf(ffn1_w2[0]))
    rowtab, coltab = _rope_tables(n_tok)
    dec = jnp.zeros((8, RET_W), F32)
    dec = dec.at[0].set(jnp.repeat(ret_decay_f[0], LANE)).at[1].set(jnp.repeat(ret_decay_b[0], LANE))

    tiles_per_seq = n_tok // FFN_TILE
    first = _ffn_call(x.reshape(batch * n_tok, d), m3, tiles_per_seq, *f1, mod_off=0, tm=FFN_TILE,
                      ctx=ctx.reshape(batch * n_ctx, d), ctx_row=batch, proj=proj, rope=(rowtab, coltab),
                      name="ffn_in")
    x1, mix_in = first[0], first[1:]
    scan_in = (mix_in[1], mix_in[2], mix_in[5], mix_in[6], mix_in[8], mix_in[9])
    zero = jnp.zeros((batch, RET_HEADS, LANE, LANE), F32)
    nc, nc_ctx = n_tok // CHUNK, n_ctx // CHUNK
    ctx_states = _states(*scan_in, dec, (zero,) * 4, batch, batch * nc, nc_ctx)[4:]
    states = _states(*scan_in, dec, ctx_states, batch, 0, nc)[:4]
    x2 = _mix_call(mix_in, states, dec, ret_norm_w[0], gla_norm_w[0], x1, m3, bf(w_out[0]), batch, nc)
    out = _ffn_call(x2, m3, tiles_per_seq, norm3_w[0], bf(ffn2_w1[0]), bf(ffn2_w3[0]), bf(ffn2_w2[0]),
                    mod_off=6, tm=FFN_TILE, final_w=final_norm_w, name="ffn_out")[0]
    return out.reshape(batch, n_tok, d)
```

```python
import functools

import jax
import jax.numpy as jnp
from jax import lax
from jax.experimental import pallas as pl
from jax.experimental.pallas import tpu as pltpu

F32 = jnp.float32
BF16 = jnp.bfloat16

EPS = 1e-6
LOG2E = 1.4426950408889634
N_MOD = 9
GRID_W = 64
ROPE_BASE = 10000.0
RET_HEADS = 4
RET_DK = 128
RET_DV = 128
GLA_HEADS = 4
GLA_DK = 64
GLA_DV = 128
GLA_RANK = 16
GLA_TAU = 16.0
RET_W = RET_HEADS * RET_DV
GLA_W = GLA_HEADS * GLA_DV
GLA_QK = GLA_HEADS * GLA_DK

LANE = 128
CHUNK = 256
FFN_TILE = 512
ROW_BLOCK = 256
FF_CHUNK = 256
VMEM_LIMIT = 60 * 1024 * 1024

C_RQ, C_RK, C_RV, C_RG = 0, 512, 1024, 1536
C_GQ, C_GK, C_GV, C_GG = 2048, 2304, 2560, 3072
C_LOW = 3584


def _silu(x):
    return x * (1.0 / (1.0 + jnp.exp(-x)))


def _log_sigmoid(z):
    return jnp.minimum(z, 0.0) - jnp.log(1.0 + jnp.exp(-jnp.abs(z)))


def _rms(x, w):
    return x * lax.rsqrt(jnp.mean(x * x, axis=-1, keepdims=True) + EPS) * w


def _dot(a, b):
    return jnp.dot(a, b, preferred_element_type=F32)


def _dot_nt(a, b):
    return lax.dot_general(a, b, (((1,), (1,)), ((), ())), preferred_element_type=F32)


def _dot_tn(a, b):
    return lax.dot_general(a, b, (((0,), (0,)), ((), ())), preferred_element_type=F32)


def _dot_select(sel, x):
    hi = x.astype(BF16)
    lo = (x - hi.astype(F32)).astype(BF16)
    return _dot(sel, hi) + _dot(sel, lo)


def _iota(shape, dim):
    return lax.broadcasted_iota(jnp.int32, shape, dim)


def _resident(shape):
    nd = len(shape)
    return pl.BlockSpec(shape, lambda *_: (0,) * nd, pipeline_mode=pl.Buffered(1))


def _mod_kernel(c_ref, w_ref, b_ref, o_ref):
    cond = _silu(c_ref[...]).astype(BF16)
    o_ref[...] = _dot(cond, w_ref[...].astype(BF16)) + b_ref[...]


def _modulation(cvec, ada_w, ada_b):
    d, n = ada_w.shape
    bn = n // 4 if n % (4 * LANE) == 0 else d
    return pl.pallas_call(
        _mod_kernel,
        grid=(n // bn,),
        in_specs=[pl.BlockSpec((8, d), lambda j: (0, 0)),
                  pl.BlockSpec((d, bn), lambda j: (0, j)),
                  pl.BlockSpec((1, bn), lambda j: (0, j))],
        out_specs=pl.BlockSpec((8, bn), lambda j: (0, j)),
        out_shape=jax.ShapeDtypeStruct((8, n), F32),
        compiler_params=pltpu.CompilerParams(dimension_semantics=("arbitrary",),
                                             vmem_limit_bytes=VMEM_LIMIT),
        name="mod",
    )(cvec, ada_w, ada_b.reshape(1, n))


def _swap32(x):
    lane = _iota(x.shape, 1)
    return jnp.where((lane & 63) < 32, pltpu.roll(x, 96, 1), pltpu.roll(x, 32, 1))


def _ffn_kernel(*refs, mod_off, proj, rope, final, n_main, d, f):
    it = iter(refs)
    x_ref = next(it)
    if n_main is not None:
        xc_ref = next(it)
        is_ctx = pl.program_id(0) >= n_main
    m_ref, nw_ref, w1_ref, w3_ref, w2_ref = (next(it) for _ in range(5))
    if proj:
        n2w_ref, win_ref, wlow_ref, gw_ref, gbias_ref = (next(it) for _ in range(5))
        if rope:
            rowtab_ref, coltab_ref = next(it), next(it)
    if final:
        fnw_ref = next(it)
    xo_ref = next(it)
    if proj:
        (rq_ref, rk_ref, rv_ref, rg_ref, gq_ref, gk_ref, gv_ref, gg_ref,
         gf_ref, gb_ref) = (next(it) for _ in range(10))
    u_ref = next(it)

    def mod(i):
        return m_ref[:, (mod_off + i) * d:(mod_off + i + 1) * d]

    for rb in range(x_ref.shape[0] // ROW_BLOCK):
        rows = slice(rb * ROW_BLOCK, (rb + 1) * ROW_BLOCK)
        x = x_ref[rows, :]
        if n_main is not None:
            x = jnp.where(is_ctx, xc_ref[rows, :], x)
        h = (_rms(x, nw_ref[...]) * (1.0 + mod(1)) + mod(0)).astype(BF16)
        for k in range(f // FF_CHUNK):
            sl = slice(k * FF_CHUNK, (k + 1) * FF_CHUNK)
            a = _dot(h, w1_ref[:, sl])
            g = _dot(h, w3_ref[:, sl])
            u_ref[rows, sl] = (_silu(a) * g).astype(BF16)
        y = _dot(u_ref[rows, :], w2_ref[...])
        x1 = x + (0.5 * mod(2)) * y

        if final:
            xo_ref[rows, :] = _rms(x1, fnw_ref[...])
        else:
            xo_ref[rows, :] = x1

        if proj:
            h2 = (_rms(x1, n2w_ref[...]) * (1.0 + mod(4)) + mod(3)).astype(BF16)

            def p(lo, hi):
                return _dot(h2, win_ref[:, lo:hi])

            if rope:
                by_row = _iota((GRID_W, 2 * LANE), 1) % LANE < LANE // 2
                g0 = rb * (ROW_BLOCK // GRID_W)
                tab = jnp.concatenate(
                    [jnp.where(by_row, jnp.broadcast_to(rowtab_ref[g0 + g:g0 + g + 1, :], (GRID_W, 2 * LANE)),
                               coltab_ref[...]) for g in range(ROW_BLOCK // GRID_W)], axis=0)
                cos, sin = tab[:, :LANE], tab[:, LANE:]
                if n_main is not None:
                    cos = jnp.where(is_ctx, 1.0, cos)
                    sin = jnp.where(is_ctx, 0.0, sin)

            for base, scale, o_ref in ((C_RQ, RET_DK ** -0.5, rq_ref), (C_RK, 1.0, rk_ref)):
                t = p(base, base + RET_W)
                for hd in range(RET_HEADS):
                    th = t[:, hd * LANE:(hd + 1) * LANE] * scale
                    if rope:
                        th = th * cos + _swap32(th) * sin
                    o_ref[rows, hd * LANE:(hd + 1) * LANE] = th.astype(BF16)
            rv_ref[rows, :] = p(C_RV, C_RV + RET_W).astype(BF16)
            rg_ref[rows, :] = _silu(p(C_RG, C_RG + RET_W)).astype(BF16)
            gq_ref[rows, :] = (p(C_GQ, C_GQ + GLA_QK) * GLA_DK ** -0.5).astype(BF16)
            gk_ref[rows, :] = p(C_GK, C_GK + GLA_QK).astype(BF16)
            gv_ref[rows, :] = p(C_GV, C_GV + GLA_W).astype(BF16)
            gg_ref[rows, :] = _silu(p(C_GG, C_GG + GLA_W)).astype(BF16)
            low = _dot(h2, wlow_ref[...]).astype(BF16)
            z = _dot(low, gw_ref[...]) + gbias_ref[...]
            ls = _log_sigmoid(z) * (1.0 / GLA_TAU)
            gf_ref[rows, :] = ls[:, :GLA_QK]
            gb_ref[rows, :] = ls[:, GLA_QK:]


def _ffn_call(x, m3, tiles_per_seq, nw, w1, w3, w2, *, mod_off, tm, ctx=None, ctx_row=None, proj=None,
              rope=None, final_w=None, name):
    t, d = x.shape
    f = w1.shape[1]
    n_main = t // tm
    n_tiles = n_main + (0 if ctx is None else ctx.shape[0] // tm)
    t = n_tiles * tm
    main = lambda i: jnp.minimum(i, n_main - 1)
    tile = lambda w: pl.BlockSpec((tm, w), lambda i: (i, 0))
    in_specs = [pl.BlockSpec((tm, d), lambda i: (main(i), 0))]
    args = [x]
    if ctx is None:
        row_of_tile = lambda i: i // tiles_per_seq
    else:
        row_of_tile = lambda i: jnp.where(i >= n_main, ctx_row, i // tiles_per_seq)
        in_specs.append(pl.BlockSpec((tm, d), lambda i: (jnp.maximum(i - n_main, 0), 0)))
        args.append(ctx)
    in_specs += [pl.BlockSpec((None, 1, N_MOD * d), lambda i: (row_of_tile(i), 0, 0)),
                 _resident((1, d)), _resident((d, f)), _resident((d, f)), _resident((f, d))]
    args += [m3, nw.reshape(1, d), w1, w3, w2]
    if proj is not None:
        n2w, win, wlow, gw, gbias = proj
        in_specs += [_resident((1, d)), _resident((d, C_LOW))] + [_resident(a.shape) for a in (wlow, gw, gbias)]
        args += [n2w.reshape(1, d), win, wlow, gw, gbias]
        if rope is not None:
            rowtab, coltab = rope
            in_specs += [pl.BlockSpec((tm // GRID_W, 2 * LANE), lambda i: (main(i) % tiles_per_seq, 0)),
                         _resident(coltab.shape)]
            args += [rowtab, coltab]
    if final_w is not None:
        in_specs.append(_resident((1, d)))
        args.append(final_w.reshape(1, d))
    out_specs = [tile(d)]
    out_shape = [jax.ShapeDtypeStruct((t, d), F32)]
    if proj is not None:
        for w, dt in ((RET_W, BF16),) * 4 + ((GLA_QK, BF16),) * 2 + ((GLA_W, BF16),) * 2 + ((GLA_QK, F32),) * 2:
            out_specs.append(tile(w))
            out_shape.append(jax.ShapeDtypeStruct((t, w), dt))
    kern = functools.partial(_ffn_kernel, mod_off=mod_off, proj=proj is not None,
                             rope=rope is not None, final=final_w is not None,
                             n_main=None if ctx is None else n_main, d=d, f=f)
    return pl.pallas_call(
        kern, grid=(n_tiles,), in_specs=in_specs, out_specs=out_specs, out_shape=out_shape,
        scratch_shapes=[pltpu.VMEM((tm, f), BF16)],
        compiler_params=pltpu.CompilerParams(dimension_semantics=("arbitrary",),
                                             vmem_limit_bytes=VMEM_LIMIT),
        name=name,
    )(*args)


def _head_lanes(shape, hd):
    lane = _iota(shape, len(shape) - 1)
    return (lane >= 64) if hd % 2 else (lane < 64)


def _state_kernel(*refs, batch, cps):
    it = iter(refs)
    per_batch = lambda: [next(it) for _ in range(batch)]
    rkf_ref, rvf_ref, gkf_ref, gvf_ref, gf_ref = (per_batch() for _ in range(5))
    rkb_ref, rvb_ref, gkb_ref, gvb_ref, gb_ref = (per_batch() for _ in range(5))
    dec_ref = next(it)
    irf_ref, irb_ref, igf_ref, igb_ref = (next(it) for _ in range(4))
    orf_ref, orb_ref, ogf_ref, ogb_ref = (next(it) for _ in range(4))
    frf_ref, frb_ref, fgf_ref, fgb_ref = (next(it) for _ in range(4))
    srf, srb, sgf, sgb = (next(it) for _ in range(4))
    n = CHUNK

    @pl.when(pl.program_id(0) == 0)
    def _():
        srf[...] = irf_ref[...]
        srb[...] = irb_ref[...]
        sgf[...] = igf_ref[...]
        sgb[...] = igb_ref[...]

    row = _iota((n, LANE), 0).astype(F32)
    lg = _log_sigmoid(dec_ref[...])
    ri = _iota((n, n), 0)
    ci = _iota((n, n), 1)
    after = (ci > ri).astype(BF16)
    before = (ci < ri).astype(BF16)

    def gla(b, rows, g_ref, k_ref, v_ref, tri, edge, st):
        g = g_ref[b][rows, :]
        e = _dot_select(tri, g)
        tot = e[edge:edge + 1, :] + g[edge:edge + 1, :]
        kd = (k_ref[b][rows, :].astype(F32) * jnp.exp(e)).astype(BF16)
        for hd in range(GLA_HEADS):
            pr = slice((hd // 2) * LANE, (hd // 2 + 1) * LANE)
            upd = _dot_tn(v_ref[b][rows, hd * LANE:(hd + 1) * LANE], kd[:, pr])
            upd = jnp.where(_head_lanes(upd.shape, hd), upd, 0.0)
            st[b, hd] = st[b, hd] * jnp.exp(tot[:, pr]) + upd

    for j in range(cps):
        jb = cps - 1 - j
        fr = slice(j * n, (j + 1) * n)
        br = slice(jb * n, (jb + 1) * n)
        orf_ref[:, j] = srf[...].astype(BF16)
        ogf_ref[:, j] = sgf[...].astype(BF16)
        orb_ref[:, jb] = srb[...].astype(BF16)
        ogb_ref[:, jb] = sgb[...].astype(BF16)
        for b in range(batch):
            for hd in range(RET_HEADS):
                sl = slice(hd * LANE, (hd + 1) * LANE)
                lgf = lg[0:1, sl]
                lgb = lg[1:2, sl]
                kf = (rkf_ref[b][fr, sl].astype(F32) * jnp.exp((n - 1.0 - row) * lgf)).astype(BF16)
                srf[b, hd] = srf[b, hd] * jnp.exp(n * lgf) + _dot_tn(kf, rvf_ref[b][fr, sl])
                kb = (rkb_ref[b][br, sl].astype(F32) * jnp.exp(row * lgb)).astype(BF16)
                srb[b, hd] = srb[b, hd] * jnp.exp(n * lgb) + _dot_tn(kb, rvb_ref[b][br, sl])
            gla(b, fr, gf_ref, gkf_ref, gvf_ref, after, 0, sgf)
            gla(b, br, gb_ref, gkb_ref, gvb_ref, before, n - 1, sgb)

    frf_ref[...] = srf[...]
    frb_ref[...] = srb[...]
    fgf_ref[...] = sgf[...]
    fgb_ref[...] = sgb[...]


def _states(rk, rv, gk, gv, gf, gb, dec, init, batch, first_chunk, nc):
    cps = 2 if nc % 2 == 0 and first_chunk % 2 == 0 else 1
    steps = nc // cps

    def per_batch(w, backward):
        at = (lambda c: steps - 1 - c) if backward else (lambda c: c)
        return [pl.BlockSpec((cps * CHUNK, w), lambda c, b=b: ((first_chunk + b * nc) // cps + at(c), 0))
                for b in range(batch)]

    fwd = lambda w: per_batch(w, False)
    bwd = lambda w: per_batch(w, True)
    st_shape = (batch, RET_HEADS, LANE, LANE)
    init_spec = pl.BlockSpec(st_shape, lambda c: (0, 0, 0, 0))
    chunk_shape = (batch, cps, RET_HEADS, LANE, LANE)
    of_spec = pl.BlockSpec(chunk_shape, lambda c: (0, c, 0, 0, 0))
    ob_spec = pl.BlockSpec(chunk_shape, lambda c: (0, steps - 1 - c, 0, 0, 0))
    per_chunk = jax.ShapeDtypeStruct((batch, nc, RET_HEADS, LANE, LANE), BF16)
    final = jax.ShapeDtypeStruct(st_shape, F32)
    rep = lambda a: [a] * batch
    return pl.pallas_call(
        functools.partial(_state_kernel, batch=batch, cps=cps),
        grid=(steps,),
        in_specs=fwd(RET_W) + fwd(RET_W) + fwd(GLA_QK) + fwd(GLA_W) + fwd(GLA_QK)
        + bwd(RET_W) + bwd(RET_W) + bwd(GLA_QK) + bwd(GLA_W) + bwd(GLA_QK)
        + [pl.BlockSpec(dec.shape, lambda c: (0, 0))] + [init_spec] * 4,
        out_specs=[of_spec, ob_spec, of_spec, ob_spec] + [init_spec] * 4,
        out_shape=[per_chunk] * 4 + [final] * 4,
        scratch_shapes=[pltpu.VMEM(st_shape, F32)] * 4,
        compiler_params=pltpu.CompilerParams(dimension_semantics=("arbitrary",),
                                             vmem_limit_bytes=VMEM_LIMIT),
        name="states",
    )(*rep(rk), *rep(rv), *rep(gk), *rep(gv), *rep(gf), *rep(rk), *rep(rv), *rep(gk), *rep(gv), *rep(gb),
      dec, *init)


def _block_row(x, parent, r):
    n, w = x.shape
    if parent == n:
        return jnp.broadcast_to(x[r:r + 1, :], (n, w))
    x3 = x.reshape(n // parent, parent, w)
    return jnp.broadcast_to(x3[:, r:r + 1, :], x3.shape).reshape(n, w)


def _gla_levels(q, k_even, k_odd, gf, gb, bf, bb):
    n = q.shape[0]
    row = _iota(q.shape, 0)

    def factors(u_exp, w_exp):
        ew = jnp.exp2(w_exp).astype(BF16)
        return (q * jnp.exp2(u_exp).astype(BF16), k_even * ew, k_odd * ew)

    levels = []
    s = n // 2
    while s >= 2:
        if s >= 4:
            last_of_first = _block_row(bf, 2 * s, s - 1)
            first_of_second = _block_row(bb, 2 * s, s)
        else:
            upper = (row & 4) != 0
            last_of_first = jnp.where(upper, _block_row(bf, 8, 5), _block_row(bf, 8, 1))
            first_of_second = jnp.where(upper, _block_row(bb, 8, 6), _block_row(bb, 8, 2))
        df = bf - last_of_first
        db = bb - first_of_second
        levels.append((2 * s, [factors(jnp.minimum(df, db), -jnp.maximum(df, db))]))
        s //= 2
    odd = (row & 1) == 1
    qa = q * jnp.where(odd, jnp.exp2(gf), 2.0).astype(BF16)
    qb = q * jnp.where(odd, 2.0, jnp.exp2(gb)).astype(BF16)
    zero = jnp.zeros_like(k_even)
    ev = lambda x: jnp.where(odd, zero, x)
    od = lambda x: jnp.where(odd, x, zero)
    levels.append((2, [(qa, ev(k_even), ev(k_odd)), (qb, od(k_even), od(k_odd))]))
    return levels


def _mix_kernel(rq_ref, rk_ref, rv_ref, rg_ref, gq_ref, gk_ref, gv_ref, gg_ref, gf_ref, gb_ref,
                srf_ref, srb_ref, sgf_ref, sgb_ref, dec_ref, rnw_ref, gnw_ref,
                x_ref, m_ref, wout_ref, o_ref, mix_ref, decay_ref, qdec_ref, *, d):
    n = CHUNK
    half = n // 2
    ri = _iota((n, n), 0)
    ci = _iota((n, n), 1)
    step = pl.program_id(0)
    cur = step % 2

    @pl.when(step == 0)
    def _():
        mix_ref[1] = jnp.zeros(mix_ref.shape[1:], BF16)
        lg = _log_sigmoid(dec_ref[...])
        dist = (ri - ci).astype(F32)
        row = _iota((n, LANE), 0).astype(F32)
        for hd in range(RET_HEADS):
            sl = slice(hd * LANE, (hd + 1) * LANE)
            lgf = lg[0:1, sl]
            lgb = lg[1:2, sl]
            decay_ref[hd] = jnp.exp(jnp.where(dist > 0, dist * lgf[:, 0:1],
                                              jnp.where(dist < 0, -dist * lgb[:, 0:1], jnp.log(2.0))))
            qdec_ref[hd] = jnp.exp((row + 1.0) * lgf)
            qdec_ref[RET_HEADS + hd] = jnp.exp((n - row) * lgb)

    for hd in range(RET_HEADS):
        sl = slice(hd * LANE, (hd + 1) * LANE)
        q = rq_ref[:, sl]
        p = (_dot_nt(q, rk_ref[:, sl]) * decay_ref[hd]).astype(BF16)
        qf = q.astype(F32)
        qs = jnp.concatenate([(qf * qdec_ref[hd]).astype(BF16),
                              (qf * qdec_ref[RET_HEADS + hd]).astype(BF16)], axis=1)
        st = jnp.concatenate([srf_ref[hd], srb_ref[hd]], axis=0)
        o = _dot(p, rv_ref[:, sl]) + _dot(qs, st)
        mu = jnp.mean(o, axis=-1, keepdims=True)
        oc = o - mu
        var = jnp.mean(oc * oc, axis=-1, keepdims=True)
        r = oc * lax.rsqrt(var + EPS) * rnw_ref[:, sl] * rg_ref[:, sl].astype(F32)
        mix_ref[cur, :, sl] = r.astype(BF16)

    gf = gf_ref[...] * LOG2E
    gb = gb_ref[...] * LOG2E
    bf = _dot_select((ci <= ri).astype(BF16), gf)
    bb = _dot_select((ci >= ri).astype(BF16), gb)
    q = gq_ref[...]
    k = gk_ref[...]
    even_head = (_iota(k.shape, 1) & GLA_DK) == 0
    no_k = jnp.zeros_like(k)
    o_ref[...] = x_ref[...] + m_ref[:, 5 * d:6 * d] * _dot(mix_ref[1 - cur], wout_ref[...])
    levels = _gla_levels(q, jnp.where(even_head, k, no_k), jnp.where(even_head, no_k, k), gf, gb, bf, bb)
    qsf = q * jnp.exp2(bf).astype(BF16)
    qsb = q * jnp.exp2(bb).astype(BF16)
    code = _iota((half, n), 0) ^ (_iota((half, n), 1) & (half - 1))
    for pair in range(GLA_HEADS // 2):
        pr = slice(pair * LANE, (pair + 1) * LANE)
        near = [None, None]
        far = [None, None]
        for size, blocks in levels:
            for rb in range(2):
                kb = rb if size < n else 1 - rb
                rows = slice(rb * half, (rb + 1) * half)
                keys = slice(kb * half, (kb + 1) * half)
                u = jnp.concatenate([blk[0][rows, pr] for blk in blocks], axis=1)
                w = jnp.concatenate([jnp.concatenate([blk[j][keys, pr] for blk in blocks], axis=1)
                                     for j in (1, 2)], axis=0)
                t = _dot_nt(u, w)
                if size == n:
                    far[rb] = t
                else:
                    near[rb] = t if near[rb] is None else jnp.where(code < size, t, near[rb])
        qcat = jnp.concatenate([qsf[:, pr], qsb[:, pr]], axis=1)
        for j in range(2):
            hd = 2 * pair + j
            sl = slice(hd * LANE, (hd + 1) * LANE)
            mine = slice(j * half, (j + 1) * half)
            p = jnp.concatenate([jnp.concatenate([near[0][:, mine], far[0][:, mine]], axis=1),
                                 jnp.concatenate([far[1][:, mine], near[1][:, mine]], axis=1)], axis=0)
            st = jnp.concatenate([sgf_ref[hd], sgb_ref[hd]], axis=1)
            o = _dot(p.astype(BF16), gv_ref[:, sl]) + _dot_nt(qcat, st)
            ms = jnp.mean(o * o, axis=-1, keepdims=True)
            r = o * lax.rsqrt(ms + EPS) * gnw_ref[:, sl] * gg_ref[:, sl].astype(F32)
            mix_ref[cur, :, RET_W + hd * LANE:RET_W + (hd + 1) * LANE] = r.astype(BF16)


def _mix_call(mix_in, states, dec, rnw, gnw, x1, m3, wout, batch, nc):
    n_chunks = batch * nc
    t, d = n_chunks * CHUNK, x1.shape[1]
    mixed = lambda i: jnp.minimum(i, n_chunks - 1)
    done = lambda i: jnp.maximum(i - 1, 0)
    tile = lambda w: pl.BlockSpec((CHUNK, w), lambda i: (mixed(i), 0))
    st_spec = pl.BlockSpec((None, None, RET_HEADS, LANE, LANE),
                           lambda i: (mixed(i) // nc, mixed(i) % nc, 0, 0, 0))
    const = lambda shape: pl.BlockSpec(shape, lambda i: (0,) * len(shape))
    widths = (RET_W,) * 4 + (GLA_QK,) * 2 + (GLA_W,) * 2 + (GLA_QK,) * 2
    return pl.pallas_call(
        functools.partial(_mix_kernel, d=d),
        grid=(n_chunks + 1,),
        in_specs=[tile(w) for w in widths] + [st_spec] * 4
        + [const(dec.shape), const((1, RET_W)), const((1, GLA_W)),
           pl.BlockSpec((CHUNK, d), lambda i: (done(i), 0)),
           pl.BlockSpec((None, 1, N_MOD * d), lambda i: (done(i) // nc, 0, 0)),
           const(wout.shape)],
        out_specs=pl.BlockSpec((CHUNK, d), lambda i: (done(i), 0)),
        out_shape=jax.ShapeDtypeStruct((t, d), F32),
        scratch_shapes=[pltpu.VMEM((2, CHUNK, RET_W + GLA_W), BF16),
                        pltpu.VMEM((RET_HEADS, CHUNK, CHUNK), F32),
                        pltpu.VMEM((2 * RET_HEADS, CHUNK, LANE), F32)],
        compiler_params=pltpu.CompilerParams(dimension_semantics=("arbitrary",),
                                             vmem_limit_bytes=VMEM_LIMIT),
        name="mix",
    )(*mix_in, *states, dec, rnw.reshape(1, RET_W), gnw.reshape(1, GLA_W), x1, m3, wout)


def _rope_tables(n_tok):
    freqs = ROPE_BASE ** (-jnp.arange(RET_DK // 4, dtype=F32) / (RET_DK // 4))

    def table(n_pos, first_half):
        ang = jnp.arange(n_pos, dtype=F32)[:, None] * freqs
        zero = jnp.zeros((n_pos, LANE // 2), F32)
        cos = jnp.concatenate([jnp.cos(ang)] * 2, axis=-1)
        sin = jnp.concatenate([-jnp.sin(ang), jnp.sin(ang)], axis=-1)
        halves = (cos, zero, sin, zero) if first_half else (zero, cos, zero, sin)
        return jnp.concatenate(halves, axis=-1)

    return table(n_tok // GRID_W, True), table(GRID_W, False)


def _pack_w_in(w_in):
    low = w_in[:, C_LOW:]
    pad = jnp.zeros((w_in.shape[0], LANE - low.shape[1]), w_in.dtype)
    return w_in.astype(BF16), jnp.concatenate([low, pad], axis=1).astype(BF16)


def _pack_gate(w_f, b_f, w_b, b_b):
    gw = jnp.zeros((LANE, 2 * GLA_QK), F32)
    gw = gw.at[:GLA_RANK, :GLA_QK].set(w_f).at[GLA_RANK:2 * GLA_RANK, GLA_QK:].set(w_b)
    return gw.astype(BF16), jnp.concatenate([b_f, b_b]).reshape(1, 2 * GLA_QK)


def kernel(x, c, ctx, c_ctx, ada_w, ada_b, norm1_w, ffn1_w1, ffn1_w3, ffn1_w2, norm2_w, w_in,
           ret_decay_f, ret_decay_b, ret_norm_w, gla_gate_w_f, gla_gate_b_f, gla_gate_w_b, gla_gate_b_b,
           gla_norm_w, w_out, norm3_w, ffn2_w1, ffn2_w3, ffn2_w2, final_norm_w):
    batch, n_tok, d = x.shape
    n_ctx = ctx.shape[1]
    depth = ada_w.shape[0]
    assert depth == 1 and batch + 1 <= 8
    assert n_tok % FFN_TILE == 0 and n_tok % CHUNK == 0 and n_ctx % CHUNK == 0
    assert (batch * n_ctx) % FFN_TILE == 0

    cvec = jnp.zeros((8, d), F32).at[:batch].set(c).at[batch].set(c_ctx)
    m3 = _modulation(cvec, ada_w[0], ada_b[0]).reshape(8, 1, N_MOD * d)

    bf = lambda w: w.astype(BF16)
    gw, gbias = _pack_gate(gla_gate_w_f[0], gla_gate_b_f[0], gla_gate_w_b[0], gla_gate_b_b[0])
    proj = (norm2_w[0], *_pack_w_in(w_in[0]), gw, gbias)
    f1 = (norm1_w[0], bf(ffn1_w1[0]), bf(ffn1_w3[0]), bf(ffn1_w2[0]))
    rowtab, coltab = _rope_tables(n_tok)
    dec = jnp.zeros((8, RET_W), F32)
    dec = dec.at[0].set(jnp.repeat(ret_decay_f[0], LANE)).at[1].set(jnp.repeat(ret_decay_b[0], LANE))

    tiles_per_seq = n_tok // FFN_TILE
    first = _ffn_call(x.reshape(batch * n_tok, d), m3, tiles_per_seq, *f1, mod_off=0, tm=FFN_TILE,
                      ctx=ctx.reshape(batch * n_ctx, d), ctx_row=batch, proj=proj, rope=(rowtab, coltab),
                      name="ffn_in")
    x1, mix_in = first[0], first[1:]
    scan_in = (mix_in[1], mix_in[2], mix_in[5], mix_in[6], mix_in[8], mix_in[9])
    zero = jnp.zeros((batch, RET_HEADS, LANE, LANE), F32)
    nc, nc_ctx = n_tok // CHUNK, n_ctx // CHUNK
    ctx_states = _states(*scan_in, dec, (zero,) * 4, batch, batch * nc, nc_ctx)[4:]
    states = _states(*scan_in, dec, ctx_states, batch, 0, nc)[:4]
    x2 = _mix_call(mix_in, states, dec, ret_norm_w[0], gla_norm_w[0], x1, m3, bf(w_out[0]), batch, nc)
    out = _ffn_call(x2, m3, tiles_per_seq, norm3_w[0], bf(ffn2_w1[0]), bf(ffn2_w3[0]), bf(ffn2_w2[0]),
                    mod_off=6, tm=FFN_TILE, final_w=final_norm_w, name="ffn_out")[0]
    return out.reshape(batch, n_tok, d)
```

```python
import functools

import jax
import jax.numpy as jnp
from jax import lax
from jax.experimental import pallas as pl
from jax.experimental.pallas import tpu as pltpu

F32 = jnp.float32
BF16 = jnp.bfloat16

EPS = 1e-6
LOG2E = 1.4426950408889634
N_MOD = 9
GRID_W = 64
ROPE_BASE = 10000.0
RET_HEADS = 4
RET_DK = 128
RET_DV = 128
GLA_HEADS = 4
GLA_DK = 64
GLA_DV = 128
GLA_RANK = 16
GLA_TAU = 16.0
RET_W = RET_HEADS * RET_DV
GLA_W = GLA_HEADS * GLA_DV
GLA_QK = GLA_HEADS * GLA_DK

LANE = 128
BF16_ROWS = 16
CHUNK = 256
FFN_TILE = 512
ROW_BLOCK = 256
FF_CHUNK = 256
VMEM_LIMIT = 60 * 1024 * 1024

C_RQ, C_RK, C_RV, C_RG = 0, 512, 1024, 1536
C_GQ, C_GK, C_GV, C_GG = 2048, 2304, 2560, 3072
C_LOW = 3584


def _silu(x):
    return x * (1.0 / (1.0 + jnp.exp(-x)))


def _log_sigmoid(z):
    return jnp.minimum(z, 0.0) - jnp.log(1.0 + jnp.exp(-jnp.abs(z)))


def _rms(x, w):
    return x * lax.rsqrt(jnp.mean(x * x, axis=-1, keepdims=True) + EPS) * w


def _dot(a, b):
    return jnp.dot(a, b, preferred_element_type=F32)


def _dot_nt(a, b):
    return lax.dot_general(a, b, (((1,), (1,)), ((), ())), preferred_element_type=F32)


def _dot_tn(a, b):
    return lax.dot_general(a, b, (((0,), (0,)), ((), ())), preferred_element_type=F32)


def _dot_select(sel, x):
    hi = x.astype(BF16)
    lo = (x - hi.astype(F32)).astype(BF16)
    return _dot(sel, hi) + _dot(sel, lo)


def _iota(shape, dim):
    return lax.broadcasted_iota(jnp.int32, shape, dim)


def _resident(shape):
    nd = len(shape)
    return pl.BlockSpec(shape, lambda *_: (0,) * nd, pipeline_mode=pl.Buffered(1))


def _mod_kernel(c_ref, w_ref, b_ref, o_ref):
    cond = _silu(c_ref[...]).astype(BF16)
    o_ref[...] = _dot(cond, w_ref[...].astype(BF16)) + b_ref[...]


def _modulation(cvec, ada_w, ada_b):
    d, n = ada_w.shape
    bn = n // 4 if n % (4 * LANE) == 0 else d
    return pl.pallas_call(
        _mod_kernel,
        grid=(n // bn,),
        in_specs=[pl.BlockSpec((8, d), lambda j: (0, 0)),
                  pl.BlockSpec((d, bn), lambda j: (0, j)),
                  pl.BlockSpec((1, bn), lambda j: (0, j))],
        out_specs=pl.BlockSpec((8, bn), lambda j: (0, j)),
        out_shape=jax.ShapeDtypeStruct((8, n), F32),
        compiler_params=pltpu.CompilerParams(dimension_semantics=("arbitrary",),
                                             vmem_limit_bytes=VMEM_LIMIT),
        name="mod",
    )(cvec, ada_w, ada_b.reshape(1, n))


def _swap32(x):
    lane = _iota(x.shape, 1)
    return jnp.where((lane & 63) < 32, pltpu.roll(x, 96, 1), pltpu.roll(x, 32, 1))


def _ffn_kernel(*refs, mod_off, proj, rope, final, n_main, n_cast, d, f):
    it = iter(refs)
    x_ref = next(it)
    if n_main is not None:
        xc_ref = next(it)
        is_ctx = pl.program_id(0) >= n_main
    m_ref, nw_ref, w1_ref, w3_ref, w2_ref = (next(it) for _ in range(5))
    if proj:
        n2w_ref, win_ref, wlow_ref, gw_ref, gbias_ref = (next(it) for _ in range(5))
        if rope:
            rowtab_ref, coltab_ref = next(it), next(it)
    if final:
        fnw_ref = next(it)
    cast_in = [next(it) for _ in range(n_cast)]
    xo_ref = next(it)
    if proj:
        (rq_ref, rk_ref, rv_ref, rg_ref, gq_ref, gk_ref, gv_ref, gg_ref,
         gf_ref, gb_ref) = (next(it) for _ in range(10))
    cast_out = [next(it) for _ in range(n_cast)]
    u_ref = next(it)
    for src_ref, dst_ref in zip(cast_in, cast_out):
        dst_ref[...] = src_ref[...].astype(BF16)

    def mod(i):
        return m_ref[:, (mod_off + i) * d:(mod_off + i + 1) * d]

    for rb in range(x_ref.shape[0] // ROW_BLOCK):
        rows = slice(rb * ROW_BLOCK, (rb + 1) * ROW_BLOCK)
        x = x_ref[rows, :]
        if n_main is not None:
            x = jnp.where(is_ctx, xc_ref[rows, :], x)
        h = (_rms(x, nw_ref[...]) * (1.0 + mod(1)) + mod(0)).astype(BF16)
        for k in range(f // FF_CHUNK):
            sl = slice(k * FF_CHUNK, (k + 1) * FF_CHUNK)
            a = _dot(h, w1_ref[:, sl])
            g = _dot(h, w3_ref[:, sl])
            u_ref[rows, sl] = (_silu(a) * g).astype(BF16)
        y = _dot(u_ref[rows, :], w2_ref[...])
        x1 = x + (0.5 * mod(2)) * y

        if final:
            xo_ref[rows, :] = _rms(x1, fnw_ref[...])
        else:
            xo_ref[rows, :] = x1

        if proj:
            h2 = (_rms(x1, n2w_ref[...]) * (1.0 + mod(4)) + mod(3)).astype(BF16)

            def p(lo, hi):
                return _dot(h2, win_ref[:, lo:hi])

            if rope:
                by_row = _iota((GRID_W, 2 * LANE), 1) % LANE < LANE // 2
                g0 = rb * (ROW_BLOCK // GRID_W)
                tab = jnp.concatenate(
                    [jnp.where(by_row, jnp.broadcast_to(rowtab_ref[g0 + g:g0 + g + 1, :], (GRID_W, 2 * LANE)),
                               coltab_ref[...]) for g in range(ROW_BLOCK // GRID_W)], axis=0)
                cos, sin = tab[:, :LANE], tab[:, LANE:]
                if n_main is not None:
                    cos = jnp.where(is_ctx, 1.0, cos)
                    sin = jnp.where(is_ctx, 0.0, sin)

            for base, scale, o_ref in ((C_RQ, RET_DK ** -0.5, rq_ref), (C_RK, 1.0, rk_ref)):
                t = p(base, base + RET_W)
                for hd in range(RET_HEADS):
                    th = t[:, hd * LANE:(hd + 1) * LANE] * scale
                    if rope:
                        th = th * cos + _swap32(th) * sin
                    o_ref[rows, hd * LANE:(hd + 1) * LANE] = th.astype(BF16)
            rv_ref[rows, :] = p(C_RV, C_RV + RET_W).astype(BF16)
            rg_ref[rows, :] = _silu(p(C_RG, C_RG + RET_W)).astype(BF16)
            gq_ref[rows, :] = (p(C_GQ, C_GQ + GLA_QK) * GLA_DK ** -0.5).astype(BF16)
            gk_ref[rows, :] = p(C_GK, C_GK + GLA_QK).astype(BF16)
            gv_ref[rows, :] = p(C_GV, C_GV + GLA_W).astype(BF16)
            gg_ref[rows, :] = _silu(p(C_GG, C_GG + GLA_W)).astype(BF16)
            low = _dot(h2, wlow_ref[...]).astype(BF16)
            z = _dot(low, gw_ref[...]) + gbias_ref[...]
            ls = _log_sigmoid(z) * (1.0 / GLA_TAU)
            gf_ref[rows, :] = ls[:, :GLA_QK]
            gb_ref[rows, :] = ls[:, GLA_QK:]


def _ffn_call(x, m3, tiles_per_seq, nw, w1, w3, w2, *, mod_off, tm, ctx=None, ctx_row=None, proj=None,
              rope=None, final_w=None, cast=(), name):
    t, d = x.shape
    f = w1.shape[1]
    n_main = t // tm
    n_tiles = n_main + (0 if ctx is None else ctx.shape[0] // tm)
    t = n_tiles * tm
    main = lambda i: jnp.minimum(i, n_main - 1)
    tile = lambda w: pl.BlockSpec((tm, w), lambda i: (i, 0))
    in_specs = [pl.BlockSpec((tm, d), lambda i: (main(i), 0))]
    args = [x]
    if ctx is None:
        row_of_tile = lambda i: i // tiles_per_seq
    else:
        row_of_tile = lambda i: jnp.where(i >= n_main, ctx_row, i // tiles_per_seq)
        in_specs.append(pl.BlockSpec((tm, d), lambda i: (jnp.maximum(i - n_main, 0), 0)))
        args.append(ctx)
    in_specs += [pl.BlockSpec((None, 1, N_MOD * d), lambda i: (row_of_tile(i), 0, 0)),
                 _resident((1, d)), _resident((d, f)), _resident((d, f)), _resident((f, d))]
    args += [m3, nw.reshape(1, d), w1, w3, w2]
    if proj is not None:
        n2w, win, wlow, gw, gbias = proj
        in_specs += [_resident((1, d)), _resident((d, C_LOW))] + [_resident(a.shape) for a in (wlow, gw, gbias)]
        args += [n2w.reshape(1, d), win, wlow, gw, gbias]
        if rope is not None:
            rowtab, coltab = rope
            in_specs += [pl.BlockSpec((tm // GRID_W, 2 * LANE), lambda i: (main(i) % tiles_per_seq, 0)),
                         _resident(coltab.shape)]
            args += [rowtab, coltab]
    if final_w is not None:
        in_specs.append(_resident((1, d)))
        args.append(final_w.reshape(1, d))
    out_specs = [tile(d)]
    out_shape = [jax.ShapeDtypeStruct((t, d), F32)]
    if proj is not None:
        for w, dt in ((RET_W, BF16),) * 4 + ((GLA_QK, BF16),) * 2 + ((GLA_W, BF16),) * 2 + ((GLA_QK, F32),) * 2:
            out_specs.append(tile(w))
            out_shape.append(jax.ShapeDtypeStruct((t, w), dt))
    for w in cast:
        rows = next(r for r in range(BF16_ROWS, w.shape[0] + 1, BF16_ROWS)
                    if w.shape[0] % r == 0 and w.shape[0] // r <= n_main)
        spec = pl.BlockSpec((rows, w.shape[1]), lambda i, last=w.shape[0] // rows - 1: (jnp.minimum(i, last), 0))
        in_specs.append(spec)
        args.append(w)
        out_specs.append(spec)
        out_shape.append(jax.ShapeDtypeStruct(w.shape, BF16))
    kern = functools.partial(_ffn_kernel, mod_off=mod_off, proj=proj is not None,
                             rope=rope is not None, final=final_w is not None,
                             n_main=None if ctx is None else n_main, n_cast=len(cast), d=d, f=f)
    return pl.pallas_call(
        kern, grid=(n_tiles,), in_specs=in_specs, out_specs=out_specs, out_shape=out_shape,
        scratch_shapes=[pltpu.VMEM((tm, f), BF16)],
        compiler_params=pltpu.CompilerParams(dimension_semantics=("arbitrary",),
                                             vmem_limit_bytes=VMEM_LIMIT),
        name=name,
    )(*args)


def _head_lanes(shape, hd):
    lane = _iota(shape, len(shape) - 1)
    return (lane >= 64) if hd % 2 else (lane < 64)


def _state_kernel(*refs, batch, cps):
    it = iter(refs)
    per_batch = lambda: [next(it) for _ in range(batch)]
    rkf_ref, rvf_ref, gkf_ref, gvf_ref, gf_ref = (per_batch() for _ in range(5))
    rkb_ref, rvb_ref, gkb_ref, gvb_ref, gb_ref = (per_batch() for _ in range(5))
    dec_ref = next(it)
    irf_ref, irb_ref, igf_ref, igb_ref = (next(it) for _ in range(4))
    orf_ref, orb_ref, ogf_ref, ogb_ref = (next(it) for _ in range(4))
    frf_ref, frb_ref, fgf_ref, fgb_ref = (next(it) for _ in range(4))
    srf, srb, sgf, sgb = (next(it) for _ in range(4))
    n = CHUNK

    @pl.when(pl.program_id(0) == 0)
    def _():
        srf[...] = irf_ref[...]
        srb[...] = irb_ref[...]
        sgf[...] = igf_ref[...]
        sgb[...] = igb_ref[...]

    row = _iota((n, LANE), 0).astype(F32)
    lg = _log_sigmoid(dec_ref[...])
    ri = _iota((n, n), 0)
    ci = _iota((n, n), 1)
    after = (ci > ri).astype(BF16)
    before = (ci < ri).astype(BF16)

    def gla(b, rows, g_ref, k_ref, v_ref, tri, edge, st):
        g = g_ref[b][rows, :]
        e = _dot_select(tri, g)
        tot = e[edge:edge + 1, :] + g[edge:edge + 1, :]
        kd = (k_ref[b][rows, :].astype(F32) * jnp.exp(e)).astype(BF16)
        for hd in range(GLA_HEADS):
            pr = slice((hd // 2) * LANE, (hd // 2 + 1) * LANE)
            upd = _dot_tn(v_ref[b][rows, hd * LANE:(hd + 1) * LANE], kd[:, pr])
            upd = jnp.where(_head_lanes(upd.shape, hd), upd, 0.0)
            st[b, hd] = st[b, hd] * jnp.exp(tot[:, pr]) + upd

    for j in range(cps):
        jb = cps - 1 - j
        fr = slice(j * n, (j + 1) * n)
        br = slice(jb * n, (jb + 1) * n)
        orf_ref[:, j] = srf[...].astype(BF16)
        ogf_ref[:, j] = sgf[...].astype(BF16)
        orb_ref[:, jb] = srb[...].astype(BF16)
        ogb_ref[:, jb] = sgb[...].astype(BF16)
        for b in range(batch):
            for hd in range(RET_HEADS):
                sl = slice(hd * LANE, (hd + 1) * LANE)
                lgf = lg[0:1, sl]
                lgb = lg[1:2, sl]
                kf = (rkf_ref[b][fr, sl].astype(F32) * jnp.exp((n - 1.0 - row) * lgf)).astype(BF16)
                srf[b, hd] = srf[b, hd] * jnp.exp(n * lgf) + _dot_tn(kf, rvf_ref[b][fr, sl])
                kb = (rkb_ref[b][br, sl].astype(F32) * jnp.exp(row * lgb)).astype(BF16)
                srb[b, hd] = srb[b, hd] * jnp.exp(n * lgb) + _dot_tn(kb, rvb_ref[b][br, sl])
            gla(b, fr, gf_ref, gkf_ref, gvf_ref, after, 0, sgf)
            gla(b, br, gb_ref, gkb_ref, gvb_ref, before, n - 1, sgb)

    frf_ref[...] = srf[...]
    frb_ref[...] = srb[...]
    fgf_ref[...] = sgf[...]
    fgb_ref[...] = sgb[...]


def _states(rk, rv, gk, gv, gf, gb, dec, init, batch, first_chunk, nc):
    cps = 2 if nc % 2 == 0 and first_chunk % 2 == 0 else 1
    steps = nc // cps

    def per_batch(w, backward):
        at = (lambda c: steps - 1 - c) if backward else (lambda c: c)
        return [pl.BlockSpec((cps * CHUNK, w), lambda c, b=b: ((first_chunk + b * nc) // cps + at(c), 0))
                for b in range(batch)]

    fwd = lambda w: per_batch(w, False)
    bwd = lambda w: per_batch(w, True)
    st_shape = (batch, RET_HEADS, LANE, LANE)
    init_spec = pl.BlockSpec(st_shape, lambda c: (0, 0, 0, 0))
    chunk_shape = (batch, cps, RET_HEADS, LANE, LANE)
    of_spec = pl.BlockSpec(chunk_shape, lambda c: (0, c, 0, 0, 0))
    ob_spec = pl.BlockSpec(chunk_shape, lambda c: (0, steps - 1 - c, 0, 0, 0))
    per_chunk = jax.ShapeDtypeStruct((batch, nc, RET_HEADS, LANE, LANE), BF16)
    final = jax.ShapeDtypeStruct(st_shape, F32)
    rep = lambda a: [a] * batch
    return pl.pallas_call(
        functools.partial(_state_kernel, batch=batch, cps=cps),
        grid=(steps,),
        in_specs=fwd(RET_W) + fwd(RET_W) + fwd(GLA_QK) + fwd(GLA_W) + fwd(GLA_QK)
        + bwd(RET_W) + bwd(RET_W) + bwd(GLA_QK) + bwd(GLA_W) + bwd(GLA_QK)
        + [pl.BlockSpec(dec.shape, lambda c: (0, 0))] + [init_spec] * 4,
        out_specs=[of_spec, ob_spec, of_spec, ob_spec] + [init_spec] * 4,
        out_shape=[per_chunk] * 4 + [final] * 4,
        scratch_shapes=[pltpu.VMEM(st_shape, F32)] * 4,
        compiler_params=pltpu.CompilerParams(dimension_semantics=("arbitrary",),
                                             vmem_limit_bytes=VMEM_LIMIT),
        name="states",
    )(*rep(rk), *rep(rv), *rep(gk), *rep(gv), *rep(gf), *rep(rk), *rep(rv), *rep(gk), *rep(gv), *rep(gb),
      dec, *init)


def _block_row(x, parent, r):
    n, w = x.shape
    if parent == n:
        return jnp.broadcast_to(x[r:r + 1, :], (n, w))
    x3 = x.reshape(n // parent, parent, w)
    return jnp.broadcast_to(x3[:, r:r + 1, :], x3.shape).reshape(n, w)


def _gla_levels(q, k_even, k_odd, gf, gb, bf, bb):
    n = q.shape[0]
    row = _iota(q.shape, 0)

    def factors(u_exp, w_exp):
        ew = jnp.exp2(w_exp).astype(BF16)
        return (q * jnp.exp2(u_exp).astype(BF16), k_even * ew, k_odd * ew)

    levels = []
    s = n // 2
    while s >= 2:
        if s >= 4:
            last_of_first = _block_row(bf, 2 * s, s - 1)
            first_of_second = _block_row(bb, 2 * s, s)
        else:
            upper = (row & 4) != 0
            last_of_first = jnp.where(upper, _block_row(bf, 8, 5), _block_row(bf, 8, 1))
            first_of_second = jnp.where(upper, _block_row(bb, 8, 6), _block_row(bb, 8, 2))
        df = bf - last_of_first
        db = bb - first_of_second
        levels.append((2 * s, [factors(jnp.minimum(df, db), -jnp.maximum(df, db))]))
        s //= 2
    odd = (row & 1) == 1
    qa = q * jnp.where(odd, jnp.exp2(gf), 2.0).astype(BF16)
    qb = q * jnp.where(odd, 2.0, jnp.exp2(gb)).astype(BF16)
    zero = jnp.zeros_like(k_even)
    ev = lambda x: jnp.where(odd, zero, x)
    od = lambda x: jnp.where(odd, x, zero)
    levels.append((2, [(qa, ev(k_even), ev(k_odd)), (qb, od(k_even), od(k_odd))]))
    return levels


def _mix_kernel(rq_ref, rk_ref, rv_ref, rg_ref, gq_ref, gk_ref, gv_ref, gg_ref, gf_ref, gb_ref,
                srf_ref, srb_ref, sgf_ref, sgb_ref, dec_ref, rnw_ref, gnw_ref,
                x_ref, m_ref, wout_ref, o_ref, mix_ref, decay_ref, qdec_ref, *, d):
    n = CHUNK
    half = n // 2
    ri = _iota((n, n), 0)
    ci = _iota((n, n), 1)
    step = pl.program_id(0)
    cur = step % 2

    @pl.when(step == 0)
    def _():
        mix_ref[1] = jnp.zeros(mix_ref.shape[1:], BF16)
        lg = _log_sigmoid(dec_ref[...])
        dist = (ri - ci).astype(F32)
        row = _iota((n, LANE), 0).astype(F32)
        for hd in range(RET_HEADS):
            sl = slice(hd * LANE, (hd + 1) * LANE)
            lgf = lg[0:1, sl]
            lgb = lg[1:2, sl]
            decay_ref[hd] = jnp.exp(jnp.where(dist > 0, dist * lgf[:, 0:1],
                                              jnp.where(dist < 0, -dist * lgb[:, 0:1], jnp.log(2.0))))
            qdec_ref[hd] = jnp.exp((row + 1.0) * lgf)
            qdec_ref[RET_HEADS + hd] = jnp.exp((n - row) * lgb)

    for hd in range(RET_HEADS):
        sl = slice(hd * LANE, (hd + 1) * LANE)
        q = rq_ref[:, sl]
        p = (_dot_nt(q, rk_ref[:, sl]) * decay_ref[hd]).astype(BF16)
        qf = q.astype(F32)
        qs = jnp.concatenate([(qf * qdec_ref[hd]).astype(BF16),
                              (qf * qdec_ref[RET_HEADS + hd]).astype(BF16)], axis=1)
        st = jnp.concatenate([srf_ref[hd], srb_ref[hd]], axis=0)
        o = _dot(p, rv_ref[:, sl]) + _dot(qs, st)
        mu = jnp.mean(o, axis=-1, keepdims=True)
        oc = o - mu
        var = jnp.mean(oc * oc, axis=-1, keepdims=True)
        r = oc * lax.rsqrt(var + EPS) * rnw_ref[:, sl] * rg_ref[:, sl].astype(F32)
        mix_ref[cur, :, sl] = r.astype(BF16)

    gf = gf_ref[...] * LOG2E
    gb = gb_ref[...] * LOG2E
    bf = _dot_select((ci <= ri).astype(BF16), gf)
    bb = _dot_select((ci >= ri).astype(BF16), gb)
    q = gq_ref[...]
    k = gk_ref[...]
    even_head = (_iota(k.shape, 1) & GLA_DK) == 0
    no_k = jnp.zeros_like(k)
    o_ref[...] = x_ref[...] + m_ref[:, 5 * d:6 * d] * _dot(mix_ref[1 - cur], wout_ref[...])
    levels = _gla_levels(q, jnp.where(even_head, k, no_k), jnp.where(even_head, no_k, k), gf, gb, bf, bb)
    qsf = q * jnp.exp2(bf).astype(BF16)
    qsb = q * jnp.exp2(bb).astype(BF16)
    code = _iota((half, n), 0) ^ (_iota((half, n), 1) & (half - 1))
    for pair in range(GLA_HEADS // 2):
        pr = slice(pair * LANE, (pair + 1) * LANE)
        near = [None, None]
        far = [None, None]
        for size, blocks in levels:
            for rb in range(2):
                kb = rb if size < n else 1 - rb
                rows = slice(rb * half, (rb + 1) * half)
                keys = slice(kb * half, (kb + 1) * half)
                u = jnp.concatenate([blk[0][rows, pr] for blk in blocks], axis=1)
                w = jnp.concatenate([jnp.concatenate([blk[j][keys, pr] for blk in blocks], axis=1)
                                     for j in (1, 2)], axis=0)
                t = _dot_nt(u, w)
                if size == n:
                    far[rb] = t
                else:
                    near[rb] = t if near[rb] is None else jnp.where(code < size, t, near[rb])
        qcat = jnp.concatenate([qsf[:, pr], qsb[:, pr]], axis=1)
        for j in range(2):
            hd = 2 * pair + j
            sl = slice(hd * LANE, (hd + 1) * LANE)
            mine = slice(j * half, (j + 1) * half)
            p = jnp.concatenate([jnp.concatenate([near[0][:, mine], far[0][:, mine]], axis=1),
                                 jnp.concatenate([far[1][:, mine], near[1][:, mine]], axis=1)], axis=0)
            st = jnp.concatenate([sgf_ref[hd], sgb_ref[hd]], axis=1)
            o = _dot(p.astype(BF16), gv_ref[:, sl]) + _dot_nt(qcat, st)
            ms = jnp.mean(o * o, axis=-1, keepdims=True)
            r = o * lax.rsqrt(ms + EPS) * gnw_ref[:, sl] * gg_ref[:, sl].astype(F32)
            mix_ref[cur, :, RET_W + hd * LANE:RET_W + (hd + 1) * LANE] = r.astype(BF16)


def _mix_call(mix_in, states, dec, rnw, gnw, x1, m3, wout, batch, nc):
    n_chunks = batch * nc
    t, d = n_chunks * CHUNK, x1.shape[1]
    mixed = lambda i: jnp.minimum(i, n_chunks - 1)
    done = lambda i: jnp.maximum(i - 1, 0)
    tile = lambda w: pl.BlockSpec((CHUNK, w), lambda i: (mixed(i), 0))
    st_spec = pl.BlockSpec((None, None, RET_HEADS, LANE, LANE),
                           lambda i: (mixed(i) // nc, mixed(i) % nc, 0, 0, 0))
    const = lambda shape: pl.BlockSpec(shape, lambda i: (0,) * len(shape))
    widths = (RET_W,) * 4 + (GLA_QK,) * 2 + (GLA_W,) * 2 + (GLA_QK,) * 2
    return pl.pallas_call(
        functools.partial(_mix_kernel, d=d),
        grid=(n_chunks + 1,),
        in_specs=[tile(w) for w in widths] + [st_spec] * 4
        + [const(dec.shape), const((1, RET_W)), const((1, GLA_W)),
           pl.BlockSpec((CHUNK, d), lambda i: (done(i), 0)),
           pl.BlockSpec((None, 1, N_MOD * d), lambda i: (done(i) // nc, 0, 0)),
           const(wout.shape)],
        out_specs=pl.BlockSpec((CHUNK, d), lambda i: (done(i), 0)),
        out_shape=jax.ShapeDtypeStruct((t, d), F32),
        scratch_shapes=[pltpu.VMEM((2, CHUNK, RET_W + GLA_W), BF16),
                        pltpu.VMEM((RET_HEADS, CHUNK, CHUNK), F32),
                        pltpu.VMEM((2 * RET_HEADS, CHUNK, LANE), F32)],
        compiler_params=pltpu.CompilerParams(dimension_semantics=("arbitrary",),
                                             vmem_limit_bytes=VMEM_LIMIT),
        name="mix",
    )(*mix_in, *states, dec, rnw.reshape(1, RET_W), gnw.reshape(1, GLA_W), x1, m3, wout)


def _rope_tables(n_tok):
    freqs = ROPE_BASE ** (-jnp.arange(RET_DK // 4, dtype=F32) / (RET_DK // 4))

    def table(n_pos, first_half):
        ang = jnp.arange(n_pos, dtype=F32)[:, None] * freqs
        zero = jnp.zeros((n_pos, LANE // 2), F32)
        cos = jnp.concatenate([jnp.cos(ang)] * 2, axis=-1)
        sin = jnp.concatenate([-jnp.sin(ang), jnp.sin(ang)], axis=-1)
        halves = (cos, zero, sin, zero) if first_half else (zero, cos, zero, sin)
        return jnp.concatenate(halves, axis=-1)

    return table(n_tok // GRID_W, True), table(GRID_W, False)


def _pack_w_in(w_in):
    low = w_in[:, C_LOW:]
    pad = jnp.zeros((w_in.shape[0], LANE - low.shape[1]), w_in.dtype)
    return w_in.astype(BF16), jnp.concatenate([low, pad], axis=1).astype(BF16)


def _pack_gate(w_f, b_f, w_b, b_b):
    gw = jnp.zeros((LANE, 2 * GLA_QK), F32)
    gw = gw.at[:GLA_RANK, :GLA_QK].set(w_f).at[GLA_RANK:2 * GLA_RANK, GLA_QK:].set(w_b)
    return gw.astype(BF16), jnp.concatenate([b_f, b_b]).reshape(1, 2 * GLA_QK)


def kernel(x, c, ctx, c_ctx, ada_w, ada_b, norm1_w, ffn1_w1, ffn1_w3, ffn1_w2, norm2_w, w_in,
           ret_decay_f, ret_decay_b, ret_norm_w, gla_gate_w_f, gla_gate_b_f, gla_gate_w_b, gla_gate_b_b,
           gla_norm_w, w_out, norm3_w, ffn2_w1, ffn2_w3, ffn2_w2, final_norm_w):
    batch, n_tok, d = x.shape
    n_ctx = ctx.shape[1]
    depth = ada_w.shape[0]
    assert depth == 1 and batch + 1 <= 8
    assert n_tok % FFN_TILE == 0 and n_tok % CHUNK == 0 and n_ctx % CHUNK == 0
    assert (batch * n_ctx) % FFN_TILE == 0

    cvec = jnp.zeros((8, d), F32).at[:batch].set(c).at[batch].set(c_ctx)
    m3 = _modulation(cvec, ada_w[0], ada_b[0]).reshape(8, 1, N_MOD * d)

    bf = lambda w: w.astype(BF16)
    gw, gbias = _pack_gate(gla_gate_w_f[0], gla_gate_b_f[0], gla_gate_w_b[0], gla_gate_b_b[0])
    proj = (norm2_w[0], *_pack_w_in(w_in[0]), gw, gbias)
    f1 = (norm1_w[0], bf(ffn1_w1[0]), bf(ffn1_w3[0]), bf(ffn1_w2[0]))
    rowtab, coltab = _rope_tables(n_tok)
    dec = jnp.zeros((8, RET_W), F32)
    dec = dec.at[0].set(jnp.repeat(ret_decay_f[0], LANE)).at[1].set(jnp.repeat(ret_decay_b[0], LANE))

    tiles_per_seq = n_tok // FFN_TILE
    first = _ffn_call(x.reshape(batch * n_tok, d), m3, tiles_per_seq, *f1, mod_off=0, tm=FFN_TILE,
                      ctx=ctx.reshape(batch * n_ctx, d), ctx_row=batch, proj=proj, rope=(rowtab, coltab),
                      cast=(ffn2_w1[0], ffn2_w3[0], ffn2_w2[0], w_out[0]), name="ffn_in")
    x1, mix_in, (w1b, w3b, w2b, woutb) = first[0], first[1:11], first[11:]
    scan_in = (mix_in[1], mix_in[2], mix_in[5], mix_in[6], mix_in[8], mix_in[9])
    zero = jnp.zeros((batch, RET_HEADS, LANE, LANE), F32)
    nc, nc_ctx = n_tok // CHUNK, n_ctx // CHUNK
    ctx_states = _states(*scan_in, dec, (zero,) * 4, batch, batch * nc, nc_ctx)[4:]
    states = _states(*scan_in, dec, ctx_states, batch, 0, nc)[:4]
    x2 = _mix_call(mix_in, states, dec, ret_norm_w[0], gla_norm_w[0], x1, m3, woutb, batch, nc)
    out = _ffn_call(x2, m3, tiles_per_seq, norm3_w[0], w1b, w3b, w2b, mod_off=6, tm=FFN_TILE,
                    final_w=final_norm_w, name="ffn_out")[0]
    return out.reshape(batch, n_tok, d)
```

```python
import functools

import jax
import jax.numpy as jnp
from jax import lax
from jax.experimental import pallas as pl
from jax.experimental.pallas import tpu as pltpu

F32 = jnp.float32
BF16 = jnp.bfloat16

EPS = 1e-6
LOG2E = 1.4426950408889634
N_MOD = 9
GRID_W = 64
ROPE_BASE = 10000.0
RET_HEADS = 4
RET_DK = 128
RET_DV = 128
GLA_HEADS = 4
GLA_DK = 64
GLA_DV = 128
GLA_RANK = 16
GLA_TAU = 16.0
RET_W = RET_HEADS * RET_DV
GLA_W = GLA_HEADS * GLA_DV
GLA_QK = GLA_HEADS * GLA_DK

LANE = 128
BF16_ROWS = 16
CHUNK = 256
FFN_TILE = 512
ROW_BLOCK = 256
FF_CHUNK = 256
VMEM_LIMIT = 60 * 1024 * 1024

C_RQ, C_RK, C_RV, C_RG = 0, 512, 1024, 1536
C_GQ, C_GK, C_GV, C_GG = 2048, 2304, 2560, 3072
C_LOW = 3584


def _silu(x):
    return x * (1.0 / (1.0 + jnp.exp(-x)))


def _log_sigmoid(z):
    return jnp.minimum(z, 0.0) - jnp.log(1.0 + jnp.exp(-jnp.abs(z)))


def _rms(x, w):
    return x * lax.rsqrt(jnp.mean(x * x, axis=-1, keepdims=True) + EPS) * w


def _dot(a, b):
    return jnp.dot(a, b, preferred_element_type=F32)


def _dot_nt(a, b):
    return lax.dot_general(a, b, (((1,), (1,)), ((), ())), preferred_element_type=F32)


def _dot_tn(a, b):
    return lax.dot_general(a, b, (((0,), (0,)), ((), ())), preferred_element_type=F32)


def _dot_select(sel, x):
    hi = x.astype(BF16)
    lo = (x - hi.astype(F32)).astype(BF16)
    return _dot(sel, hi) + _dot(sel, lo)


def _iota(shape, dim):
    return lax.broadcasted_iota(jnp.int32, shape, dim)


def _resident(shape):
    nd = len(shape)
    return pl.BlockSpec(shape, lambda *_: (0,) * nd, pipeline_mode=pl.Buffered(1))


def _cast_jobs(arrays, n_steps):
    specs, shapes = [], []
    for w in arrays:
        rows = next(r for r in range(BF16_ROWS, w.shape[0] + 1, BF16_ROWS)
                    if w.shape[0] % r == 0 and w.shape[0] // r <= n_steps)
        specs.append(pl.BlockSpec((rows, w.shape[1]),
                                  lambda i, last=w.shape[0] // rows - 1: (jnp.minimum(i, last), 0)))
        shapes.append(jax.ShapeDtypeStruct(w.shape, BF16))
    return specs, shapes


def _run_cast_jobs(src_refs, dst_refs):
    for src_ref, dst_ref in zip(src_refs, dst_refs):
        dst_ref[...] = src_ref[...].astype(BF16)


def _mod_kernel(*refs, n_cast):
    c_ref, w_ref, b_ref = refs[:3]
    o_ref = refs[3 + n_cast]
    cond = _silu(c_ref[...]).astype(BF16)
    o_ref[...] = _dot(cond, w_ref[...].astype(BF16)) + b_ref[...]
    _run_cast_jobs(refs[3:3 + n_cast], refs[4 + n_cast:])


def _modulation(cvec, ada_w, ada_b, cast=()):
    d, n = ada_w.shape
    bn = n // 8 if n % (8 * LANE) == 0 else d
    cast_specs, cast_shapes = _cast_jobs(cast, n // bn)
    return pl.pallas_call(
        functools.partial(_mod_kernel, n_cast=len(cast)),
        grid=(n // bn,),
        in_specs=[pl.BlockSpec((8, d), lambda j: (0, 0)),
                  pl.BlockSpec((d, bn), lambda j: (0, j)),
                  pl.BlockSpec((1, bn), lambda j: (0, j))] + cast_specs,
        out_specs=[pl.BlockSpec((8, bn), lambda j: (0, j))] + cast_specs,
        out_shape=[jax.ShapeDtypeStruct((8, n), F32)] + cast_shapes,
        compiler_params=pltpu.CompilerParams(dimension_semantics=("arbitrary",),
                                             vmem_limit_bytes=VMEM_LIMIT),
        name="mod",
    )(cvec, ada_w, ada_b.reshape(1, n), *cast)


def _swap32(x):
    lane = _iota(x.shape, 1)
    return jnp.where((lane & 63) < 32, pltpu.roll(x, 96, 1), pltpu.roll(x, 32, 1))


def _ffn_kernel(*refs, mod_off, proj, rope, final, n_main, n_cast, d, f):
    it = iter(refs)
    x_ref = next(it)
    if n_main is not None:
        xc_ref = next(it)
        is_ctx = pl.program_id(0) >= n_main
    m_ref, nw_ref, w1_ref, w3_ref, w2_ref = (next(it) for _ in range(5))
    if proj:
        n2w_ref, win_ref, wlow_ref, gw_ref, gbias_ref = (next(it) for _ in range(5))
        if rope:
            rowtab_ref, coltab_ref = next(it), next(it)
    if final:
        fnw_ref = next(it)
    cast_in = [next(it) for _ in range(n_cast)]
    xo_ref = next(it)
    if proj:
        (rq_ref, rk_ref, rv_ref, rg_ref, gq_ref, gk_ref, gv_ref, gg_ref,
         gf_ref, gb_ref) = (next(it) for _ in range(10))
    cast_out = [next(it) for _ in range(n_cast)]
    u_ref = next(it)
    _run_cast_jobs(cast_in, cast_out)

    def mod(i):
        return m_ref[:, (mod_off + i) * d:(mod_off + i + 1) * d]

    for rb in range(x_ref.shape[0] // ROW_BLOCK):
        rows = slice(rb * ROW_BLOCK, (rb + 1) * ROW_BLOCK)
        x = x_ref[rows, :]
        if n_main is not None:
            x = jnp.where(is_ctx, xc_ref[rows, :], x)
        h = (_rms(x, nw_ref[...]) * (1.0 + mod(1)) + mod(0)).astype(BF16)
        for k in range(f // FF_CHUNK):
            sl = slice(k * FF_CHUNK, (k + 1) * FF_CHUNK)
            a = _dot(h, w1_ref[:, sl])
            g = _dot(h, w3_ref[:, sl])
            u_ref[rows, sl] = (_silu(a) * g).astype(BF16)
        y = _dot(u_ref[rows, :], w2_ref[...])
        x1 = x + (0.5 * mod(2)) * y

        if final:
            xo_ref[rows, :] = _rms(x1, fnw_ref[...])
        else:
            xo_ref[rows, :] = x1

        if proj:
            h2 = (_rms(x1, n2w_ref[...]) * (1.0 + mod(4)) + mod(3)).astype(BF16)

            def p(lo, hi):
                return _dot(h2, win_ref[:, lo:hi])

            if rope:
                by_row = _iota((GRID_W, 2 * LANE), 1) % LANE < LANE // 2
                g0 = rb * (ROW_BLOCK // GRID_W)
                tab = jnp.concatenate(
                    [jnp.where(by_row, jnp.broadcast_to(rowtab_ref[g0 + g:g0 + g + 1, :], (GRID_W, 2 * LANE)),
                               coltab_ref[...]) for g in range(ROW_BLOCK // GRID_W)], axis=0)
                cos, sin = tab[:, :LANE], tab[:, LANE:]
                if n_main is not None:
                    cos = jnp.where(is_ctx, 1.0, cos)
                    sin = jnp.where(is_ctx, 0.0, sin)

            for base, scale, o_ref in ((C_RQ, RET_DK ** -0.5, rq_ref), (C_RK, 1.0, rk_ref)):
                t = p(base, base + RET_W)
                for hd in range(RET_HEADS):
                    th = t[:, hd * LANE:(hd + 1) * LANE] * scale
                    if rope:
                        th = th * cos + _swap32(th) * sin
                    o_ref[rows, hd * LANE:(hd + 1) * LANE] = th.astype(BF16)
            rv_ref[rows, :] = p(C_RV, C_RV + RET_W).astype(BF16)
            rg_ref[rows, :] = _silu(p(C_RG, C_RG + RET_W)).astype(BF16)
            gq_ref[rows, :] = (p(C_GQ, C_GQ + GLA_QK) * GLA_DK ** -0.5).astype(BF16)
            gk_ref[rows, :] = p(C_GK, C_GK + GLA_QK).astype(BF16)
            gv_ref[rows, :] = p(C_GV, C_GV + GLA_W).astype(BF16)
            gg_ref[rows, :] = _silu(p(C_GG, C_GG + GLA_W)).astype(BF16)
            low = _dot(h2, wlow_ref[...]).astype(BF16)
            z = _dot(low, gw_ref[...]) + gbias_ref[...]
            ls = _log_sigmoid(z) * (1.0 / GLA_TAU)
            gf_ref[rows, :] = ls[:, :GLA_QK]
            gb_ref[rows, :] = ls[:, GLA_QK:]


def _ffn_call(x, m3, tiles_per_seq, nw, w1, w3, w2, *, mod_off, tm, ctx=None, ctx_row=None, proj=None,
              rope=None, final_w=None, cast=(), name):
    t, d = x.shape
    f = w1.shape[1]
    n_main = t // tm
    n_tiles = n_main + (0 if ctx is None else ctx.shape[0] // tm)
    t = n_tiles * tm
    main = lambda i: jnp.minimum(i, n_main - 1)
    tile = lambda w: pl.BlockSpec((tm, w), lambda i: (i, 0))
    in_specs = [pl.BlockSpec((tm, d), lambda i: (main(i), 0))]
    args = [x]
    if ctx is None:
        row_of_tile = lambda i: i // tiles_per_seq
    else:
        row_of_tile = lambda i: jnp.where(i >= n_main, ctx_row, i // tiles_per_seq)
        in_specs.append(pl.BlockSpec((tm, d), lambda i: (jnp.maximum(i - n_main, 0), 0)))
        args.append(ctx)
    in_specs += [pl.BlockSpec((None, 1, N_MOD * d), lambda i: (row_of_tile(i), 0, 0)),
                 _resident((1, d)), _resident((d, f)), _resident((d, f)), _resident((f, d))]
    args += [m3, nw.reshape(1, d), w1, w3, w2]
    if proj is not None:
        n2w, win, wlow, gw, gbias = proj
        in_specs += [_resident((1, d)), _resident((d, C_LOW))] + [_resident(a.shape) for a in (wlow, gw, gbias)]
        args += [n2w.reshape(1, d), win, wlow, gw, gbias]
        if rope is not None:
            rowtab, coltab = rope
            in_specs += [pl.BlockSpec((tm // GRID_W, 2 * LANE), lambda i: (main(i) % tiles_per_seq, 0)),
                         _resident(coltab.shape)]
            args += [rowtab, coltab]
    if final_w is not None:
        in_specs.append(_resident((1, d)))
        args.append(final_w.reshape(1, d))
    out_specs = [tile(d)]
    out_shape = [jax.ShapeDtypeStruct((t, d), F32)]
    if proj is not None:
        for w, dt in ((RET_W, BF16),) * 4 + ((GLA_QK, BF16),) * 2 + ((GLA_W, BF16),) * 2 + ((GLA_QK, F32),) * 2:
            out_specs.append(tile(w))
            out_shape.append(jax.ShapeDtypeStruct((t, w), dt))
    cast_specs, cast_shapes = _cast_jobs(cast, n_main)
    in_specs += cast_specs
    args += list(cast)
    out_specs += cast_specs
    out_shape += cast_shapes
    kern = functools.partial(_ffn_kernel, mod_off=mod_off, proj=proj is not None,
                             rope=rope is not None, final=final_w is not None,
                             n_main=None if ctx is None else n_main, n_cast=len(cast), d=d, f=f)
    return pl.pallas_call(
        kern, grid=(n_tiles,), in_specs=in_specs, out_specs=out_specs, out_shape=out_shape,
        scratch_shapes=[pltpu.VMEM((tm, f), BF16)],
        compiler_params=pltpu.CompilerParams(dimension_semantics=("arbitrary",),
                                             vmem_limit_bytes=VMEM_LIMIT),
        name=name,
    )(*args)


def _head_lanes(shape, hd):
    lane = _iota(shape, len(shape) - 1)
    return (lane >= 64) if hd % 2 else (lane < 64)


def _state_kernel(*refs, batch, cps):
    it = iter(refs)
    per_batch = lambda: [next(it) for _ in range(batch)]
    rkf_ref, rvf_ref, gkf_ref, gvf_ref, gf_ref = (per_batch() for _ in range(5))
    rkb_ref, rvb_ref, gkb_ref, gvb_ref, gb_ref = (per_batch() for _ in range(5))
    dec_ref = next(it)
    irf_ref, irb_ref, igf_ref, igb_ref = (next(it) for _ in range(4))
    orf_ref, orb_ref, ogf_ref, ogb_ref = (next(it) for _ in range(4))
    frf_ref, frb_ref, fgf_ref, fgb_ref = (next(it) for _ in range(4))
    srf, srb, sgf, sgb = (next(it) for _ in range(4))
    n = CHUNK

    @pl.when(pl.program_id(0) == 0)
    def _():
        srf[...] = irf_ref[...]
        srb[...] = irb_ref[...]
        sgf[...] = igf_ref[...]
        sgb[...] = igb_ref[...]

    row = _iota((n, LANE), 0).astype(F32)
    lg = _log_sigmoid(dec_ref[...])
    ri = _iota((n, n), 0)
    ci = _iota((n, n), 1)
    after = (ci > ri).astype(BF16)
    before = (ci < ri).astype(BF16)

    def gla(b, rows, g_ref, k_ref, v_ref, tri, edge, st):
        g = g_ref[b][rows, :]
        e = _dot_select(tri, g)
        tot = e[edge:edge + 1, :] + g[edge:edge + 1, :]
        kd = (k_ref[b][rows, :].astype(F32) * jnp.exp(e)).astype(BF16)
        for hd in range(GLA_HEADS):
            pr = slice((hd // 2) * LANE, (hd // 2 + 1) * LANE)
            upd = _dot_tn(v_ref[b][rows, hd * LANE:(hd + 1) * LANE], kd[:, pr])
            upd = jnp.where(_head_lanes(upd.shape, hd), upd, 0.0)
            st[b, hd] = st[b, hd] * jnp.exp(tot[:, pr]) + upd

    for j in range(cps):
        jb = cps - 1 - j
        fr = slice(j * n, (j + 1) * n)
        br = slice(jb * n, (jb + 1) * n)
        orf_ref[:, j] = srf[...].astype(BF16)
        ogf_ref[:, j] = sgf[...].astype(BF16)
        orb_ref[:, jb] = srb[...].astype(BF16)
        ogb_ref[:, jb] = sgb[...].astype(BF16)
        for b in range(batch):
            for hd in range(RET_HEADS):
                sl = slice(hd * LANE, (hd + 1) * LANE)
                lgf = lg[0:1, sl]
                lgb = lg[1:2, sl]
                kf = (rkf_ref[b][fr, sl].astype(F32) * jnp.exp((n - 1.0 - row) * lgf)).astype(BF16)
                srf[b, hd] = srf[b, hd] * jnp.exp(n * lgf) + _dot_tn(kf, rvf_ref[b][fr, sl])
                kb = (rkb_ref[b][br, sl].astype(F32) * jnp.exp(row * lgb)).astype(BF16)
                srb[b, hd] = srb[b, hd] * jnp.exp(n * lgb) + _dot_tn(kb, rvb_ref[b][br, sl])
            gla(b, fr, gf_ref, gkf_ref, gvf_ref, after, 0, sgf)
            gla(b, br, gb_ref, gkb_ref, gvb_ref, before, n - 1, sgb)

    frf_ref[...] = srf[...]
    frb_ref[...] = srb[...]
    fgf_ref[...] = sgf[...]
    fgb_ref[...] = sgb[...]


def _states(rk, rv, gk, gv, gf, gb, dec, init, batch, first_chunk, nc):
    cps = 2 if nc % 2 == 0 and first_chunk % 2 == 0 else 1
    steps = nc // cps

    def per_batch(w, backward):
        at = (lambda c: steps - 1 - c) if backward else (lambda c: c)
        return [pl.BlockSpec((cps * CHUNK, w), lambda c, b=b: ((first_chunk + b * nc) // cps + at(c), 0))
                for b in range(batch)]

    fwd = lambda w: per_batch(w, False)
    bwd = lambda w: per_batch(w, True)
    st_shape = (batch, RET_HEADS, LANE, LANE)
    init_spec = pl.BlockSpec(st_shape, lambda c: (0, 0, 0, 0))
    chunk_shape = (batch, cps, RET_HEADS, LANE, LANE)
    of_spec = pl.BlockSpec(chunk_shape, lambda c: (0, c, 0, 0, 0))
    ob_spec = pl.BlockSpec(chunk_shape, lambda c: (0, steps - 1 - c, 0, 0, 0))
    per_chunk = jax.ShapeDtypeStruct((batch, nc, RET_HEADS, LANE, LANE), BF16)
    final = jax.ShapeDtypeStruct(st_shape, F32)
    rep = lambda a: [a] * batch
    return pl.pallas_call(
        functools.partial(_state_kernel, batch=batch, cps=cps),
        grid=(steps,),
        in_specs=fwd(RET_W) + fwd(RET_W) + fwd(GLA_QK) + fwd(GLA_W) + fwd(GLA_QK)
        + bwd(RET_W) + bwd(RET_W) + bwd(GLA_QK) + bwd(GLA_W) + bwd(GLA_QK)
        + [pl.BlockSpec(dec.shape, lambda c: (0, 0))] + [init_spec] * 4,
        out_specs=[of_spec, ob_spec, of_spec, ob_spec] + [init_spec] * 4,
        out_shape=[per_chunk] * 4 + [final] * 4,
        scratch_shapes=[pltpu.VMEM(st_shape, F32)] * 4,
        compiler_params=pltpu.CompilerParams(dimension_semantics=("arbitrary",),
                                             vmem_limit_bytes=VMEM_LIMIT),
        name="states",
    )(*rep(rk), *rep(rv), *rep(gk), *rep(gv), *rep(gf), *rep(rk), *rep(rv), *rep(gk), *rep(gv), *rep(gb),
      dec, *init)


def _block_row(x, parent, r):
    n, w = x.shape
    if parent == n:
        return jnp.broadcast_to(x[r:r + 1, :], (n, w))
    x3 = x.reshape(n // parent, parent, w)
    return jnp.broadcast_to(x3[:, r:r + 1, :], x3.shape).reshape(n, w)


def _gla_levels(q, k_even, k_odd, gf, gb, bf, bb):
    n = q.shape[0]
    row = _iota(q.shape, 0)

    def factors(u_exp, w_exp):
        ew = jnp.exp2(w_exp).astype(BF16)
        return (q * jnp.exp2(u_exp).astype(BF16), k_even * ew, k_odd * ew)

    levels = []
    s = n // 2
    while s >= 2:
        if s >= 4:
            last_of_first = _block_row(bf, 2 * s, s - 1)
            first_of_second = _block_row(bb, 2 * s, s)
        else:
            upper = (row & 4) != 0
            last_of_first = jnp.where(upper, _block_row(bf, 8, 5), _block_row(bf, 8, 1))
            first_of_second = jnp.where(upper, _block_row(bb, 8, 6), _block_row(bb, 8, 2))
        df = bf - last_of_first
        db = bb - first_of_second
        levels.append((2 * s, [factors(jnp.minimum(df, db), -jnp.maximum(df, db))]))
        s //= 2
    odd = (row & 1) == 1
    qa = q * jnp.where(odd, jnp.exp2(gf), 2.0).astype(BF16)
    qb = q * jnp.where(odd, 2.0, jnp.exp2(gb)).astype(BF16)
    zero = jnp.zeros_like(k_even)
    ev = lambda x: jnp.where(odd, zero, x)
    od = lambda x: jnp.where(odd, x, zero)
    levels.append((2, [(qa, ev(k_even), ev(k_odd)), (qb, od(k_even), od(k_odd))]))
    return levels


def _mix_kernel(rq_ref, rk_ref, rv_ref, rg_ref, gq_ref, gk_ref, gv_ref, gg_ref, gf_ref, gb_ref,
                srf_ref, srb_ref, sgf_ref, sgb_ref, dec_ref, rnw_ref, gnw_ref,
                x_ref, m_ref, wout_ref, o_ref, mix_ref, decay_ref, qdec_ref, *, d):
    n = CHUNK
    half = n // 2
    ri = _iota((n, n), 0)
    ci = _iota((n, n), 1)
    step = pl.program_id(0)
    cur = step % 2

    @pl.when(step == 0)
    def _():
        mix_ref[1] = jnp.zeros(mix_ref.shape[1:], BF16)
        lg = _log_sigmoid(dec_ref[...])
        dist = (ri - ci).astype(F32)
        row = _iota((n, LANE), 0).astype(F32)
        for hd in range(RET_HEADS):
            sl = slice(hd * LANE, (hd + 1) * LANE)
            lgf = lg[0:1, sl]
            lgb = lg[1:2, sl]
            decay_ref[hd] = jnp.exp(jnp.where(dist > 0, dist * lgf[:, 0:1],
                                              jnp.where(dist < 0, -dist * lgb[:, 0:1], jnp.log(2.0))))
            qdec_ref[hd] = jnp.exp((row + 1.0) * lgf)
            qdec_ref[RET_HEADS + hd] = jnp.exp((n - row) * lgb)

    for hd in range(RET_HEADS):
        sl = slice(hd * LANE, (hd + 1) * LANE)
        q = rq_ref[:, sl]
        p = (_dot_nt(q, rk_ref[:, sl]) * decay_ref[hd]).astype(BF16)
        qf = q.astype(F32)
        qs = jnp.concatenate([(qf * qdec_ref[hd]).astype(BF16),
                              (qf * qdec_ref[RET_HEADS + hd]).astype(BF16)], axis=1)
        st = jnp.concatenate([srf_ref[hd], srb_ref[hd]], axis=0)
        o = _dot(p, rv_ref[:, sl]) + _dot(qs, st)
        mu = jnp.mean(o, axis=-1, keepdims=True)
        oc = o - mu
        var = jnp.mean(oc * oc, axis=-1, keepdims=True)
        r = oc * lax.rsqrt(var + EPS) * rnw_ref[:, sl] * rg_ref[:, sl].astype(F32)
        mix_ref[cur, :, sl] = r.astype(BF16)

    gf = gf_ref[...] * LOG2E
    gb = gb_ref[...] * LOG2E
    bf = _dot_select((ci <= ri).astype(BF16), gf)
    bb = _dot_select((ci >= ri).astype(BF16), gb)
    q = gq_ref[...]
    k = gk_ref[...]
    even_head = (_iota(k.shape, 1) & GLA_DK) == 0
    no_k = jnp.zeros_like(k)
    o_ref[...] = x_ref[...] + m_ref[:, 5 * d:6 * d] * _dot(mix_ref[1 - cur], wout_ref[...])
    levels = _gla_levels(q, jnp.where(even_head, k, no_k), jnp.where(even_head, no_k, k), gf, gb, bf, bb)
    qsf = q * jnp.exp2(bf).astype(BF16)
    qsb = q * jnp.exp2(bb).astype(BF16)
    code = _iota((half, n), 0) ^ (_iota((half, n), 1) & (half - 1))
    for pair in range(GLA_HEADS // 2):
        pr = slice(pair * LANE, (pair + 1) * LANE)
        near = [None, None]
        far = [None, None]
        for size, blocks in levels:
            for rb in range(2):
                kb = rb if size < n else 1 - rb
                rows = slice(rb * half, (rb + 1) * half)
                keys = slice(kb * half, (kb + 1) * half)
                u = jnp.concatenate([blk[0][rows, pr] for blk in blocks], axis=1)
                w = jnp.concatenate([jnp.concatenate([blk[j][keys, pr] for blk in blocks], axis=1)
                                     for j in (1, 2)], axis=0)
                t = _dot_nt(u, w)
                if size == n:
                    far[rb] = t
                else:
                    near[rb] = t if near[rb] is None else jnp.where(code < size, t, near[rb])
        qcat = jnp.concatenate([qsf[:, pr], qsb[:, pr]], axis=1)
        for j in range(2):
            hd = 2 * pair + j
            sl = slice(hd * LANE, (hd + 1) * LANE)
            mine = slice(j * half, (j + 1) * half)
            p = jnp.concatenate([jnp.concatenate([near[0][:, mine], far[0][:, mine]], axis=1),
                                 jnp.concatenate([far[1][:, mine], near[1][:, mine]], axis=1)], axis=0)
            st = jnp.concatenate([sgf_ref[hd], sgb_ref[hd]], axis=1)
            o = _dot(p.astype(BF16), gv_ref[:, sl]) + _dot_nt(qcat, st)
            ms = jnp.mean(o * o, axis=-1, keepdims=True)
            r = o * lax.rsqrt(ms + EPS) * gnw_ref[:, sl] * gg_ref[:, sl].astype(F32)
            mix_ref[cur, :, RET_W + hd * LANE:RET_W + (hd + 1) * LANE] = r.astype(BF16)


def _mix_call(mix_in, states, dec, rnw, gnw, x1, m3, wout, batch, nc):
    n_chunks = batch * nc
    t, d = n_chunks * CHUNK, x1.shape[1]
    mixed = lambda i: jnp.minimum(i, n_chunks - 1)
    done = lambda i: jnp.maximum(i - 1, 0)
    tile = lambda w: pl.BlockSpec((CHUNK, w), lambda i: (mixed(i), 0))
    st_spec = pl.BlockSpec((None, None, RET_HEADS, LANE, LANE),
                           lambda i: (mixed(i) // nc, mixed(i) % nc, 0, 0, 0))
    const = lambda shape: pl.BlockSpec(shape, lambda i: (0,) * len(shape))
    widths = (RET_W,) * 4 + (GLA_QK,) * 2 + (GLA_W,) * 2 + (GLA_QK,) * 2
    return pl.pallas_call(
        functools.partial(_mix_kernel, d=d),
        grid=(n_chunks + 1,),
        in_specs=[tile(w) for w in widths] + [st_spec] * 4
        + [const(dec.shape), const((1, RET_W)), const((1, GLA_W)),
           pl.BlockSpec((CHUNK, d), lambda i: (done(i), 0)),
           pl.BlockSpec((None, 1, N_MOD * d), lambda i: (done(i) // nc, 0, 0)),
           const(wout.shape)],
        out_specs=pl.BlockSpec((CHUNK, d), lambda i: (done(i), 0)),
        out_shape=jax.ShapeDtypeStruct((t, d), F32),
        scratch_shapes=[pltpu.VMEM((2, CHUNK, RET_W + GLA_W), BF16),
                        pltpu.VMEM((RET_HEADS, CHUNK, CHUNK), F32),
                        pltpu.VMEM((2 * RET_HEADS, CHUNK, LANE), F32)],
        compiler_params=pltpu.CompilerParams(dimension_semantics=("arbitrary",),
                                             vmem_limit_bytes=VMEM_LIMIT),
        name="mix",
    )(*mix_in, *states, dec, rnw.reshape(1, RET_W), gnw.reshape(1, GLA_W), x1, m3, wout)


def _rope_tables(n_tok):
    freqs = ROPE_BASE ** (-jnp.arange(RET_DK // 4, dtype=F32) / (RET_DK // 4))

    def table(n_pos, first_half):
        ang = jnp.arange(n_pos, dtype=F32)[:, None] * freqs
        zero = jnp.zeros((n_pos, LANE // 2), F32)
        cos = jnp.concatenate([jnp.cos(ang)] * 2, axis=-1)
        sin = jnp.concatenate([-jnp.sin(ang), jnp.sin(ang)], axis=-1)
        halves = (cos, zero, sin, zero) if first_half else (zero, cos, zero, sin)
        return jnp.concatenate(halves, axis=-1)

    return table(n_tok // GRID_W, True), table(GRID_W, False)


def _pack_w_low(w_in):
    low = w_in[:, C_LOW:]
    pad = jnp.zeros((w_in.shape[0], LANE - low.shape[1]), w_in.dtype)
    return jnp.concatenate([low, pad], axis=1).astype(BF16)


def _pack_gate(w_f, b_f, w_b, b_b):
    gw = jnp.zeros((LANE, 2 * GLA_QK), F32)
    gw = gw.at[:GLA_RANK, :GLA_QK].set(w_f).at[GLA_RANK:2 * GLA_RANK, GLA_QK:].set(w_b)
    return gw.astype(BF16), jnp.concatenate([b_f, b_b]).reshape(1, 2 * GLA_QK)


def kernel(x, c, ctx, c_ctx, ada_w, ada_b, norm1_w, ffn1_w1, ffn1_w3, ffn1_w2, norm2_w, w_in,
           ret_decay_f, ret_decay_b, ret_norm_w, gla_gate_w_f, gla_gate_b_f, gla_gate_w_b, gla_gate_b_b,
           gla_norm_w, w_out, norm3_w, ffn2_w1, ffn2_w3, ffn2_w2, final_norm_w):
    batch, n_tok, d = x.shape
    n_ctx = ctx.shape[1]
    depth = ada_w.shape[0]
    assert depth == 1 and batch + 1 <= 8
    assert n_tok % FFN_TILE == 0 and n_tok % CHUNK == 0 and n_ctx % CHUNK == 0
    assert (batch * n_ctx) % FFN_TILE == 0

    cvec = jnp.zeros((8, d), F32).at[:batch].set(c).at[batch].set(c_ctx)
    m, w1a, w3a, w2a, w_in_b = _modulation(cvec, ada_w[0], ada_b[0],
                                           cast=(ffn1_w1[0], ffn1_w3[0], ffn1_w2[0], w_in[0]))
    m3 = m.reshape(8, 1, N_MOD * d)

    gw, gbias = _pack_gate(gla_gate_w_f[0], gla_gate_b_f[0], gla_gate_w_b[0], gla_gate_b_b[0])
    proj = (norm2_w[0], w_in_b, _pack_w_low(w_in[0]), gw, gbias)
    f1 = (norm1_w[0], w1a, w3a, w2a)
    rowtab, coltab = _rope_tables(n_tok)
    dec = jnp.zeros((8, RET_W), F32)
    dec = dec.at[0].set(jnp.repeat(ret_decay_f[0], LANE)).at[1].set(jnp.repeat(ret_decay_b[0], LANE))

    tiles_per_seq = n_tok // FFN_TILE
    first = _ffn_call(x.reshape(batch * n_tok, d), m3, tiles_per_seq, *f1, mod_off=0, tm=FFN_TILE,
                      ctx=ctx.reshape(batch * n_ctx, d), ctx_row=batch, proj=proj, rope=(rowtab, coltab),
                      cast=(ffn2_w1[0], ffn2_w3[0], ffn2_w2[0], w_out[0]), name="ffn_in")
    x1, mix_in, (w1b, w3b, w2b, woutb) = first[0], first[1:11], first[11:]
    scan_in = (mix_in[1], mix_in[2], mix_in[5], mix_in[6], mix_in[8], mix_in[9])
    zero = jnp.zeros((batch, RET_HEADS, LANE, LANE), F32)
    nc, nc_ctx = n_tok // CHUNK, n_ctx // CHUNK
    ctx_states = _states(*scan_in, dec, (zero,) * 4, batch, batch * nc, nc_ctx)[4:]
    states = _states(*scan_in, dec, ctx_states, batch, 0, nc)[:4]
    x2 = _mix_call(mix_in, states, dec, ret_norm_w[0], gla_norm_w[0], x1, m3, woutb, batch, nc)
    out = _ffn_call(x2, m3, tiles_per_seq, norm3_w[0], w1b, w3b, w2b, mod_off=6, tm=FFN_TILE,
                    final_w=final_norm_w, name="ffn_out")[0]
    return out.reshape(batch, n_tok, d)
```

```python
import functools

import jax
import jax.numpy as jnp
from jax import lax
from jax.experimental import pallas as pl
from jax.experimental.pallas import tpu as pltpu

F32 = jnp.float32
BF16 = jnp.bfloat16

EPS = 1e-6
LOG2E = 1.4426950408889634
N_MOD = 9
GRID_W = 64
ROPE_BASE = 10000.0
RET_HEADS = 4
RET_DK = 128
RET_DV = 128
GLA_HEADS = 4
GLA_DK = 64
GLA_DV = 128
GLA_RANK = 16
GLA_TAU = 16.0
RET_W = RET_HEADS * RET_DV
GLA_W = GLA_HEADS * GLA_DV
GLA_QK = GLA_HEADS * GLA_DK

LANE = 128
BF16_ROWS = 16
CHUNK = 256
FFN_TILE = 512
ROW_BLOCK = 256
FF_CHUNK = 256
VMEM_LIMIT = 60 * 1024 * 1024

C_RQ, C_RK, C_RV, C_RG = 0, 512, 1024, 1536
C_GQ, C_GK, C_GV, C_GG = 2048, 2304, 2560, 3072
C_LOW = 3584


def _silu(x):
    return x * (1.0 / (1.0 + jnp.exp(-x)))


def _log_sigmoid(z):
    return jnp.minimum(z, 0.0) - jnp.log(1.0 + jnp.exp(-jnp.abs(z)))


def _rms(x, w):
    return x * lax.rsqrt(jnp.mean(x * x, axis=-1, keepdims=True) + EPS) * w


def _dot(a, b):
    return jnp.dot(a, b, preferred_element_type=F32)


def _dot_nt(a, b):
    return lax.dot_general(a, b, (((1,), (1,)), ((), ())), preferred_element_type=F32)


def _dot_tn(a, b):
    return lax.dot_general(a, b, (((0,), (0,)), ((), ())), preferred_element_type=F32)


def _dot_select(sel, x):
    hi = x.astype(BF16)
    lo = (x - hi.astype(F32)).astype(BF16)
    return _dot(sel, hi) + _dot(sel, lo)


def _iota(shape, dim):
    return lax.broadcasted_iota(jnp.int32, shape, dim)


def _resident(shape):
    nd = len(shape)
    return pl.BlockSpec(shape, lambda *_: (0,) * nd, pipeline_mode=pl.Buffered(1))


def _cast_jobs(arrays, n_steps):
    specs, shapes = [], []
    for w in arrays:
        rows = next(r for r in range(BF16_ROWS, w.shape[0] + 1, BF16_ROWS)
                    if w.shape[0] % r == 0 and w.shape[0] // r <= n_steps)
        specs.append(pl.BlockSpec((rows, w.shape[1]),
                                  lambda i, last=w.shape[0] // rows - 1: (jnp.minimum(i, last), 0)))
        shapes.append(jax.ShapeDtypeStruct(w.shape, BF16))
    return specs, shapes


def _run_cast_jobs(src_refs, dst_refs):
    for src_ref, dst_ref in zip(src_refs, dst_refs):
        dst_ref[...] = src_ref[...].astype(BF16)


W_IN_BLOCK = 512


def _mod_kernel(*refs, n_cast):
    c_ref, w_ref, b_ref = refs[:3]
    cast_in, (wt_ref, wt_low_ref) = refs[3:3 + n_cast], refs[3 + n_cast:5 + n_cast]
    o_ref = refs[5 + n_cast]
    cast_out, (win_ref, wlow_ref) = refs[6 + n_cast:6 + 2 * n_cast], refs[6 + 2 * n_cast:]
    cond = _silu(c_ref[...]).astype(BF16)
    o_ref[...] = _dot(cond, w_ref[...].astype(BF16)) + b_ref[...]
    _run_cast_jobs(cast_in, cast_out)
    win_ref[...] = wt_ref[...].T.astype(BF16)
    low = wt_low_ref[...]
    low = jnp.concatenate([low, jnp.zeros((LANE - low.shape[0], low.shape[1]), F32)], axis=0)
    wlow_ref[...] = low.T.astype(BF16)


def _modulation(cvec, ada_w, ada_b, cast, w_in_t):
    d, n = ada_w.shape
    bn = n // 8 if n % (8 * LANE) == 0 else d
    steps = n // bn
    assert C_LOW % W_IN_BLOCK == 0 and C_LOW // W_IN_BLOCK <= steps
    n_low = w_in_t.shape[0] - C_LOW
    last = C_LOW // W_IN_BLOCK - 1
    cast_specs, cast_shapes = _cast_jobs(cast, steps)
    return pl.pallas_call(
        functools.partial(_mod_kernel, n_cast=len(cast)),
        grid=(steps,),
        in_specs=[pl.BlockSpec((8, d), lambda j: (0, 0)),
                  pl.BlockSpec((d, bn), lambda j: (0, j)),
                  pl.BlockSpec((1, bn), lambda j: (0, j))] + cast_specs
        + [pl.BlockSpec((W_IN_BLOCK, d), lambda j: (jnp.minimum(j, last), 0)),
           pl.BlockSpec((n_low, d), lambda j: (C_LOW // n_low, 0))],
        out_specs=[pl.BlockSpec((8, bn), lambda j: (0, j))] + cast_specs
        + [pl.BlockSpec((d, W_IN_BLOCK), lambda j: (0, jnp.minimum(j, last))),
           pl.BlockSpec((d, LANE), lambda j: (0, 0))],
        out_shape=[jax.ShapeDtypeStruct((8, n), F32)] + cast_shapes
        + [jax.ShapeDtypeStruct((d, C_LOW), BF16), jax.ShapeDtypeStruct((d, LANE), BF16)],
        compiler_params=pltpu.CompilerParams(dimension_semantics=("arbitrary",),
                                             vmem_limit_bytes=VMEM_LIMIT),
        name="mod",
    )(cvec, ada_w, ada_b.reshape(1, n), *cast, w_in_t, w_in_t)


def _swap32(x):
    lane = _iota(x.shape, 1)
    return jnp.where((lane & 63) < 32, pltpu.roll(x, 96, 1), pltpu.roll(x, 32, 1))


def _ffn_kernel(*refs, mod_off, proj, rope, final, n_main, n_cast, d, f):
    it = iter(refs)
    x_ref = next(it)
    if n_main is not None:
        xc_ref = next(it)
        is_ctx = pl.program_id(0) >= n_main
    m_ref, nw_ref, w1_ref, w3_ref, w2_ref = (next(it) for _ in range(5))
    if proj:
        n2w_ref, win_ref, wlow_ref, gw_ref, gbias_ref = (next(it) for _ in range(5))
        if rope:
            rowtab_ref, coltab_ref = next(it), next(it)
    if final:
        fnw_ref = next(it)
    cast_in = [next(it) for _ in range(n_cast)]
    xo_ref = next(it)
    if proj:
        (rq_ref, rk_ref, rv_ref, rg_ref, gq_ref, gk_ref, gv_ref, gg_ref,
         gf_ref, gb_ref) = (next(it) for _ in range(10))
    cast_out = [next(it) for _ in range(n_cast)]
    u_ref = next(it)
    _run_cast_jobs(cast_in, cast_out)

    def mod(i):
        return m_ref[:, (mod_off + i) * d:(mod_off + i + 1) * d]

    for rb in range(x_ref.shape[0] // ROW_BLOCK):
        rows = slice(rb * ROW_BLOCK, (rb + 1) * ROW_BLOCK)
        x = x_ref[rows, :]
        if n_main is not None:
            x = jnp.where(is_ctx, xc_ref[rows, :], x)
        h = (_rms(x, nw_ref[...]) * (1.0 + mod(1)) + mod(0)).astype(BF16)
        for k in range(f // FF_CHUNK):
            sl = slice(k * FF_CHUNK, (k + 1) * FF_CHUNK)
            a = _dot(h, w1_ref[:, sl])
            g = _dot(h, w3_ref[:, sl])
            u_ref[rows, sl] = (_silu(a) * g).astype(BF16)
        y = _dot(u_ref[rows, :], w2_ref[...])
        x1 = x + (0.5 * mod(2)) * y

        if final:
            xo_ref[rows, :] = _rms(x1, fnw_ref[...])
        else:
            xo_ref[rows, :] = x1

        if proj:
            h2 = (_rms(x1, n2w_ref[...]) * (1.0 + mod(4)) + mod(3)).astype(BF16)

            def p(lo, hi):
                return _dot(h2, win_ref[:, lo:hi])

            if rope:
                by_row = _iota((GRID_W, 2 * LANE), 1) % LANE < LANE // 2
                g0 = rb * (ROW_BLOCK // GRID_W)
                tab = jnp.concatenate(
                    [jnp.where(by_row, jnp.broadcast_to(rowtab_ref[g0 + g:g0 + g + 1, :], (GRID_W, 2 * LANE)),
                               coltab_ref[...]) for g in range(ROW_BLOCK // GRID_W)], axis=0)
                cos, sin = tab[:, :LANE], tab[:, LANE:]
                if n_main is not None:
                    cos = jnp.where(is_ctx, 1.0, cos)
                    sin = jnp.where(is_ctx, 0.0, sin)

            for base, scale, o_ref in ((C_RQ, RET_DK ** -0.5, rq_ref), (C_RK, 1.0, rk_ref)):
                t = p(base, base + RET_W)
                for hd in range(RET_HEADS):
                    th = t[:, hd * LANE:(hd + 1) * LANE] * scale
                    if rope:
                        th = th * cos + _swap32(th) * sin
                    o_ref[rows, hd * LANE:(hd + 1) * LANE] = th.astype(BF16)
            rv_ref[rows, :] = p(C_RV, C_RV + RET_W).astype(BF16)
            rg_ref[rows, :] = _silu(p(C_RG, C_RG + RET_W)).astype(BF16)
            gq_ref[rows, :] = (p(C_GQ, C_GQ + GLA_QK) * GLA_DK ** -0.5).astype(BF16)
            gk_ref[rows, :] = p(C_GK, C_GK + GLA_QK).astype(BF16)
            gv_ref[rows, :] = p(C_GV, C_GV + GLA_W).astype(BF16)
            gg_ref[rows, :] = _silu(p(C_GG, C_GG + GLA_W)).astype(BF16)
            low = _dot(h2, wlow_ref[...]).astype(BF16)
            z = _dot(low, gw_ref[...]) + gbias_ref[...]
            ls = _log_sigmoid(z) * (1.0 / GLA_TAU)
            gf_ref[rows, :] = ls[:, :GLA_QK]
            gb_ref[rows, :] = ls[:, GLA_QK:]


def _ffn_call(x, m3, tiles_per_seq, nw, w1, w3, w2, *, mod_off, tm, ctx=None, ctx_row=None, proj=None,
              rope=None, final_w=None, cast=(), name):
    t, d = x.shape
    f = w1.shape[1]
    n_main = t // tm
    n_tiles = n_main + (0 if ctx is None else ctx.shape[0] // tm)
    t = n_tiles * tm
    main = lambda i: jnp.minimum(i, n_main - 1)
    tile = lambda w: pl.BlockSpec((tm, w), lambda i: (i, 0))
    in_specs = [pl.BlockSpec((tm, d), lambda i: (main(i), 0))]
    args = [x]
    if ctx is None:
        row_of_tile = lambda i: i // tiles_per_seq
    else:
        row_of_tile = lambda i: jnp.where(i >= n_main, ctx_row, i // tiles_per_seq)
        in_specs.append(pl.BlockSpec((tm, d), lambda i: (jnp.maximum(i - n_main, 0), 0)))
        args.append(ctx)
    in_specs += [pl.BlockSpec((None, 1, N_MOD * d), lambda i: (row_of_tile(i), 0, 0)),
                 _resident((1, d)), _resident((d, f)), _resident((d, f)), _resident((f, d))]
    args += [m3, nw.reshape(1, d), w1, w3, w2]
    if proj is not None:
        n2w, win, wlow, gw, gbias = proj
        in_specs += [_resident((1, d))] + [_resident(a.shape) for a in (win, wlow, gw, gbias)]
        args += [n2w.reshape(1, d), win, wlow, gw, gbias]
        if rope is not None:
            rowtab, coltab = rope
            in_specs += [pl.BlockSpec((tm // GRID_W, 2 * LANE), lambda i: (main(i) % tiles_per_seq, 0)),
                         _resident(coltab.shape)]
            args += [rowtab, coltab]
    if final_w is not None:
        in_specs.append(_resident((1, d)))
        args.append(final_w.reshape(1, d))
    out_specs = [tile(d)]
    out_shape = [jax.ShapeDtypeStruct((t, d), F32)]
    if proj is not None:
        for w, dt in ((RET_W, BF16),) * 4 + ((GLA_QK, BF16),) * 2 + ((GLA_W, BF16),) * 2 + ((GLA_QK, F32),) * 2:
            out_specs.append(tile(w))
            out_shape.append(jax.ShapeDtypeStruct((t, w), dt))
    cast_specs, cast_shapes = _cast_jobs(cast, n_main)
    in_specs += cast_specs
    args += list(cast)
    out_specs += cast_specs
    out_shape += cast_shapes
    kern = functools.partial(_ffn_kernel, mod_off=mod_off, proj=proj is not None,
                             rope=rope is not None, final=final_w is not None,
                             n_main=None if ctx is None else n_main, n_cast=len(cast), d=d, f=f)
    return pl.pallas_call(
        kern, grid=(n_tiles,), in_specs=in_specs, out_specs=out_specs, out_shape=out_shape,
        scratch_shapes=[pltpu.VMEM((tm, f), BF16)],
        compiler_params=pltpu.CompilerParams(dimension_semantics=("arbitrary",),
                                             vmem_limit_bytes=VMEM_LIMIT),
        name=name,
    )(*args)


def _head_lanes(shape, hd):
    lane = _iota(shape, len(shape) - 1)
    return (lane >= 64) if hd % 2 else (lane < 64)


def _state_kernel(*refs, batch, cps):
    it = iter(refs)
    per_batch = lambda: [next(it) for _ in range(batch)]
    rkf_ref, rvf_ref, gkf_ref, gvf_ref, gf_ref = (per_batch() for _ in range(5))
    rkb_ref, rvb_ref, gkb_ref, gvb_ref, gb_ref = (per_batch() for _ in range(5))
    dec_ref = next(it)
    irf_ref, irb_ref, igf_ref, igb_ref = (next(it) for _ in range(4))
    orf_ref, orb_ref, ogf_ref, ogb_ref = (next(it) for _ in range(4))
    frf_ref, frb_ref, fgf_ref, fgb_ref = (next(it) for _ in range(4))
    srf, srb, sgf, sgb = (next(it) for _ in range(4))
    n = CHUNK

    @pl.when(pl.program_id(0) == 0)
    def _():
        srf[...] = irf_ref[...]
        srb[...] = irb_ref[...]
        sgf[...] = igf_ref[...]
        sgb[...] = igb_ref[...]

    row = _iota((n, LANE), 0).astype(F32)
    lg = _log_sigmoid(dec_ref[...])
    ri = _iota((n, n), 0)
    ci = _iota((n, n), 1)
    after = (ci > ri).astype(BF16)
    before = (ci < ri).astype(BF16)

    def gla(b, rows, g_ref, k_ref, v_ref, tri, edge, st):
        g = g_ref[b][rows, :]
        e = _dot_select(tri, g)
        tot = e[edge:edge + 1, :] + g[edge:edge + 1, :]
        kd = (k_ref[b][rows, :].astype(F32) * jnp.exp(e)).astype(BF16)
        for hd in range(GLA_HEADS):
            pr = slice((hd // 2) * LANE, (hd // 2 + 1) * LANE)
            upd = _dot_tn(v_ref[b][rows, hd * LANE:(hd + 1) * LANE], kd[:, pr])
            upd = jnp.where(_head_lanes(upd.shape, hd), upd, 0.0)
            st[b, hd] = st[b, hd] * jnp.exp(tot[:, pr]) + upd

    for j in range(cps):
        jb = cps - 1 - j
        fr = slice(j * n, (j + 1) * n)
        br = slice(jb * n, (jb + 1) * n)
        orf_ref[:, j] = srf[...].astype(BF16)
        ogf_ref[:, j] = sgf[...].astype(BF16)
        orb_ref[:, jb] = srb[...].astype(BF16)
        ogb_ref[:, jb] = sgb[...].astype(BF16)
        for b in range(batch):
            for hd in range(RET_HEADS):
                sl = slice(hd * LANE, (hd + 1) * LANE)
                lgf = lg[0:1, sl]
                lgb = lg[1:2, sl]
                kf = (rkf_ref[b][fr, sl].astype(F32) * jnp.exp((n - 1.0 - row) * lgf)).astype(BF16)
                srf[b, hd] = srf[b, hd] * jnp.exp(n * lgf) + _dot_tn(kf, rvf_ref[b][fr, sl])
                kb = (rkb_ref[b][br, sl].astype(F32) * jnp.exp(row * lgb)).astype(BF16)
                srb[b, hd] = srb[b, hd] * jnp.exp(n * lgb) + _dot_tn(kb, rvb_ref[b][br, sl])
            gla(b, fr, gf_ref, gkf_ref, gvf_ref, after, 0, sgf)
            gla(b, br, gb_ref, gkb_ref, gvb_ref, before, n - 1, sgb)

    frf_ref[...] = srf[...]
    frb_ref[...] = srb[...]
    fgf_ref[...] = sgf[...]
    fgb_ref[...] = sgb[...]


def _states(rk, rv, gk, gv, gf, gb, dec, init, batch, first_chunk, nc):
    cps = 2 if nc % 2 == 0 and first_chunk % 2 == 0 else 1
    steps = nc // cps

    def per_batch(w, backward):
        at = (lambda c: steps - 1 - c) if backward else (lambda c: c)
        return [pl.BlockSpec((cps * CHUNK, w), lambda c, b=b: ((first_chunk + b * nc) // cps + at(c), 0))
                for b in range(batch)]

    fwd = lambda w: per_batch(w, False)
    bwd = lambda w: per_batch(w, True)
    st_shape = (batch, RET_HEADS, LANE, LANE)
    init_spec = pl.BlockSpec(st_shape, lambda c: (0, 0, 0, 0))
    chunk_shape = (batch, cps, RET_HEADS, LANE, LANE)
    of_spec = pl.BlockSpec(chunk_shape, lambda c: (0, c, 0, 0, 0))
    ob_spec = pl.BlockSpec(chunk_shape, lambda c: (0, steps - 1 - c, 0, 0, 0))
    per_chunk = jax.ShapeDtypeStruct((batch, nc, RET_HEADS, LANE, LANE), BF16)
    final = jax.ShapeDtypeStruct(st_shape, F32)
    rep = lambda a: [a] * batch
    return pl.pallas_call(
        functools.partial(_state_kernel, batch=batch, cps=cps),
        grid=(steps,),
        in_specs=fwd(RET_W) + fwd(RET_W) + fwd(GLA_QK) + fwd(GLA_W) + fwd(GLA_QK)
        + bwd(RET_W) + bwd(RET_W) + bwd(GLA_QK) + bwd(GLA_W) + bwd(GLA_QK)
        + [pl.BlockSpec(dec.shape, lambda c: (0, 0))] + [init_spec] * 4,
        out_specs=[of_spec, ob_spec, of_spec, ob_spec] + [init_spec] * 4,
        out_shape=[per_chunk] * 4 + [final] * 4,
        scratch_shapes=[pltpu.VMEM(st_shape, F32)] * 4,
        compiler_params=pltpu.CompilerParams(dimension_semantics=("arbitrary",),
                                             vmem_limit_bytes=VMEM_LIMIT),
        name="states",
    )(*rep(rk), *rep(rv), *rep(gk), *rep(gv), *rep(gf), *rep(rk), *rep(rv), *rep(gk), *rep(gv), *rep(gb),
      dec, *init)


def _block_row(x, parent, r):
    n, w = x.shape
    if parent == n:
        return jnp.broadcast_to(x[r:r + 1, :], (n, w))
    x3 = x.reshape(n // parent, parent, w)
    return jnp.broadcast_to(x3[:, r:r + 1, :], x3.shape).reshape(n, w)


def _gla_levels(q, k_even, k_odd, gf, gb, bf, bb):
    n = q.shape[0]
    row = _iota(q.shape, 0)

    def factors(u_exp, w_exp):
        ew = jnp.exp2(w_exp).astype(BF16)
        return (q * jnp.exp2(u_exp).astype(BF16), k_even * ew, k_odd * ew)

    levels = []
    s = n // 2
    while s >= 2:
        if s >= 4:
            last_of_first = _block_row(bf, 2 * s, s - 1)
            first_of_second = _block_row(bb, 2 * s, s)
        else:
            upper = (row & 4) != 0
            last_of_first = jnp.where(upper, _block_row(bf, 8, 5), _block_row(bf, 8, 1))
            first_of_second = jnp.where(upper, _block_row(bb, 8, 6), _block_row(bb, 8, 2))
        df = bf - last_of_first
        db = bb - first_of_second
        levels.append((2 * s, [factors(jnp.minimum(df, db), -jnp.maximum(df, db))]))
        s //= 2
    odd = (row & 1) == 1
    qa = q * jnp.where(odd, jnp.exp2(gf), 2.0).astype(BF16)
    qb = q * jnp.where(odd, 2.0, jnp.exp2(gb)).astype(BF16)
    zero = jnp.zeros_like(k_even)
    ev = lambda x: jnp.where(odd, zero, x)
    od = lambda x: jnp.where(odd, x, zero)
    levels.append((2, [(qa, ev(k_even), ev(k_odd)), (qb, od(k_even), od(k_odd))]))
    return levels


def _mix_kernel(rq_ref, rk_ref, rv_ref, rg_ref, gq_ref, gk_ref, gv_ref, gg_ref, gf_ref, gb_ref,
                srf_ref, srb_ref, sgf_ref, sgb_ref, dec_ref, rnw_ref, gnw_ref,
                x_ref, m_ref, wout_ref, o_ref, mix_ref, decay_ref, qdec_ref, *, d):
    n = CHUNK
    half = n // 2
    ri = _iota((n, n), 0)
    ci = _iota((n, n), 1)
    step = pl.program_id(0)
    cur = step % 2

    @pl.when(step == 0)
    def _():
        mix_ref[1] = jnp.zeros(mix_ref.shape[1:], BF16)
        lg = _log_sigmoid(dec_ref[...])
        dist = (ri - ci).astype(F32)
        row = _iota((n, LANE), 0).astype(F32)
        for hd in range(RET_HEADS):
            sl = slice(hd * LANE, (hd + 1) * LANE)
            lgf = lg[0:1, sl]
            lgb = lg[1:2, sl]
            decay_ref[hd] = jnp.exp(jnp.where(dist > 0, dist * lgf[:, 0:1],
                                              jnp.where(dist < 0, -dist * lgb[:, 0:1], jnp.log(2.0))))
            qdec_ref[hd] = jnp.exp((row + 1.0) * lgf)
            qdec_ref[RET_HEADS + hd] = jnp.exp((n - row) * lgb)

    for hd in range(RET_HEADS):
        sl = slice(hd * LANE, (hd + 1) * LANE)
        q = rq_ref[:, sl]
        p = (_dot_nt(q, rk_ref[:, sl]) * decay_ref[hd]).astype(BF16)
        qf = q.astype(F32)
        qs = jnp.concatenate([(qf * qdec_ref[hd]).astype(BF16),
                              (qf * qdec_ref[RET_HEADS + hd]).astype(BF16)], axis=1)
        st = jnp.concatenate([srf_ref[hd], srb_ref[hd]], axis=0)
        o = _dot(p, rv_ref[:, sl]) + _dot(qs, st)
        mu = jnp.mean(o, axis=-1, keepdims=True)
        oc = o - mu
        var = jnp.mean(oc * oc, axis=-1, keepdims=True)
        r = oc * lax.rsqrt(var + EPS) * rnw_ref[:, sl] * rg_ref[:, sl].astype(F32)
        mix_ref[cur, :, sl] = r.astype(BF16)

    gf = gf_ref[...] * LOG2E
    gb = gb_ref[...] * LOG2E
    bf = _dot_select((ci <= ri).astype(BF16), gf)
    bb = _dot_select((ci >= ri).astype(BF16), gb)
    q = gq_ref[...]
    k = gk_ref[...]
    even_head = (_iota(k.shape, 1) & GLA_DK) == 0
    no_k = jnp.zeros_like(k)
    o_ref[...] = x_ref[...] + m_ref[:, 5 * d:6 * d] * _dot(mix_ref[1 - cur], wout_ref[...])
    levels = _gla_levels(q, jnp.where(even_head, k, no_k), jnp.where(even_head, no_k, k), gf, gb, bf, bb)
    qsf = q * jnp.exp2(bf).astype(BF16)
    qsb = q * jnp.exp2(bb).astype(BF16)
    code = _iota((half, n), 0) ^ (_iota((half, n), 1) & (half - 1))
    for pair in range(GLA_HEADS // 2):
        pr = slice(pair * LANE, (pair + 1) * LANE)
        near = [None, None]
        far = [None, None]
        for size, blocks in levels:
            for rb in range(2):
                kb = rb if size < n else 1 - rb
                rows = slice(rb * half, (rb + 1) * half)
                keys = slice(kb * half, (kb + 1) * half)
                u = jnp.concatenate([blk[0][rows, pr] for blk in blocks], axis=1)
                w = jnp.concatenate([jnp.concatenate([blk[j][keys, pr] for blk in blocks], axis=1)
                                     for j in (1, 2)], axis=0)
                t = _dot_nt(u, w)
                if size == n:
                    far[rb] = t
                else:
                    near[rb] = t if near[rb] is None else jnp.where(code < size, t, near[rb])
        qcat = jnp.concatenate([qsf[:, pr], qsb[:, pr]], axis=1)
        for j in range(2):
            hd = 2 * pair + j
            sl = slice(hd * LANE, (hd + 1) * LANE)
            mine = slice(j * half, (j + 1) * half)
            p = jnp.concatenate([jnp.concatenate([near[0][:, mine], far[0][:, mine]], axis=1),
                                 jnp.concatenate([far[1][:, mine], near[1][:, mine]], axis=1)], axis=0)
            st = jnp.concatenate([sgf_ref[hd], sgb_ref[hd]], axis=1)
            o = _dot(p.astype(BF16), gv_ref[:, sl]) + _dot_nt(qcat, st)
            ms = jnp.mean(o * o, axis=-1, keepdims=True)
            r = o * lax.rsqrt(ms + EPS) * gnw_ref[:, sl] * gg_ref[:, sl].astype(F32)
            mix_ref[cur, :, RET_W + hd * LANE:RET_W + (hd + 1) * LANE] = r.astype(BF16)


def _mix_call(mix_in, states, dec, rnw, gnw, x1, m3, wout, batch, nc):
    n_chunks = batch * nc
    t, d = n_chunks * CHUNK, x1.shape[1]
    mixed = lambda i: jnp.minimum(i, n_chunks - 1)
    done = lambda i: jnp.maximum(i - 1, 0)
    tile = lambda w: pl.BlockSpec((CHUNK, w), lambda i: (mixed(i), 0))
    st_spec = pl.BlockSpec((None, None, RET_HEADS, LANE, LANE),
                           lambda i: (mixed(i) // nc, mixed(i) % nc, 0, 0, 0))
    const = lambda shape: pl.BlockSpec(shape, lambda i: (0,) * len(shape))
    widths = (RET_W,) * 4 + (GLA_QK,) * 2 + (GLA_W,) * 2 + (GLA_QK,) * 2
    return pl.pallas_call(
        functools.partial(_mix_kernel, d=d),
        grid=(n_chunks + 1,),
        in_specs=[tile(w) for w in widths] + [st_spec] * 4
        + [const(dec.shape), const((1, RET_W)), const((1, GLA_W)),
           pl.BlockSpec((CHUNK, d), lambda i: (done(i), 0)),
           pl.BlockSpec((None, 1, N_MOD * d), lambda i: (done(i) // nc, 0, 0)),
           const(wout.shape)],
        out_specs=pl.BlockSpec((CHUNK, d), lambda i: (done(i), 0)),
        out_shape=jax.ShapeDtypeStruct((t, d), F32),
        scratch_shapes=[pltpu.VMEM((2, CHUNK, RET_W + GLA_W), BF16),
                        pltpu.VMEM((RET_HEADS, CHUNK, CHUNK), F32),
                        pltpu.VMEM((2 * RET_HEADS, CHUNK, LANE), F32)],
        compiler_params=pltpu.CompilerParams(dimension_semantics=("arbitrary",),
                                             vmem_limit_bytes=VMEM_LIMIT),
        name="mix",
    )(*mix_in, *states, dec, rnw.reshape(1, RET_W), gnw.reshape(1, GLA_W), x1, m3, wout)


def _rope_tables(n_tok):
    freqs = ROPE_BASE ** (-jnp.arange(RET_DK // 4, dtype=F32) / (RET_DK // 4))

    def table(n_pos, first_half):
        ang = jnp.arange(n_pos, dtype=F32)[:, None] * freqs
        zero = jnp.zeros((n_pos, LANE // 2), F32)
        cos = jnp.concatenate([jnp.cos(ang)] * 2, axis=-1)
        sin = jnp.concatenate([-jnp.sin(ang), jnp.sin(ang)], axis=-1)
        halves = (cos, zero, sin, zero) if first_half else (zero, cos, zero, sin)
        return jnp.concatenate(halves, axis=-1)

    return table(n_tok // GRID_W, True), table(GRID_W, False)


def _pack_gate(w_f, b_f, w_b, b_b):
    gw = jnp.zeros((LANE, 2 * GLA_QK), F32)
    gw = gw.at[:GLA_RANK, :GLA_QK].set(w_f).at[GLA_RANK:2 * GLA_RANK, GLA_QK:].set(w_b)
    return gw.astype(BF16), jnp.concatenate([b_f, b_b]).reshape(1, 2 * GLA_QK)


def kernel(x, c, ctx, c_ctx, ada_w, ada_b, norm1_w, ffn1_w1, ffn1_w3, ffn1_w2, norm2_w, w_in,
           ret_decay_f, ret_decay_b, ret_norm_w, gla_gate_w_f, gla_gate_b_f, gla_gate_w_b, gla_gate_b_b,
           gla_norm_w, w_out, norm3_w, ffn2_w1, ffn2_w3, ffn2_w2, final_norm_w):
    batch, n_tok, d = x.shape
    n_ctx = ctx.shape[1]
    depth = ada_w.shape[0]
    assert depth == 1 and batch + 1 <= 8
    assert n_tok % FFN_TILE == 0 and n_tok % CHUNK == 0 and n_ctx % CHUNK == 0
    assert (batch * n_ctx) % FFN_TILE == 0

    cvec = jnp.zeros((8, d), F32).at[:batch].set(c).at[batch].set(c_ctx)
    m, w1a, w3a, w2a, w_in_b, w_low_b = _modulation(cvec, ada_w[0], ada_b[0],
                                                    (ffn1_w1[0], ffn1_w3[0], ffn1_w2[0]), w_in[0].T)
    m3 = m.reshape(8, 1, N_MOD * d)

    gw, gbias = _pack_gate(gla_gate_w_f[0], gla_gate_b_f[0], gla_gate_w_b[0], gla_gate_b_b[0])
    proj = (norm2_w[0], w_in_b, w_low_b, gw, gbias)
    f1 = (norm1_w[0], w1a, w3a, w2a)
    rowtab, coltab = _rope_tables(n_tok)
    dec = jnp.zeros((8, RET_W), F32)
    dec = dec.at[0].set(jnp.repeat(ret_decay_f[0], LANE)).at[1].set(jnp.repeat(ret_decay_b[0], LANE))

    tiles_per_seq = n_tok // FFN_TILE
    first = _ffn_call(x.reshape(batch * n_tok, d), m3, tiles_per_seq, *f1, mod_off=0, tm=FFN_TILE,
                      ctx=ctx.reshape(batch * n_ctx, d), ctx_row=batch, proj=proj, rope=(rowtab, coltab),
                      cast=(ffn2_w1[0], ffn2_w3[0], ffn2_w2[0], w_out[0]), name="ffn_in")
    x1, mix_in, (w1b, w3b, w2b, woutb) = first[0], first[1:11], first[11:]
    scan_in = (mix_in[1], mix_in[2], mix_in[5], mix_in[6], mix_in[8], mix_in[9])
    zero = jnp.zeros((batch, RET_HEADS, LANE, LANE), F32)
    nc, nc_ctx = n_tok // CHUNK, n_ctx // CHUNK
    ctx_states = _states(*scan_in, dec, (zero,) * 4, batch, batch * nc, nc_ctx)[4:]
    states = _states(*scan_in, dec, ctx_states, batch, 0, nc)[:4]
    x2 = _mix_call(mix_in, states, dec, ret_norm_w[0], gla_norm_w[0], x1, m3, woutb, batch, nc)
    out = _ffn_call(x2, m3, tiles_per_seq, norm3_w[0], w1b, w3b, w2b, mod_off=6, tm=FFN_TILE,
                    final_w=final_norm_w, name="ffn_out")[0]
    return out.reshape(batch, n_tok, d)
```

```python
import functools

import jax
import jax.numpy as jnp
from jax import lax
from jax.experimental import pallas as pl
from jax.experimental.pallas import tpu as pltpu

F32 = jnp.float32
BF16 = jnp.bfloat16

EPS = 1e-6
LOG2E = 1.4426950408889634
N_MOD = 9
GRID_W = 64
ROPE_BASE = 10000.0
RET_HEADS = 4
RET_DK = 128
RET_DV = 128
GLA_HEADS = 4
GLA_DK = 64
GLA_DV = 128
GLA_RANK = 16
GLA_TAU = 16.0
RET_W = RET_HEADS * RET_DV
GLA_W = GLA_HEADS * GLA_DV
GLA_QK = GLA_HEADS * GLA_DK

LANE = 128
BF16_ROWS = 16
CHUNK = 256
FFN_TILE = 512
ROW_BLOCK = 256
FF_CHUNK = 256
VMEM_LIMIT = 60 * 1024 * 1024

C_RQ, C_RK, C_RV, C_RG = 0, 512, 1024, 1536
C_GQ, C_GK, C_GV, C_GG = 2048, 2304, 2560, 3072
C_LOW = 3584


def _silu(x):
    return x * (1.0 / (1.0 + jnp.exp(-x)))


def _log_sigmoid(z):
    return jnp.minimum(z, 0.0) - jnp.log(1.0 + jnp.exp(-jnp.abs(z)))


def _rms(x, w):
    return x * lax.rsqrt(jnp.mean(x * x, axis=-1, keepdims=True) + EPS) * w


def _dot(a, b):
    return jnp.dot(a, b, preferred_element_type=F32)


def _dot_nt(a, b):
    return lax.dot_general(a, b, (((1,), (1,)), ((), ())), preferred_element_type=F32)


def _dot_tn(a, b):
    return lax.dot_general(a, b, (((0,), (0,)), ((), ())), preferred_element_type=F32)


def _dot_select(sel, x):
    hi = x.astype(BF16)
    lo = (x - hi.astype(F32)).astype(BF16)
    return _dot(sel, hi) + _dot(sel, lo)


def _iota(shape, dim):
    return lax.broadcasted_iota(jnp.int32, shape, dim)


def _resident(shape):
    nd = len(shape)
    return pl.BlockSpec(shape, lambda *_: (0,) * nd, pipeline_mode=pl.Buffered(1))


def _cast_jobs(arrays, n_steps):
    specs, shapes = [], []
    for w in arrays:
        rows = next(r for r in range(BF16_ROWS, w.shape[0] + 1, BF16_ROWS)
                    if w.shape[0] % r == 0 and w.shape[0] // r <= n_steps)
        specs.append(pl.BlockSpec((rows, w.shape[1]),
                                  lambda i, last=w.shape[0] // rows - 1: (jnp.minimum(i, last), 0)))
        shapes.append(jax.ShapeDtypeStruct(w.shape, BF16))
    return specs, shapes


def _run_cast_jobs(src_refs, dst_refs):
    for src_ref, dst_ref in zip(src_refs, dst_refs):
        dst_ref[...] = src_ref[...].astype(BF16)


W_IN_BLOCK = 512


def _mod_kernel(*refs, n_cast):
    c_ref, w_ref, b_ref = refs[:3]
    cast_in, (wt_ref, wt_low_ref) = refs[3:3 + n_cast], refs[3 + n_cast:5 + n_cast]
    o_ref = refs[5 + n_cast]
    cast_out, (win_ref, wlow_ref) = refs[6 + n_cast:6 + 2 * n_cast], refs[6 + 2 * n_cast:]
    cond = _silu(c_ref[...]).astype(BF16)
    o_ref[...] = _dot(cond, w_ref[...].astype(BF16)) + b_ref[...]
    _run_cast_jobs(cast_in, cast_out)
    win_ref[...] = wt_ref[...].T.astype(BF16)
    low = wt_low_ref[...]
    low = jnp.concatenate([low, jnp.zeros((LANE - low.shape[0], low.shape[1]), F32)], axis=0)
    wlow_ref[...] = low.T.astype(BF16)


def _modulation(cvec, ada_w, ada_b, cast, w_in_t):
    d, n = ada_w.shape
    bn = n // 8 if n % (8 * LANE) == 0 else d
    steps = n // bn
    assert C_LOW % W_IN_BLOCK == 0 and C_LOW // W_IN_BLOCK <= steps
    n_low = w_in_t.shape[0] - C_LOW
    last = C_LOW // W_IN_BLOCK - 1
    cast_specs, cast_shapes = _cast_jobs(cast, steps)
    return pl.pallas_call(
        functools.partial(_mod_kernel, n_cast=len(cast)),
        grid=(steps,),
        in_specs=[pl.BlockSpec((8, d), lambda j: (0, 0)),
                  pl.BlockSpec((d, bn), lambda j: (0, j)),
                  pl.BlockSpec((1, bn), lambda j: (0, j))] + cast_specs
        + [pl.BlockSpec((W_IN_BLOCK, d), lambda j: (jnp.minimum(j, last), 0)),
           pl.BlockSpec((n_low, d), lambda j: (C_LOW // n_low, 0))],
        out_specs=[pl.BlockSpec((8, bn), lambda j: (0, j))] + cast_specs
        + [pl.BlockSpec((d, W_IN_BLOCK), lambda j: (0, jnp.minimum(j, last))),
           pl.BlockSpec((d, LANE), lambda j: (0, 0))],
        out_shape=[jax.ShapeDtypeStruct((8, n), F32)] + cast_shapes
        + [jax.ShapeDtypeStruct((d, C_LOW), BF16), jax.ShapeDtypeStruct((d, LANE), BF16)],
        compiler_params=pltpu.CompilerParams(dimension_semantics=("arbitrary",),
                                             vmem_limit_bytes=VMEM_LIMIT),
        name="mod",
    )(cvec, ada_w, ada_b.reshape(1, n), *cast, w_in_t, w_in_t)


def _swap32(x):
    lane = _iota(x.shape, 1)
    return jnp.where((lane & 63) < 32, pltpu.roll(x, 96, 1), pltpu.roll(x, 32, 1))


def _ffn_kernel(*refs, mod_off, proj, rope, final, n_main, n_cast, d, f):
    it = iter(refs)
    x_ref = next(it)
    if n_main is not None:
        xc_ref = next(it)
        is_ctx = pl.program_id(0) >= n_main
    m_ref, nw_ref, w1_ref, w3_ref, w2_ref = (next(it) for _ in range(5))
    if proj:
        n2w_ref, win_ref, wlow_ref, gw_ref, gbias_ref = (next(it) for _ in range(5))
        if rope:
            rowtab_ref, coltab_ref = next(it), next(it)
    if final:
        fnw_ref = next(it)
    cast_in = [next(it) for _ in range(n_cast)]
    xo_ref = next(it)
    if proj:
        (rq_ref, rk_ref, rv_ref, rg_ref, gq_ref, gk_ref, gv_ref, gg_ref,
         gf_ref, gb_ref) = (next(it) for _ in range(10))
    cast_out = [next(it) for _ in range(n_cast)]
    u_ref = next(it)
    _run_cast_jobs(cast_in, cast_out)

    def mod(i):
        return m_ref[:, (mod_off + i) * d:(mod_off + i + 1) * d]

    for rb in range(x_ref.shape[0] // ROW_BLOCK):
        rows = slice(rb * ROW_BLOCK, (rb + 1) * ROW_BLOCK)
        x = x_ref[rows, :]
        if n_main is not None:
            x = jnp.where(is_ctx, xc_ref[rows, :], x)
        h = (_rms(x, nw_ref[...]) * (1.0 + mod(1)) + mod(0)).astype(BF16)
        for k in range(f // FF_CHUNK):
            sl = slice(k * FF_CHUNK, (k + 1) * FF_CHUNK)
            a = _dot(h, w1_ref[:, sl])
            g = _dot(h, w3_ref[:, sl])
            u_ref[rows, sl] = (_silu(a) * g).astype(BF16)
        y = _dot(u_ref[rows, :], w2_ref[...])
        x1 = x + (0.5 * mod(2)) * y

        if final:
            xo_ref[rows, :] = _rms(x1, fnw_ref[...])
        else:
            xo_ref[rows, :] = x1

        if proj:
            h2 = (_rms(x1, n2w_ref[...]) * (1.0 + mod(4)) + mod(3)).astype(BF16)

            def p(lo, hi):
                return _dot(h2, win_ref[:, lo:hi])

            if rope:
                by_row = _iota((GRID_W, 2 * LANE), 1) % LANE < LANE // 2
                g0 = rb * (ROW_BLOCK // GRID_W)
                tab = jnp.concatenate(
                    [jnp.where(by_row, jnp.broadcast_to(rowtab_ref[g0 + g:g0 + g + 1, :], (GRID_W, 2 * LANE)),
                               coltab_ref[...]) for g in range(ROW_BLOCK // GRID_W)], axis=0)
                cos, sin = tab[:, :LANE], tab[:, LANE:]
                if n_main is not None:
                    cos = jnp.where(is_ctx, 1.0, cos)
                    sin = jnp.where(is_ctx, 0.0, sin)

            for base, scale, o_ref in ((C_RQ, RET_DK ** -0.5, rq_ref), (C_RK, 1.0, rk_ref)):
                t = p(base, base + RET_W)
                for hd in range(RET_HEADS):
                    th = t[:, hd * LANE:(hd + 1) * LANE] * scale
                    if rope:
                        th = th * cos + _swap32(th) * sin
                    o_ref[rows, hd * LANE:(hd + 1) * LANE] = th.astype(BF16)
            rv_ref[rows, :] = p(C_RV, C_RV + RET_W).astype(BF16)
            rg_ref[rows, :] = _silu(p(C_RG, C_RG + RET_W)).astype(BF16)
            gq_ref[rows, :] = (p(C_GQ, C_GQ + GLA_QK) * GLA_DK ** -0.5).astype(BF16)
            gk_ref[rows, :] = p(C_GK, C_GK + GLA_QK).astype(BF16)
            gv_ref[rows, :] = p(C_GV, C_GV + GLA_W).astype(BF16)
            gg_ref[rows, :] = _silu(p(C_GG, C_GG + GLA_W)).astype(BF16)
            low = _dot(h2, wlow_ref[...]).astype(BF16)
            z = _dot(low, gw_ref[...]) + gbias_ref[...]
            ls = _log_sigmoid(z) * (1.0 / GLA_TAU)
            gf_ref[rows, :] = ls[:, :GLA_QK]
            gb_ref[rows, :] = ls[:, GLA_QK:]


def _ffn_call(x, m3, tiles_per_seq, nw, w1, w3, w2, *, mod_off, tm, ctx=None, ctx_row=None, proj=None,
              rope=None, final_w=None, cast=(), name):
    t, d = x.shape
    f = w1.shape[1]
    n_main = t // tm
    n_tiles = n_main + (0 if ctx is None else ctx.shape[0] // tm)
    t = n_tiles * tm
    main = lambda i: jnp.minimum(i, n_main - 1)
    tile = lambda w: pl.BlockSpec((tm, w), lambda i: (i, 0))
    in_specs = [pl.BlockSpec((tm, d), lambda i: (main(i), 0))]
    args = [x]
    if ctx is None:
        row_of_tile = lambda i: i // tiles_per_seq
    else:
        row_of_tile = lambda i: jnp.where(i >= n_main, ctx_row, i // tiles_per_seq)
        in_specs.append(pl.BlockSpec((tm, d), lambda i: (jnp.maximum(i - n_main, 0), 0)))
        args.append(ctx)
    in_specs += [pl.BlockSpec((None, 1, N_MOD * d), lambda i: (row_of_tile(i), 0, 0)),
                 _resident((1, d)), _resident((d, f)), _resident((d, f)), _resident((f, d))]
    args += [m3, nw.reshape(1, d), w1, w3, w2]
    if proj is not None:
        n2w, win, wlow, gw, gbias = proj
        in_specs += [_resident((1, d))] + [_resident(a.shape) for a in (win, wlow, gw, gbias)]
        args += [n2w.reshape(1, d), win, wlow, gw, gbias]
        if rope is not None:
            rowtab, coltab = rope
            in_specs += [pl.BlockSpec((tm // GRID_W, 2 * LANE), lambda i: (main(i) % tiles_per_seq, 0)),
                         _resident(coltab.shape)]
            args += [rowtab, coltab]
    if final_w is not None:
        in_specs.append(_resident((1, d)))
        args.append(final_w.reshape(1, d))
    out_specs = [tile(d)]
    out_shape = [jax.ShapeDtypeStruct((t, d), F32)]
    if proj is not None:
        for w, dt in ((RET_W, BF16),) * 4 + ((GLA_QK, BF16),) * 2 + ((GLA_W, BF16),) * 2 + ((GLA_QK, F32),) * 2:
            out_specs.append(tile(w))
            out_shape.append(jax.ShapeDtypeStruct((t, w), dt))
    cast_specs, cast_shapes = _cast_jobs(cast, n_main)
    in_specs += cast_specs
    args += list(cast)
    out_specs += cast_specs
    out_shape += cast_shapes
    kern = functools.partial(_ffn_kernel, mod_off=mod_off, proj=proj is not None,
                             rope=rope is not None, final=final_w is not None,
                             n_main=None if ctx is None else n_main, n_cast=len(cast), d=d, f=f)
    return pl.pallas_call(
        kern, grid=(n_tiles,), in_specs=in_specs, out_specs=out_specs, out_shape=out_shape,
        scratch_shapes=[pltpu.VMEM((tm, f), BF16)],
        compiler_params=pltpu.CompilerParams(dimension_semantics=("arbitrary",),
                                             vmem_limit_bytes=VMEM_LIMIT),
        name=name,
    )(*args)


def _head_lanes(shape, hd):
    lane = _iota(shape, len(shape) - 1)
    return (lane >= 64) if hd % 2 else (lane < 64)


def _state_kernel(*refs, batch, cps):
    it = iter(refs)
    per_batch = lambda: [next(it) for _ in range(batch)]
    rkf_ref, rvf_ref, gkf_ref, gvf_ref, gf_ref = (per_batch() for _ in range(5))
    rkb_ref, rvb_ref, gkb_ref, gvb_ref, gb_ref = (per_batch() for _ in range(5))
    dec_ref = next(it)
    irf_ref, irb_ref, igf_ref, igb_ref = (next(it) for _ in range(4))
    orf_ref, orb_ref, ogf_ref, ogb_ref = (next(it) for _ in range(4))
    frf_ref, frb_ref, fgf_ref, fgb_ref = (next(it) for _ in range(4))
    srf, srb, sgf, sgb = (next(it) for _ in range(4))
    n = CHUNK

    @pl.when(pl.program_id(0) == 0)
    def _():
        srf[...] = irf_ref[...]
        srb[...] = irb_ref[...]
        sgf[...] = igf_ref[...]
        sgb[...] = igb_ref[...]

    row = _iota((n, LANE), 0).astype(F32)
    lg = _log_sigmoid(dec_ref[...])
    ri = _iota((n, n), 0)
    ci = _iota((n, n), 1)
    after = (ci > ri).astype(BF16)
    before = (ci < ri).astype(BF16)

    def gla(b, rows, g_ref, k_ref, v_ref, tri, edge, st):
        g = g_ref[b][rows, :]
        e = _dot_select(tri, g)
        tot = e[edge:edge + 1, :] + g[edge:edge + 1, :]
        kd = (k_ref[b][rows, :].astype(F32) * jnp.exp(e)).astype(BF16)
        for hd in range(GLA_HEADS):
            pr = slice((hd // 2) * LANE, (hd // 2 + 1) * LANE)
            upd = _dot_tn(v_ref[b][rows, hd * LANE:(hd + 1) * LANE], kd[:, pr])
            upd = jnp.where(_head_lanes(upd.shape, hd), upd, 0.0)
            st[b, hd] = st[b, hd] * jnp.exp(tot[:, pr]) + upd

    for j in range(cps):
        jb = cps - 1 - j
        fr = slice(j * n, (j + 1) * n)
        br = slice(jb * n, (jb + 1) * n)
        orf_ref[:, j] = srf[...].astype(BF16)
        ogf_ref[:, j] = sgf[...].astype(BF16)
        orb_ref[:, jb] = srb[...].astype(BF16)
        ogb_ref[:, jb] = sgb[...].astype(BF16)
        for b in range(batch):
            for hd in range(RET_HEADS):
                sl = slice(hd * LANE, (hd + 1) * LANE)
                lgf = lg[0:1, sl]
                lgb = lg[1:2, sl]
                kf = (rkf_ref[b][fr, sl].astype(F32) * jnp.exp((n - 1.0 - row) * lgf)).astype(BF16)
                srf[b, hd] = srf[b, hd] * jnp.exp(n * lgf) + _dot_tn(kf, rvf_ref[b][fr, sl])
                kb = (rkb_ref[b][br, sl].astype(F32) * jnp.exp(row * lgb)).astype(BF16)
                srb[b, hd] = srb[b, hd] * jnp.exp(n * lgb) + _dot_tn(kb, rvb_ref[b][br, sl])
            gla(b, fr, gf_ref, gkf_ref, gvf_ref, after, 0, sgf)
            gla(b, br, gb_ref, gkb_ref, gvb_ref, before, n - 1, sgb)

    frf_ref[...] = srf[...]
    frb_ref[...] = srb[...]
    fgf_ref[...] = sgf[...]
    fgb_ref[...] = sgb[...]


def _states(rk, rv, gk, gv, gf, gb, dec, init, batch, first_chunk, nc):
    cps = next(c for c in (4, 2, 1) if nc % c == 0 and first_chunk % c == 0)
    steps = nc // cps

    def per_batch(w, backward):
        at = (lambda c: steps - 1 - c) if backward else (lambda c: c)
        return [pl.BlockSpec((cps * CHUNK, w), lambda c, b=b: ((first_chunk + b * nc) // cps + at(c), 0))
                for b in range(batch)]

    fwd = lambda w: per_batch(w, False)
    bwd = lambda w: per_batch(w, True)
    st_shape = (batch, RET_HEADS, LANE, LANE)
    init_spec = pl.BlockSpec(st_shape, lambda c: (0, 0, 0, 0))
    chunk_shape = (batch, cps, RET_HEADS, LANE, LANE)
    of_spec = pl.BlockSpec(chunk_shape, lambda c: (0, c, 0, 0, 0))
    ob_spec = pl.BlockSpec(chunk_shape, lambda c: (0, steps - 1 - c, 0, 0, 0))
    per_chunk = jax.ShapeDtypeStruct((batch, nc, RET_HEADS, LANE, LANE), BF16)
    final = jax.ShapeDtypeStruct(st_shape, F32)
    rep = lambda a: [a] * batch
    return pl.pallas_call(
        functools.partial(_state_kernel, batch=batch, cps=cps),
        grid=(steps,),
        in_specs=fwd(RET_W) + fwd(RET_W) + fwd(GLA_QK) + fwd(GLA_W) + fwd(GLA_QK)
        + bwd(RET_W) + bwd(RET_W) + bwd(GLA_QK) + bwd(GLA_W) + bwd(GLA_QK)
        + [pl.BlockSpec(dec.shape, lambda c: (0, 0))] + [init_spec] * 4,
        out_specs=[of_spec, ob_spec, of_spec, ob_spec] + [init_spec] * 4,
        out_shape=[per_chunk] * 4 + [final] * 4,
        scratch_shapes=[pltpu.VMEM(st_shape, F32)] * 4,
        compiler_params=pltpu.CompilerParams(dimension_semantics=("arbitrary",),
                                             vmem_limit_bytes=VMEM_LIMIT),
        name="states",
    )(*rep(rk), *rep(rv), *rep(gk), *rep(gv), *rep(gf), *rep(rk), *rep(rv), *rep(gk), *rep(gv), *rep(gb),
      dec, *init)


def _block_row(x, parent, r):
    n, w = x.shape
    if parent == n:
        return jnp.broadcast_to(x[r:r + 1, :], (n, w))
    x3 = x.reshape(n // parent, parent, w)
    return jnp.broadcast_to(x3[:, r:r + 1, :], x3.shape).reshape(n, w)


def _gla_levels(q, k_even, k_odd, gf, gb, bf, bb):
    n = q.shape[0]
    row = _iota(q.shape, 0)

    def factors(u_exp, w_exp):
        ew = jnp.exp2(w_exp).astype(BF16)
        return (q * jnp.exp2(u_exp).astype(BF16), k_even * ew, k_odd * ew)

    levels = []
    s = n // 2
    while s >= 2:
        if s >= 4:
            last_of_first = _block_row(bf, 2 * s, s - 1)
            first_of_second = _block_row(bb, 2 * s, s)
        else:
            upper = (row & 4) != 0
            last_of_first = jnp.where(upper, _block_row(bf, 8, 5), _block_row(bf, 8, 1))
            first_of_second = jnp.where(upper, _block_row(bb, 8, 6), _block_row(bb, 8, 2))
        df = bf - last_of_first
        db = bb - first_of_second
        levels.append((2 * s, [factors(jnp.minimum(df, db), -jnp.maximum(df, db))]))
        s //= 2
    odd = (row & 1) == 1
    qa = q * jnp.where(odd, jnp.exp2(gf), 2.0).astype(BF16)
    qb = q * jnp.where(odd, 2.0, jnp.exp2(gb)).astype(BF16)
    zero = jnp.zeros_like(k_even)
    ev = lambda x: jnp.where(odd, zero, x)
    od = lambda x: jnp.where(odd, x, zero)
    levels.append((2, [(qa, ev(k_even), ev(k_odd)), (qb, od(k_even), od(k_odd))]))
    return levels


def _mix_kernel(rq_ref, rk_ref, rv_ref, rg_ref, gq_ref, gk_ref, gv_ref, gg_ref, gf_ref, gb_ref,
                srf_ref, srb_ref, sgf_ref, sgb_ref, dec_ref, rnw_ref, gnw_ref,
                x_ref, m_ref, wout_ref, o_ref, mix_ref, decay_ref, qdec_ref, *, d):
    n = CHUNK
    half = n // 2
    ri = _iota((n, n), 0)
    ci = _iota((n, n), 1)
    step = pl.program_id(0)
    cur = step % 2

    @pl.when(step == 0)
    def _():
        mix_ref[1] = jnp.zeros(mix_ref.shape[1:], BF16)
        lg = _log_sigmoid(dec_ref[...])
        dist = (ri - ci).astype(F32)
        row = _iota((n, LANE), 0).astype(F32)
        for hd in range(RET_HEADS):
            sl = slice(hd * LANE, (hd + 1) * LANE)
            lgf = lg[0:1, sl]
            lgb = lg[1:2, sl]
            decay_ref[hd] = jnp.exp(jnp.where(dist > 0, dist * lgf[:, 0:1],
                                              jnp.where(dist < 0, -dist * lgb[:, 0:1], jnp.log(2.0))))
            qdec_ref[hd] = jnp.exp((row + 1.0) * lgf)
            qdec_ref[RET_HEADS + hd] = jnp.exp((n - row) * lgb)

    for hd in range(RET_HEADS):
        sl = slice(hd * LANE, (hd + 1) * LANE)
        q = rq_ref[:, sl]
        p = (_dot_nt(q, rk_ref[:, sl]) * decay_ref[hd]).astype(BF16)
        qf = q.astype(F32)
        qs = jnp.concatenate([(qf * qdec_ref[hd]).astype(BF16),
                              (qf * qdec_ref[RET_HEADS + hd]).astype(BF16)], axis=1)
        st = jnp.concatenate([srf_ref[hd], srb_ref[hd]], axis=0)
        o = _dot(p, rv_ref[:, sl]) + _dot(qs, st)
        mu = jnp.mean(o, axis=-1, keepdims=True)
        oc = o - mu
        var = jnp.mean(oc * oc, axis=-1, keepdims=True)
        r = oc * lax.rsqrt(var + EPS) * rnw_ref[:, sl] * rg_ref[:, sl].astype(F32)
        mix_ref[cur, :, sl] = r.astype(BF16)

    gf = gf_ref[...] * LOG2E
    gb = gb_ref[...] * LOG2E
    bf = _dot_select((ci <= ri).astype(BF16), gf)
    bb = _dot_select((ci >= ri).astype(BF16), gb)
    q = gq_ref[...]
    k = gk_ref[...]
    even_head = (_iota(k.shape, 1) & GLA_DK) == 0
    no_k = jnp.zeros_like(k)
    o_ref[...] = x_ref[...] + m_ref[:, 5 * d:6 * d] * _dot(mix_ref[1 - cur], wout_ref[...])
    levels = _gla_levels(q, jnp.where(even_head, k, no_k), jnp.where(even_head, no_k, k), gf, gb, bf, bb)
    qsf = q * jnp.exp2(bf).astype(BF16)
    qsb = q * jnp.exp2(bb).astype(BF16)
    code = _iota((half, n), 0) ^ (_iota((half, n), 1) & (half - 1))
    for pair in range(GLA_HEADS // 2):
        pr = slice(pair * LANE, (pair + 1) * LANE)
        near = [None, None]
        far = [None, None]
        for size, blocks in levels:
            for rb in range(2):
                kb = rb if size < n else 1 - rb
                rows = slice(rb * half, (rb + 1) * half)
                keys = slice(kb * half, (kb + 1) * half)
                u = jnp.concatenate([blk[0][rows, pr] for blk in blocks], axis=1)
                w = jnp.concatenate([jnp.concatenate([blk[j][keys, pr] for blk in blocks], axis=1)
                                     for j in (1, 2)], axis=0)
                t = _dot_nt(u, w)
                if size == n:
                    far[rb] = t
                else:
                    near[rb] = t if near[rb] is None else jnp.where(code < size, t, near[rb])
        qcat = jnp.concatenate([qsf[:, pr], qsb[:, pr]], axis=1)
        for j in range(2):
            hd = 2 * pair + j
            sl = slice(hd * LANE, (hd + 1) * LANE)
            mine = slice(j * half, (j + 1) * half)
            p = jnp.concatenate([jnp.concatenate([near[0][:, mine], far[0][:, mine]], axis=1),
                                 jnp.concatenate([far[1][:, mine], near[1][:, mine]], axis=1)], axis=0)
            st = jnp.concatenate([sgf_ref[hd], sgb_ref[hd]], axis=1)
            o = _dot(p.astype(BF16), gv_ref[:, sl]) + _dot_nt(qcat, st)
            ms = jnp.mean(o * o, axis=-1, keepdims=True)
            r = o * lax.rsqrt(ms + EPS) * gnw_ref[:, sl] * gg_ref[:, sl].astype(F32)
            mix_ref[cur, :, RET_W + hd * LANE:RET_W + (hd + 1) * LANE] = r.astype(BF16)


def _mix_call(mix_in, states, dec, rnw, gnw, x1, m3, wout, batch, nc):
    n_chunks = batch * nc
    t, d = n_chunks * CHUNK, x1.shape[1]
    mixed = lambda i: jnp.minimum(i, n_chunks - 1)
    done = lambda i: jnp.maximum(i - 1, 0)
    tile = lambda w: pl.BlockSpec((CHUNK, w), lambda i: (mixed(i), 0))
    st_spec = pl.BlockSpec((None, None, RET_HEADS, LANE, LANE),
                           lambda i: (mixed(i) // nc, mixed(i) % nc, 0, 0, 0))
    const = lambda shape: pl.BlockSpec(shape, lambda i: (0,) * len(shape))
    widths = (RET_W,) * 4 + (GLA_QK,) * 2 + (GLA_W,) * 2 + (GLA_QK,) * 2
    return pl.pallas_call(
        functools.partial(_mix_kernel, d=d),
        grid=(n_chunks + 1,),
        in_specs=[tile(w) for w in widths] + [st_spec] * 4
        + [const(dec.shape), const((1, RET_W)), const((1, GLA_W)),
           pl.BlockSpec((CHUNK, d), lambda i: (done(i), 0)),
           pl.BlockSpec((None, 1, N_MOD * d), lambda i: (done(i) // nc, 0, 0)),
           const(wout.shape)],
        out_specs=pl.BlockSpec((CHUNK, d), lambda i: (done(i), 0)),
        out_shape=jax.ShapeDtypeStruct((t, d), F32),
        scratch_shapes=[pltpu.VMEM((2, CHUNK, RET_W + GLA_W), BF16),
                        pltpu.VMEM((RET_HEADS, CHUNK, CHUNK), F32),
                        pltpu.VMEM((2 * RET_HEADS, CHUNK, LANE), F32)],
        compiler_params=pltpu.CompilerParams(dimension_semantics=("arbitrary",),
                                             vmem_limit_bytes=VMEM_LIMIT),
        name="mix",
    )(*mix_in, *states, dec, rnw.reshape(1, RET_W), gnw.reshape(1, GLA_W), x1, m3, wout)


def _rope_tables(n_tok):
    freqs = ROPE_BASE ** (-jnp.arange(RET_DK // 4, dtype=F32) / (RET_DK // 4))

    def table(n_pos, first_half):
        ang = jnp.arange(n_pos, dtype=F32)[:, None] * freqs
        zero = jnp.zeros((n_pos, LANE // 2), F32)
        cos = jnp.concatenate([jnp.cos(ang)] * 2, axis=-1)
        sin = jnp.concatenate([-jnp.sin(ang), jnp.sin(ang)], axis=-1)
        halves = (cos, zero, sin, zero) if first_half else (zero, cos, zero, sin)
        return jnp.concatenate(halves, axis=-1)

    return table(n_tok // GRID_W, True), table(GRID_W, False)


def _pack_gate(w_f, b_f, w_b, b_b):
    gw = jnp.zeros((LANE, 2 * GLA_QK), F32)
    gw = gw.at[:GLA_RANK, :GLA_QK].set(w_f).at[GLA_RANK:2 * GLA_RANK, GLA_QK:].set(w_b)
    return gw.astype(BF16), jnp.concatenate([b_f, b_b]).reshape(1, 2 * GLA_QK)


def kernel(x, c, ctx, c_ctx, ada_w, ada_b, norm1_w, ffn1_w1, ffn1_w3, ffn1_w2, norm2_w, w_in,
           ret_decay_f, ret_decay_b, ret_norm_w, gla_gate_w_f, gla_gate_b_f, gla_gate_w_b, gla_gate_b_b,
           gla_norm_w, w_out, norm3_w, ffn2_w1, ffn2_w3, ffn2_w2, final_norm_w):
    batch, n_tok, d = x.shape
    n_ctx = ctx.shape[1]
    depth = ada_w.shape[0]
    assert depth == 1 and batch + 1 <= 8
    assert n_tok % FFN_TILE == 0 and n_tok % CHUNK == 0 and n_ctx % CHUNK == 0
    assert (batch * n_ctx) % FFN_TILE == 0

    cvec = jnp.zeros((8, d), F32).at[:batch].set(c).at[batch].set(c_ctx)
    m, w1a, w3a, w2a, w_in_b, w_low_b = _modulation(cvec, ada_w[0], ada_b[0],
                                                    (ffn1_w1[0], ffn1_w3[0], ffn1_w2[0]), w_in[0].T)
    m3 = m.reshape(8, 1, N_MOD * d)

    gw, gbias = _pack_gate(gla_gate_w_f[0], gla_gate_b_f[0], gla_gate_w_b[0], gla_gate_b_b[0])
    proj = (norm2_w[0], w_in_b, w_low_b, gw, gbias)
    f1 = (norm1_w[0], w1a, w3a, w2a)
    rowtab, coltab = _rope_tables(n_tok)
    dec = jnp.zeros((8, RET_W), F32)
    dec = dec.at[0].set(jnp.repeat(ret_decay_f[0], LANE)).at[1].set(jnp.repeat(ret_decay_b[0], LANE))

    tiles_per_seq = n_tok // FFN_TILE
    first = _ffn_call(x.reshape(batch * n_tok, d), m3, tiles_per_seq, *f1, mod_off=0, tm=FFN_TILE,
                      ctx=ctx.reshape(batch * n_ctx, d), ctx_row=batch, proj=proj, rope=(rowtab, coltab),
                      cast=(ffn2_w1[0], ffn2_w3[0], ffn2_w2[0], w_out[0]), name="ffn_in")
    x1, mix_in, (w1b, w3b, w2b, woutb) = first[0], first[1:11], first[11:]
    scan_in = (mix_in[1], mix_in[2], mix_in[5], mix_in[6], mix_in[8], mix_in[9])
    zero = jnp.zeros((batch, RET_HEADS, LANE, LANE), F32)
    nc, nc_ctx = n_tok // CHUNK, n_ctx // CHUNK
    ctx_states = _states(*scan_in, dec, (zero,) * 4, batch, batch * nc, nc_ctx)[4:]
    states = _states(*scan_in, dec, ctx_states, batch, 0, nc)[:4]
    x2 = _mix_call(mix_in, states, dec, ret_norm_w[0], gla_norm_w[0], x1, m3, woutb, batch, nc)
    out = _ffn_call(x2, m3, tiles_per_seq, norm3_w[0], w1b, w3b, w2b, mod_off=6, tm=FFN_TILE,
                    final_w=final_norm_w, name="ffn_out")[0]
    return out.reshape(batch, n_tok, d)
```

```python
import functools

import jax
import jax.numpy as jnp
from jax import lax
from jax.experimental import pallas as pl
from jax.experimental.pallas import tpu as pltpu

F32 = jnp.float32
BF16 = jnp.bfloat16

EPS = 1e-6
LOG2E = 1.4426950408889634
N_MOD = 9
GRID_W = 64
ROPE_BASE = 10000.0
RET_HEADS = 4
RET_DK = 128
RET_DV = 128
GLA_HEADS = 4
GLA_DK = 64
GLA_DV = 128
GLA_RANK = 16
GLA_TAU = 16.0
RET_W = RET_HEADS * RET_DV
GLA_W = GLA_HEADS * GLA_DV
GLA_QK = GLA_HEADS * GLA_DK

LANE = 128
BF16_ROWS = 16
CHUNK = 256
MIX_CHUNKS = 2
FFN_TILE = 512
ROW_BLOCK = 256
FF_CHUNK = 256
VMEM_LIMIT = 60 * 1024 * 1024

C_RQ, C_RK, C_RV, C_RG = 0, 512, 1024, 1536
C_GQ, C_GK, C_GV, C_GG = 2048, 2304, 2560, 3072
C_LOW = 3584


def _silu(x):
    return x * (1.0 / (1.0 + jnp.exp(-x)))


def _log_sigmoid(z):
    return jnp.minimum(z, 0.0) - jnp.log(1.0 + jnp.exp(-jnp.abs(z)))


def _rms(x, w):
    return x * lax.rsqrt(jnp.mean(x * x, axis=-1, keepdims=True) + EPS) * w


def _dot(a, b):
    return jnp.dot(a, b, preferred_element_type=F32)


def _dot_nt(a, b):
    return lax.dot_general(a, b, (((1,), (1,)), ((), ())), preferred_element_type=F32)


def _dot_tn(a, b):
    return lax.dot_general(a, b, (((0,), (0,)), ((), ())), preferred_element_type=F32)


def _dot_select(sel, x):
    hi = x.astype(BF16)
    lo = (x - hi.astype(F32)).astype(BF16)
    return _dot(sel, hi) + _dot(sel, lo)


def _iota(shape, dim):
    return lax.broadcasted_iota(jnp.int32, shape, dim)


def _resident(shape):
    nd = len(shape)
    return pl.BlockSpec(shape, lambda *_: (0,) * nd, pipeline_mode=pl.Buffered(1))


def _cast_jobs(arrays, n_steps):
    specs, shapes = [], []
    for w in arrays:
        rows = next(r for r in range(BF16_ROWS, w.shape[0] + 1, BF16_ROWS)
                    if w.shape[0] % r == 0 and w.shape[0] // r <= n_steps)
        specs.append(pl.BlockSpec((rows, w.shape[1]),
                                  lambda i, last=w.shape[0] // rows - 1: (jnp.minimum(i, last), 0)))
        shapes.append(jax.ShapeDtypeStruct(w.shape, BF16))
    return specs, shapes


def _run_cast_jobs(src_refs, dst_refs):
    for src_ref, dst_ref in zip(src_refs, dst_refs):
        dst_ref[...] = src_ref[...].astype(BF16)


W_IN_BLOCK = 512


def _mod_kernel(*refs, n_cast):
    c_ref, w_ref, b_ref = refs[:3]
    cast_in, (wt_ref, wt_low_ref) = refs[3:3 + n_cast], refs[3 + n_cast:5 + n_cast]
    o_ref = refs[5 + n_cast]
    cast_out, (win_ref, wlow_ref) = refs[6 + n_cast:6 + 2 * n_cast], refs[6 + 2 * n_cast:]
    cond = _silu(c_ref[...]).astype(BF16)
    o_ref[...] = _dot(cond, w_ref[...].astype(BF16)) + b_ref[...]
    _run_cast_jobs(cast_in, cast_out)
    win_ref[...] = wt_ref[...].T.astype(BF16)
    low = wt_low_ref[...]
    low = jnp.concatenate([low, jnp.zeros((LANE - low.shape[0], low.shape[1]), F32)], axis=0)
    wlow_ref[...] = low.T.astype(BF16)


def _modulation(cvec, ada_w, ada_b, cast, w_in_t):
    d, n = ada_w.shape
    bn = n // 8 if n % (8 * LANE) == 0 else d
    steps = n // bn
    assert C_LOW % W_IN_BLOCK == 0 and C_LOW // W_IN_BLOCK <= steps
    n_low = w_in_t.shape[0] - C_LOW
    last = C_LOW // W_IN_BLOCK - 1
    cast_specs, cast_shapes = _cast_jobs(cast, steps)
    return pl.pallas_call(
        functools.partial(_mod_kernel, n_cast=len(cast)),
        grid=(steps,),
        in_specs=[pl.BlockSpec((8, d), lambda j: (0, 0)),
                  pl.BlockSpec((d, bn), lambda j: (0, j)),
                  pl.BlockSpec((1, bn), lambda j: (0, j))] + cast_specs
        + [pl.BlockSpec((W_IN_BLOCK, d), lambda j: (jnp.minimum(j, last), 0)),
           pl.BlockSpec((n_low, d), lambda j: (C_LOW // n_low, 0))],
        out_specs=[pl.BlockSpec((8, bn), lambda j: (0, j))] + cast_specs
        + [pl.BlockSpec((d, W_IN_BLOCK), lambda j: (0, jnp.minimum(j, last))),
           pl.BlockSpec((d, LANE), lambda j: (0, 0))],
        out_shape=[jax.ShapeDtypeStruct((8, n), F32)] + cast_shapes
        + [jax.ShapeDtypeStruct((d, C_LOW), BF16), jax.ShapeDtypeStruct((d, LANE), BF16)],
        compiler_params=pltpu.CompilerParams(dimension_semantics=("arbitrary",),
                                             vmem_limit_bytes=VMEM_LIMIT),
        name="mod",
    )(cvec, ada_w, ada_b.reshape(1, n), *cast, w_in_t, w_in_t)


def _swap32(x):
    lane = _iota(x.shape, 1)
    return jnp.where((lane & 63) < 32, pltpu.roll(x, 96, 1), pltpu.roll(x, 32, 1))


def _ffn_kernel(*refs, mod_off, proj, rope, final, n_main, n_cast, d, f):
    it = iter(refs)
    x_ref = next(it)
    if n_main is not None:
        xc_ref = next(it)
        is_ctx = pl.program_id(0) >= n_main
    m_ref, nw_ref, w1_ref, w3_ref, w2_ref = (next(it) for _ in range(5))
    if proj:
        n2w_ref, win_ref, wlow_ref, gw_ref, gbias_ref = (next(it) for _ in range(5))
        if rope:
            rowtab_ref, coltab_ref = next(it), next(it)
    if final:
        fnw_ref = next(it)
    cast_in = [next(it) for _ in range(n_cast)]
    xo_ref = next(it)
    if proj:
        (rq_ref, rk_ref, rv_ref, rg_ref, gq_ref, gk_ref, gv_ref, gg_ref,
         gf_ref, gb_ref) = (next(it) for _ in range(10))
    cast_out = [next(it) for _ in range(n_cast)]
    u_ref = next(it)
    _run_cast_jobs(cast_in, cast_out)

    def mod(i):
        return m_ref[:, (mod_off + i) * d:(mod_off + i + 1) * d]

    for rb in range(x_ref.shape[0] // ROW_BLOCK):
        rows = slice(rb * ROW_BLOCK, (rb + 1) * ROW_BLOCK)
        x = x_ref[rows, :]
        if n_main is not None:
            x = jnp.where(is_ctx, xc_ref[rows, :], x)
        h = (_rms(x, nw_ref[...]) * (1.0 + mod(1)) + mod(0)).astype(BF16)
        for k in range(f // FF_CHUNK):
            sl = slice(k * FF_CHUNK, (k + 1) * FF_CHUNK)
            a = _dot(h, w1_ref[:, sl])
            g = _dot(h, w3_ref[:, sl])
            u_ref[rows, sl] = (_silu(a) * g).astype(BF16)
        y = _dot(u_ref[rows, :], w2_ref[...])
        x1 = x + (0.5 * mod(2)) * y

        if final:
            xo_ref[rows, :] = _rms(x1, fnw_ref[...])
        else:
            xo_ref[rows, :] = x1

        if proj:
            h2 = (_rms(x1, n2w_ref[...]) * (1.0 + mod(4)) + mod(3)).astype(BF16)

            def p(lo, hi):
                return _dot(h2, win_ref[:, lo:hi])

            if rope:
                by_row = _iota((GRID_W, 2 * LANE), 1) % LANE < LANE // 2
                g0 = rb * (ROW_BLOCK // GRID_W)
                tab = jnp.concatenate(
                    [jnp.where(by_row, jnp.broadcast_to(rowtab_ref[g0 + g:g0 + g + 1, :], (GRID_W, 2 * LANE)),
                               coltab_ref[...]) for g in range(ROW_BLOCK // GRID_W)], axis=0)
                cos, sin = tab[:, :LANE], tab[:, LANE:]
                if n_main is not None:
                    cos = jnp.where(is_ctx, 1.0, cos)
                    sin = jnp.where(is_ctx, 0.0, sin)

            low = _dot(h2, wlow_ref[...]).astype(BF16)
            z = _dot(low, gw_ref[...]) + gbias_ref[...]
            ls = _log_sigmoid(z) * (1.0 / GLA_TAU)
            gf_ref[rows, :] = ls[:, :GLA_QK]
            gb_ref[rows, :] = ls[:, GLA_QK:]
            for base, scale, o_ref in ((C_RQ, RET_DK ** -0.5, rq_ref), (C_RK, 1.0, rk_ref)):
                t = p(base, base + RET_W)
                for hd in range(RET_HEADS):
                    th = t[:, hd * LANE:(hd + 1) * LANE] * scale
                    if rope:
                        th = th * cos + _swap32(th) * sin
                    o_ref[rows, hd * LANE:(hd + 1) * LANE] = th.astype(BF16)
            rg_ref[rows, :] = _silu(p(C_RG, C_RG + RET_W)).astype(BF16)
            gg_ref[rows, :] = _silu(p(C_GG, C_GG + GLA_W)).astype(BF16)
            gq_ref[rows, :] = (p(C_GQ, C_GQ + GLA_QK) * GLA_DK ** -0.5).astype(BF16)
            gk_ref[rows, :] = p(C_GK, C_GK + GLA_QK).astype(BF16)
            rv_ref[rows, :] = p(C_RV, C_RV + RET_W).astype(BF16)
            gv_ref[rows, :] = p(C_GV, C_GV + GLA_W).astype(BF16)


def _ffn_call(x, m3, tiles_per_seq, nw, w1, w3, w2, *, mod_off, tm, ctx=None, ctx_row=None, proj=None,
              rope=None, final_w=None, cast=(), name):
    t, d = x.shape
    f = w1.shape[1]
    n_main = t // tm
    n_tiles = n_main + (0 if ctx is None else ctx.shape[0] // tm)
    t = n_tiles * tm
    main = lambda i: jnp.minimum(i, n_main - 1)
    tile = lambda w: pl.BlockSpec((tm, w), lambda i: (i, 0))
    in_specs = [pl.BlockSpec((tm, d), lambda i: (main(i), 0))]
    args = [x]
    if ctx is None:
        row_of_tile = lambda i: i // tiles_per_seq
    else:
        row_of_tile = lambda i: jnp.where(i >= n_main, ctx_row, i // tiles_per_seq)
        in_specs.append(pl.BlockSpec((tm, d), lambda i: (jnp.maximum(i - n_main, 0), 0)))
        args.append(ctx)
    in_specs += [pl.BlockSpec((None, 1, N_MOD * d), lambda i: (row_of_tile(i), 0, 0)),
                 _resident((1, d)), _resident((d, f)), _resident((d, f)), _resident((f, d))]
    args += [m3, nw.reshape(1, d), w1, w3, w2]
    if proj is not None:
        n2w, win, wlow, gw, gbias = proj
        in_specs += [_resident((1, d))] + [_resident(a.shape) for a in (win, wlow, gw, gbias)]
        args += [n2w.reshape(1, d), win, wlow, gw, gbias]
        if rope is not None:
            rowtab, coltab = rope
            in_specs += [pl.BlockSpec((tm // GRID_W, 2 * LANE), lambda i: (main(i) % tiles_per_seq, 0)),
                         _resident(coltab.shape)]
            args += [rowtab, coltab]
    if final_w is not None:
        in_specs.append(_resident((1, d)))
        args.append(final_w.reshape(1, d))
    out_specs = [tile(d)]
    out_shape = [jax.ShapeDtypeStruct((t, d), F32)]
    if proj is not None:
        for w, dt in ((RET_W, BF16),) * 4 + ((GLA_QK, BF16),) * 2 + ((GLA_W, BF16),) * 2 + ((GLA_QK, F32),) * 2:
            out_specs.append(tile(w))
            out_shape.append(jax.ShapeDtypeStruct((t, w), dt))
    cast_specs, cast_shapes = _cast_jobs(cast, n_main)
    in_specs += cast_specs
    args += list(cast)
    out_specs += cast_specs
    out_shape += cast_shapes
    kern = functools.partial(_ffn_kernel, mod_off=mod_off, proj=proj is not None,
                             rope=rope is not None, final=final_w is not None,
                             n_main=None if ctx is None else n_main, n_cast=len(cast), d=d, f=f)
    return pl.pallas_call(
        kern, grid=(n_tiles,), in_specs=in_specs, out_specs=out_specs, out_shape=out_shape,
        scratch_shapes=[pltpu.VMEM((tm, f), BF16)],
        compiler_params=pltpu.CompilerParams(dimension_semantics=("arbitrary",),
                                             vmem_limit_bytes=VMEM_LIMIT),
        name=name,
    )(*args)


def _head_lanes(shape, hd):
    lane = _iota(shape, len(shape) - 1)
    return (lane >= 64) if hd % 2 else (lane < 64)


def _state_kernel(*refs, batch, cps):
    it = iter(refs)
    per_batch = lambda: [next(it) for _ in range(batch)]
    rkf_ref, rvf_ref, gkf_ref, gvf_ref, gf_ref = (per_batch() for _ in range(5))
    rkb_ref, rvb_ref, gkb_ref, gvb_ref, gb_ref = (per_batch() for _ in range(5))
    dec_ref = next(it)
    irf_ref, irb_ref, igf_ref, igb_ref = (next(it) for _ in range(4))
    orf_ref, orb_ref, ogf_ref, ogb_ref = (next(it) for _ in range(4))
    frf_ref, frb_ref, fgf_ref, fgb_ref = (next(it) for _ in range(4))
    srf, srb, sgf, sgb = (next(it) for _ in range(4))
    n = CHUNK

    @pl.when(pl.program_id(0) == 0)
    def _():
        srf[...] = irf_ref[...]
        srb[...] = irb_ref[...]
        sgf[...] = igf_ref[...]
        sgb[...] = igb_ref[...]

    row = _iota((n, LANE), 0).astype(F32)
    lg = _log_sigmoid(dec_ref[...])
    ri = _iota((n, n), 0)
    ci = _iota((n, n), 1)
    after = (ci > ri).astype(BF16)
    before = (ci < ri).astype(BF16)

    def gla(b, rows, g_ref, k_ref, v_ref, tri, edge, st):
        g = g_ref[b][rows, :]
        e = _dot_select(tri, g)
        tot = e[edge:edge + 1, :] + g[edge:edge + 1, :]
        kd = (k_ref[b][rows, :].astype(F32) * jnp.exp(e)).astype(BF16)
        for hd in range(GLA_HEADS):
            pr = slice((hd // 2) * LANE, (hd // 2 + 1) * LANE)
            upd = _dot_tn(v_ref[b][rows, hd * LANE:(hd + 1) * LANE], kd[:, pr])
            upd = jnp.where(_head_lanes(upd.shape, hd), upd, 0.0)
            st[b, hd] = st[b, hd] * jnp.exp(tot[:, pr]) + upd

    for j in range(cps):
        jb = cps - 1 - j
        fr = slice(j * n, (j + 1) * n)
        br = slice(jb * n, (jb + 1) * n)
        orf_ref[:, j] = srf[...].astype(BF16)
        ogf_ref[:, j] = sgf[...].astype(BF16)
        orb_ref[:, jb] = srb[...].astype(BF16)
        ogb_ref[:, jb] = sgb[...].astype(BF16)
        for b in range(batch):
            for hd in range(RET_HEADS):
                sl = slice(hd * LANE, (hd + 1) * LANE)
                lgf = lg[0:1, sl]
                lgb = lg[1:2, sl]
                kf = (rkf_ref[b][fr, sl].astype(F32) * jnp.exp((n - 1.0 - row) * lgf)).astype(BF16)
                srf[b, hd] = srf[b, hd] * jnp.exp(n * lgf) + _dot_tn(kf, rvf_ref[b][fr, sl])
                kb = (rkb_ref[b][br, sl].astype(F32) * jnp.exp(row * lgb)).astype(BF16)
                srb[b, hd] = srb[b, hd] * jnp.exp(n * lgb) + _dot_tn(kb, rvb_ref[b][br, sl])
            gla(b, fr, gf_ref, gkf_ref, gvf_ref, after, 0, sgf)
            gla(b, br, gb_ref, gkb_ref, gvb_ref, before, n - 1, sgb)

    frf_ref[...] = srf[...]
    frb_ref[...] = srb[...]
    fgf_ref[...] = sgf[...]
    fgb_ref[...] = sgb[...]


def _states(rk, rv, gk, gv, gf, gb, dec, init, batch, first_chunk, nc):
    cps = 2 if nc % 2 == 0 and first_chunk % 2 == 0 else 1
    steps = nc // cps

    def per_batch(w, backward):
        at = (lambda c: steps - 1 - c) if backward else (lambda c: c)
        return [pl.BlockSpec((cps * CHUNK, w), lambda c, b=b: ((first_chunk + b * nc) // cps + at(c), 0))
                for b in range(batch)]

    fwd = lambda w: per_batch(w, False)
    bwd = lambda w: per_batch(w, True)
    st_shape = (batch, RET_HEADS, LANE, LANE)
    init_spec = pl.BlockSpec(st_shape, lambda c: (0, 0, 0, 0))
    chunk_shape = (batch, cps, RET_HEADS, LANE, LANE)
    of_spec = pl.BlockSpec(chunk_shape, lambda c: (0, c, 0, 0, 0))
    ob_spec = pl.BlockSpec(chunk_shape, lambda c: (0, steps - 1 - c, 0, 0, 0))
    per_chunk = jax.ShapeDtypeStruct((batch, nc, RET_HEADS, LANE, LANE), BF16)
    final = jax.ShapeDtypeStruct(st_shape, F32)
    rep = lambda a: [a] * batch
    return pl.pallas_call(
        functools.partial(_state_kernel, batch=batch, cps=cps),
        grid=(steps,),
        in_specs=fwd(RET_W) + fwd(RET_W) + fwd(GLA_QK) + fwd(GLA_W) + fwd(GLA_QK)
        + bwd(RET_W) + bwd(RET_W) + bwd(GLA_QK) + bwd(GLA_W) + bwd(GLA_QK)
        + [pl.BlockSpec(dec.shape, lambda c: (0, 0))] + [init_spec] * 4,
        out_specs=[of_spec, ob_spec, of_spec, ob_spec] + [init_spec] * 4,
        out_shape=[per_chunk] * 4 + [final] * 4,
        scratch_shapes=[pltpu.VMEM(st_shape, F32)] * 4,
        compiler_params=pltpu.CompilerParams(dimension_semantics=("arbitrary",),
                                             vmem_limit_bytes=VMEM_LIMIT),
        name="states",
    )(*rep(rk), *rep(rv), *rep(gk), *rep(gv), *rep(gf), *rep(rk), *rep(rv), *rep(gk), *rep(gv), *rep(gb),
      dec, *init)


def _block_row(x, parent, r):
    n, w = x.shape
    if parent == n:
        return jnp.broadcast_to(x[r:r + 1, :], (n, w))
    x3 = x.reshape(n // parent, parent, w)
    return jnp.broadcast_to(x3[:, r:r + 1, :], x3.shape).reshape(n, w)


def _gla_levels(q, k_even, k_odd, gf, gb, bf, bb):
    n = q.shape[0]
    row = _iota(q.shape, 0)

    def factors(u_exp, w_exp):
        ew = jnp.exp2(w_exp).astype(BF16)
        return (q * jnp.exp2(u_exp).astype(BF16), k_even * ew, k_odd * ew)

    levels = []
    s = n // 2
    while s >= 2:
        if s >= 4:
            last_of_first = _block_row(bf, 2 * s, s - 1)
            first_of_second = _block_row(bb, 2 * s, s)
        else:
            upper = (row & 4) != 0
            last_of_first = jnp.where(upper, _block_row(bf, 8, 5), _block_row(bf, 8, 1))
            first_of_second = jnp.where(upper, _block_row(bb, 8, 6), _block_row(bb, 8, 2))
        df = bf - last_of_first
        db = bb - first_of_second
        levels.append((2 * s, [factors(jnp.minimum(df, db), -jnp.maximum(df, db))]))
        s //= 2
    odd = (row & 1) == 1
    qa = q * jnp.where(odd, jnp.exp2(gf), 2.0).astype(BF16)
    qb = q * jnp.where(odd, 2.0, jnp.exp2(gb)).astype(BF16)
    zero = jnp.zeros_like(k_even)
    ev = lambda x: jnp.where(odd, zero, x)
    od = lambda x: jnp.where(odd, x, zero)
    levels.append((2, [(qa, ev(k_even), ev(k_odd)), (qb, od(k_even), od(k_odd))]))
    return levels


def _mix_kernel(rq_ref, rk_ref, rv_ref, rg_ref, gq_ref, gk_ref, gv_ref, gg_ref, gf_ref, gb_ref,
                srf_ref, srb_ref, sgf_ref, sgb_ref, dec_ref, rnw_ref, gnw_ref,
                x_ref, m_ref, wout_ref, o_ref, mix_ref, decay_ref, qdec_ref, *, d):
    n = CHUNK
    half = n // 2
    ri = _iota((n, n), 0)
    ci = _iota((n, n), 1)
    step = pl.program_id(0)
    cur = step % 2
    code = _iota((half, n), 0) ^ (_iota((half, n), 1) & (half - 1))

    @pl.when(step == 0)
    def _():
        mix_ref[1] = jnp.zeros(mix_ref.shape[1:], BF16)
        lg = _log_sigmoid(dec_ref[...])
        dist = (ri - ci).astype(F32)
        row = _iota((n, LANE), 0).astype(F32)
        for hd in range(RET_HEADS):
            sl = slice(hd * LANE, (hd + 1) * LANE)
            lgf = lg[0:1, sl]
            lgb = lg[1:2, sl]
            decay_ref[hd] = jnp.exp(jnp.where(dist > 0, dist * lgf[:, 0:1],
                                              jnp.where(dist < 0, -dist * lgb[:, 0:1], jnp.log(2.0))))
            qdec_ref[hd] = jnp.exp((row + 1.0) * lgf)
            qdec_ref[RET_HEADS + hd] = jnp.exp((n - row) * lgb)

    for c in range(MIX_CHUNKS):
        tok = slice(c * n, (c + 1) * n)
        for hd in range(RET_HEADS):
            sl = slice(hd * LANE, (hd + 1) * LANE)
            q = rq_ref[tok, sl]
            p = (_dot_nt(q, rk_ref[tok, sl]) * decay_ref[hd]).astype(BF16)
            qf = q.astype(F32)
            qs = jnp.concatenate([(qf * qdec_ref[hd]).astype(BF16),
                                  (qf * qdec_ref[RET_HEADS + hd]).astype(BF16)], axis=1)
            st = jnp.concatenate([srf_ref[c, hd], srb_ref[c, hd]], axis=0)
            o = _dot(p, rv_ref[tok, sl]) + _dot(qs, st)
            mu = jnp.mean(o, axis=-1, keepdims=True)
            oc = o - mu
            var = jnp.mean(oc * oc, axis=-1, keepdims=True)
            r = oc * lax.rsqrt(var + EPS) * rnw_ref[:, sl] * rg_ref[tok, sl].astype(F32)
            mix_ref[cur, tok, sl] = r.astype(BF16)

        gf = gf_ref[tok, :] * LOG2E
        gb = gb_ref[tok, :] * LOG2E
        bf = _dot_select((ci <= ri).astype(BF16), gf)
        bb = _dot_select((ci >= ri).astype(BF16), gb)
        q = gq_ref[tok, :]
        k = gk_ref[tok, :]
        even_head = (_iota(k.shape, 1) & GLA_DK) == 0
        no_k = jnp.zeros_like(k)
        if c == 0:
            o_ref[...] = x_ref[...] + m_ref[:, 5 * d:6 * d] * _dot(mix_ref[1 - cur], wout_ref[...])
        levels = _gla_levels(q, jnp.where(even_head, k, no_k), jnp.where(even_head, no_k, k), gf, gb, bf, bb)
        qsf = q * jnp.exp2(bf).astype(BF16)
        qsb = q * jnp.exp2(bb).astype(BF16)
        for pair in range(GLA_HEADS // 2):
            pr = slice(pair * LANE, (pair + 1) * LANE)
            near = [None, None]
            far = [None, None]
            for size, blocks in levels:
                for rb in range(2):
                    kb = rb if size < n else 1 - rb
                    rows = slice(rb * half, (rb + 1) * half)
                    keys = slice(kb * half, (kb + 1) * half)
                    u = jnp.concatenate([blk[0][rows, pr] for blk in blocks], axis=1)
                    w = jnp.concatenate([jnp.concatenate([blk[j][keys, pr] for blk in blocks], axis=1)
                                         for j in (1, 2)], axis=0)
                    t = _dot_nt(u, w)
                    if size == n:
                        far[rb] = t
                    else:
                        near[rb] = t if near[rb] is None else jnp.where(code < size, t, near[rb])
            qcat = jnp.concatenate([qsf[:, pr], qsb[:, pr]], axis=1)
            for j in range(2):
                hd = 2 * pair + j
                sl = slice(hd * LANE, (hd + 1) * LANE)
                mine = slice(j * half, (j + 1) * half)
                p = jnp.concatenate([jnp.concatenate([near[0][:, mine], far[0][:, mine]], axis=1),
                                     jnp.concatenate([far[1][:, mine], near[1][:, mine]], axis=1)], axis=0)
                st = jnp.concatenate([sgf_ref[c, hd], sgb_ref[c, hd]], axis=1)
                o = _dot(p.astype(BF16), gv_ref[tok, sl]) + _dot_nt(qcat, st)
                ms = jnp.mean(o * o, axis=-1, keepdims=True)
                r = o * lax.rsqrt(ms + EPS) * gnw_ref[:, sl] * gg_ref[tok, sl].astype(F32)
                mix_ref[cur, tok, RET_W + hd * LANE:RET_W + (hd + 1) * LANE] = r.astype(BF16)


def _mix_call(mix_in, states, dec, rnw, gnw, x1, m3, wout, batch, nc):
    assert nc % MIX_CHUNKS == 0
    per_seq = nc // MIX_CHUNKS
    n_steps = batch * per_seq
    rows = MIX_CHUNKS * CHUNK
    t, d = n_steps * rows, x1.shape[1]
    mixed = lambda i: jnp.minimum(i, n_steps - 1)
    done = lambda i: jnp.maximum(i - 1, 0)
    tile = lambda w: pl.BlockSpec((rows, w), lambda i: (mixed(i), 0))
    st_spec = pl.BlockSpec((None, MIX_CHUNKS, RET_HEADS, LANE, LANE),
                           lambda i: (mixed(i) // per_seq, mixed(i) % per_seq, 0, 0, 0))
    const = lambda shape: pl.BlockSpec(shape, lambda i: (0,) * len(shape))
    widths = (RET_W,) * 4 + (GLA_QK,) * 2 + (GLA_W,) * 2 + (GLA_QK,) * 2
    return pl.pallas_call(
        functools.partial(_mix_kernel, d=d),
        grid=(n_steps + 1,),
        in_specs=[tile(w) for w in widths] + [st_spec] * 4
        + [const(dec.shape), const((1, RET_W)), const((1, GLA_W)),
           pl.BlockSpec((rows, d), lambda i: (done(i), 0)),
           pl.BlockSpec((None, 1, N_MOD * d), lambda i: (done(i) // per_seq, 0, 0)),
           const(wout.shape)],
        out_specs=pl.BlockSpec((rows, d), lambda i: (done(i), 0)),
        out_shape=jax.ShapeDtypeStruct((t, d), F32),
        scratch_shapes=[pltpu.VMEM((2, rows, RET_W + GLA_W), BF16),
                        pltpu.VMEM((RET_HEADS, CHUNK, CHUNK), F32),
                        pltpu.VMEM((2 * RET_HEADS, CHUNK, LANE), F32)],
        compiler_params=pltpu.CompilerParams(dimension_semantics=("arbitrary",),
                                             vmem_limit_bytes=VMEM_LIMIT),
        name="mix",
    )(*mix_in, *states, dec, rnw.reshape(1, RET_W), gnw.reshape(1, GLA_W), x1, m3, wout)


def _rope_tables(n_tok):
    freqs = ROPE_BASE ** (-jnp.arange(RET_DK // 4, dtype=F32) / (RET_DK // 4))

    def table(n_pos, first_half):
        ang = jnp.arange(n_pos, dtype=F32)[:, None] * freqs
        zero = jnp.zeros((n_pos, LANE // 2), F32)
        cos = jnp.concatenate([jnp.cos(ang)] * 2, axis=-1)
        sin = jnp.concatenate([-jnp.sin(ang), jnp.sin(ang)], axis=-1)
        halves = (cos, zero, sin, zero) if first_half else (zero, cos, zero, sin)
        return jnp.concatenate(halves, axis=-1)

    return table(n_tok // GRID_W, True), table(GRID_W, False)


def _pack_gate(w_f, b_f, w_b, b_b):
    gw = jnp.zeros((LANE, 2 * GLA_QK), F32)
    gw = gw.at[:GLA_RANK, :GLA_QK].set(w_f).at[GLA_RANK:2 * GLA_RANK, GLA_QK:].set(w_b)
    return gw.astype(BF16), jnp.concatenate([b_f, b_b]).reshape(1, 2 * GLA_QK)


def kernel(x, c, ctx, c_ctx, ada_w, ada_b, norm1_w, ffn1_w1, ffn1_w3, ffn1_w2, norm2_w, w_in,
           ret_decay_f, ret_decay_b, ret_norm_w, gla_gate_w_f, gla_gate_b_f, gla_gate_w_b, gla_gate_b_b,
           gla_norm_w, w_out, norm3_w, ffn2_w1, ffn2_w3, ffn2_w2, final_norm_w):
    batch, n_tok, d = x.shape
    n_ctx = ctx.shape[1]
    depth = ada_w.shape[0]
    assert depth == 1 and batch + 1 <= 8
    assert n_tok % FFN_TILE == 0 and n_tok % CHUNK == 0 and n_ctx % CHUNK == 0
    assert (batch * n_ctx) % FFN_TILE == 0

    cvec = jnp.zeros((8, d), F32).at[:batch].set(c).at[batch].set(c_ctx)
    m, w1a, w3a, w2a, w_in_b, w_low_b = _modulation(cvec, ada_w[0], ada_b[0],
                                                    (ffn1_w1[0], ffn1_w3[0], ffn1_w2[0]), w_in[0].T)
    m3 = m.reshape(8, 1, N_MOD * d)

    gw, gbias = _pack_gate(gla_gate_w_f[0], gla_gate_b_f[0], gla_gate_w_b[0], gla_gate_b_b[0])
    proj = (norm2_w[0], w_in_b, w_low_b, gw, gbias)
    f1 = (norm1_w[0], w1a, w3a, w2a)
    rowtab, coltab = _rope_tables(n_tok)
    dec = jnp.zeros((8, RET_W), F32)
    dec = dec.at[0].set(jnp.repeat(ret_decay_f[0], LANE)).at[1].set(jnp.repeat(ret_decay_b[0], LANE))

    tiles_per_seq = n_tok // FFN_TILE
    first = _ffn_call(x.reshape(batch * n_tok, d), m3, tiles_per_seq, *f1, mod_off=0, tm=FFN_TILE,
                      ctx=ctx.reshape(batch * n_ctx, d), ctx_row=batch, proj=proj, rope=(rowtab, coltab),
                      cast=(ffn2_w1[0], ffn2_w3[0], ffn2_w2[0], w_out[0]), name="ffn_in")
    x1, mix_in, (w1b, w3b, w2b, woutb) = first[0], first[1:11], first[11:]
    scan_in = (mix_in[1], mix_in[2], mix_in[5], mix_in[6], mix_in[8], mix_in[9])
    zero = jnp.zeros((batch, RET_HEADS, LANE, LANE), F32)
    nc, nc_ctx = n_tok // CHUNK, n_ctx // CHUNK
    ctx_states = _states(*scan_in, dec, (zero,) * 4, batch, batch * nc, nc_ctx)[4:]
    states = _states(*scan_in, dec, ctx_states, batch, 0, nc)[:4]
    x2 = _mix_call(mix_in, states, dec, ret_norm_w[0], gla_norm_w[0], x1, m3, woutb, batch, nc)
    out = _ffn_call(x2, m3, tiles_per_seq, norm3_w[0], w1b, w3b, w2b, mod_off=6, tm=FFN_TILE,
                    final_w=final_norm_w, name="ffn_out")[0]
    return out.reshape(batch, n_tok, d)
```

```python
import functools

import jax
import jax.numpy as jnp
from jax import lax
from jax.experimental import pallas as pl
from jax.experimental.pallas import tpu as pltpu

F32 = jnp.float32
BF16 = jnp.bfloat16

EPS = 1e-6
LOG2E = 1.4426950408889634
N_MOD = 9
GRID_W = 64
ROPE_BASE = 10000.0
RET_HEADS = 4
RET_DK = 128
RET_DV = 128
GLA_HEADS = 4
GLA_DK = 64
GLA_DV = 128
GLA_RANK = 16
GLA_TAU = 16.0
RET_W = RET_HEADS * RET_DV
GLA_W = GLA_HEADS * GLA_DV
GLA_QK = GLA_HEADS * GLA_DK

LANE = 128
SUBLANES = 8
BF16_ROWS = 16
MOD_STEPS = 8
CHUNK = 256
MIX_CHUNKS = 2
FFN_TILE = 512
ROW_BLOCK = 256
FF_CHUNK = 256
VMEM_LIMIT = 60 * 1024 * 1024

C_RQ, C_RK, C_RV, C_RG = 0, 512, 1024, 1536
C_GQ, C_GK, C_GV, C_GG = 2048, 2304, 2560, 3072
C_LOW = 3584


def _silu(x):
    return x * (1.0 / (1.0 + jnp.exp(-x)))


def _log_sigmoid(z):
    return jnp.minimum(z, 0.0) - jnp.log(1.0 + jnp.exp(-jnp.abs(z)))


def _rms(x, w):
    return x * lax.rsqrt(jnp.mean(x * x, axis=-1, keepdims=True) + EPS) * w


def _dot(a, b):
    return jnp.dot(a, b, preferred_element_type=F32)


def _dot_nt(a, b):
    return lax.dot_general(a, b, (((1,), (1,)), ((), ())), preferred_element_type=F32)


def _dot_tn(a, b):
    return lax.dot_general(a, b, (((0,), (0,)), ((), ())), preferred_element_type=F32)


def _dot_select(sel, x):
    hi = x.astype(BF16)
    lo = (x - hi.astype(F32)).astype(BF16)
    return _dot(sel, hi) + _dot(sel, lo)


def _iota(shape, dim):
    return lax.broadcasted_iota(jnp.int32, shape, dim)


def _resident(shape):
    nd = len(shape)
    return pl.BlockSpec(shape, lambda *_: (0,) * nd, pipeline_mode=pl.Buffered(1))


def _cast_jobs(arrays, n_steps):
    specs, shapes = [], []
    for w in arrays:
        rows = next(r for r in range(BF16_ROWS, w.shape[0] + 1, BF16_ROWS)
                    if w.shape[0] % r == 0 and w.shape[0] // r <= n_steps)
        specs.append(pl.BlockSpec((rows, w.shape[1]),
                                  lambda i, last=w.shape[0] // rows - 1: (jnp.minimum(i, last), 0)))
        shapes.append(jax.ShapeDtypeStruct(w.shape, BF16))
    return specs, shapes


def _run_cast_jobs(src_refs, dst_refs):
    for src_ref, dst_ref in zip(src_refs, dst_refs):
        dst_ref[...] = src_ref[...].astype(BF16)


W_IN_BLOCK = 512


def _mod_kernel(*refs, n_cast):
    c_ref, w_ref, b_ref = refs[:3]
    cast_in, (wt_ref, wt_low_ref) = refs[3:3 + n_cast], refs[3 + n_cast:5 + n_cast]
    o_ref = refs[5 + n_cast]
    cast_out, (win_ref, wlow_ref) = refs[6 + n_cast:6 + 2 * n_cast], refs[6 + 2 * n_cast:]
    cond = _silu(c_ref[...]).astype(BF16)
    o_ref[...] = _dot(cond, w_ref[...].astype(BF16)) + b_ref[...]
    _run_cast_jobs(cast_in, cast_out)
    win_ref[...] = wt_ref[...].T.astype(BF16)
    low = wt_low_ref[...]
    low = jnp.concatenate([low, jnp.zeros((LANE - low.shape[0], low.shape[1]), F32)], axis=0)
    wlow_ref[...] = low.T.astype(BF16)


def _modulation(cvec, ada_w, ada_b, cast, w_in_t):
    d, n = ada_w.shape
    bn = n // MOD_STEPS if n % (MOD_STEPS * LANE) == 0 else d
    steps = n // bn
    assert C_LOW % W_IN_BLOCK == 0 and C_LOW // W_IN_BLOCK <= steps
    n_low = w_in_t.shape[0] - C_LOW
    last = C_LOW // W_IN_BLOCK - 1
    cast_specs, cast_shapes = _cast_jobs(cast, steps)
    return pl.pallas_call(
        functools.partial(_mod_kernel, n_cast=len(cast)),
        grid=(steps,),
        in_specs=[pl.BlockSpec((SUBLANES, d), lambda j: (0, 0)),
                  pl.BlockSpec((d, bn), lambda j: (0, j)),
                  pl.BlockSpec((1, bn), lambda j: (0, j))] + cast_specs
        + [pl.BlockSpec((W_IN_BLOCK, d), lambda j: (jnp.minimum(j, last), 0)),
           pl.BlockSpec((n_low, d), lambda j: (C_LOW // n_low, 0))],
        out_specs=[pl.BlockSpec((SUBLANES, bn), lambda j: (0, j))] + cast_specs
        + [pl.BlockSpec((d, W_IN_BLOCK), lambda j: (0, jnp.minimum(j, last))),
           pl.BlockSpec((d, LANE), lambda j: (0, 0))],
        out_shape=[jax.ShapeDtypeStruct((SUBLANES, n), F32)] + cast_shapes
        + [jax.ShapeDtypeStruct((d, C_LOW), BF16), jax.ShapeDtypeStruct((d, LANE), BF16)],
        compiler_params=pltpu.CompilerParams(dimension_semantics=("arbitrary",),
                                             vmem_limit_bytes=VMEM_LIMIT),
        name="mod",
    )(cvec, ada_w, ada_b.reshape(1, n), *cast, w_in_t, w_in_t)


def _swap32(x):
    lane = _iota(x.shape, 1)
    quarter = RET_DK // 4
    first = (lane & (2 * quarter - 1)) < quarter
    return jnp.where(first, pltpu.roll(x, LANE - quarter, 1), pltpu.roll(x, quarter, 1))


def _ffn_kernel(*refs, mod_off, proj, rope, final, n_main, n_cast, d, f):
    it = iter(refs)
    x_ref = next(it)
    if n_main is not None:
        xc_ref = next(it)
        is_ctx = pl.program_id(0) >= n_main
    m_ref, nw_ref, w1_ref, w3_ref, w2_ref = (next(it) for _ in range(5))
    if proj:
        n2w_ref, win_ref, wlow_ref, gw_ref, gbias_ref = (next(it) for _ in range(5))
        if rope:
            rowtab_ref, coltab_ref = next(it), next(it)
    if final:
        fnw_ref = next(it)
    cast_in = [next(it) for _ in range(n_cast)]
    xo_ref = next(it)
    if proj:
        (rq_ref, rk_ref, rv_ref, rg_ref, gq_ref, gk_ref, gv_ref, gg_ref,
         gf_ref, gb_ref) = (next(it) for _ in range(10))
    cast_out = [next(it) for _ in range(n_cast)]
    u_ref = next(it)
    _run_cast_jobs(cast_in, cast_out)

    def mod(i):
        return m_ref[:, (mod_off + i) * d:(mod_off + i + 1) * d]

    h2s = []
    for rb in range(x_ref.shape[0] // ROW_BLOCK):
        rows = slice(rb * ROW_BLOCK, (rb + 1) * ROW_BLOCK)
        x = x_ref[rows, :]
        if n_main is not None:
            x = jnp.where(is_ctx, xc_ref[rows, :], x)
        h = (_rms(x, nw_ref[...]) * (1.0 + mod(1)) + mod(0)).astype(BF16)
        for k in range(f // FF_CHUNK):
            sl = slice(k * FF_CHUNK, (k + 1) * FF_CHUNK)
            a = _dot(h, w1_ref[:, sl])
            g = _dot(h, w3_ref[:, sl])
            u_ref[rows, sl] = (_silu(a) * g).astype(BF16)
        y = _dot(u_ref[rows, :], w2_ref[...])
        x1 = x + (0.5 * mod(2)) * y

        if final:
            xo_ref[rows, :] = _rms(x1, fnw_ref[...])
        else:
            xo_ref[rows, :] = x1

        if proj:
            h2s.append((_rms(x1, n2w_ref[...]) * (1.0 + mod(4)) + mod(3)).astype(BF16))

    for rb, h2 in enumerate(h2s):
        rows = slice(rb * ROW_BLOCK, (rb + 1) * ROW_BLOCK)

        def p(lo, hi):
            return _dot(h2, win_ref[:, lo:hi])

        if rope:
            by_row = _iota((GRID_W, 2 * LANE), 1) % LANE < LANE // 2
            g0 = rb * (ROW_BLOCK // GRID_W)
            tab = jnp.concatenate(
                [jnp.where(by_row, jnp.broadcast_to(rowtab_ref[g0 + g:g0 + g + 1, :], (GRID_W, 2 * LANE)),
                           coltab_ref[...]) for g in range(ROW_BLOCK // GRID_W)], axis=0)
            cos, sin = tab[:, :LANE], tab[:, LANE:]
            if n_main is not None:
                cos = jnp.where(is_ctx, 1.0, cos)
                sin = jnp.where(is_ctx, 0.0, sin)

        low = _dot(h2, wlow_ref[...]).astype(BF16)
        z = _dot(low, gw_ref[...]) + gbias_ref[...]
        ls = _log_sigmoid(z) * (1.0 / GLA_TAU)
        gf_ref[rows, :] = ls[:, :GLA_QK]
        gb_ref[rows, :] = ls[:, GLA_QK:]
        for base, scale, o_ref in ((C_RQ, RET_DK ** -0.5, rq_ref), (C_RK, 1.0, rk_ref)):
            t = p(base, base + RET_W)
            for hd in range(RET_HEADS):
                th = t[:, hd * LANE:(hd + 1) * LANE] * scale
                if rope:
                    th = th * cos + _swap32(th) * sin
                o_ref[rows, hd * LANE:(hd + 1) * LANE] = th.astype(BF16)
        rg_ref[rows, :] = _silu(p(C_RG, C_RG + RET_W)).astype(BF16)
        gg_ref[rows, :] = _silu(p(C_GG, C_GG + GLA_W)).astype(BF16)
        gq_ref[rows, :] = (p(C_GQ, C_GQ + GLA_QK) * GLA_DK ** -0.5).astype(BF16)
        gk_ref[rows, :] = p(C_GK, C_GK + GLA_QK).astype(BF16)
        rv_ref[rows, :] = p(C_RV, C_RV + RET_W).astype(BF16)
        gv_ref[rows, :] = p(C_GV, C_GV + GLA_W).astype(BF16)


def _ffn_call(x, m3, tiles_per_seq, nw, w1, w3, w2, *, mod_off, tm, ctx=None, ctx_row=None, proj=None,
              rope=None, final_w=None, cast=(), name):
    t, d = x.shape
    f = w1.shape[1]
    n_main = t // tm
    n_tiles = n_main + (0 if ctx is None else ctx.shape[0] // tm)
    t = n_tiles * tm
    main = lambda i: jnp.minimum(i, n_main - 1)
    tile = lambda w: pl.BlockSpec((tm, w), lambda i: (i, 0))
    in_specs = [pl.BlockSpec((tm, d), lambda i: (main(i), 0))]
    args = [x]
    if ctx is None:
        row_of_tile = lambda i: i // tiles_per_seq
    else:
        row_of_tile = lambda i: jnp.where(i >= n_main, ctx_row, i // tiles_per_seq)
        in_specs.append(pl.BlockSpec((tm, d), lambda i: (jnp.maximum(i - n_main, 0), 0)))
        args.append(ctx)
    in_specs += [pl.BlockSpec((None, 1, N_MOD * d), lambda i: (row_of_tile(i), 0, 0)),
                 _resident((1, d)), _resident((d, f)), _resident((d, f)), _resident((f, d))]
    args += [m3, nw.reshape(1, d), w1, w3, w2]
    if proj is not None:
        n2w, win, wlow, gw, gbias = proj
        in_specs += [_resident((1, d))] + [_resident(a.shape) for a in (win, wlow, gw, gbias)]
        args += [n2w.reshape(1, d), win, wlow, gw, gbias]
        if rope is not None:
            rowtab, coltab = rope
            in_specs += [pl.BlockSpec((tm // GRID_W, 2 * LANE), lambda i: (main(i) % tiles_per_seq, 0)),
                         _resident(coltab.shape)]
            args += [rowtab, coltab]
    if final_w is not None:
        in_specs.append(_resident((1, d)))
        args.append(final_w.reshape(1, d))
    out_specs = [tile(d)]
    out_shape = [jax.ShapeDtypeStruct((t, d), F32)]
    if proj is not None:
        for w, dt in ((RET_W, BF16),) * 4 + ((GLA_QK, BF16),) * 2 + ((GLA_W, BF16),) * 2 + ((GLA_QK, F32),) * 2:
            out_specs.append(tile(w))
            out_shape.append(jax.ShapeDtypeStruct((t, w), dt))
    cast_specs, cast_shapes = _cast_jobs(cast, n_main)
    in_specs += cast_specs
    args += list(cast)
    out_specs += cast_specs
    out_shape += cast_shapes
    kern = functools.partial(_ffn_kernel, mod_off=mod_off, proj=proj is not None,
                             rope=rope is not None, final=final_w is not None,
                             n_main=None if ctx is None else n_main, n_cast=len(cast), d=d, f=f)
    return pl.pallas_call(
        kern, grid=(n_tiles,), in_specs=in_specs, out_specs=out_specs, out_shape=out_shape,
        scratch_shapes=[pltpu.VMEM((tm, f), BF16)],
        compiler_params=pltpu.CompilerParams(dimension_semantics=("arbitrary",),
                                             vmem_limit_bytes=VMEM_LIMIT),
        name=name,
    )(*args)


def _head_lanes(shape, hd):
    lane = _iota(shape, len(shape) - 1)
    return (lane >= GLA_DK) if hd % 2 else (lane < GLA_DK)


def _state_kernel(*refs, batch, cps, steps):
    it = iter(refs)
    per_batch = lambda: [next(it) for _ in range(batch)]
    rk_ref, rv_ref, gk_ref, gv_ref, gf_ref, gb_ref = (per_batch() for _ in range(6))
    dec_ref = next(it)
    irf_ref, irb_ref, igf_ref, igb_ref = (next(it) for _ in range(4))
    orf_ref, orb_ref, ogf_ref, ogb_ref = (next(it) for _ in range(4))
    frf_ref, frb_ref, fgf_ref, fgb_ref = (next(it) for _ in range(4))
    srf, srb, sgf, sgb, ub_ret, ub_gla, ub_dec = (next(it) for _ in range(7))
    n = CHUNK
    step = pl.program_id(0)
    lg = _log_sigmoid(dec_ref[...])

    @pl.when(step == 0)
    def _():
        srf[...] = irf_ref[...]
        srb[...] = irb_ref[...]
        sgf[...] = igf_ref[...]
        sgb[...] = igb_ref[...]

    @pl.when(step < steps)
    def _():
        row = _iota((n, LANE), 0).astype(F32)
        ri = _iota((n, n), 0)
        ci = _iota((n, n), 1)
        after = (ci > ri).astype(BF16)
        before = (ci < ri).astype(BF16)

        def gla(b, rows, g_ref, tri, edge):
            g = g_ref[b][rows, :]
            e = _dot_select(tri, g)
            tot = e[edge:edge + 1, :] + g[edge:edge + 1, :]
            kd = gk_ref[b][rows, :] * jnp.exp(e).astype(BF16)
            upd = []
            for hd in range(GLA_HEADS):
                pr = slice((hd // 2) * LANE, (hd // 2 + 1) * LANE)
                u = _dot_tn(gv_ref[b][rows, hd * LANE:(hd + 1) * LANE], kd[:, pr])
                upd.append(jnp.where(_head_lanes(u.shape, hd), u, 0.0))
            return upd, jnp.exp(tot)

        for j in range(cps):
            chunk = step * cps + j
            rows = slice(j * n, (j + 1) * n)
            orf_ref[:, j] = srf[...].astype(BF16)
            ogf_ref[:, j] = sgf[...].astype(BF16)
            for b in range(batch):
                for hd in range(RET_HEADS):
                    sl = slice(hd * LANE, (hd + 1) * LANE)
                    lgf = lg[0:1, sl]
                    lgb = lg[1:2, sl]
                    k = rk_ref[b][rows, sl]
                    v = rv_ref[b][rows, sl]
                    kf = k * jnp.exp((n - 1.0 - row) * lgf).astype(BF16)
                    srf[b, hd] = srf[b, hd] * jnp.exp(n * lgf) + _dot_tn(kf, v)
                    kb = k * jnp.exp(row * lgb).astype(BF16)
                    ub_ret[chunk, b, hd] = _dot_tn(kb, v).astype(BF16)
                upd, dec = gla(b, rows, gf_ref, after, 0)
                for hd in range(GLA_HEADS):
                    pr = slice((hd // 2) * LANE, (hd // 2 + 1) * LANE)
                    sgf[b, hd] = sgf[b, hd] * dec[:, pr] + upd[hd]
                upd, dec = gla(b, rows, gb_ref, before, n - 1)
                for hd in range(GLA_HEADS):
                    ub_gla[chunk, b, hd] = upd[hd].astype(BF16)
                ub_dec[chunk, b] = jnp.broadcast_to(dec, ub_dec.shape[2:])
        frf_ref[...] = srf[...]
        fgf_ref[...] = sgf[...]

    @pl.when(step >= steps)
    def _():
        for j in range(cps):
            jb = cps - 1 - j
            chunk = (2 * steps - 1 - step) * cps + jb
            orb_ref[:, jb] = srb[...].astype(BF16)
            ogb_ref[:, jb] = sgb[...].astype(BF16)
            for b in range(batch):
                for hd in range(RET_HEADS):
                    lgb = lg[1:2, hd * LANE:(hd + 1) * LANE]
                    srb[b, hd] = srb[b, hd] * jnp.exp(n * lgb) + ub_ret[chunk, b, hd].astype(F32)
                for hd in range(GLA_HEADS):
                    pr = slice((hd // 2) * LANE, (hd // 2 + 1) * LANE)
                    sgb[b, hd] = sgb[b, hd] * ub_dec[chunk, b, 0:1, pr] + ub_gla[chunk, b, hd].astype(F32)
        frb_ref[...] = srb[...]
        fgb_ref[...] = sgb[...]


def _states(rk, rv, gk, gv, gf, gb, dec, init, batch, first_chunk, nc):
    cps = 2 if nc % 2 == 0 and first_chunk % 2 == 0 else 1
    steps = nc // cps
    reading = lambda c: jnp.minimum(c, steps - 1)
    scanning = lambda c: steps - 1 - jnp.maximum(c - steps, 0)

    def tokens(w):
        return [pl.BlockSpec((cps * CHUNK, w), lambda c, b=b: ((first_chunk + b * nc) // cps + reading(c), 0))
                for b in range(batch)]

    st_shape = (batch, RET_HEADS, LANE, LANE)
    init_spec = pl.BlockSpec(st_shape, lambda c: (0, 0, 0, 0))
    chunk_shape = (batch, cps, RET_HEADS, LANE, LANE)
    of_spec = pl.BlockSpec(chunk_shape, lambda c: (0, reading(c), 0, 0, 0))
    ob_spec = pl.BlockSpec(chunk_shape, lambda c: (0, scanning(c), 0, 0, 0))
    per_chunk = jax.ShapeDtypeStruct((batch, nc, RET_HEADS, LANE, LANE), BF16)
    final = jax.ShapeDtypeStruct(st_shape, F32)
    kept = (nc, batch, RET_HEADS, LANE, LANE)
    rep = lambda a: [a] * batch
    return pl.pallas_call(
        functools.partial(_state_kernel, batch=batch, cps=cps, steps=steps),
        grid=(2 * steps,),
        in_specs=tokens(RET_W) + tokens(RET_W) + tokens(GLA_QK) + tokens(GLA_W) + tokens(GLA_QK) + tokens(GLA_QK)
        + [pl.BlockSpec(dec.shape, lambda c: (0, 0))] + [init_spec] * 4,
        out_specs=[of_spec, ob_spec, of_spec, ob_spec] + [init_spec] * 4,
        out_shape=[per_chunk] * 4 + [final] * 4,
        scratch_shapes=[pltpu.VMEM(st_shape, F32)] * 4
        + [pltpu.VMEM(kept, BF16), pltpu.VMEM(kept, BF16), pltpu.VMEM((nc, batch, SUBLANES, GLA_QK), F32)],
        compiler_params=pltpu.CompilerParams(dimension_semantics=("arbitrary",),
                                             vmem_limit_bytes=VMEM_LIMIT),
        name="states",
    )(*rep(rk), *rep(rv), *rep(gk), *rep(gv), *rep(gf), *rep(gb), dec, *init)


def _block_row(x, parent, r):
    n, w = x.shape
    if parent == n:
        return jnp.broadcast_to(x[r:r + 1, :], (n, w))
    x3 = x.reshape(n // parent, parent, w)
    return jnp.broadcast_to(x3[:, r:r + 1, :], x3.shape).reshape(n, w)


def _gla_levels(q, k_even, k_odd, gf, gb, bf, bb):
    n = q.shape[0]
    row = _iota(q.shape, 0)

    def factors(u_exp, w_exp):
        ew = jnp.exp2(w_exp).astype(BF16)
        return (q * jnp.exp2(u_exp).astype(BF16), k_even * ew, k_odd * ew)

    levels = []
    s = n // 2
    while s >= 2:
        if s >= 4:
            last_of_first = _block_row(bf, 2 * s, s - 1)
            first_of_second = _block_row(bb, 2 * s, s)
        else:
            upper = (row & 4) != 0
            last_of_first = jnp.where(upper, _block_row(bf, SUBLANES, 5), _block_row(bf, SUBLANES, 1))
            first_of_second = jnp.where(upper, _block_row(bb, SUBLANES, 6), _block_row(bb, SUBLANES, 2))
        df = bf - last_of_first
        db = bb - first_of_second
        levels.append((2 * s, [factors(jnp.minimum(df, db), -jnp.maximum(df, db))]))
        s //= 2
    odd = (row & 1) == 1
    qa = q * jnp.where(odd, jnp.exp2(gf), 2.0).astype(BF16)
    qb = q * jnp.where(odd, 2.0, jnp.exp2(gb)).astype(BF16)
    zero = jnp.zeros_like(k_even)
    ev = lambda x: jnp.where(odd, zero, x)
    od = lambda x: jnp.where(odd, x, zero)
    levels.append((2, [(qa, ev(k_even), ev(k_odd)), (qb, od(k_even), od(k_odd))]))
    return levels


def _mix_kernel(rq_ref, rk_ref, rv_ref, rg_ref, gq_ref, gk_ref, gv_ref, gg_ref, gf_ref, gb_ref,
                srf_ref, srb_ref, sgf_ref, sgb_ref, dec_ref, rnw_ref, gnw_ref,
                x_ref, m_ref, wout_ref, o_ref, mix_ref, decay_ref, qdec_ref, *, d):
    n = CHUNK
    half = n // 2
    ri = _iota((n, n), 0)
    ci = _iota((n, n), 1)
    step = pl.program_id(0)
    cur = step % 2
    code = _iota((half, n), 0) ^ (_iota((half, n), 1) & (half - 1))

    @pl.when(step == 0)
    def _():
        mix_ref[1] = jnp.zeros(mix_ref.shape[1:], BF16)
        lg = _log_sigmoid(dec_ref[...])
        dist = (ri - ci).astype(F32)
        row = _iota((n, LANE), 0).astype(F32)
        for hd in range(RET_HEADS):
            sl = slice(hd * LANE, (hd + 1) * LANE)
            lgf = lg[0:1, sl]
            lgb = lg[1:2, sl]
            decay_ref[hd] = jnp.exp(jnp.where(dist > 0, dist * lgf[:, 0:1],
                                              jnp.where(dist < 0, -dist * lgb[:, 0:1], jnp.log(2.0))))
            qdec_ref[hd] = jnp.exp((row + 1.0) * lgf)
            qdec_ref[RET_HEADS + hd] = jnp.exp((n - row) * lgb)

    for c in range(MIX_CHUNKS):
        tok = slice(c * n, (c + 1) * n)
        for hd in range(RET_HEADS):
            sl = slice(hd * LANE, (hd + 1) * LANE)
            q = rq_ref[tok, sl]
            p = (_dot_nt(q, rk_ref[tok, sl]) * decay_ref[hd]).astype(BF16)
            qf = q.astype(F32)
            qs = jnp.concatenate([(qf * qdec_ref[hd]).astype(BF16),
                                  (qf * qdec_ref[RET_HEADS + hd]).astype(BF16)], axis=1)
            st = jnp.concatenate([srf_ref[c, hd], srb_ref[c, hd]], axis=0)
            o = _dot(p, rv_ref[tok, sl]) + _dot(qs, st)
            mu = jnp.mean(o, axis=-1, keepdims=True)
            oc = o - mu
            var = jnp.mean(oc * oc, axis=-1, keepdims=True)
            r = oc * lax.rsqrt(var + EPS) * rnw_ref[:, sl] * rg_ref[tok, sl].astype(F32)
            mix_ref[cur, tok, sl] = r.astype(BF16)

        gf = gf_ref[tok, :] * LOG2E
        gb = gb_ref[tok, :] * LOG2E
        bf = _dot_select((ci <= ri).astype(BF16), gf)
        bb = _dot_select((ci >= ri).astype(BF16), gb)
        q = gq_ref[tok, :]
        k = gk_ref[tok, :]
        even_head = (_iota(k.shape, 1) & GLA_DK) == 0
        no_k = jnp.zeros_like(k)
        if c == 0:
            o_ref[...] = x_ref[...] + m_ref[:, 5 * d:6 * d] * _dot(mix_ref[1 - cur], wout_ref[...])
        levels = _gla_levels(q, jnp.where(even_head, k, no_k), jnp.where(even_head, no_k, k), gf, gb, bf, bb)
        qsf = q * jnp.exp2(bf).astype(BF16)
        qsb = q * jnp.exp2(bb).astype(BF16)
        for pair in range(GLA_HEADS // 2):
            pr = slice(pair * LANE, (pair + 1) * LANE)
            near = [None, None]
            far = [None, None]
            for size, blocks in levels:
                for rb in range(2):
                    kb = rb if size < n else 1 - rb
                    rows = slice(rb * half, (rb + 1) * half)
                    keys = slice(kb * half, (kb + 1) * half)
                    u = jnp.concatenate([blk[0][rows, pr] for blk in blocks], axis=1)
                    w = jnp.concatenate([jnp.concatenate([blk[j][keys, pr] for blk in blocks], axis=1)
                                         for j in (1, 2)], axis=0)
                    t = _dot_nt(u, w)
                    if size == n:
                        far[rb] = t
                    else:
                        near[rb] = t if near[rb] is None else jnp.where(code < size, t, near[rb])
            qcat = jnp.concatenate([qsf[:, pr], qsb[:, pr]], axis=1)
            for j in range(2):
                hd = 2 * pair + j
                sl = slice(hd * LANE, (hd + 1) * LANE)
                mine = slice(j * half, (j + 1) * half)
                p = jnp.concatenate([jnp.concatenate([near[0][:, mine], far[0][:, mine]], axis=1),
                                     jnp.concatenate([far[1][:, mine], near[1][:, mine]], axis=1)], axis=0)
                st = jnp.concatenate([sgf_ref[c, hd], sgb_ref[c, hd]], axis=1)
                o = _dot(p.astype(BF16), gv_ref[tok, sl]) + _dot_nt(qcat, st)
                ms = jnp.mean(o * o, axis=-1, keepdims=True)
                r = o * lax.rsqrt(ms + EPS) * gnw_ref[:, sl] * gg_ref[tok, sl].astype(F32)
                mix_ref[cur, tok, RET_W + hd * LANE:RET_W + (hd + 1) * LANE] = r.astype(BF16)


def _mix_call(mix_in, states, dec, rnw, gnw, x1, m3, wout, batch, nc):
    assert nc % MIX_CHUNKS == 0
    per_seq = nc // MIX_CHUNKS
    n_steps = batch * per_seq
    rows = MIX_CHUNKS * CHUNK
    t, d = n_steps * rows, x1.shape[1]
    mixed = lambda i: jnp.minimum(i, n_steps - 1)
    done = lambda i: jnp.maximum(i - 1, 0)
    tile = lambda w: pl.BlockSpec((rows, w), lambda i: (mixed(i), 0))
    st_spec = pl.BlockSpec((None, MIX_CHUNKS, RET_HEADS, LANE, LANE),
                           lambda i: (mixed(i) // per_seq, mixed(i) % per_seq, 0, 0, 0))
    const = lambda shape: pl.BlockSpec(shape, lambda i: (0,) * len(shape))
    widths = (RET_W,) * 4 + (GLA_QK,) * 2 + (GLA_W,) * 2 + (GLA_QK,) * 2
    return pl.pallas_call(
        functools.partial(_mix_kernel, d=d),
        grid=(n_steps + 1,),
        in_specs=[tile(w) for w in widths] + [st_spec] * 4
        + [const(dec.shape), const((1, RET_W)), const((1, GLA_W)),
           pl.BlockSpec((rows, d), lambda i: (done(i), 0)),
           pl.BlockSpec((None, 1, N_MOD * d), lambda i: (done(i) // per_seq, 0, 0)),
           const(wout.shape)],
        out_specs=pl.BlockSpec((rows, d), lambda i: (done(i), 0)),
        out_shape=jax.ShapeDtypeStruct((t, d), F32),
        scratch_shapes=[pltpu.VMEM((2, rows, RET_W + GLA_W), BF16),
                        pltpu.VMEM((RET_HEADS, CHUNK, CHUNK), F32),
                        pltpu.VMEM((2 * RET_HEADS, CHUNK, LANE), F32)],
        compiler_params=pltpu.CompilerParams(dimension_semantics=("arbitrary",),
                                             vmem_limit_bytes=VMEM_LIMIT),
        name="mix",
    )(*mix_in, *states, dec, rnw.reshape(1, RET_W), gnw.reshape(1, GLA_W), x1, m3, wout)


def _rope_tables(n_tok):
    freqs = ROPE_BASE ** (-jnp.arange(RET_DK // 4, dtype=F32) / (RET_DK // 4))

    def table(n_pos, first_half):
        ang = jnp.arange(n_pos, dtype=F32)[:, None] * freqs
        zero = jnp.zeros((n_pos, LANE // 2), F32)
        cos = jnp.concatenate([jnp.cos(ang)] * 2, axis=-1)
        sin = jnp.concatenate([-jnp.sin(ang), jnp.sin(ang)], axis=-1)
        halves = (cos, zero, sin, zero) if first_half else (zero, cos, zero, sin)
        return jnp.concatenate(halves, axis=-1)

    return table(n_tok // GRID_W, True), table(GRID_W, False)


def _pack_gate(w_f, b_f, w_b, b_b):
    gw = jnp.zeros((LANE, 2 * GLA_QK), F32)
    gw = gw.at[:GLA_RANK, :GLA_QK].set(w_f).at[GLA_RANK:2 * GLA_RANK, GLA_QK:].set(w_b)
    return gw.astype(BF16), jnp.concatenate([b_f, b_b]).reshape(1, 2 * GLA_QK)


def kernel(x, c, ctx, c_ctx, ada_w, ada_b, norm1_w, ffn1_w1, ffn1_w3, ffn1_w2, norm2_w, w_in,
           ret_decay_f, ret_decay_b, ret_norm_w, gla_gate_w_f, gla_gate_b_f, gla_gate_w_b, gla_gate_b_b,
           gla_norm_w, w_out, norm3_w, ffn2_w1, ffn2_w3, ffn2_w2, final_norm_w):
    batch, n_tok, d = x.shape
    n_ctx = ctx.shape[1]
    depth = ada_w.shape[0]
    assert depth == 1 and batch + 1 <= SUBLANES
    assert n_tok % FFN_TILE == 0 and n_tok % CHUNK == 0 and n_ctx % CHUNK == 0
    assert (batch * n_ctx) % FFN_TILE == 0

    cvec = jnp.zeros((SUBLANES, d), F32).at[:batch].set(c).at[batch].set(c_ctx)
    m, w1a, w3a, w2a, w_in_b, w_low_b = _modulation(cvec, ada_w[0], ada_b[0],
                                                    (ffn1_w1[0], ffn1_w3[0], ffn1_w2[0]), w_in[0].T)
    m3 = m.reshape(SUBLANES, 1, N_MOD * d)

    gw, gbias = _pack_gate(gla_gate_w_f[0], gla_gate_b_f[0], gla_gate_w_b[0], gla_gate_b_b[0])
    proj = (norm2_w[0], w_in_b, w_low_b, gw, gbias)
    f1 = (norm1_w[0], w1a, w3a, w2a)
    rowtab, coltab = _rope_tables(n_tok)
    dec = jnp.zeros((SUBLANES, RET_W), F32)
    dec = dec.at[0].set(jnp.repeat(ret_decay_f[0], LANE)).at[1].set(jnp.repeat(ret_decay_b[0], LANE))

    tiles_per_seq = n_tok // FFN_TILE
    first = _ffn_call(x.reshape(batch * n_tok, d), m3, tiles_per_seq, *f1, mod_off=0, tm=FFN_TILE,
                      ctx=ctx.reshape(batch * n_ctx, d), ctx_row=batch, proj=proj, rope=(rowtab, coltab),
                      cast=(ffn2_w1[0], ffn2_w3[0], ffn2_w2[0], w_out[0]), name="ffn_in")
    x1, mix_in, (w1b, w3b, w2b, woutb) = first[0], first[1:11], first[11:]
    scan_in = (mix_in[1], mix_in[2], mix_in[5], mix_in[6], mix_in[8], mix_in[9])
    zero = jnp.zeros((batch, RET_HEADS, LANE, LANE), F32)
    nc, nc_ctx = n_tok // CHUNK, n_ctx // CHUNK
    ctx_states = _states(*scan_in, dec, (zero,) * 4, batch, batch * nc, nc_ctx)[4:]
    states = _states(*scan_in, dec, ctx_states, batch, 0, nc)[:4]
    x2 = _mix_call(mix_in, states, dec, ret_norm_w[0], gla_norm_w[0], x1, m3, woutb, batch, nc)
    out = _ffn_call(x2, m3, tiles_per_seq, norm3_w[0], w1b, w3b, w2b, mod_off=6, tm=FFN_TILE,
                    final_w=final_norm_w, name="ffn_out")[0]
    return out.reshape(batch, n_tok, d)
```

```python
import functools

import jax
import jax.numpy as jnp
from jax import lax
from jax.experimental import pallas as pl
from jax.experimental.pallas import tpu as pltpu

F32 = jnp.float32
BF16 = jnp.bfloat16

EPS = 1e-6
LOG2E = 1.4426950408889634
N_MOD = 9
GRID_W = 64
ROPE_BASE = 10000.0
RET_HEADS = 4
RET_DK = 128
RET_DV = 128
GLA_HEADS = 4
GLA_DK = 64
GLA_DV = 128
GLA_RANK = 16
GLA_TAU = 16.0
RET_W = RET_HEADS * RET_DV
GLA_W = GLA_HEADS * GLA_DV
GLA_QK = GLA_HEADS * GLA_DK

LANE = 128
SUBLANES = 8
BF16_ROWS = 16
MOD_STEPS = 8
CHUNK = 256
MIX_CHUNKS = 2
FFN_TILE = 512
ROW_BLOCK = 256
FF_CHUNK = 256
VMEM_LIMIT = 60 * 1024 * 1024

C_RQ, C_RK, C_RV, C_RG = 0, 512, 1024, 1536
C_GQ, C_GK, C_GV, C_GG = 2048, 2304, 2560, 3072
C_LOW = 3584


def _silu(x):
    return x * (1.0 / (1.0 + jnp.exp(-x)))


def _log_sigmoid(z):
    return jnp.minimum(z, 0.0) - jnp.log(1.0 + jnp.exp(-jnp.abs(z)))


def _rms(x, w):
    return x * lax.rsqrt(jnp.mean(x * x, axis=-1, keepdims=True) + EPS) * w


def _dot(a, b):
    return jnp.dot(a, b, preferred_element_type=F32)


def _dot_nt(a, b):
    return lax.dot_general(a, b, (((1,), (1,)), ((), ())), preferred_element_type=F32)


def _dot_tn(a, b):
    return lax.dot_general(a, b, (((0,), (0,)), ((), ())), preferred_element_type=F32)


def _dot_select(sel, x):
    hi = x.astype(BF16)
    lo = (x - hi.astype(F32)).astype(BF16)
    return _dot(sel, hi) + _dot(sel, lo)


def _iota(shape, dim):
    return lax.broadcasted_iota(jnp.int32, shape, dim)


def _resident(shape):
    nd = len(shape)
    return pl.BlockSpec(shape, lambda *_: (0,) * nd, pipeline_mode=pl.Buffered(1))


def _cast_jobs(arrays, n_steps):
    specs, shapes = [], []
    for w in arrays:
        rows = next(r for r in range(BF16_ROWS, w.shape[0] + 1, BF16_ROWS)
                    if w.shape[0] % r == 0 and w.shape[0] // r <= n_steps)
        specs.append(pl.BlockSpec((rows, w.shape[1]),
                                  lambda i, last=w.shape[0] // rows - 1: (jnp.minimum(i, last), 0)))
        shapes.append(jax.ShapeDtypeStruct(w.shape, BF16))
    return specs, shapes


def _run_cast_jobs(src_refs, dst_refs):
    for src_ref, dst_ref in zip(src_refs, dst_refs):
        dst_ref[...] = src_ref[...].astype(BF16)


W_IN_BLOCK = 512


def _mod_kernel(*refs, n_cast):
    c_ref, w_ref, b_ref = refs[:3]
    cast_in, (wt_ref, wt_low_ref) = refs[3:3 + n_cast], refs[3 + n_cast:5 + n_cast]
    o_ref = refs[5 + n_cast]
    cast_out, (win_ref, wlow_ref) = refs[6 + n_cast:6 + 2 * n_cast], refs[6 + 2 * n_cast:]
    cond = _silu(c_ref[...]).astype(BF16)
    o_ref[...] = _dot(cond, w_ref[...].astype(BF16)) + b_ref[...]
    _run_cast_jobs(cast_in, cast_out)
    win_ref[...] = wt_ref[...].T.astype(BF16)
    low = wt_low_ref[...]
    low = jnp.concatenate([low, jnp.zeros((LANE - low.shape[0], low.shape[1]), F32)], axis=0)
    wlow_ref[...] = low.T.astype(BF16)


def _modulation(cvec, ada_w, ada_b, cast, w_in_t):
    d, n = ada_w.shape
    bn = n // MOD_STEPS if n % (MOD_STEPS * LANE) == 0 else d
    steps = n // bn
    assert C_LOW % W_IN_BLOCK == 0 and C_LOW // W_IN_BLOCK <= steps
    n_low = w_in_t.shape[0] - C_LOW
    last = C_LOW // W_IN_BLOCK - 1
    cast_specs, cast_shapes = _cast_jobs(cast, steps)
    return pl.pallas_call(
        functools.partial(_mod_kernel, n_cast=len(cast)),
        grid=(steps,),
        in_specs=[pl.BlockSpec((SUBLANES, d), lambda j: (0, 0)),
                  pl.BlockSpec((d, bn), lambda j: (0, j)),
                  pl.BlockSpec((1, bn), lambda j: (0, j))] + cast_specs
        + [pl.BlockSpec((W_IN_BLOCK, d), lambda j: (jnp.minimum(j, last), 0)),
           pl.BlockSpec((n_low, d), lambda j: (C_LOW // n_low, 0))],
        out_specs=[pl.BlockSpec((SUBLANES, bn), lambda j: (0, j))] + cast_specs
        + [pl.BlockSpec((d, W_IN_BLOCK), lambda j: (0, jnp.minimum(j, last))),
           pl.BlockSpec((d, LANE), lambda j: (0, 0))],
        out_shape=[jax.ShapeDtypeStruct((SUBLANES, n), F32)] + cast_shapes
        + [jax.ShapeDtypeStruct((d, C_LOW), BF16), jax.ShapeDtypeStruct((d, LANE), BF16)],
        compiler_params=pltpu.CompilerParams(dimension_semantics=("arbitrary",),
                                             vmem_limit_bytes=VMEM_LIMIT),
        name="mod",
    )(cvec, ada_w, ada_b.reshape(1, n), *cast, w_in_t, w_in_t)


def _swap32(x):
    lane = _iota(x.shape, 1)
    quarter = RET_DK // 4
    first = (lane & (2 * quarter - 1)) < quarter
    return jnp.where(first, pltpu.roll(x, LANE - quarter, 1), pltpu.roll(x, quarter, 1))


def _ffn_kernel(*refs, mod_off, proj, rope, final, n_main, n_cast, d, f):
    it = iter(refs)
    x_ref = next(it)
    if n_main is not None:
        xc_ref = next(it)
        is_ctx = pl.program_id(0) >= n_main
    m_ref, nw_ref, w1_ref, w3_ref, w2_ref = (next(it) for _ in range(5))
    if proj:
        n2w_ref, win_ref, wlow_ref, gw_ref, gbias_ref = (next(it) for _ in range(5))
        if rope:
            rowtab_ref, coltab_ref = next(it), next(it)
    if final:
        fnw_ref = next(it)
    cast_in = [next(it) for _ in range(n_cast)]
    xo_ref = next(it)
    if proj:
        (rq_ref, rk_ref, rv_ref, rg_ref, gq_ref, gk_ref, gv_ref, gg_ref,
         gf_ref, gb_ref) = (next(it) for _ in range(10))
    cast_out = [next(it) for _ in range(n_cast)]
    u_ref = next(it)
    _run_cast_jobs(cast_in, cast_out)

    def mod(i):
        return m_ref[:, (mod_off + i) * d:(mod_off + i + 1) * d]

    h2s = []
    for rb in range(x_ref.shape[0] // ROW_BLOCK):
        rows = slice(rb * ROW_BLOCK, (rb + 1) * ROW_BLOCK)
        x = x_ref[rows, :]
        if n_main is not None:
            x = jnp.where(is_ctx, xc_ref[rows, :], x)
        h = (_rms(x, nw_ref[...]) * (1.0 + mod(1)) + mod(0)).astype(BF16)
        for k in range(f // FF_CHUNK):
            sl = slice(k * FF_CHUNK, (k + 1) * FF_CHUNK)
            a = _dot(h, w1_ref[:, sl])
            g = _dot(h, w3_ref[:, sl])
            u_ref[rows, sl] = (_silu(a) * g).astype(BF16)
        y = _dot(u_ref[rows, :], w2_ref[...])
        x1 = x + (0.5 * mod(2)) * y

        if final:
            xo_ref[rows, :] = _rms(x1, fnw_ref[...])
        else:
            xo_ref[rows, :] = x1

        if proj:
            h2s.append((_rms(x1, n2w_ref[...]) * (1.0 + mod(4)) + mod(3)).astype(BF16))

    for rb, h2 in enumerate(h2s):
        rows = slice(rb * ROW_BLOCK, (rb + 1) * ROW_BLOCK)

        def p(lo, hi):
            return _dot(h2, win_ref[:, lo:hi])

        if rope:
            by_row = _iota((GRID_W, 2 * LANE), 1) % LANE < LANE // 2
            g0 = rb * (ROW_BLOCK // GRID_W)
            tab = jnp.concatenate(
                [jnp.where(by_row, jnp.broadcast_to(rowtab_ref[g0 + g:g0 + g + 1, :], (GRID_W, 2 * LANE)),
                           coltab_ref[...]) for g in range(ROW_BLOCK // GRID_W)], axis=0)
            cos, sin = tab[:, :LANE], tab[:, LANE:]
            if n_main is not None:
                cos = jnp.where(is_ctx, 1.0, cos)
                sin = jnp.where(is_ctx, 0.0, sin)

        low = _dot(h2, wlow_ref[...]).astype(BF16)
        z = _dot(low, gw_ref[...]) + gbias_ref[...]
        ls = _log_sigmoid(z) * (1.0 / GLA_TAU)
        gf_ref[rows, :] = ls[:, :GLA_QK]
        gb_ref[rows, :] = ls[:, GLA_QK:]
        for base, scale, o_ref in ((C_RQ, RET_DK ** -0.5, rq_ref), (C_RK, 1.0, rk_ref)):
            t = p(base, base + RET_W)
            for hd in range(RET_HEADS):
                th = t[:, hd * LANE:(hd + 1) * LANE] * scale
                if rope:
                    th = th * cos + _swap32(th) * sin
                o_ref[rows, hd * LANE:(hd + 1) * LANE] = th.astype(BF16)
        rg_ref[rows, :] = _silu(p(C_RG, C_RG + RET_W)).astype(BF16)
        gg_ref[rows, :] = _silu(p(C_GG, C_GG + GLA_W)).astype(BF16)
        gq_ref[rows, :] = (p(C_GQ, C_GQ + GLA_QK) * GLA_DK ** -0.5).astype(BF16)
        gk_ref[rows, :] = p(C_GK, C_GK + GLA_QK).astype(BF16)
        rv_ref[rows, :] = p(C_RV, C_RV + RET_W).astype(BF16)
        gv_ref[rows, :] = p(C_GV, C_GV + GLA_W).astype(BF16)


def _ffn_call(x, m3, tiles_per_seq, nw, w1, w3, w2, *, mod_off, tm, ctx=None, ctx_row=None, proj=None,
              rope=None, final_w=None, cast=(), name):
    t, d = x.shape
    f = w1.shape[1]
    n_main = t // tm
    n_tiles = n_main + (0 if ctx is None else ctx.shape[0] // tm)
    t = n_tiles * tm
    main = lambda i: jnp.minimum(i, n_main - 1)
    tile = lambda w: pl.BlockSpec((tm, w), lambda i: (i, 0))
    in_specs = [pl.BlockSpec((tm, d), lambda i: (main(i), 0))]
    args = [x]
    if ctx is None:
        row_of_tile = lambda i: i // tiles_per_seq
    else:
        row_of_tile = lambda i: jnp.where(i >= n_main, ctx_row, i // tiles_per_seq)
        in_specs.append(pl.BlockSpec((tm, d), lambda i: (jnp.maximum(i - n_main, 0), 0)))
        args.append(ctx)
    in_specs += [pl.BlockSpec((None, 1, N_MOD * d), lambda i: (row_of_tile(i), 0, 0)),
                 _resident((1, d)), _resident((d, f)), _resident((d, f)), _resident((f, d))]
    args += [m3, nw.reshape(1, d), w1, w3, w2]
    if proj is not None:
        n2w, win, wlow, gw, gbias = proj
        in_specs += [_resident((1, d))] + [_resident(a.shape) for a in (win, wlow, gw, gbias)]
        args += [n2w.reshape(1, d), win, wlow, gw, gbias]
        if rope is not None:
            rowtab, coltab = rope
            in_specs += [pl.BlockSpec((tm // GRID_W, 2 * LANE), lambda i: (main(i) % tiles_per_seq, 0)),
                         _resident(coltab.shape)]
            args += [rowtab, coltab]
    if final_w is not None:
        in_specs.append(_resident((1, d)))
        args.append(final_w.reshape(1, d))
    out_specs = [tile(d)]
    out_shape = [jax.ShapeDtypeStruct((t, d), F32)]
    if proj is not None:
        for w, dt in ((RET_W, BF16),) * 4 + ((GLA_QK, BF16),) * 2 + ((GLA_W, BF16),) * 2 + ((GLA_QK, F32),) * 2:
            out_specs.append(tile(w))
            out_shape.append(jax.ShapeDtypeStruct((t, w), dt))
    cast_specs, cast_shapes = _cast_jobs(cast, n_main)
    in_specs += cast_specs
    args += list(cast)
    out_specs += cast_specs
    out_shape += cast_shapes
    kern = functools.partial(_ffn_kernel, mod_off=mod_off, proj=proj is not None,
                             rope=rope is not None, final=final_w is not None,
                             n_main=None if ctx is None else n_main, n_cast=len(cast), d=d, f=f)
    return pl.pallas_call(
        kern, grid=(n_tiles,), in_specs=in_specs, out_specs=out_specs, out_shape=out_shape,
        scratch_shapes=[pltpu.VMEM((tm, f), BF16)],
        compiler_params=pltpu.CompilerParams(dimension_semantics=("arbitrary",),
                                             vmem_limit_bytes=VMEM_LIMIT),
        name=name,
    )(*args)


def _head_lanes(shape, hd):
    lane = _iota(shape, len(shape) - 1)
    return (lane >= GLA_DK) if hd % 2 else (lane < GLA_DK)


def _state_kernel(*refs, batch, cps, steps):
    it = iter(refs)
    per_batch = lambda: [next(it) for _ in range(batch)]
    rk_ref, rv_ref, gk_ref, gv_ref, gf_ref, gb_ref = (per_batch() for _ in range(6))
    dec_ref = next(it)
    irf_ref, irb_ref, igf_ref, igb_ref = (next(it) for _ in range(4))
    orf_ref, orb_ref, ogf_ref, ogb_ref = (next(it) for _ in range(4))
    frf_ref, frb_ref, fgf_ref, fgb_ref = (next(it) for _ in range(4))
    srf, srb, sgf, sgb, ub_ret, ub_gla, ub_dec = (next(it) for _ in range(7))
    n = CHUNK
    step = pl.program_id(0)
    lg = _log_sigmoid(dec_ref[...])

    @pl.when(step == 0)
    def _():
        srf[...] = irf_ref[...]
        srb[...] = irb_ref[...]
        sgf[...] = igf_ref[...]
        sgb[...] = igb_ref[...]

    @pl.when(step < steps)
    def _():
        row = _iota((n, LANE), 0).astype(F32)
        ri = _iota((n, n), 0)
        ci = _iota((n, n), 1)
        after = (ci > ri).astype(BF16)
        before = (ci < ri).astype(BF16)

        def gla(b, rows, g_ref, tri, edge):
            g = g_ref[b][rows, :]
            e = _dot_select(tri, g)
            tot = e[edge:edge + 1, :] + g[edge:edge + 1, :]
            kd = gk_ref[b][rows, :] * jnp.exp(e).astype(BF16)
            upd = []
            for hd in range(GLA_HEADS):
                pr = slice((hd // 2) * LANE, (hd // 2 + 1) * LANE)
                u = _dot_tn(gv_ref[b][rows, hd * LANE:(hd + 1) * LANE], kd[:, pr])
                upd.append(jnp.where(_head_lanes(u.shape, hd), u, 0.0))
            return upd, jnp.exp(tot)

        for j in range(cps):
            chunk = step * cps + j
            rows = slice(j * n, (j + 1) * n)
            orf_ref[:, j] = srf[...].astype(BF16)
            ogf_ref[:, j] = sgf[...].astype(BF16)
            for b in range(batch):
                for hd in range(RET_HEADS):
                    sl = slice(hd * LANE, (hd + 1) * LANE)
                    lgf = lg[0:1, sl]
                    lgb = lg[1:2, sl]
                    k = rk_ref[b][rows, sl]
                    v = rv_ref[b][rows, sl]
                    kf = k * jnp.exp((n - 1.0 - row) * lgf).astype(BF16)
                    srf[b, hd] = srf[b, hd] * jnp.exp(n * lgf) + _dot_tn(kf, v)
                    kb = k * jnp.exp(row * lgb).astype(BF16)
                    ub_ret[chunk, b, hd] = _dot_tn(kb, v).astype(BF16)
                upd, dec = gla(b, rows, gf_ref, after, 0)
                for hd in range(GLA_HEADS):
                    pr = slice((hd // 2) * LANE, (hd // 2 + 1) * LANE)
                    sgf[b, hd] = sgf[b, hd] * dec[:, pr] + upd[hd]
                upd, dec = gla(b, rows, gb_ref, before, n - 1)
                for hd in range(GLA_HEADS):
                    ub_gla[chunk, b, hd] = upd[hd].astype(BF16)
                ub_dec[chunk, b] = jnp.broadcast_to(dec, ub_dec.shape[2:])
        frf_ref[...] = srf[...]
        fgf_ref[...] = sgf[...]

    @pl.when(step >= steps)
    def _():
        for j in range(cps):
            jb = cps - 1 - j
            chunk = (2 * steps - 1 - step) * cps + jb
            orb_ref[:, jb] = srb[...].astype(BF16)
            ogb_ref[:, jb] = sgb[...].astype(BF16)
            for b in range(batch):
                for hd in range(RET_HEADS):
                    lgb = lg[1:2, hd * LANE:(hd + 1) * LANE]
                    srb[b, hd] = srb[b, hd] * jnp.exp(n * lgb) + ub_ret[chunk, b, hd].astype(F32)
                for hd in range(GLA_HEADS):
                    pr = slice((hd // 2) * LANE, (hd // 2 + 1) * LANE)
                    sgb[b, hd] = sgb[b, hd] * ub_dec[chunk, b, 0:1, pr] + ub_gla[chunk, b, hd].astype(F32)
        frb_ref[...] = srb[...]
        fgb_ref[...] = sgb[...]


def _states(rk, rv, gk, gv, gf, gb, dec, init, batch, first_chunk, nc):
    cps = 2 if nc % 2 == 0 and first_chunk % 2 == 0 else 1
    steps = nc // cps
    reading = lambda c: jnp.minimum(c, steps - 1)
    scanning = lambda c: steps - 1 - jnp.maximum(c - steps, 0)

    def tokens(w):
        return [pl.BlockSpec((cps * CHUNK, w), lambda c, b=b: ((first_chunk + b * nc) // cps + reading(c), 0))
                for b in range(batch)]

    st_shape = (batch, RET_HEADS, LANE, LANE)
    init_spec = pl.BlockSpec(st_shape, lambda c: (0, 0, 0, 0))
    chunk_shape = (batch, cps, RET_HEADS, LANE, LANE)
    of_spec = pl.BlockSpec(chunk_shape, lambda c: (0, reading(c), 0, 0, 0))
    ob_spec = pl.BlockSpec(chunk_shape, lambda c: (0, scanning(c), 0, 0, 0))
    per_chunk = jax.ShapeDtypeStruct((batch, nc, RET_HEADS, LANE, LANE), BF16)
    final = jax.ShapeDtypeStruct(st_shape, F32)
    kept = (nc, batch, RET_HEADS, LANE, LANE)
    rep = lambda a: [a] * batch
    return pl.pallas_call(
        functools.partial(_state_kernel, batch=batch, cps=cps, steps=steps),
        grid=(2 * steps,),
        in_specs=tokens(RET_W) + tokens(RET_W) + tokens(GLA_QK) + tokens(GLA_W) + tokens(GLA_QK) + tokens(GLA_QK)
        + [pl.BlockSpec(dec.shape, lambda c: (0, 0))] + [init_spec] * 4,
        out_specs=[of_spec, ob_spec, of_spec, ob_spec] + [init_spec] * 4,
        out_shape=[per_chunk] * 4 + [final] * 4,
        scratch_shapes=[pltpu.VMEM(st_shape, F32)] * 4
        + [pltpu.VMEM(kept, BF16), pltpu.VMEM(kept, BF16), pltpu.VMEM((nc, batch, SUBLANES, GLA_QK), F32)],
        compiler_params=pltpu.CompilerParams(dimension_semantics=("arbitrary",),
                                             vmem_limit_bytes=VMEM_LIMIT),
        name="states",
    )(*rep(rk), *rep(rv), *rep(gk), *rep(gv), *rep(gf), *rep(gb), dec, *init)


def _block_row(x, parent, r):
    n, w = x.shape
    if parent == n:
        return jnp.broadcast_to(x[r:r + 1, :], (n, w))
    x3 = x.reshape(n // parent, parent, w)
    return jnp.broadcast_to(x3[:, r:r + 1, :], x3.shape).reshape(n, w)


def _gla_levels(q, k_even, k_odd, gf, gb, bf, bb):
    n = q.shape[0]
    row = _iota(q.shape, 0)

    def factors(u_exp, w_exp):
        ew = jnp.exp2(w_exp).astype(BF16)
        return (q * jnp.exp2(u_exp).astype(BF16), k_even * ew, k_odd * ew)

    levels = []
    s = n // 2
    while s >= 2:
        if s >= 4:
            last_of_first = _block_row(bf, 2 * s, s - 1)
            first_of_second = _block_row(bb, 2 * s, s)
        else:
            upper = (row & 4) != 0
            last_of_first = jnp.where(upper, _block_row(bf, SUBLANES, 5), _block_row(bf, SUBLANES, 1))
            first_of_second = jnp.where(upper, _block_row(bb, SUBLANES, 6), _block_row(bb, SUBLANES, 2))
        df = bf - last_of_first
        db = bb - first_of_second
        levels.append((2 * s, [factors(jnp.minimum(df, db), -jnp.maximum(df, db))]))
        s //= 2
    odd = (row & 1) == 1
    qa = q * jnp.where(odd, jnp.exp2(gf), 2.0).astype(BF16)
    qb = q * jnp.where(odd, 2.0, jnp.exp2(gb)).astype(BF16)
    zero = jnp.zeros_like(k_even)
    ev = lambda x: jnp.where(odd, zero, x)
    od = lambda x: jnp.where(odd, x, zero)
    levels.append((2, [(qa, ev(k_even), ev(k_odd)), (qb, od(k_even), od(k_odd))]))
    return levels


def _mix_kernel(rq_ref, rk_ref, rv_ref, rg_ref, gq_ref, gk_ref, gv_ref, gg_ref, gf_ref, gb_ref,
                srf_ref, srb_ref, sgf_ref, sgb_ref, dec_ref, rnw_ref, gnw_ref,
                x_ref, m_ref, wout_ref, o_ref, mix_ref, decay_ref, qdec_ref, *, d):
    n = CHUNK
    half = n // 2
    ri = _iota((n, n), 0)
    ci = _iota((n, n), 1)
    step = pl.program_id(0)
    cur = step % 2
    code = _iota((half, n), 0) ^ (_iota((half, n), 1) & (half - 1))

    @pl.when(step == 0)
    def _():
        mix_ref[1] = jnp.zeros(mix_ref.shape[1:], BF16)
        lg = _log_sigmoid(dec_ref[...])
        dist = (ri - ci).astype(F32)
        row = _iota((n, LANE), 0).astype(F32)
        for hd in range(RET_HEADS):
            sl = slice(hd * LANE, (hd + 1) * LANE)
            lgf = lg[0:1, sl]
            lgb = lg[1:2, sl]
            decay_ref[hd] = jnp.exp(jnp.where(dist > 0, dist * lgf[:, 0:1],
                                              jnp.where(dist < 0, -dist * lgb[:, 0:1], jnp.log(2.0))))
            qdec_ref[hd] = jnp.exp((row + 1.0) * lgf)
            qdec_ref[RET_HEADS + hd] = jnp.exp((n - row) * lgb)

    def retention(c):
        tok = slice(c * n, (c + 1) * n)
        for hd in range(RET_HEADS):
            sl = slice(hd * LANE, (hd + 1) * LANE)
            q = rq_ref[tok, sl]
            p = (_dot_nt(q, rk_ref[tok, sl]) * decay_ref[hd]).astype(BF16)
            qf = q.astype(F32)
            qs = jnp.concatenate([(qf * qdec_ref[hd]).astype(BF16),
                                  (qf * qdec_ref[RET_HEADS + hd]).astype(BF16)], axis=1)
            st = jnp.concatenate([srf_ref[c, hd], srb_ref[c, hd]], axis=0)
            o = _dot(p, rv_ref[tok, sl]) + _dot(qs, st)
            mu = jnp.mean(o, axis=-1, keepdims=True)
            oc = o - mu
            var = jnp.mean(oc * oc, axis=-1, keepdims=True)
            r = oc * lax.rsqrt(var + EPS) * rnw_ref[:, sl] * rg_ref[tok, sl].astype(F32)
            mix_ref[cur, tok, sl] = r.astype(BF16)

    def gla_factors(c):
        tok = slice(c * n, (c + 1) * n)
        gf = gf_ref[tok, :] * LOG2E
        gb = gb_ref[tok, :] * LOG2E
        bf = _dot_select((ci <= ri).astype(BF16), gf)
        bb = _dot_select((ci >= ri).astype(BF16), gb)
        q = gq_ref[tok, :]
        k = gk_ref[tok, :]
        even_head = (_iota(k.shape, 1) & GLA_DK) == 0
        no_k = jnp.zeros_like(k)
        levels = _gla_levels(q, jnp.where(even_head, k, no_k), jnp.where(even_head, no_k, k), gf, gb, bf, bb)
        return levels, q * jnp.exp2(bf).astype(BF16), q * jnp.exp2(bb).astype(BF16)

    def finish_previous():
        o_ref[...] = x_ref[...] + m_ref[:, 5 * d:6 * d] * _dot(mix_ref[1 - cur], wout_ref[...])

    def gla_outputs(c, levels, qsf, qsb):
        tok = slice(c * n, (c + 1) * n)
        for pair in range(GLA_HEADS // 2):
            pr = slice(pair * LANE, (pair + 1) * LANE)
            near = [None, None]
            far = [None, None]
            for size, blocks in levels:
                for rb in range(2):
                    kb = rb if size < n else 1 - rb
                    rows = slice(rb * half, (rb + 1) * half)
                    keys = slice(kb * half, (kb + 1) * half)
                    u = jnp.concatenate([blk[0][rows, pr] for blk in blocks], axis=1)
                    w = jnp.concatenate([jnp.concatenate([blk[j][keys, pr] for blk in blocks], axis=1)
                                         for j in (1, 2)], axis=0)
                    t = _dot_nt(u, w)
                    if size == n:
                        far[rb] = t
                    else:
                        near[rb] = t if near[rb] is None else jnp.where(code < size, t, near[rb])
            qcat = jnp.concatenate([qsf[:, pr], qsb[:, pr]], axis=1)
            for j in range(2):
                hd = 2 * pair + j
                sl = slice(hd * LANE, (hd + 1) * LANE)
                mine = slice(j * half, (j + 1) * half)
                p = jnp.concatenate([jnp.concatenate([near[0][:, mine], far[0][:, mine]], axis=1),
                                     jnp.concatenate([far[1][:, mine], near[1][:, mine]], axis=1)], axis=0)
                st = jnp.concatenate([sgf_ref[c, hd], sgb_ref[c, hd]], axis=1)
                o = _dot(p.astype(BF16), gv_ref[tok, sl]) + _dot_nt(qcat, st)
                ms = jnp.mean(o * o, axis=-1, keepdims=True)
                r = o * lax.rsqrt(ms + EPS) * gnw_ref[:, sl] * gg_ref[tok, sl].astype(F32)
                mix_ref[cur, tok, RET_W + hd * LANE:RET_W + (hd + 1) * LANE] = r.astype(BF16)

    factors = []
    for c in range(MIX_CHUNKS):
        factors.append(gla_factors(c))
        if c == 0:
            finish_previous()
        retention(c)
    for c in range(MIX_CHUNKS):
        gla_outputs(c, *factors[c])


def _mix_call(mix_in, states, dec, rnw, gnw, x1, m3, wout, batch, nc):
    assert nc % MIX_CHUNKS == 0
    per_seq = nc // MIX_CHUNKS
    n_steps = batch * per_seq
    rows = MIX_CHUNKS * CHUNK
    t, d = n_steps * rows, x1.shape[1]
    mixed = lambda i: jnp.minimum(i, n_steps - 1)
    done = lambda i: jnp.maximum(i - 1, 0)
    tile = lambda w: pl.BlockSpec((rows, w), lambda i: (mixed(i), 0))
    st_spec = pl.BlockSpec((None, MIX_CHUNKS, RET_HEADS, LANE, LANE),
                           lambda i: (mixed(i) // per_seq, mixed(i) % per_seq, 0, 0, 0))
    const = lambda shape: pl.BlockSpec(shape, lambda i: (0,) * len(shape))
    widths = (RET_W,) * 4 + (GLA_QK,) * 2 + (GLA_W,) * 2 + (GLA_QK,) * 2
    return pl.pallas_call(
        functools.partial(_mix_kernel, d=d),
        grid=(n_steps + 1,),
        in_specs=[tile(w) for w in widths] + [st_spec] * 4
        + [const(dec.shape), const((1, RET_W)), const((1, GLA_W)),
           pl.BlockSpec((rows, d), lambda i: (done(i), 0)),
           pl.BlockSpec((None, 1, N_MOD * d), lambda i: (done(i) // per_seq, 0, 0)),
           const(wout.shape)],
        out_specs=pl.BlockSpec((rows, d), lambda i: (done(i), 0)),
        out_shape=jax.ShapeDtypeStruct((t, d), F32),
        scratch_shapes=[pltpu.VMEM((2, rows, RET_W + GLA_W), BF16),
                        pltpu.VMEM((RET_HEADS, CHUNK, CHUNK), F32),
                        pltpu.VMEM((2 * RET_HEADS, CHUNK, LANE), F32)],
        compiler_params=pltpu.CompilerParams(dimension_semantics=("arbitrary",),
                                             vmem_limit_bytes=VMEM_LIMIT),
        name="mix",
    )(*mix_in, *states, dec, rnw.reshape(1, RET_W), gnw.reshape(1, GLA_W), x1, m3, wout)


def _rope_tables(n_tok):
    freqs = ROPE_BASE ** (-jnp.arange(RET_DK // 4, dtype=F32) / (RET_DK // 4))

    def table(n_pos, first_half):
        ang = jnp.arange(n_pos, dtype=F32)[:, None] * freqs
        zero = jnp.zeros((n_pos, LANE // 2), F32)
        cos = jnp.concatenate([jnp.cos(ang)] * 2, axis=-1)
        sin = jnp.concatenate([-jnp.sin(ang), jnp.sin(ang)], axis=-1)
        halves = (cos, zero, sin, zero) if first_half else (zero, cos, zero, sin)
        return jnp.concatenate(halves, axis=-1)

    return table(n_tok // GRID_W, True), table(GRID_W, False)


def _pack_gate(w_f, b_f, w_b, b_b):
    gw = jnp.zeros((LANE, 2 * GLA_QK), F32)
    gw = gw.at[:GLA_RANK, :GLA_QK].set(w_f).at[GLA_RANK:2 * GLA_RANK, GLA_QK:].set(w_b)
    return gw.astype(BF16), jnp.concatenate([b_f, b_b]).reshape(1, 2 * GLA_QK)


def kernel(x, c, ctx, c_ctx, ada_w, ada_b, norm1_w, ffn1_w1, ffn1_w3, ffn1_w2, norm2_w, w_in,
           ret_decay_f, ret_decay_b, ret_norm_w, gla_gate_w_f, gla_gate_b_f, gla_gate_w_b, gla_gate_b_b,
           gla_norm_w, w_out, norm3_w, ffn2_w1, ffn2_w3, ffn2_w2, final_norm_w):
    batch, n_tok, d = x.shape
    n_ctx = ctx.shape[1]
    depth = ada_w.shape[0]
    assert depth == 1 and batch + 1 <= SUBLANES
    assert n_tok % FFN_TILE == 0 and n_tok % CHUNK == 0 and n_ctx % CHUNK == 0
    assert (batch * n_ctx) % FFN_TILE == 0

    cvec = jnp.zeros((SUBLANES, d), F32).at[:batch].set(c).at[batch].set(c_ctx)
    m, w1a, w3a, w2a, w_in_b, w_low_b = _modulation(cvec, ada_w[0], ada_b[0],
                                                    (ffn1_w1[0], ffn1_w3[0], ffn1_w2[0]), w_in[0].T)
    m3 = m.reshape(SUBLANES, 1, N_MOD * d)

    gw, gbias = _pack_gate(gla_gate_w_f[0], gla_gate_b_f[0], gla_gate_w_b[0], gla_gate_b_b[0])
    proj = (norm2_w[0], w_in_b, w_low_b, gw, gbias)
    f1 = (norm1_w[0], w1a, w3a, w2a)
    rowtab, coltab = _rope_tables(n_tok)
    dec = jnp.zeros((SUBLANES, RET_W), F32)
    dec = dec.at[0].set(jnp.repeat(ret_decay_f[0], LANE)).at[1].set(jnp.repeat(ret_decay_b[0], LANE))

    tiles_per_seq = n_tok // FFN_TILE
    first = _ffn_call(x.reshape(batch * n_tok, d), m3, tiles_per_seq, *f1, mod_off=0, tm=FFN_TILE,
                      ctx=ctx.reshape(batch * n_ctx, d), ctx_row=batch, proj=proj, rope=(rowtab, coltab),
                      cast=(ffn2_w1[0], ffn2_w3[0], ffn2_w2[0], w_out[0]), name="ffn_in")
    x1, mix_in, (w1b, w3b, w2b, woutb) = first[0], first[1:11], first[11:]
    scan_in = (mix_in[1], mix_in[2], mix_in[5], mix_in[6], mix_in[8], mix_in[9])
    zero = jnp.zeros((batch, RET_HEADS, LANE, LANE), F32)
    nc, nc_ctx = n_tok // CHUNK, n_ctx // CHUNK
    ctx_states = _states(*scan_in, dec, (zero,) * 4, batch, batch * nc, nc_ctx)[4:]
    states = _states(*scan_in, dec, ctx_states, batch, 0, nc)[:4]
    x2 = _mix_call(mix_in, states, dec, ret_norm_w[0], gla_norm_w[0], x1, m3, woutb, batch, nc)
    out = _ffn_call(x2, m3, tiles_per_seq, norm3_w[0], w1b, w3b, w2b, mod_off=6, tm=FFN_TILE,
                    final_w=final_norm_w, name="ffn_out")[0]
    return out.reshape(batch, n_tok, d)
```

```python
import functools

import jax
import jax.numpy as jnp
from jax import lax
from jax.experimental import pallas as pl
from jax.experimental.pallas import tpu as pltpu

F32 = jnp.float32
BF16 = jnp.bfloat16

EPS = 1e-6
LOG2E = 1.4426950408889634
N_MOD = 9
GRID_W = 64
ROPE_BASE = 10000.0
RET_HEADS = 4
RET_DK = 128
RET_DV = 128
GLA_HEADS = 4
GLA_DK = 64
GLA_DV = 128
GLA_RANK = 16
GLA_TAU = 16.0
RET_W = RET_HEADS * RET_DV
GLA_W = GLA_HEADS * GLA_DV
GLA_QK = GLA_HEADS * GLA_DK

LANE = 128
SUBLANES = 8
BF16_ROWS = 16
MOD_STEPS = 8
CHUNK = 256
MIX_CHUNKS = 2
FFN_TILE = 512
ROW_BLOCK = 256
FF_CHUNK = 256
VMEM_LIMIT = 60 * 1024 * 1024

C_RQ, C_RK, C_RV, C_RG = 0, 512, 1024, 1536
C_GQ, C_GK, C_GV, C_GG = 2048, 2304, 2560, 3072
C_LOW = 3584


def _silu(x):
    return x * (1.0 / (1.0 + jnp.exp(-x)))


def _log_sigmoid(z):
    return jnp.minimum(z, 0.0) - jnp.log(1.0 + jnp.exp(-jnp.abs(z)))


def _rms(x, w):
    return x * lax.rsqrt(jnp.mean(x * x, axis=-1, keepdims=True) + EPS) * w


def _dot(a, b):
    return jnp.dot(a, b, preferred_element_type=F32)


def _dot_nt(a, b):
    return lax.dot_general(a, b, (((1,), (1,)), ((), ())), preferred_element_type=F32)


def _dot_tn(a, b):
    return lax.dot_general(a, b, (((0,), (0,)), ((), ())), preferred_element_type=F32)


def _dot_select(sel, x):
    hi = x.astype(BF16)
    lo = (x - hi.astype(F32)).astype(BF16)
    return _dot(sel, hi) + _dot(sel, lo)


def _iota(shape, dim):
    return lax.broadcasted_iota(jnp.int32, shape, dim)


def _resident(shape):
    nd = len(shape)
    return pl.BlockSpec(shape, lambda *_: (0,) * nd, pipeline_mode=pl.Buffered(1))


def _cast_jobs(arrays, n_steps):
    specs, shapes = [], []
    for w in arrays:
        rows = next(r for r in range(BF16_ROWS, w.shape[0] + 1, BF16_ROWS)
                    if w.shape[0] % r == 0 and w.shape[0] // r <= n_steps)
        specs.append(pl.BlockSpec((rows, w.shape[1]),
                                  lambda i, last=w.shape[0] // rows - 1: (jnp.minimum(i, last), 0)))
        shapes.append(jax.ShapeDtypeStruct(w.shape, BF16))
    return specs, shapes


def _run_cast_jobs(src_refs, dst_refs):
    for src_ref, dst_ref in zip(src_refs, dst_refs):
        dst_ref[...] = src_ref[...].astype(BF16)


W_IN_BLOCK = 512


def _mod_kernel(*refs, n_cast):
    c_ref, w_ref, b_ref = refs[:3]
    cast_in, (wt_ref, wt_low_ref) = refs[3:3 + n_cast], refs[3 + n_cast:5 + n_cast]
    o_ref = refs[5 + n_cast]
    cast_out, (win_ref, wlow_ref) = refs[6 + n_cast:6 + 2 * n_cast], refs[6 + 2 * n_cast:]
    cond = _silu(c_ref[...]).astype(BF16)
    o_ref[...] = _dot(cond, w_ref[...].astype(BF16)) + b_ref[...]
    _run_cast_jobs(cast_in, cast_out)
    win_ref[...] = wt_ref[...].T.astype(BF16)
    low = wt_low_ref[...]
    low = jnp.concatenate([low, jnp.zeros((LANE - low.shape[0], low.shape[1]), F32)], axis=0)
    wlow_ref[...] = low.T.astype(BF16)


def _modulation(cvec, ada_w, ada_b, cast, w_in_t):
    d, n = ada_w.shape
    bn = n // MOD_STEPS if n % (MOD_STEPS * LANE) == 0 else d
    steps = n // bn
    assert C_LOW % W_IN_BLOCK == 0 and C_LOW // W_IN_BLOCK <= steps
    n_low = w_in_t.shape[0] - C_LOW
    last = C_LOW // W_IN_BLOCK - 1
    cast_specs, cast_shapes = _cast_jobs(cast, steps)
    return pl.pallas_call(
        functools.partial(_mod_kernel, n_cast=len(cast)),
        grid=(steps,),
        in_specs=[pl.BlockSpec((SUBLANES, d), lambda j: (0, 0)),
                  pl.BlockSpec((d, bn), lambda j: (0, j)),
                  pl.BlockSpec((1, bn), lambda j: (0, j))] + cast_specs
        + [pl.BlockSpec((W_IN_BLOCK, d), lambda j: (jnp.minimum(j, last), 0)),
           pl.BlockSpec((n_low, d), lambda j: (C_LOW // n_low, 0))],
        out_specs=[pl.BlockSpec((SUBLANES, bn), lambda j: (0, j))] + cast_specs
        + [pl.BlockSpec((d, W_IN_BLOCK), lambda j: (0, jnp.minimum(j, last))),
           pl.BlockSpec((d, LANE), lambda j: (0, 0))],
        out_shape=[jax.ShapeDtypeStruct((SUBLANES, n), F32)] + cast_shapes
        + [jax.ShapeDtypeStruct((d, C_LOW), BF16), jax.ShapeDtypeStruct((d, LANE), BF16)],
        compiler_params=pltpu.CompilerParams(dimension_semantics=("arbitrary",),
                                             vmem_limit_bytes=VMEM_LIMIT),
        name="mod",
    )(cvec, ada_w, ada_b.reshape(1, n), *cast, w_in_t, w_in_t)


def _swap32(x):
    lane = _iota(x.shape, 1)
    quarter = RET_DK // 4
    first = (lane & (2 * quarter - 1)) < quarter
    return jnp.where(first, pltpu.roll(x, LANE - quarter, 1), pltpu.roll(x, quarter, 1))


def _ffn_kernel(*refs, mod_off, proj, rope, final, n_main, n_cast, d, f):
    it = iter(refs)
    x_ref = next(it)
    if n_main is not None:
        xc_ref = next(it)
        is_ctx = pl.program_id(0) >= n_main
    m_ref, nw_ref, w1_ref, w3_ref, w2_ref = (next(it) for _ in range(5))
    if proj:
        n2w_ref, win_ref, wlow_ref, gw_ref, gbias_ref = (next(it) for _ in range(5))
        if rope:
            rowtab_ref, coltab_ref = next(it), next(it)
    if final:
        fnw_ref = next(it)
    cast_in = [next(it) for _ in range(n_cast)]
    xo_ref = next(it)
    if proj:
        (rq_ref, rk_ref, rv_ref, rg_ref, gq_ref, gk_ref, gv_ref, gg_ref,
         gf_ref, gb_ref) = (next(it) for _ in range(10))
    cast_out = [next(it) for _ in range(n_cast)]
    u_ref = next(it)
    _run_cast_jobs(cast_in, cast_out)

    def mod(i):
        return m_ref[:, (mod_off + i) * d:(mod_off + i + 1) * d]

    h2s = []
    for rb in range(x_ref.shape[0] // ROW_BLOCK):
        rows = slice(rb * ROW_BLOCK, (rb + 1) * ROW_BLOCK)
        x = x_ref[rows, :]
        if n_main is not None:
            x = jnp.where(is_ctx, xc_ref[rows, :], x)
        h = (_rms(x, nw_ref[...]) * (1.0 + mod(1)) + mod(0)).astype(BF16)
        for k in range(f // FF_CHUNK):
            sl = slice(k * FF_CHUNK, (k + 1) * FF_CHUNK)
            a = _dot(h, w1_ref[:, sl])
            g = _dot(h, w3_ref[:, sl])
            u_ref[rows, sl] = (_silu(a) * g).astype(BF16)
        y = _dot(u_ref[rows, :], w2_ref[...])
        x1 = x + (0.5 * mod(2)) * y

        if final:
            xo_ref[rows, :] = _rms(x1, fnw_ref[...])
        else:
            xo_ref[rows, :] = x1

        if proj:
            h2s.append((_rms(x1, n2w_ref[...]) * (1.0 + mod(4)) + mod(3)).astype(BF16))

    for rb, h2 in enumerate(h2s):
        rows = slice(rb * ROW_BLOCK, (rb + 1) * ROW_BLOCK)

        def p(lo, hi):
            return _dot(h2, win_ref[:, lo:hi])

        if rope:
            by_row = _iota((GRID_W, 2 * LANE), 1) % LANE < LANE // 2
            g0 = rb * (ROW_BLOCK // GRID_W)
            tab = jnp.concatenate(
                [jnp.where(by_row, jnp.broadcast_to(rowtab_ref[g0 + g:g0 + g + 1, :], (GRID_W, 2 * LANE)),
                           coltab_ref[...]) for g in range(ROW_BLOCK // GRID_W)], axis=0)
            cos, sin = tab[:, :LANE], tab[:, LANE:]
            if n_main is not None:
                cos = jnp.where(is_ctx, 1.0, cos)
                sin = jnp.where(is_ctx, 0.0, sin)

        low = _dot(h2, wlow_ref[...]).astype(BF16)
        z = _dot(low, gw_ref[...]) + gbias_ref[...]
        ls = _log_sigmoid(z) * (1.0 / GLA_TAU)
        gf_ref[rows, :] = ls[:, :GLA_QK]
        gb_ref[rows, :] = ls[:, GLA_QK:]
        for base, scale, o_ref in ((C_RQ, RET_DK ** -0.5, rq_ref), (C_RK, 1.0, rk_ref)):
            t = p(base, base + RET_W)
            for hd in range(RET_HEADS):
                th = t[:, hd * LANE:(hd + 1) * LANE] * scale
                if rope:
                    th = th * cos + _swap32(th) * sin
                o_ref[rows, hd * LANE:(hd + 1) * LANE] = th.astype(BF16)
        rg_ref[rows, :] = _silu(p(C_RG, C_RG + RET_W)).astype(BF16)
        gg_ref[rows, :] = _silu(p(C_GG, C_GG + GLA_W)).astype(BF16)
        gq_ref[rows, :] = (p(C_GQ, C_GQ + GLA_QK) * GLA_DK ** -0.5).astype(BF16)
        gk_ref[rows, :] = p(C_GK, C_GK + GLA_QK).astype(BF16)
        rv_ref[rows, :] = p(C_RV, C_RV + RET_W).astype(BF16)
        gv_ref[rows, :] = p(C_GV, C_GV + GLA_W).astype(BF16)


def _ffn_call(x, m3, tiles_per_seq, nw, w1, w3, w2, *, mod_off, tm, ctx=None, ctx_row=None, proj=None,
              rope=None, final_w=None, cast=(), name):
    t, d = x.shape
    f = w1.shape[1]
    n_main = t // tm
    n_tiles = n_main + (0 if ctx is None else ctx.shape[0] // tm)
    t = n_tiles * tm
    main = lambda i: jnp.minimum(i, n_main - 1)
    tile = lambda w: pl.BlockSpec((tm, w), lambda i: (i, 0))
    in_specs = [pl.BlockSpec((tm, d), lambda i: (main(i), 0))]
    args = [x]
    if ctx is None:
        row_of_tile = lambda i: i // tiles_per_seq
    else:
        row_of_tile = lambda i: jnp.where(i >= n_main, ctx_row, i // tiles_per_seq)
        in_specs.append(pl.BlockSpec((tm, d), lambda i: (jnp.maximum(i - n_main, 0), 0)))
        args.append(ctx)
    in_specs += [pl.BlockSpec((None, 1, N_MOD * d), lambda i: (row_of_tile(i), 0, 0)),
                 _resident((1, d)), _resident((d, f)), _resident((d, f)), _resident((f, d))]
    args += [m3, nw.reshape(1, d), w1, w3, w2]
    if proj is not None:
        n2w, win, wlow, gw, gbias = proj
        in_specs += [_resident((1, d))] + [_resident(a.shape) for a in (win, wlow, gw, gbias)]
        args += [n2w.reshape(1, d), win, wlow, gw, gbias]
        if rope is not None:
            rowtab, coltab = rope
            in_specs += [pl.BlockSpec((tm // GRID_W, 2 * LANE), lambda i: (main(i) % tiles_per_seq, 0)),
                         _resident(coltab.shape)]
            args += [rowtab, coltab]
    if final_w is not None:
        in_specs.append(_resident((1, d)))
        args.append(final_w.reshape(1, d))
    out_specs = [tile(d)]
    out_shape = [jax.ShapeDtypeStruct((t, d), F32)]
    if proj is not None:
        for w, dt in ((RET_W, BF16),) * 4 + ((GLA_QK, BF16),) * 2 + ((GLA_W, BF16),) * 2 + ((GLA_QK, F32),) * 2:
            out_specs.append(tile(w))
            out_shape.append(jax.ShapeDtypeStruct((t, w), dt))
    cast_specs, cast_shapes = _cast_jobs(cast, n_main)
    in_specs += cast_specs
    args += list(cast)
    out_specs += cast_specs
    out_shape += cast_shapes
    kern = functools.partial(_ffn_kernel, mod_off=mod_off, proj=proj is not None,
                             rope=rope is not None, final=final_w is not None,
                             n_main=None if ctx is None else n_main, n_cast=len(cast), d=d, f=f)
    return pl.pallas_call(
        kern, grid=(n_tiles,), in_specs=in_specs, out_specs=out_specs, out_shape=out_shape,
        scratch_shapes=[pltpu.VMEM((tm, f), BF16)],
        compiler_params=pltpu.CompilerParams(dimension_semantics=("arbitrary",),
                                             vmem_limit_bytes=VMEM_LIMIT),
        name=name,
    )(*args)


def _head_lanes(shape, hd):
    lane = _iota(shape, len(shape) - 1)
    return (lane >= GLA_DK) if hd % 2 else (lane < GLA_DK)


def _state_kernel(*refs, batch, cps, steps):
    it = iter(refs)
    per_batch = lambda: [next(it) for _ in range(batch)]
    rk_ref, rv_ref, gk_ref, gv_ref, gf_ref, gb_ref = (per_batch() for _ in range(6))
    dec_ref = next(it)
    irf_ref, irb_ref, igf_ref, igb_ref = (next(it) for _ in range(4))
    orf_ref, orb_ref, ogf_ref, ogb_ref = (next(it) for _ in range(4))
    frf_ref, frb_ref, fgf_ref, fgb_ref = (next(it) for _ in range(4))
    srf, srb, sgf, sgb, ub_ret, ub_gla, ub_dec = (next(it) for _ in range(7))
    n = CHUNK
    step = pl.program_id(0)
    lg = _log_sigmoid(dec_ref[...])

    @pl.when(step == 0)
    def _():
        srf[...] = irf_ref[...]
        srb[...] = irb_ref[...]
        sgf[...] = igf_ref[...]
        sgb[...] = igb_ref[...]

    @pl.when(step < steps)
    def _():
        row = _iota((n, LANE), 0).astype(F32)
        ri = _iota((n, n), 0)
        ci = _iota((n, n), 1)
        after = (ci > ri).astype(BF16)
        before = (ci < ri).astype(BF16)
        heads = [slice(hd * LANE, (hd + 1) * LANE) for hd in range(RET_HEADS)]
        to_end = [jnp.exp((n - 1.0 - row) * lg[0:1, sl]).astype(BF16) for sl in heads]
        to_start = [jnp.exp(row * lg[1:2, sl]).astype(BF16) for sl in heads]

        def gla_keys(b, rows, g_ref, tri, edge):
            g = g_ref[b][rows, :]
            e = _dot_select(tri, g)
            tot = e[edge:edge + 1, :] + g[edge:edge + 1, :]
            return gk_ref[b][rows, :] * jnp.exp(e).astype(BF16), jnp.exp(tot)

        def gla(b, rows, keys):
            kd, dec = keys
            upd = []
            for hd in range(GLA_HEADS):
                pr = slice((hd // 2) * LANE, (hd // 2 + 1) * LANE)
                u = _dot_tn(gv_ref[b][rows, hd * LANE:(hd + 1) * LANE], kd[:, pr])
                upd.append(jnp.where(_head_lanes(u.shape, hd), u, 0.0))
            return upd, dec

        for j in range(cps):
            chunk = step * cps + j
            rows = slice(j * n, (j + 1) * n)
            orf_ref[:, j] = srf[...].astype(BF16)
            ogf_ref[:, j] = sgf[...].astype(BF16)
            keys_f = [gla_keys(b, rows, gf_ref, after, 0) for b in range(batch)]
            keys_b = [gla_keys(b, rows, gb_ref, before, n - 1) for b in range(batch)]
            for b in range(batch):
                for hd in range(RET_HEADS):
                    sl = slice(hd * LANE, (hd + 1) * LANE)
                    k = rk_ref[b][rows, sl]
                    v = rv_ref[b][rows, sl]
                    srf[b, hd] = srf[b, hd] * jnp.exp(n * lg[0:1, sl]) + _dot_tn(k * to_end[hd], v)
                    ub_ret[chunk, b, hd] = _dot_tn(k * to_start[hd], v).astype(BF16)
                upd, dec = gla(b, rows, keys_f[b])
                for hd in range(GLA_HEADS):
                    pr = slice((hd // 2) * LANE, (hd // 2 + 1) * LANE)
                    sgf[b, hd] = sgf[b, hd] * dec[:, pr] + upd[hd]
                upd, dec = gla(b, rows, keys_b[b])
                for hd in range(GLA_HEADS):
                    ub_gla[chunk, b, hd] = upd[hd].astype(BF16)
                ub_dec[chunk, b] = jnp.broadcast_to(dec, ub_dec.shape[2:])
        frf_ref[...] = srf[...]
        fgf_ref[...] = sgf[...]

    @pl.when(step >= steps)
    def _():
        for j in range(cps):
            jb = cps - 1 - j
            chunk = (2 * steps - 1 - step) * cps + jb
            orb_ref[:, jb] = srb[...].astype(BF16)
            ogb_ref[:, jb] = sgb[...].astype(BF16)
            for b in range(batch):
                for hd in range(RET_HEADS):
                    lgb = lg[1:2, hd * LANE:(hd + 1) * LANE]
                    srb[b, hd] = srb[b, hd] * jnp.exp(n * lgb) + ub_ret[chunk, b, hd].astype(F32)
                for hd in range(GLA_HEADS):
                    pr = slice((hd // 2) * LANE, (hd // 2 + 1) * LANE)
                    sgb[b, hd] = sgb[b, hd] * ub_dec[chunk, b, 0:1, pr] + ub_gla[chunk, b, hd].astype(F32)
        frb_ref[...] = srb[...]
        fgb_ref[...] = sgb[...]


def _states(rk, rv, gk, gv, gf, gb, dec, init, batch, first_chunk, nc):
    cps = 2 if nc % 2 == 0 and first_chunk % 2 == 0 else 1
    steps = nc // cps
    reading = lambda c: jnp.minimum(c, steps - 1)
    scanning = lambda c: steps - 1 - jnp.maximum(c - steps, 0)

    def tokens(w):
        return [pl.BlockSpec((cps * CHUNK, w), lambda c, b=b: ((first_chunk + b * nc) // cps + reading(c), 0))
                for b in range(batch)]

    st_shape = (batch, RET_HEADS, LANE, LANE)
    init_spec = pl.BlockSpec(st_shape, lambda c: (0, 0, 0, 0))
    chunk_shape = (batch, cps, RET_HEADS, LANE, LANE)
    of_spec = pl.BlockSpec(chunk_shape, lambda c: (0, reading(c), 0, 0, 0))
    ob_spec = pl.BlockSpec(chunk_shape, lambda c: (0, scanning(c), 0, 0, 0))
    per_chunk = jax.ShapeDtypeStruct((batch, nc, RET_HEADS, LANE, LANE), BF16)
    final = jax.ShapeDtypeStruct(st_shape, F32)
    kept = (nc, batch, RET_HEADS, LANE, LANE)
    rep = lambda a: [a] * batch
    return pl.pallas_call(
        functools.partial(_state_kernel, batch=batch, cps=cps, steps=steps),
        grid=(2 * steps,),
        in_specs=tokens(RET_W) + tokens(RET_W) + tokens(GLA_QK) + tokens(GLA_W) + tokens(GLA_QK) + tokens(GLA_QK)
        + [pl.BlockSpec(dec.shape, lambda c: (0, 0))] + [init_spec] * 4,
        out_specs=[of_spec, ob_spec, of_spec, ob_spec] + [init_spec] * 4,
        out_shape=[per_chunk] * 4 + [final] * 4,
        scratch_shapes=[pltpu.VMEM(st_shape, F32)] * 4
        + [pltpu.VMEM(kept, BF16), pltpu.VMEM(kept, BF16), pltpu.VMEM((nc, batch, SUBLANES, GLA_QK), F32)],
        compiler_params=pltpu.CompilerParams(dimension_semantics=("arbitrary",),
                                             vmem_limit_bytes=VMEM_LIMIT),
        name="states",
    )(*rep(rk), *rep(rv), *rep(gk), *rep(gv), *rep(gf), *rep(gb), dec, *init)


def _block_row(x, parent, r):
    n, w = x.shape
    if parent == n:
        return jnp.broadcast_to(x[r:r + 1, :], (n, w))
    x3 = x.reshape(n // parent, parent, w)
    return jnp.broadcast_to(x3[:, r:r + 1, :], x3.shape).reshape(n, w)


def _gla_levels(q, k_even, k_odd, gf, gb, bf, bb):
    n = q.shape[0]
    row = _iota(q.shape, 0)

    def factors(u_exp, w_exp):
        ew = jnp.exp2(w_exp).astype(BF16)
        return (q * jnp.exp2(u_exp).astype(BF16), k_even * ew, k_odd * ew)

    levels = []
    s = n // 2
    while s >= 2:
        if s >= 4:
            last_of_first = _block_row(bf, 2 * s, s - 1)
            first_of_second = _block_row(bb, 2 * s, s)
        else:
            upper = (row & 4) != 0
            last_of_first = jnp.where(upper, _block_row(bf, SUBLANES, 5), _block_row(bf, SUBLANES, 1))
            first_of_second = jnp.where(upper, _block_row(bb, SUBLANES, 6), _block_row(bb, SUBLANES, 2))
        df = bf - last_of_first
        db = bb - first_of_second
        levels.append((2 * s, [factors(jnp.minimum(df, db), -jnp.maximum(df, db))]))
        s //= 2
    odd = (row & 1) == 1
    qa = q * jnp.where(odd, jnp.exp2(gf), 2.0).astype(BF16)
    qb = q * jnp.where(odd, 2.0, jnp.exp2(gb)).astype(BF16)
    zero = jnp.zeros_like(k_even)
    ev = lambda x: jnp.where(odd, zero, x)
    od = lambda x: jnp.where(odd, x, zero)
    levels.append((2, [(qa, ev(k_even), ev(k_odd)), (qb, od(k_even), od(k_odd))]))
    return levels


def _mix_kernel(rq_ref, rk_ref, rv_ref, rg_ref, gq_ref, gk_ref, gv_ref, gg_ref, gf_ref, gb_ref,
                srf_ref, srb_ref, sgf_ref, sgb_ref, dec_ref, rnw_ref, gnw_ref,
                x_ref, m_ref, wout_ref, o_ref, mix_ref, decay_ref, qdec_ref, *, d):
    n = CHUNK
    half = n // 2
    ri = _iota((n, n), 0)
    ci = _iota((n, n), 1)
    step = pl.program_id(0)
    cur = step % 2
    code = _iota((half, n), 0) ^ (_iota((half, n), 1) & (half - 1))

    @pl.when(step == 0)
    def _():
        mix_ref[1] = jnp.zeros(mix_ref.shape[1:], BF16)
        lg = _log_sigmoid(dec_ref[...])
        dist = (ri - ci).astype(F32)
        row = _iota((n, LANE), 0).astype(F32)
        for hd in range(RET_HEADS):
            sl = slice(hd * LANE, (hd + 1) * LANE)
            lgf = lg[0:1, sl]
            lgb = lg[1:2, sl]
            decay_ref[hd] = jnp.exp(jnp.where(dist > 0, dist * lgf[:, 0:1],
                                              jnp.where(dist < 0, -dist * lgb[:, 0:1], jnp.log(2.0))))
            qdec_ref[hd] = jnp.exp((row + 1.0) * lgf)
            qdec_ref[RET_HEADS + hd] = jnp.exp((n - row) * lgb)

    def retention(c):
        tok = slice(c * n, (c + 1) * n)
        for hd in range(RET_HEADS):
            sl = slice(hd * LANE, (hd + 1) * LANE)
            q = rq_ref[tok, sl]
            p = (_dot_nt(q, rk_ref[tok, sl]) * decay_ref[hd]).astype(BF16)
            qf = q.astype(F32)
            qs = jnp.concatenate([(qf * qdec_ref[hd]).astype(BF16),
                                  (qf * qdec_ref[RET_HEADS + hd]).astype(BF16)], axis=1)
            st = jnp.concatenate([srf_ref[c, hd], srb_ref[c, hd]], axis=0)
            o = _dot(p, rv_ref[tok, sl]) + _dot(qs, st)
            mu = jnp.mean(o, axis=-1, keepdims=True)
            oc = o - mu
            var = jnp.mean(oc * oc, axis=-1, keepdims=True)
            r = oc * lax.rsqrt(var + EPS) * rnw_ref[:, sl] * rg_ref[tok, sl].astype(F32)
            mix_ref[cur, tok, sl] = r.astype(BF16)

    def gla_factors(c):
        tok = slice(c * n, (c + 1) * n)
        gf = gf_ref[tok, :] * LOG2E
        gb = gb_ref[tok, :] * LOG2E
        bf = _dot_select((ci <= ri).astype(BF16), gf)
        bb = _dot_select((ci >= ri).astype(BF16), gb)
        q = gq_ref[tok, :]
        k = gk_ref[tok, :]
        even_head = (_iota(k.shape, 1) & GLA_DK) == 0
        no_k = jnp.zeros_like(k)
        levels = _gla_levels(q, jnp.where(even_head, k, no_k), jnp.where(even_head, no_k, k), gf, gb, bf, bb)
        return levels, q * jnp.exp2(bf).astype(BF16), q * jnp.exp2(bb).astype(BF16)

    def finish_previous():
        o_ref[...] = x_ref[...] + m_ref[:, 5 * d:6 * d] * _dot(mix_ref[1 - cur], wout_ref[...])

    def gla_outputs(c, levels, qsf, qsb):
        tok = slice(c * n, (c + 1) * n)
        for pair in range(GLA_HEADS // 2):
            pr = slice(pair * LANE, (pair + 1) * LANE)
            near = [None, None]
            far = [None, None]
            for size, blocks in levels:
                for rb in range(2):
                    kb = rb if size < n else 1 - rb
                    rows = slice(rb * half, (rb + 1) * half)
                    keys = slice(kb * half, (kb + 1) * half)
                    u = jnp.concatenate([blk[0][rows, pr] for blk in blocks], axis=1)
                    w = jnp.concatenate([jnp.concatenate([blk[j][keys, pr] for blk in blocks], axis=1)
                                         for j in (1, 2)], axis=0)
                    t = _dot_nt(u, w)
                    if size == n:
                        far[rb] = t
                    else:
                        near[rb] = t if near[rb] is None else jnp.where(code < size, t, near[rb])
            qcat = jnp.concatenate([qsf[:, pr], qsb[:, pr]], axis=1)
            for j in range(2):
                hd = 2 * pair + j
                sl = slice(hd * LANE, (hd + 1) * LANE)
                mine = slice(j * half, (j + 1) * half)
                p = jnp.concatenate([jnp.concatenate([near[0][:, mine], far[0][:, mine]], axis=1),
                                     jnp.concatenate([far[1][:, mine], near[1][:, mine]], axis=1)], axis=0)
                st = jnp.concatenate([sgf_ref[c, hd], sgb_ref[c, hd]], axis=1)
                o = _dot(p.astype(BF16), gv_ref[tok, sl]) + _dot_nt(qcat, st)
                ms = jnp.mean(o * o, axis=-1, keepdims=True)
                r = o * lax.rsqrt(ms + EPS) * gnw_ref[:, sl] * gg_ref[tok, sl].astype(F32)
                mix_ref[cur, tok, RET_W + hd * LANE:RET_W + (hd + 1) * LANE] = r.astype(BF16)

    factors = []
    for c in range(MIX_CHUNKS):
        factors.append(gla_factors(c))
        if c == 0:
            finish_previous()
        retention(c)
    for c in range(MIX_CHUNKS):
        gla_outputs(c, *factors[c])


def _mix_call(mix_in, states, dec, rnw, gnw, x1, m3, wout, batch, nc):
    assert nc % MIX_CHUNKS == 0
    per_seq = nc // MIX_CHUNKS
    n_steps = batch * per_seq
    rows = MIX_CHUNKS * CHUNK
    t, d = n_steps * rows, x1.shape[1]
    mixed = lambda i: jnp.minimum(i, n_steps - 1)
    done = lambda i: jnp.maximum(i - 1, 0)
    tile = lambda w: pl.BlockSpec((rows, w), lambda i: (mixed(i), 0))
    st_spec = pl.BlockSpec((None, MIX_CHUNKS, RET_HEADS, LANE, LANE),
                           lambda i: (mixed(i) // per_seq, mixed(i) % per_seq, 0, 0, 0))
    const = lambda shape: pl.BlockSpec(shape, lambda i: (0,) * len(shape))
    widths = (RET_W,) * 4 + (GLA_QK,) * 2 + (GLA_W,) * 2 + (GLA_QK,) * 2
    return pl.pallas_call(
        functools.partial(_mix_kernel, d=d),
        grid=(n_steps + 1,),
        in_specs=[tile(w) for w in widths] + [st_spec] * 4
        + [const(dec.shape), const((1, RET_W)), const((1, GLA_W)),
           pl.BlockSpec((rows, d), lambda i: (done(i), 0)),
           pl.BlockSpec((None, 1, N_MOD * d), lambda i: (done(i) // per_seq, 0, 0)),
           const(wout.shape)],
        out_specs=pl.BlockSpec((rows, d), lambda i: (done(i), 0)),
        out_shape=jax.ShapeDtypeStruct((t, d), F32),
        scratch_shapes=[pltpu.VMEM((2, rows, RET_W + GLA_W), BF16),
                        pltpu.VMEM((RET_HEADS, CHUNK, CHUNK), F32),
                        pltpu.VMEM((2 * RET_HEADS, CHUNK, LANE), F32)],
        compiler_params=pltpu.CompilerParams(dimension_semantics=("arbitrary",),
                                             vmem_limit_bytes=VMEM_LIMIT),
        name="mix",
    )(*mix_in, *states, dec, rnw.reshape(1, RET_W), gnw.reshape(1, GLA_W), x1, m3, wout)


def _rope_tables(n_tok):
    freqs = ROPE_BASE ** (-jnp.arange(RET_DK // 4, dtype=F32) / (RET_DK // 4))

    def table(n_pos, first_half):
        ang = jnp.arange(n_pos, dtype=F32)[:, None] * freqs
        zero = jnp.zeros((n_pos, LANE // 2), F32)
        cos = jnp.concatenate([jnp.cos(ang)] * 2, axis=-1)
        sin = jnp.concatenate([-jnp.sin(ang), jnp.sin(ang)], axis=-1)
        halves = (cos, zero, sin, zero) if first_half else (zero, cos, zero, sin)
        return jnp.concatenate(halves, axis=-1)

    return table(n_tok // GRID_W, True), table(GRID_W, False)


def _pack_gate(w_f, b_f, w_b, b_b):
    gw = jnp.zeros((LANE, 2 * GLA_QK), F32)
    gw = gw.at[:GLA_RANK, :GLA_QK].set(w_f).at[GLA_RANK:2 * GLA_RANK, GLA_QK:].set(w_b)
    return gw.astype(BF16), jnp.concatenate([b_f, b_b]).reshape(1, 2 * GLA_QK)


def kernel(x, c, ctx, c_ctx, ada_w, ada_b, norm1_w, ffn1_w1, ffn1_w3, ffn1_w2, norm2_w, w_in,
           ret_decay_f, ret_decay_b, ret_norm_w, gla_gate_w_f, gla_gate_b_f, gla_gate_w_b, gla_gate_b_b,
           gla_norm_w, w_out, norm3_w, ffn2_w1, ffn2_w3, ffn2_w2, final_norm_w):
    batch, n_tok, d = x.shape
    n_ctx = ctx.shape[1]
    depth = ada_w.shape[0]
    assert depth == 1 and batch + 1 <= SUBLANES
    assert n_tok % FFN_TILE == 0 and n_tok % CHUNK == 0 and n_ctx % CHUNK == 0
    assert (batch * n_ctx) % FFN_TILE == 0

    cvec = jnp.zeros((SUBLANES, d), F32).at[:batch].set(c).at[batch].set(c_ctx)
    m, w1a, w3a, w2a, w_in_b, w_low_b = _modulation(cvec, ada_w[0], ada_b[0],
                                                    (ffn1_w1[0], ffn1_w3[0], ffn1_w2[0]), w_in[0].T)
    m3 = m.reshape(SUBLANES, 1, N_MOD * d)

    gw, gbias = _pack_gate(gla_gate_w_f[0], gla_gate_b_f[0], gla_gate_w_b[0], gla_gate_b_b[0])
    proj = (norm2_w[0], w_in_b, w_low_b, gw, gbias)
    f1 = (norm1_w[0], w1a, w3a, w2a)
    rowtab, coltab = _rope_tables(n_tok)
    dec = jnp.zeros((SUBLANES, RET_W), F32)
    dec = dec.at[0].set(jnp.repeat(ret_decay_f[0], LANE)).at[1].set(jnp.repeat(ret_decay_b[0], LANE))

    tiles_per_seq = n_tok // FFN_TILE
    first = _ffn_call(x.reshape(batch * n_tok, d), m3, tiles_per_seq, *f1, mod_off=0, tm=FFN_TILE,
                      ctx=ctx.reshape(batch * n_ctx, d), ctx_row=batch, proj=proj, rope=(rowtab, coltab),
                      cast=(ffn2_w1[0], ffn2_w3[0], ffn2_w2[0], w_out[0]), name="ffn_in")
    x1, mix_in, (w1b, w3b, w2b, woutb) = first[0], first[1:11], first[11:]
    scan_in = (mix_in[1], mix_in[2], mix_in[5], mix_in[6], mix_in[8], mix_in[9])
    zero = jnp.zeros((batch, RET_HEADS, LANE, LANE), F32)
    nc, nc_ctx = n_tok // CHUNK, n_ctx // CHUNK
    ctx_states = _states(*scan_in, dec, (zero,) * 4, batch, batch * nc, nc_ctx)[4:]
    states = _states(*scan_in, dec, ctx_states, batch, 0, nc)[:4]
    x2 = _mix_call(mix_in, states, dec, ret_norm_w[0], gla_norm_w[0], x1, m3, woutb, batch, nc)
    out = _ffn_call(x2, m3, tiles_per_seq, norm3_w[0], w1b, w3b, w2b, mod_off=6, tm=FFN_TILE,
                    final_w=final_norm_w, name="ffn_out")[0]
    return out.reshape(batch, n_tok, d)
```

```python
import functools

import jax
import jax.numpy as jnp
from jax import lax
from jax.experimental import pallas as pl
from jax.experimental.pallas import tpu as pltpu

F32 = jnp.float32
BF16 = jnp.bfloat16

EPS = 1e-6
LOG2E = 1.4426950408889634
N_MOD = 9
GRID_W = 64
ROPE_BASE = 10000.0
RET_HEADS = 4
RET_DK = 128
RET_DV = 128
GLA_HEADS = 4
GLA_DK = 64
GLA_DV = 128
GLA_RANK = 16
GLA_TAU = 16.0
RET_W = RET_HEADS * RET_DV
GLA_W = GLA_HEADS * GLA_DV
GLA_QK = GLA_HEADS * GLA_DK

LANE = 128
SUBLANES = 8
BF16_ROWS = 16
MOD_STEPS = 8
CHUNK = 256
MIX_CHUNKS = 2
FFN_TILE = 512
ROW_BLOCK = 256
FF_CHUNK = 256
VMEM_LIMIT = 60 * 1024 * 1024

C_RQ, C_RK, C_RV, C_RG = 0, 512, 1024, 1536
C_GQ, C_GK, C_GV, C_GG = 2048, 2304, 2560, 3072
C_LOW = 3584


def _silu(x):
    return x * (1.0 / (1.0 + jnp.exp(-x)))


def _log_sigmoid(z):
    return jnp.minimum(z, 0.0) - jnp.log(1.0 + jnp.exp(-jnp.abs(z)))


def _rms(x, w):
    return x * lax.rsqrt(jnp.mean(x * x, axis=-1, keepdims=True) + EPS) * w


def _dot(a, b):
    return jnp.dot(a, b, preferred_element_type=F32)


def _dot_nt(a, b):
    return lax.dot_general(a, b, (((1,), (1,)), ((), ())), preferred_element_type=F32)


def _dot_tn(a, b):
    return lax.dot_general(a, b, (((0,), (0,)), ((), ())), preferred_element_type=F32)


def _dot_select(sel, x):
    hi = x.astype(BF16)
    lo = (x - hi.astype(F32)).astype(BF16)
    return _dot(sel, hi) + _dot(sel, lo)


def _iota(shape, dim):
    return lax.broadcasted_iota(jnp.int32, shape, dim)


def _resident(shape):
    nd = len(shape)
    return pl.BlockSpec(shape, lambda *_: (0,) * nd, pipeline_mode=pl.Buffered(1))


def _cast_jobs(arrays, n_steps):
    specs, shapes = [], []
    for w in arrays:
        rows = next(r for r in range(BF16_ROWS, w.shape[0] + 1, BF16_ROWS)
                    if w.shape[0] % r == 0 and w.shape[0] // r <= n_steps)
        specs.append(pl.BlockSpec((rows, w.shape[1]),
                                  lambda i, last=w.shape[0] // rows - 1: (jnp.minimum(i, last), 0)))
        shapes.append(jax.ShapeDtypeStruct(w.shape, BF16))
    return specs, shapes


def _run_cast_jobs(src_refs, dst_refs):
    for src_ref, dst_ref in zip(src_refs, dst_refs):
        dst_ref[...] = src_ref[...].astype(BF16)


W_IN_BLOCK = 512


def _mod_kernel(*refs, n_cast):
    c_ref, w_ref, b_ref = refs[:3]
    cast_in, (wt_ref, wt_low_ref) = refs[3:3 + n_cast], refs[3 + n_cast:5 + n_cast]
    o_ref = refs[5 + n_cast]
    cast_out, (win_ref, wlow_ref) = refs[6 + n_cast:6 + 2 * n_cast], refs[6 + 2 * n_cast:]
    cond = _silu(c_ref[...]).astype(BF16)
    o_ref[...] = _dot(cond, w_ref[...].astype(BF16)) + b_ref[...]
    _run_cast_jobs(cast_in, cast_out)
    win_ref[...] = wt_ref[...].T.astype(BF16)
    low = wt_low_ref[...]
    low = jnp.concatenate([low, jnp.zeros((LANE - low.shape[0], low.shape[1]), F32)], axis=0)
    wlow_ref[...] = low.T.astype(BF16)


def _modulation(cvec, ada_w, ada_b, cast, w_in_t):
    d, n = ada_w.shape
    bn = n // MOD_STEPS if n % (MOD_STEPS * LANE) == 0 else d
    steps = n // bn
    assert C_LOW % W_IN_BLOCK == 0 and C_LOW // W_IN_BLOCK <= steps
    n_low = w_in_t.shape[0] - C_LOW
    last = C_LOW // W_IN_BLOCK - 1
    cast_specs, cast_shapes = _cast_jobs(cast, steps)
    return pl.pallas_call(
        functools.partial(_mod_kernel, n_cast=len(cast)),
        grid=(steps,),
        in_specs=[pl.BlockSpec((SUBLANES, d), lambda j: (0, 0)),
                  pl.BlockSpec((d, bn), lambda j: (0, j)),
                  pl.BlockSpec((1, bn), lambda j: (0, j))] + cast_specs
        + [pl.BlockSpec((W_IN_BLOCK, d), lambda j: (jnp.minimum(j, last), 0)),
           pl.BlockSpec((n_low, d), lambda j: (C_LOW // n_low, 0))],
        out_specs=[pl.BlockSpec((SUBLANES, bn), lambda j: (0, j))] + cast_specs
        + [pl.BlockSpec((d, W_IN_BLOCK), lambda j: (0, jnp.minimum(j, last))),
           pl.BlockSpec((d, LANE), lambda j: (0, 0))],
        out_shape=[jax.ShapeDtypeStruct((SUBLANES, n), F32)] + cast_shapes
        + [jax.ShapeDtypeStruct((d, C_LOW), BF16), jax.ShapeDtypeStruct((d, LANE), BF16)],
        compiler_params=pltpu.CompilerParams(dimension_semantics=("arbitrary",),
                                             vmem_limit_bytes=VMEM_LIMIT),
        name="mod",
    )(cvec, ada_w, ada_b.reshape(1, n), *cast, w_in_t, w_in_t)


def _swap32(x):
    lane = _iota(x.shape, 1)
    quarter = RET_DK // 4
    first = (lane & (2 * quarter - 1)) < quarter
    return jnp.where(first, pltpu.roll(x, LANE - quarter, 1), pltpu.roll(x, quarter, 1))


def _ffn_kernel(*refs, mod_off, proj, rope, final, n_main, n_cast, d, f):
    it = iter(refs)
    x_ref = next(it)
    if n_main is not None:
        xc_ref = next(it)
        is_ctx = pl.program_id(0) >= n_main
    m_ref, nw_ref, w1_ref, w3_ref, w2_ref = (next(it) for _ in range(5))
    if proj:
        n2w_ref, win_ref, wlow_ref, gw_ref, gbias_ref = (next(it) for _ in range(5))
        if rope:
            rowtab_ref, coltab_ref = next(it), next(it)
    if final:
        fnw_ref = next(it)
    cast_in = [next(it) for _ in range(n_cast)]
    xo_ref = next(it)
    if proj:
        (rq_ref, rk_ref, rv_ref, rg_ref, gq_ref, gk_ref, gv_ref, gg_ref,
         gf_ref, gb_ref) = (next(it) for _ in range(10))
    cast_out = [next(it) for _ in range(n_cast)]
    u_ref = next(it)
    _run_cast_jobs(cast_in, cast_out)

    def mod(i):
        return m_ref[:, (mod_off + i) * d:(mod_off + i + 1) * d]

    h2s = []
    for rb in range(x_ref.shape[0] // ROW_BLOCK):
        rows = slice(rb * ROW_BLOCK, (rb + 1) * ROW_BLOCK)
        x = x_ref[rows, :]
        if n_main is not None:
            x = jnp.where(is_ctx, xc_ref[rows, :], x)
        h = (_rms(x, nw_ref[...]) * (1.0 + mod(1)) + mod(0)).astype(BF16)
        for k in range(f // FF_CHUNK):
            sl = slice(k * FF_CHUNK, (k + 1) * FF_CHUNK)
            a = _dot(h, w1_ref[:, sl])
            g = _dot(h, w3_ref[:, sl])
            u_ref[rows, sl] = (_silu(a) * g).astype(BF16)
        y = _dot(u_ref[rows, :], w2_ref[...])
        x1 = x + (0.5 * mod(2)) * y

        if final:
            xo_ref[rows, :] = _rms(x1, fnw_ref[...])
        else:
            xo_ref[rows, :] = x1

        if proj:
            h2s.append((_rms(x1, n2w_ref[...]) * (1.0 + mod(4)) + mod(3)).astype(BF16))

    for rb, h2 in enumerate(h2s):
        rows = slice(rb * ROW_BLOCK, (rb + 1) * ROW_BLOCK)

        def p(lo, hi):
            return _dot(h2, win_ref[:, lo:hi])

        if rope:
            by_row = _iota((GRID_W, 2 * LANE), 1) % LANE < LANE // 2
            g0 = rb * (ROW_BLOCK // GRID_W)
            tab = jnp.concatenate(
                [jnp.where(by_row, jnp.broadcast_to(rowtab_ref[g0 + g:g0 + g + 1, :], (GRID_W, 2 * LANE)),
                           coltab_ref[...]) for g in range(ROW_BLOCK // GRID_W)], axis=0)
            cos, sin = tab[:, :LANE], tab[:, LANE:]
            if n_main is not None:
                cos = jnp.where(is_ctx, 1.0, cos)
                sin = jnp.where(is_ctx, 0.0, sin)

        low = _dot(h2, wlow_ref[...]).astype(BF16)
        z = _dot(low, gw_ref[...]) + gbias_ref[...]
        ls = _log_sigmoid(z) * (1.0 / GLA_TAU)
        gf_ref[rows, :] = ls[:, :GLA_QK]
        gb_ref[rows, :] = ls[:, GLA_QK:]
        for base, scale, o_ref in ((C_RQ, RET_DK ** -0.5, rq_ref), (C_RK, 1.0, rk_ref)):
            t = p(base, base + RET_W)
            for hd in range(RET_HEADS):
                th = t[:, hd * LANE:(hd + 1) * LANE] * scale
                if rope:
                    th = th * cos + _swap32(th) * sin
                o_ref[rows, hd * LANE:(hd + 1) * LANE] = th.astype(BF16)
        rg_ref[rows, :] = _silu(p(C_RG, C_RG + RET_W)).astype(BF16)
        gg_ref[rows, :] = _silu(p(C_GG, C_GG + GLA_W)).astype(BF16)
        gq_ref[rows, :] = (p(C_GQ, C_GQ + GLA_QK) * GLA_DK ** -0.5).astype(BF16)
        gk_ref[rows, :] = p(C_GK, C_GK + GLA_QK).astype(BF16)
        rv_ref[rows, :] = p(C_RV, C_RV + RET_W).astype(BF16)
        gv_ref[rows, :] = p(C_GV, C_GV + GLA_W).astype(BF16)


def _ffn_call(x, m3, tiles_per_seq, nw, w1, w3, w2, *, mod_off, tm, ctx=None, ctx_row=None, proj=None,
              rope=None, final_w=None, cast=(), name):
    t, d = x.shape
    f = w1.shape[1]
    n_main = t // tm
    n_tiles = n_main + (0 if ctx is None else ctx.shape[0] // tm)
    t = n_tiles * tm
    main = lambda i: jnp.minimum(i, n_main - 1)
    tile = lambda w: pl.BlockSpec((tm, w), lambda i: (i, 0))
    in_specs = [pl.BlockSpec((tm, d), lambda i: (main(i), 0))]
    args = [x]
    if ctx is None:
        row_of_tile = lambda i: i // tiles_per_seq
    else:
        row_of_tile = lambda i: jnp.where(i >= n_main, ctx_row, i // tiles_per_seq)
        in_specs.append(pl.BlockSpec((tm, d), lambda i: (jnp.maximum(i - n_main, 0), 0)))
        args.append(ctx)
    in_specs += [pl.BlockSpec((None, 1, N_MOD * d), lambda i: (row_of_tile(i), 0, 0)),
                 _resident((1, d)), _resident((d, f)), _resident((d, f)), _resident((f, d))]
    args += [m3, nw.reshape(1, d), w1, w3, w2]
    if proj is not None:
        n2w, win, wlow, gw, gbias = proj
        in_specs += [_resident((1, d))] + [_resident(a.shape) for a in (win, wlow, gw, gbias)]
        args += [n2w.reshape(1, d), win, wlow, gw, gbias]
        if rope is not None:
            rowtab, coltab = rope
            in_specs += [pl.BlockSpec((tm // GRID_W, 2 * LANE), lambda i: (main(i) % tiles_per_seq, 0)),
                         _resident(coltab.shape)]
            args += [rowtab, coltab]
    if final_w is not None:
        in_specs.append(_resident((1, d)))
        args.append(final_w.reshape(1, d))
    out_specs = [tile(d)]
    out_shape = [jax.ShapeDtypeStruct((t, d), F32)]
    if proj is not None:
        for w, dt in ((RET_W, BF16),) * 4 + ((GLA_QK, BF16),) * 2 + ((GLA_W, BF16),) * 2 + ((GLA_QK, F32),) * 2:
            out_specs.append(tile(w))
            out_shape.append(jax.ShapeDtypeStruct((t, w), dt))
    cast_specs, cast_shapes = _cast_jobs(cast, n_main)
    in_specs += cast_specs
    args += list(cast)
    out_specs += cast_specs
    out_shape += cast_shapes
    kern = functools.partial(_ffn_kernel, mod_off=mod_off, proj=proj is not None,
                             rope=rope is not None, final=final_w is not None,
                             n_main=None if ctx is None else n_main, n_cast=len(cast), d=d, f=f)
    return pl.pallas_call(
        kern, grid=(n_tiles,), in_specs=in_specs, out_specs=out_specs, out_shape=out_shape,
        scratch_shapes=[pltpu.VMEM((tm, f), BF16)],
        compiler_params=pltpu.CompilerParams(dimension_semantics=("arbitrary",),
                                             vmem_limit_bytes=VMEM_LIMIT),
        name=name,
    )(*args)


def _head_lanes(shape, hd):
    lane = _iota(shape, len(shape) - 1)
    return (lane >= GLA_DK) if hd % 2 else (lane < GLA_DK)


def _state_kernel(*refs, batch, cps, steps, bcps, bsteps):
    it = iter(refs)
    per_batch = lambda: [next(it) for _ in range(batch)]
    rk_ref, rv_ref, gk_ref, gv_ref, gf_ref, gb_ref = (per_batch() for _ in range(6))
    dec_ref = next(it)
    irf_ref, irb_ref, igf_ref, igb_ref = (next(it) for _ in range(4))
    orf_ref, orb_ref, ogf_ref, ogb_ref = (next(it) for _ in range(4))
    frf_ref, frb_ref, fgf_ref, fgb_ref = (next(it) for _ in range(4))
    srf, srb, sgf, sgb, ub_ret, ub_gla, ub_dec = (next(it) for _ in range(7))
    n = CHUNK
    step = pl.program_id(0)
    lg = _log_sigmoid(dec_ref[...])

    @pl.when(step == 0)
    def _():
        srf[...] = irf_ref[...]
        srb[...] = irb_ref[...]
        sgf[...] = igf_ref[...]
        sgb[...] = igb_ref[...]

    @pl.when(step < steps)
    def _():
        row = _iota((n, LANE), 0).astype(F32)
        ri = _iota((n, n), 0)
        ci = _iota((n, n), 1)
        after = (ci > ri).astype(BF16)
        before = (ci < ri).astype(BF16)
        heads = [slice(hd * LANE, (hd + 1) * LANE) for hd in range(RET_HEADS)]
        to_end = [jnp.exp((n - 1.0 - row) * lg[0:1, sl]).astype(BF16) for sl in heads]
        to_start = [jnp.exp(row * lg[1:2, sl]).astype(BF16) for sl in heads]

        def gla_keys(b, rows, g_ref, tri, edge):
            g = g_ref[b][rows, :]
            e = _dot_select(tri, g)
            tot = e[edge:edge + 1, :] + g[edge:edge + 1, :]
            return gk_ref[b][rows, :] * jnp.exp(e).astype(BF16), jnp.exp(tot)

        def gla(b, rows, keys):
            kd, dec = keys
            upd = []
            for hd in range(GLA_HEADS):
                pr = slice((hd // 2) * LANE, (hd // 2 + 1) * LANE)
                u = _dot_tn(gv_ref[b][rows, hd * LANE:(hd + 1) * LANE], kd[:, pr])
                upd.append(jnp.where(_head_lanes(u.shape, hd), u, 0.0))
            return upd, dec

        for j in range(cps):
            chunk = step * cps + j
            rows = slice(j * n, (j + 1) * n)
            orf_ref[:, j] = srf[...].astype(BF16)
            ogf_ref[:, j] = sgf[...].astype(BF16)
            keys_f = [gla_keys(b, rows, gf_ref, after, 0) for b in range(batch)]
            keys_b = [gla_keys(b, rows, gb_ref, before, n - 1) for b in range(batch)]
            for b in range(batch):
                for hd in range(RET_HEADS):
                    sl = slice(hd * LANE, (hd + 1) * LANE)
                    k = rk_ref[b][rows, sl]
                    v = rv_ref[b][rows, sl]
                    srf[b, hd] = srf[b, hd] * jnp.exp(n * lg[0:1, sl]) + _dot_tn(k * to_end[hd], v)
                    ub_ret[chunk, b, hd] = _dot_tn(k * to_start[hd], v).astype(BF16)
                upd, dec = gla(b, rows, keys_f[b])
                for hd in range(GLA_HEADS):
                    pr = slice((hd // 2) * LANE, (hd // 2 + 1) * LANE)
                    sgf[b, hd] = sgf[b, hd] * dec[:, pr] + upd[hd]
                upd, dec = gla(b, rows, keys_b[b])
                for hd in range(GLA_HEADS):
                    ub_gla[chunk, b, hd] = upd[hd].astype(BF16)
                ub_dec[chunk, b] = jnp.broadcast_to(dec, ub_dec.shape[2:])
        frf_ref[...] = srf[...]
        fgf_ref[...] = sgf[...]

    @pl.when(step >= steps)
    def _():
        for j in range(bcps):
            jb = bcps - 1 - j
            chunk = (steps + bsteps - 1 - step) * bcps + jb
            orb_ref[:, jb] = srb[...].astype(BF16)
            ogb_ref[:, jb] = sgb[...].astype(BF16)
            for b in range(batch):
                for hd in range(RET_HEADS):
                    lgb = lg[1:2, hd * LANE:(hd + 1) * LANE]
                    srb[b, hd] = srb[b, hd] * jnp.exp(n * lgb) + ub_ret[chunk, b, hd].astype(F32)
                for hd in range(GLA_HEADS):
                    pr = slice((hd // 2) * LANE, (hd // 2 + 1) * LANE)
                    sgb[b, hd] = sgb[b, hd] * ub_dec[chunk, b, 0:1, pr] + ub_gla[chunk, b, hd].astype(F32)
        frb_ref[...] = srb[...]
        fgb_ref[...] = sgb[...]


def _states(rk, rv, gk, gv, gf, gb, dec, init, batch, first_chunk, nc):
    cps = 2 if nc % 2 == 0 and first_chunk % 2 == 0 else 1
    bcps = next(c for c in (8, 4, 2, 1) if nc % c == 0)
    steps, bsteps = nc // cps, nc // bcps
    reading = lambda c: jnp.minimum(c, steps - 1)
    scanning = lambda c: bsteps - 1 - jnp.maximum(c - steps, 0)

    def tokens(w):
        return [pl.BlockSpec((cps * CHUNK, w), lambda c, b=b: ((first_chunk + b * nc) // cps + reading(c), 0))
                for b in range(batch)]

    st_shape = (batch, RET_HEADS, LANE, LANE)
    init_spec = pl.BlockSpec(st_shape, lambda c: (0, 0, 0, 0))
    of_spec = pl.BlockSpec((batch, cps, RET_HEADS, LANE, LANE), lambda c: (0, reading(c), 0, 0, 0))
    ob_spec = pl.BlockSpec((batch, bcps, RET_HEADS, LANE, LANE), lambda c: (0, scanning(c), 0, 0, 0))
    per_chunk = jax.ShapeDtypeStruct((batch, nc, RET_HEADS, LANE, LANE), BF16)
    final = jax.ShapeDtypeStruct(st_shape, F32)
    kept = (nc, batch, RET_HEADS, LANE, LANE)
    rep = lambda a: [a] * batch
    return pl.pallas_call(
        functools.partial(_state_kernel, batch=batch, cps=cps, steps=steps, bcps=bcps, bsteps=bsteps),
        grid=(steps + bsteps,),
        in_specs=tokens(RET_W) + tokens(RET_W) + tokens(GLA_QK) + tokens(GLA_W) + tokens(GLA_QK) + tokens(GLA_QK)
        + [pl.BlockSpec(dec.shape, lambda c: (0, 0))] + [init_spec] * 4,
        out_specs=[of_spec, ob_spec, of_spec, ob_spec] + [init_spec] * 4,
        out_shape=[per_chunk] * 4 + [final] * 4,
        scratch_shapes=[pltpu.VMEM(st_shape, F32)] * 4
        + [pltpu.VMEM(kept, BF16), pltpu.VMEM(kept, BF16), pltpu.VMEM((nc, batch, SUBLANES, GLA_QK), F32)],
        compiler_params=pltpu.CompilerParams(dimension_semantics=("arbitrary",),
                                             vmem_limit_bytes=VMEM_LIMIT),
        name="states",
    )(*rep(rk), *rep(rv), *rep(gk), *rep(gv), *rep(gf), *rep(gb), dec, *init)


def _block_row(x, parent, r):
    n, w = x.shape
    if parent == n:
        return jnp.broadcast_to(x[r:r + 1, :], (n, w))
    x3 = x.reshape(n // parent, parent, w)
    return jnp.broadcast_to(x3[:, r:r + 1, :], x3.shape).reshape(n, w)


def _gla_levels(q, k_even, k_odd, gf, gb, bf, bb):
    n = q.shape[0]
    row = _iota(q.shape, 0)

    def factors(u_exp, w_exp):
        ew = jnp.exp2(w_exp).astype(BF16)
        return (q * jnp.exp2(u_exp).astype(BF16), k_even * ew, k_odd * ew)

    levels = []
    s = n // 2
    while s >= 2:
        if s >= 4:
            last_of_first = _block_row(bf, 2 * s, s - 1)
            first_of_second = _block_row(bb, 2 * s, s)
        else:
            upper = (row & 4) != 0
            last_of_first = jnp.where(upper, _block_row(bf, SUBLANES, 5), _block_row(bf, SUBLANES, 1))
            first_of_second = jnp.where(upper, _block_row(bb, SUBLANES, 6), _block_row(bb, SUBLANES, 2))
        df = bf - last_of_first
        db = bb - first_of_second
        levels.append((2 * s, [factors(jnp.minimum(df, db), -jnp.maximum(df, db))]))
        s //= 2
    odd = (row & 1) == 1
    qa = q * jnp.where(odd, jnp.exp2(gf), 2.0).astype(BF16)
    qb = q * jnp.where(odd, 2.0, jnp.exp2(gb)).astype(BF16)
    zero = jnp.zeros_like(k_even)
    ev = lambda x: jnp.where(odd, zero, x)
    od = lambda x: jnp.where(odd, x, zero)
    levels.append((2, [(qa, ev(k_even), ev(k_odd)), (qb, od(k_even), od(k_odd))]))
    return levels


def _mix_kernel(rq_ref, rk_ref, rv_ref, rg_ref, gq_ref, gk_ref, gv_ref, gg_ref, gf_ref, gb_ref,
                srf_ref, srb_ref, sgf_ref, sgb_ref, dec_ref, rnw_ref, gnw_ref,
                x_ref, m_ref, wout_ref, o_ref, mix_ref, decay_ref, qdec_ref, *, d):
    n = CHUNK
    half = n // 2
    ri = _iota((n, n), 0)
    ci = _iota((n, n), 1)
    step = pl.program_id(0)
    cur = step % 2
    code = _iota((half, n), 0) ^ (_iota((half, n), 1) & (half - 1))

    @pl.when(step == 0)
    def _():
        mix_ref[1] = jnp.zeros(mix_ref.shape[1:], BF16)
        lg = _log_sigmoid(dec_ref[...])
        dist = (ri - ci).astype(F32)
        row = _iota((n, LANE), 0).astype(F32)
        for hd in range(RET_HEADS):
            sl = slice(hd * LANE, (hd + 1) * LANE)
            lgf = lg[0:1, sl]
            lgb = lg[1:2, sl]
            decay_ref[hd] = jnp.exp(jnp.where(dist > 0, dist * lgf[:, 0:1],
                                              jnp.where(dist < 0, -dist * lgb[:, 0:1], jnp.log(2.0))))
            qdec_ref[hd] = jnp.exp((row + 1.0) * lgf)
            qdec_ref[RET_HEADS + hd] = jnp.exp((n - row) * lgb)

    def retention(c):
        tok = slice(c * n, (c + 1) * n)
        for hd in range(RET_HEADS):
            sl = slice(hd * LANE, (hd + 1) * LANE)
            q = rq_ref[tok, sl]
            p = (_dot_nt(q, rk_ref[tok, sl]) * decay_ref[hd]).astype(BF16)
            qf = q.astype(F32)
            qs = jnp.concatenate([(qf * qdec_ref[hd]).astype(BF16),
                                  (qf * qdec_ref[RET_HEADS + hd]).astype(BF16)], axis=1)
            st = jnp.concatenate([srf_ref[c, hd], srb_ref[c, hd]], axis=0)
            o = _dot(p, rv_ref[tok, sl]) + _dot(qs, st)
            mu = jnp.mean(o, axis=-1, keepdims=True)
            oc = o - mu
            var = jnp.mean(oc * oc, axis=-1, keepdims=True)
            r = oc * lax.rsqrt(var + EPS) * rnw_ref[:, sl] * rg_ref[tok, sl].astype(F32)
            mix_ref[cur, tok, sl] = r.astype(BF16)

    def gla_factors(c):
        tok = slice(c * n, (c + 1) * n)
        gf = gf_ref[tok, :] * LOG2E
        gb = gb_ref[tok, :] * LOG2E
        bf = _dot_select((ci <= ri).astype(BF16), gf)
        bb = _dot_select((ci >= ri).astype(BF16), gb)
        q = gq_ref[tok, :]
        k = gk_ref[tok, :]
        even_head = (_iota(k.shape, 1) & GLA_DK) == 0
        no_k = jnp.zeros_like(k)
        levels = _gla_levels(q, jnp.where(even_head, k, no_k), jnp.where(even_head, no_k, k), gf, gb, bf, bb)
        return levels, q * jnp.exp2(bf).astype(BF16), q * jnp.exp2(bb).astype(BF16)

    def finish_previous():
        o_ref[...] = x_ref[...] + m_ref[:, 5 * d:6 * d] * _dot(mix_ref[1 - cur], wout_ref[...])

    def gla_outputs(c, levels, qsf, qsb):
        tok = slice(c * n, (c + 1) * n)
        for pair in range(GLA_HEADS // 2):
            pr = slice(pair * LANE, (pair + 1) * LANE)
            near = [None, None]
            far = [None, None]
            for size, blocks in levels:
                for rb in range(2):
                    kb = rb if size < n else 1 - rb
                    rows = slice(rb * half, (rb + 1) * half)
                    keys = slice(kb * half, (kb + 1) * half)
                    u = jnp.concatenate([blk[0][rows, pr] for blk in blocks], axis=1)
                    w = jnp.concatenate([jnp.concatenate([blk[j][keys, pr] for blk in blocks], axis=1)
                                         for j in (1, 2)], axis=0)
                    t = _dot_nt(u, w)
                    if size == n:
                        far[rb] = t
                    else:
                        near[rb] = t if near[rb] is None else jnp.where(code < size, t, near[rb])
            qcat = jnp.concatenate([qsf[:, pr], qsb[:, pr]], axis=1)
            for j in range(2):
                hd = 2 * pair + j
                sl = slice(hd * LANE, (hd + 1) * LANE)
                mine = slice(j * half, (j + 1) * half)
                p = jnp.concatenate([jnp.concatenate([near[0][:, mine], far[0][:, mine]], axis=1),
                                     jnp.concatenate([far[1][:, mine], near[1][:, mine]], axis=1)], axis=0)
                st = jnp.concatenate([sgf_ref[c, hd], sgb_ref[c, hd]], axis=1)
                o = _dot(p.astype(BF16), gv_ref[tok, sl]) + _dot_nt(qcat, st)
                ms = jnp.mean(o * o, axis=-1, keepdims=True)
                r = o * lax.rsqrt(ms + EPS) * gnw_ref[:, sl] * gg_ref[tok, sl].astype(F32)
                mix_ref[cur, tok, RET_W + hd * LANE:RET_W + (hd + 1) * LANE] = r.astype(BF16)

    factors = []
    for c in range(MIX_CHUNKS):
        factors.append(gla_factors(c))
        if c == 0:
            finish_previous()
        retention(c)
    for c in range(MIX_CHUNKS):
        gla_outputs(c, *factors[c])


def _mix_call(mix_in, states, dec, rnw, gnw, x1, m3, wout, batch, nc):
    assert nc % MIX_CHUNKS == 0
    per_seq = nc // MIX_CHUNKS
    n_steps = batch * per_seq
    rows = MIX_CHUNKS * CHUNK
    t, d = n_steps * rows, x1.shape[1]
    mixed = lambda i: jnp.minimum(i, n_steps - 1)
    done = lambda i: jnp.maximum(i - 1, 0)
    tile = lambda w: pl.BlockSpec((rows, w), lambda i: (mixed(i), 0))
    st_spec = pl.BlockSpec((None, MIX_CHUNKS, RET_HEADS, LANE, LANE),
                           lambda i: (mixed(i) // per_seq, mixed(i) % per_seq, 0, 0, 0))
    const = lambda shape: pl.BlockSpec(shape, lambda i: (0,) * len(shape))
    widths = (RET_W,) * 4 + (GLA_QK,) * 2 + (GLA_W,) * 2 + (GLA_QK,) * 2
    return pl.pallas_call(
        functools.partial(_mix_kernel, d=d),
        grid=(n_steps + 1,),
        in_specs=[tile(w) for w in widths] + [st_spec] * 4
        + [const(dec.shape), const((1, RET_W)), const((1, GLA_W)),
           pl.BlockSpec((rows, d), lambda i: (done(i), 0)),
           pl.BlockSpec((None, 1, N_MOD * d), lambda i: (done(i) // per_seq, 0, 0)),
           const(wout.shape)],
        out_specs=pl.BlockSpec((rows, d), lambda i: (done(i), 0)),
        out_shape=jax.ShapeDtypeStruct((t, d), F32),
        scratch_shapes=[pltpu.VMEM((2, rows, RET_W + GLA_W), BF16),
                        pltpu.VMEM((RET_HEADS, CHUNK, CHUNK), F32),
                        pltpu.VMEM((2 * RET_HEADS, CHUNK, LANE), F32)],
        compiler_params=pltpu.CompilerParams(dimension_semantics=("arbitrary",),
                                             vmem_limit_bytes=VMEM_LIMIT),
        name="mix",
    )(*mix_in, *states, dec, rnw.reshape(1, RET_W), gnw.reshape(1, GLA_W), x1, m3, wout)


def _rope_tables(n_tok):
    freqs = ROPE_BASE ** (-jnp.arange(RET_DK // 4, dtype=F32) / (RET_DK // 4))

    def table(n_pos, first_half):
        ang = jnp.arange(n_pos, dtype=F32)[:, None] * freqs
        zero = jnp.zeros((n_pos, LANE // 2), F32)
        cos = jnp.concatenate([jnp.cos(ang)] * 2, axis=-1)
        sin = jnp.concatenate([-jnp.sin(ang), jnp.sin(ang)], axis=-1)
        halves = (cos, zero, sin, zero) if first_half else (zero, cos, zero, sin)
        return jnp.concatenate(halves, axis=-1)

    return table(n_tok // GRID_W, True), table(GRID_W, False)


def _pack_gate(w_f, b_f, w_b, b_b):
    gw = jnp.zeros((LANE, 2 * GLA_QK), F32)
    gw = gw.at[:GLA_RANK, :GLA_QK].set(w_f).at[GLA_RANK:2 * GLA_RANK, GLA_QK:].set(w_b)
    return gw.astype(BF16), jnp.concatenate([b_f, b_b]).reshape(1, 2 * GLA_QK)


def kernel(x, c, ctx, c_ctx, ada_w, ada_b, norm1_w, ffn1_w1, ffn1_w3, ffn1_w2, norm2_w, w_in,
           ret_decay_f, ret_decay_b, ret_norm_w, gla_gate_w_f, gla_gate_b_f, gla_gate_w_b, gla_gate_b_b,
           gla_norm_w, w_out, norm3_w, ffn2_w1, ffn2_w3, ffn2_w2, final_norm_w):
    batch, n_tok, d = x.shape
    n_ctx = ctx.shape[1]
    depth = ada_w.shape[0]
    assert depth == 1 and batch + 1 <= SUBLANES
    assert n_tok % FFN_TILE == 0 and n_tok % CHUNK == 0 and n_ctx % CHUNK == 0
    assert (batch * n_ctx) % FFN_TILE == 0

    cvec = jnp.zeros((SUBLANES, d), F32).at[:batch].set(c).at[batch].set(c_ctx)
    m, w1a, w3a, w2a, w_in_b, w_low_b = _modulation(cvec, ada_w[0], ada_b[0],
                                                    (ffn1_w1[0], ffn1_w3[0], ffn1_w2[0]), w_in[0].T)
    m3 = m.reshape(SUBLANES, 1, N_MOD * d)

    gw, gbias = _pack_gate(gla_gate_w_f[0], gla_gate_b_f[0], gla_gate_w_b[0], gla_gate_b_b[0])
    proj = (norm2_w[0], w_in_b, w_low_b, gw, gbias)
    f1 = (norm1_w[0], w1a, w3a, w2a)
    rowtab, coltab = _rope_tables(n_tok)
    dec = jnp.zeros((SUBLANES, RET_W), F32)
    dec = dec.at[0].set(jnp.repeat(ret_decay_f[0], LANE)).at[1].set(jnp.repeat(ret_decay_b[0], LANE))

    tiles_per_seq = n_tok // FFN_TILE
    first = _ffn_call(x.reshape(batch * n_tok, d), m3, tiles_per_seq, *f1, mod_off=0, tm=FFN_TILE,
                      ctx=ctx.reshape(batch * n_ctx, d), ctx_row=batch, proj=proj, rope=(rowtab, coltab),
                      cast=(ffn2_w1[0], ffn2_w3[0], ffn2_w2[0], w_out[0]), name="ffn_in")
    x1, mix_in, (w1b, w3b, w2b, woutb) = first[0], first[1:11], first[11:]
    scan_in = (mix_in[1], mix_in[2], mix_in[5], mix_in[6], mix_in[8], mix_in[9])
    zero = jnp.zeros((batch, RET_HEADS, LANE, LANE), F32)
    nc, nc_ctx = n_tok // CHUNK, n_ctx // CHUNK
    ctx_states = _states(*scan_in, dec, (zero,) * 4, batch, batch * nc, nc_ctx)[4:]
    states = _states(*scan_in, dec, ctx_states, batch, 0, nc)[:4]
    x2 = _mix_call(mix_in, states, dec, ret_norm_w[0], gla_norm_w[0], x1, m3, woutb, batch, nc)
    out = _ffn_call(x2, m3, tiles_per_seq, norm3_w[0], w1b, w3b, w2b, mod_off=6, tm=FFN_TILE,
                    final_w=final_norm_w, name="ffn_out")[0]
    return out.reshape(batch, n_tok, d)
```

```python
import functools

import jax
import jax.numpy as jnp
from jax import lax
from jax.experimental import pallas as pl
from jax.experimental.pallas import tpu as pltpu

F32 = jnp.float32
BF16 = jnp.bfloat16

EPS = 1e-6
LOG2E = 1.4426950408889634
N_MOD = 9
GRID_W = 64
ROPE_BASE = 10000.0
RET_HEADS = 4
RET_DK = 128
RET_DV = 128
GLA_HEADS = 4
GLA_DK = 64
GLA_DV = 128
GLA_RANK = 16
GLA_TAU = 16.0
RET_W = RET_HEADS * RET_DV
GLA_W = GLA_HEADS * GLA_DV
GLA_QK = GLA_HEADS * GLA_DK

LANE = 128
SUBLANES = 8
BF16_ROWS = 16
MOD_STEPS = 8
CHUNK = 256
MIX_CHUNKS = 4
FFN_TILE = 512
ROW_BLOCK = 256
FF_CHUNK = 256
VMEM_LIMIT = 60 * 1024 * 1024

C_RQ, C_RK, C_RV, C_RG = 0, 512, 1024, 1536
C_GQ, C_GK, C_GV, C_GG = 2048, 2304, 2560, 3072
C_LOW = 3584


def _silu(x):
    return x * (1.0 / (1.0 + jnp.exp(-x)))


def _log_sigmoid(z):
    return jnp.minimum(z, 0.0) - jnp.log(1.0 + jnp.exp(-jnp.abs(z)))


def _rms(x, w):
    return x * lax.rsqrt(jnp.mean(x * x, axis=-1, keepdims=True) + EPS) * w


def _dot(a, b):
    return jnp.dot(a, b, preferred_element_type=F32)


def _dot_nt(a, b):
    return lax.dot_general(a, b, (((1,), (1,)), ((), ())), preferred_element_type=F32)


def _dot_tn(a, b):
    return lax.dot_general(a, b, (((0,), (0,)), ((), ())), preferred_element_type=F32)


def _dot_select(sel, x):
    hi = x.astype(BF16)
    lo = (x - hi.astype(F32)).astype(BF16)
    return _dot(sel, hi) + _dot(sel, lo)


def _iota(shape, dim):
    return lax.broadcasted_iota(jnp.int32, shape, dim)


def _resident(shape):
    nd = len(shape)
    return pl.BlockSpec(shape, lambda *_: (0,) * nd, pipeline_mode=pl.Buffered(1))


def _cast_jobs(arrays, n_steps):
    specs, shapes = [], []
    for w in arrays:
        rows = next(r for r in range(BF16_ROWS, w.shape[0] + 1, BF16_ROWS)
                    if w.shape[0] % r == 0 and w.shape[0] // r <= n_steps)
        specs.append(pl.BlockSpec((rows, w.shape[1]),
                                  lambda i, last=w.shape[0] // rows - 1: (jnp.minimum(i, last), 0)))
        shapes.append(jax.ShapeDtypeStruct(w.shape, BF16))
    return specs, shapes


def _run_cast_jobs(src_refs, dst_refs):
    for src_ref, dst_ref in zip(src_refs, dst_refs):
        dst_ref[...] = src_ref[...].astype(BF16)


W_IN_BLOCK = 512


def _mod_kernel(*refs, n_cast):
    c_ref, w_ref, b_ref = refs[:3]
    cast_in, (wt_ref, wt_low_ref) = refs[3:3 + n_cast], refs[3 + n_cast:5 + n_cast]
    o_ref = refs[5 + n_cast]
    cast_out, (win_ref, wlow_ref) = refs[6 + n_cast:6 + 2 * n_cast], refs[6 + 2 * n_cast:]
    cond = _silu(c_ref[...]).astype(BF16)
    o_ref[...] = _dot(cond, w_ref[...].astype(BF16)) + b_ref[...]
    _run_cast_jobs(cast_in, cast_out)
    win_ref[...] = wt_ref[...].T.astype(BF16)
    low = wt_low_ref[...]
    low = jnp.concatenate([low, jnp.zeros((LANE - low.shape[0], low.shape[1]), F32)], axis=0)
    wlow_ref[...] = low.T.astype(BF16)


def _modulation(cvec, ada_w, ada_b, cast, w_in_t):
    d, n = ada_w.shape
    bn = n // MOD_STEPS if n % (MOD_STEPS * LANE) == 0 else d
    steps = n // bn
    assert C_LOW % W_IN_BLOCK == 0 and C_LOW // W_IN_BLOCK <= steps
    n_low = w_in_t.shape[0] - C_LOW
    last = C_LOW // W_IN_BLOCK - 1
    cast_specs, cast_shapes = _cast_jobs(cast, steps)
    return pl.pallas_call(
        functools.partial(_mod_kernel, n_cast=len(cast)),
        grid=(steps,),
        in_specs=[pl.BlockSpec((SUBLANES, d), lambda j: (0, 0)),
                  pl.BlockSpec((d, bn), lambda j: (0, j)),
                  pl.BlockSpec((1, bn), lambda j: (0, j))] + cast_specs
        + [pl.BlockSpec((W_IN_BLOCK, d), lambda j: (jnp.minimum(j, last), 0)),
           pl.BlockSpec((n_low, d), lambda j: (C_LOW // n_low, 0))],
        out_specs=[pl.BlockSpec((SUBLANES, bn), lambda j: (0, j))] + cast_specs
        + [pl.BlockSpec((d, W_IN_BLOCK), lambda j: (0, jnp.minimum(j, last))),
           pl.BlockSpec((d, LANE), lambda j: (0, 0))],
        out_shape=[jax.ShapeDtypeStruct((SUBLANES, n), F32)] + cast_shapes
        + [jax.ShapeDtypeStruct((d, C_LOW), BF16), jax.ShapeDtypeStruct((d, LANE), BF16)],
        compiler_params=pltpu.CompilerParams(dimension_semantics=("arbitrary",),
                                             vmem_limit_bytes=VMEM_LIMIT),
        name="mod",
    )(cvec, ada_w, ada_b.reshape(1, n), *cast, w_in_t, w_in_t)


def _swap32(x):
    lane = _iota(x.shape, 1)
    quarter = RET_DK // 4
    first = (lane & (2 * quarter - 1)) < quarter
    return jnp.where(first, pltpu.roll(x, LANE - quarter, 1), pltpu.roll(x, quarter, 1))


def _ffn_kernel(*refs, mod_off, proj, rope, final, n_main, n_cast, d, f):
    it = iter(refs)
    x_ref = next(it)
    if n_main is not None:
        xc_ref = next(it)
        is_ctx = pl.program_id(0) >= n_main
    m_ref, nw_ref, w1_ref, w3_ref, w2_ref = (next(it) for _ in range(5))
    if proj:
        n2w_ref, win_ref, wlow_ref, gw_ref, gbias_ref = (next(it) for _ in range(5))
        if rope:
            rowtab_ref, coltab_ref = next(it), next(it)
    if final:
        fnw_ref = next(it)
    cast_in = [next(it) for _ in range(n_cast)]
    xo_ref = next(it)
    if proj:
        (rq_ref, rk_ref, rv_ref, rg_ref, gq_ref, gk_ref, gv_ref, gg_ref,
         gf_ref, gb_ref) = (next(it) for _ in range(10))
    cast_out = [next(it) for _ in range(n_cast)]
    u_ref = next(it)
    _run_cast_jobs(cast_in, cast_out)

    def mod(i):
        return m_ref[:, (mod_off + i) * d:(mod_off + i + 1) * d]

    h2s = []
    for rb in range(x_ref.shape[0] // ROW_BLOCK):
        rows = slice(rb * ROW_BLOCK, (rb + 1) * ROW_BLOCK)
        x = x_ref[rows, :]
        if n_main is not None:
            x = jnp.where(is_ctx, xc_ref[rows, :], x)
        h = (_rms(x, nw_ref[...]) * (1.0 + mod(1)) + mod(0)).astype(BF16)
        for k in range(f // FF_CHUNK):
            sl = slice(k * FF_CHUNK, (k + 1) * FF_CHUNK)
            a = _dot(h, w1_ref[:, sl])
            g = _dot(h, w3_ref[:, sl])
            u_ref[rows, sl] = (_silu(a) * g).astype(BF16)
        y = _dot(u_ref[rows, :], w2_ref[...])
        x1 = x + (0.5 * mod(2)) * y

        if final:
            xo_ref[rows, :] = _rms(x1, fnw_ref[...])
        else:
            xo_ref[rows, :] = x1

        if proj:
            h2s.append((_rms(x1, n2w_ref[...]) * (1.0 + mod(4)) + mod(3)).astype(BF16))

    for rb, h2 in enumerate(h2s):
        rows = slice(rb * ROW_BLOCK, (rb + 1) * ROW_BLOCK)

        def p(lo, hi):
            return _dot(h2, win_ref[:, lo:hi])

        if rope:
            by_row = _iota((GRID_W, 2 * LANE), 1) % LANE < LANE // 2
            g0 = rb * (ROW_BLOCK // GRID_W)
            tab = jnp.concatenate(
                [jnp.where(by_row, jnp.broadcast_to(rowtab_ref[g0 + g:g0 + g + 1, :], (GRID_W, 2 * LANE)),
                           coltab_ref[...]) for g in range(ROW_BLOCK // GRID_W)], axis=0)
            cos, sin = tab[:, :LANE], tab[:, LANE:]
            if n_main is not None:
                cos = jnp.where(is_ctx, 1.0, cos)
                sin = jnp.where(is_ctx, 0.0, sin)

        low = _dot(h2, wlow_ref[...]).astype(BF16)
        z = _dot(low, gw_ref[...]) + gbias_ref[...]
        ls = _log_sigmoid(z) * (1.0 / GLA_TAU)
        gf_ref[rows, :] = ls[:, :GLA_QK]
        gb_ref[rows, :] = ls[:, GLA_QK:]
        for base, scale, o_ref in ((C_RQ, RET_DK ** -0.5, rq_ref), (C_RK, 1.0, rk_ref)):
            t = p(base, base + RET_W)
            for hd in range(RET_HEADS):
                th = t[:, hd * LANE:(hd + 1) * LANE] * scale
                if rope:
                    th = th * cos + _swap32(th) * sin
                o_ref[rows, hd * LANE:(hd + 1) * LANE] = th.astype(BF16)
        rg_ref[rows, :] = _silu(p(C_RG, C_RG + RET_W)).astype(BF16)
        gg_ref[rows, :] = _silu(p(C_GG, C_GG + GLA_W)).astype(BF16)
        gq_ref[rows, :] = (p(C_GQ, C_GQ + GLA_QK) * GLA_DK ** -0.5).astype(BF16)
        gk_ref[rows, :] = p(C_GK, C_GK + GLA_QK).astype(BF16)
        rv_ref[rows, :] = p(C_RV, C_RV + RET_W).astype(BF16)
        gv_ref[rows, :] = p(C_GV, C_GV + GLA_W).astype(BF16)


def _ffn_call(x, m3, tiles_per_seq, nw, w1, w3, w2, *, mod_off, tm, ctx=None, ctx_row=None, proj=None,
              rope=None, final_w=None, cast=(), name):
    t, d = x.shape
    f = w1.shape[1]
    n_main = t // tm
    n_tiles = n_main + (0 if ctx is None else ctx.shape[0] // tm)
    t = n_tiles * tm
    main = lambda i: jnp.minimum(i, n_main - 1)
    tile = lambda w: pl.BlockSpec((tm, w), lambda i: (i, 0))
    in_specs = [pl.BlockSpec((tm, d), lambda i: (main(i), 0))]
    args = [x]
    if ctx is None:
        row_of_tile = lambda i: i // tiles_per_seq
    else:
        row_of_tile = lambda i: jnp.where(i >= n_main, ctx_row, i // tiles_per_seq)
        in_specs.append(pl.BlockSpec((tm, d), lambda i: (jnp.maximum(i - n_main, 0), 0)))
        args.append(ctx)
    in_specs += [pl.BlockSpec((None, 1, N_MOD * d), lambda i: (row_of_tile(i), 0, 0)),
                 _resident((1, d)), _resident((d, f)), _resident((d, f)), _resident((f, d))]
    args += [m3, nw.reshape(1, d), w1, w3, w2]
    if proj is not None:
        n2w, win, wlow, gw, gbias = proj
        in_specs += [_resident((1, d))] + [_resident(a.shape) for a in (win, wlow, gw, gbias)]
        args += [n2w.reshape(1, d), win, wlow, gw, gbias]
        if rope is not None:
            rowtab, coltab = rope
            in_specs += [pl.BlockSpec((tm // GRID_W, 2 * LANE), lambda i: (main(i) % tiles_per_seq, 0)),
                         _resident(coltab.shape)]
            args += [rowtab, coltab]
    if final_w is not None:
        in_specs.append(_resident((1, d)))
        args.append(final_w.reshape(1, d))
    out_specs = [tile(d)]
    out_shape = [jax.ShapeDtypeStruct((t, d), F32)]
    if proj is not None:
        for w, dt in ((RET_W, BF16),) * 4 + ((GLA_QK, BF16),) * 2 + ((GLA_W, BF16),) * 2 + ((GLA_QK, F32),) * 2:
            out_specs.append(tile(w))
            out_shape.append(jax.ShapeDtypeStruct((t, w), dt))
    cast_specs, cast_shapes = _cast_jobs(cast, n_main)
    in_specs += cast_specs
    args += list(cast)
    out_specs += cast_specs
    out_shape += cast_shapes
    kern = functools.partial(_ffn_kernel, mod_off=mod_off, proj=proj is not None,
                             rope=rope is not None, final=final_w is not None,
                             n_main=None if ctx is None else n_main, n_cast=len(cast), d=d, f=f)
    return pl.pallas_call(
        kern, grid=(n_tiles,), in_specs=in_specs, out_specs=out_specs, out_shape=out_shape,
        scratch_shapes=[pltpu.VMEM((tm, f), BF16)],
        compiler_params=pltpu.CompilerParams(dimension_semantics=("arbitrary",),
                                             vmem_limit_bytes=VMEM_LIMIT),
        name=name,
    )(*args)


def _head_lanes(shape, hd):
    lane = _iota(shape, len(shape) - 1)
    return (lane >= GLA_DK) if hd % 2 else (lane < GLA_DK)


def _state_kernel(*refs, batch, cps, steps, bcps, bsteps):
    it = iter(refs)
    per_batch = lambda: [next(it) for _ in range(batch)]
    rk_ref, rv_ref, gk_ref, gv_ref, gf_ref, gb_ref = (per_batch() for _ in range(6))
    dec_ref = next(it)
    irf_ref, irb_ref, igf_ref, igb_ref = (next(it) for _ in range(4))
    orf_ref, orb_ref, ogf_ref, ogb_ref = (next(it) for _ in range(4))
    frf_ref, frb_ref, fgf_ref, fgb_ref = (next(it) for _ in range(4))
    srf, srb, sgf, sgb, ub_ret, ub_gla, ub_dec = (next(it) for _ in range(7))
    n = CHUNK
    step = pl.program_id(0)
    lg = _log_sigmoid(dec_ref[...])

    @pl.when(step == 0)
    def _():
        srf[...] = irf_ref[...]
        srb[...] = irb_ref[...]
        sgf[...] = igf_ref[...]
        sgb[...] = igb_ref[...]

    @pl.when(step < steps)
    def _():
        row = _iota((n, LANE), 0).astype(F32)
        ri = _iota((n, n), 0)
        ci = _iota((n, n), 1)
        after = (ci > ri).astype(BF16)
        before = (ci < ri).astype(BF16)
        heads = [slice(hd * LANE, (hd + 1) * LANE) for hd in range(RET_HEADS)]
        to_end = [jnp.exp((n - 1.0 - row) * lg[0:1, sl]).astype(BF16) for sl in heads]
        to_start = [jnp.exp(row * lg[1:2, sl]).astype(BF16) for sl in heads]

        def gla_keys(b, rows, g_ref, tri, edge):
            g = g_ref[b][rows, :]
            e = _dot_select(tri, g)
            tot = e[edge:edge + 1, :] + g[edge:edge + 1, :]
            return gk_ref[b][rows, :] * jnp.exp(e).astype(BF16), jnp.exp(tot)

        def gla(b, rows, keys):
            kd, dec = keys
            upd = []
            for hd in range(GLA_HEADS):
                pr = slice((hd // 2) * LANE, (hd // 2 + 1) * LANE)
                u = _dot_tn(gv_ref[b][rows, hd * LANE:(hd + 1) * LANE], kd[:, pr])
                upd.append(jnp.where(_head_lanes(u.shape, hd), u, 0.0))
            return upd, dec

        for j in range(cps):
            chunk = step * cps + j
            rows = slice(j * n, (j + 1) * n)
            orf_ref[:, j] = srf[...].astype(BF16)
            ogf_ref[:, j] = sgf[...].astype(BF16)
            keys_f = [gla_keys(b, rows, gf_ref, after, 0) for b in range(batch)]
            keys_b = [gla_keys(b, rows, gb_ref, before, n - 1) for b in range(batch)]
            for b in range(batch):
                for hd in range(RET_HEADS):
                    sl = slice(hd * LANE, (hd + 1) * LANE)
                    k = rk_ref[b][rows, sl]
                    v = rv_ref[b][rows, sl]
                    srf[b, hd] = srf[b, hd] * jnp.exp(n * lg[0:1, sl]) + _dot_tn(k * to_end[hd], v)
                    ub_ret[chunk, b, hd] = _dot_tn(k * to_start[hd], v).astype(BF16)
                upd, dec = gla(b, rows, keys_f[b])
                for hd in range(GLA_HEADS):
                    pr = slice((hd // 2) * LANE, (hd // 2 + 1) * LANE)
                    sgf[b, hd] = sgf[b, hd] * dec[:, pr] + upd[hd]
                upd, dec = gla(b, rows, keys_b[b])
                for hd in range(GLA_HEADS):
                    ub_gla[chunk, b, hd] = upd[hd].astype(BF16)
                ub_dec[chunk, b] = jnp.broadcast_to(dec, ub_dec.shape[2:])
        frf_ref[...] = srf[...]
        fgf_ref[...] = sgf[...]

    @pl.when(step >= steps)
    def _():
        for j in range(bcps):
            jb = bcps - 1 - j
            chunk = (steps + bsteps - 1 - step) * bcps + jb
            orb_ref[:, jb] = srb[...].astype(BF16)
            ogb_ref[:, jb] = sgb[...].astype(BF16)
            for b in range(batch):
                for hd in range(RET_HEADS):
                    lgb = lg[1:2, hd * LANE:(hd + 1) * LANE]
                    srb[b, hd] = srb[b, hd] * jnp.exp(n * lgb) + ub_ret[chunk, b, hd].astype(F32)
                for hd in range(GLA_HEADS):
                    pr = slice((hd // 2) * LANE, (hd // 2 + 1) * LANE)
                    sgb[b, hd] = sgb[b, hd] * ub_dec[chunk, b, 0:1, pr] + ub_gla[chunk, b, hd].astype(F32)
        frb_ref[...] = srb[...]
        fgb_ref[...] = sgb[...]


def _states(rk, rv, gk, gv, gf, gb, dec, init, batch, first_chunk, nc):
    cps = 2 if nc % 2 == 0 and first_chunk % 2 == 0 else 1
    bcps = next(c for c in (8, 4, 2, 1) if nc % c == 0)
    steps, bsteps = nc // cps, nc // bcps
    reading = lambda c: jnp.minimum(c, steps - 1)
    scanning = lambda c: bsteps - 1 - jnp.maximum(c - steps, 0)

    def tokens(w):
        return [pl.BlockSpec((cps * CHUNK, w), lambda c, b=b: ((first_chunk + b * nc) // cps + reading(c), 0))
                for b in range(batch)]

    st_shape = (batch, RET_HEADS, LANE, LANE)
    init_spec = pl.BlockSpec(st_shape, lambda c: (0, 0, 0, 0))
    of_spec = pl.BlockSpec((batch, cps, RET_HEADS, LANE, LANE), lambda c: (0, reading(c), 0, 0, 0))
    ob_spec = pl.BlockSpec((batch, bcps, RET_HEADS, LANE, LANE), lambda c: (0, scanning(c), 0, 0, 0))
    per_chunk = jax.ShapeDtypeStruct((batch, nc, RET_HEADS, LANE, LANE), BF16)
    final = jax.ShapeDtypeStruct(st_shape, F32)
    kept = (nc, batch, RET_HEADS, LANE, LANE)
    rep = lambda a: [a] * batch
    return pl.pallas_call(
        functools.partial(_state_kernel, batch=batch, cps=cps, steps=steps, bcps=bcps, bsteps=bsteps),
        grid=(steps + bsteps,),
        in_specs=tokens(RET_W) + tokens(RET_W) + tokens(GLA_QK) + tokens(GLA_W) + tokens(GLA_QK) + tokens(GLA_QK)
        + [pl.BlockSpec(dec.shape, lambda c: (0, 0))] + [init_spec] * 4,
        out_specs=[of_spec, ob_spec, of_spec, ob_spec] + [init_spec] * 4,
        out_shape=[per_chunk] * 4 + [final] * 4,
        scratch_shapes=[pltpu.VMEM(st_shape, F32)] * 4
        + [pltpu.VMEM(kept, BF16), pltpu.VMEM(kept, BF16), pltpu.VMEM((nc, batch, SUBLANES, GLA_QK), F32)],
        compiler_params=pltpu.CompilerParams(dimension_semantics=("arbitrary",),
                                             vmem_limit_bytes=VMEM_LIMIT),
        name="states",
    )(*rep(rk), *rep(rv), *rep(gk), *rep(gv), *rep(gf), *rep(gb), dec, *init)


def _block_row(x, parent, r):
    n, w = x.shape
    if parent == n:
        return jnp.broadcast_to(x[r:r + 1, :], (n, w))
    x3 = x.reshape(n // parent, parent, w)
    return jnp.broadcast_to(x3[:, r:r + 1, :], x3.shape).reshape(n, w)


def _gla_levels(q, k_even, k_odd, gf, gb, bf, bb):
    n = q.shape[0]
    row = _iota(q.shape, 0)

    def factors(u_exp, w_exp):
        ew = jnp.exp2(w_exp).astype(BF16)
        return (q * jnp.exp2(u_exp).astype(BF16), k_even * ew, k_odd * ew)

    levels = []
    s = n // 2
    while s >= 2:
        if s >= 4:
            last_of_first = _block_row(bf, 2 * s, s - 1)
            first_of_second = _block_row(bb, 2 * s, s)
        else:
            upper = (row & 4) != 0
            last_of_first = jnp.where(upper, _block_row(bf, SUBLANES, 5), _block_row(bf, SUBLANES, 1))
            first_of_second = jnp.where(upper, _block_row(bb, SUBLANES, 6), _block_row(bb, SUBLANES, 2))
        df = bf - last_of_first
        db = bb - first_of_second
        levels.append((2 * s, [factors(jnp.minimum(df, db), -jnp.maximum(df, db))]))
        s //= 2
    odd = (row & 1) == 1
    qa = q * jnp.where(odd, jnp.exp2(gf), 2.0).astype(BF16)
    qb = q * jnp.where(odd, 2.0, jnp.exp2(gb)).astype(BF16)
    zero = jnp.zeros_like(k_even)
    ev = lambda x: jnp.where(odd, zero, x)
    od = lambda x: jnp.where(odd, x, zero)
    levels.append((2, [(qa, ev(k_even), ev(k_odd)), (qb, od(k_even), od(k_odd))]))
    return levels


def _mix_kernel(rq_ref, rk_ref, rv_ref, rg_ref, gq_ref, gk_ref, gv_ref, gg_ref, gf_ref, gb_ref,
                srf_ref, srb_ref, sgf_ref, sgb_ref, dec_ref, rnw_ref, gnw_ref,
                x_ref, m_ref, wout_ref, o_ref, mix_ref, decay_ref, qdec_ref, *, d):
    n = CHUNK
    half = n // 2
    ri = _iota((n, n), 0)
    ci = _iota((n, n), 1)
    step = pl.program_id(0)
    cur = step % 2
    code = _iota((half, n), 0) ^ (_iota((half, n), 1) & (half - 1))

    @pl.when(step == 0)
    def _():
        mix_ref[1] = jnp.zeros(mix_ref.shape[1:], BF16)
        lg = _log_sigmoid(dec_ref[...])
        dist = (ri - ci).astype(F32)
        row = _iota((n, LANE), 0).astype(F32)
        for hd in range(RET_HEADS):
            sl = slice(hd * LANE, (hd + 1) * LANE)
            lgf = lg[0:1, sl]
            lgb = lg[1:2, sl]
            decay_ref[hd] = jnp.exp(jnp.where(dist > 0, dist * lgf[:, 0:1],
                                              jnp.where(dist < 0, -dist * lgb[:, 0:1], jnp.log(2.0))))
            qdec_ref[hd] = jnp.exp((row + 1.0) * lgf)
            qdec_ref[RET_HEADS + hd] = jnp.exp((n - row) * lgb)

    def retention(c):
        tok = slice(c * n, (c + 1) * n)
        for hd in range(RET_HEADS):
            sl = slice(hd * LANE, (hd + 1) * LANE)
            q = rq_ref[tok, sl]
            p = (_dot_nt(q, rk_ref[tok, sl]) * decay_ref[hd]).astype(BF16)
            qf = q.astype(F32)
            qs = jnp.concatenate([(qf * qdec_ref[hd]).astype(BF16),
                                  (qf * qdec_ref[RET_HEADS + hd]).astype(BF16)], axis=1)
            st = jnp.concatenate([srf_ref[c, hd], srb_ref[c, hd]], axis=0)
            o = _dot(p, rv_ref[tok, sl]) + _dot(qs, st)
            mu = jnp.mean(o, axis=-1, keepdims=True)
            oc = o - mu
            var = jnp.mean(oc * oc, axis=-1, keepdims=True)
            r = oc * lax.rsqrt(var + EPS) * rnw_ref[:, sl] * rg_ref[tok, sl].astype(F32)
            mix_ref[cur, tok, sl] = r.astype(BF16)

    def gla_factors(c):
        tok = slice(c * n, (c + 1) * n)
        gf = gf_ref[tok, :] * LOG2E
        gb = gb_ref[tok, :] * LOG2E
        bf = _dot_select((ci <= ri).astype(BF16), gf)
        bb = _dot_select((ci >= ri).astype(BF16), gb)
        q = gq_ref[tok, :]
        k = gk_ref[tok, :]
        even_head = (_iota(k.shape, 1) & GLA_DK) == 0
        no_k = jnp.zeros_like(k)
        levels = _gla_levels(q, jnp.where(even_head, k, no_k), jnp.where(even_head, no_k, k), gf, gb, bf, bb)
        return levels, q * jnp.exp2(bf).astype(BF16), q * jnp.exp2(bb).astype(BF16)

    def finish_previous():
        o_ref[...] = x_ref[...] + m_ref[:, 5 * d:6 * d] * _dot(mix_ref[1 - cur], wout_ref[...])

    def gla_outputs(c, levels, qsf, qsb):
        tok = slice(c * n, (c + 1) * n)
        for pair in range(GLA_HEADS // 2):
            pr = slice(pair * LANE, (pair + 1) * LANE)
            near = [None, None]
            far = [None, None]
            for size, blocks in levels:
                for rb in range(2):
                    kb = rb if size < n else 1 - rb
                    rows = slice(rb * half, (rb + 1) * half)
                    keys = slice(kb * half, (kb + 1) * half)
                    u = jnp.concatenate([blk[0][rows, pr] for blk in blocks], axis=1)
                    w = jnp.concatenate([jnp.concatenate([blk[j][keys, pr] for blk in blocks], axis=1)
                                         for j in (1, 2)], axis=0)
                    t = _dot_nt(u, w)
                    if size == n:
                        far[rb] = t
                    else:
                        near[rb] = t if near[rb] is None else jnp.where(code < size, t, near[rb])
            qcat = jnp.concatenate([qsf[:, pr], qsb[:, pr]], axis=1)
            for j in range(2):
                hd = 2 * pair + j
                sl = slice(hd * LANE, (hd + 1) * LANE)
                mine = slice(j * half, (j + 1) * half)
                p = jnp.concatenate([jnp.concatenate([near[0][:, mine], far[0][:, mine]], axis=1),
                                     jnp.concatenate([far[1][:, mine], near[1][:, mine]], axis=1)], axis=0)
                st = jnp.concatenate([sgf_ref[c, hd], sgb_ref[c, hd]], axis=1)
                o = _dot(p.astype(BF16), gv_ref[tok, sl]) + _dot_nt(qcat, st)
                ms = jnp.mean(o * o, axis=-1, keepdims=True)
                r = o * lax.rsqrt(ms + EPS) * gnw_ref[:, sl] * gg_ref[tok, sl].astype(F32)
                mix_ref[cur, tok, RET_W + hd * LANE:RET_W + (hd + 1) * LANE] = r.astype(BF16)

    factors = []
    for c in range(MIX_CHUNKS):
        factors.append(gla_factors(c))
        if c == 0:
            finish_previous()
        retention(c)
    for c in range(MIX_CHUNKS):
        gla_outputs(c, *factors[c])


def _mix_call(mix_in, states, dec, rnw, gnw, x1, m3, wout, batch, nc):
    assert nc % MIX_CHUNKS == 0
    per_seq = nc // MIX_CHUNKS
    n_steps = batch * per_seq
    rows = MIX_CHUNKS * CHUNK
    t, d = n_steps * rows, x1.shape[1]
    mixed = lambda i: jnp.minimum(i, n_steps - 1)
    done = lambda i: jnp.maximum(i - 1, 0)
    tile = lambda w: pl.BlockSpec((rows, w), lambda i: (mixed(i), 0))
    st_spec = pl.BlockSpec((None, MIX_CHUNKS, RET_HEADS, LANE, LANE),
                           lambda i: (mixed(i) // per_seq, mixed(i) % per_seq, 0, 0, 0))
    const = lambda shape: pl.BlockSpec(shape, lambda i: (0,) * len(shape))
    widths = (RET_W,) * 4 + (GLA_QK,) * 2 + (GLA_W,) * 2 + (GLA_QK,) * 2
    return pl.pallas_call(
        functools.partial(_mix_kernel, d=d),
        grid=(n_steps + 1,),
        in_specs=[tile(w) for w in widths] + [st_spec] * 4
        + [const(dec.shape), const((1, RET_W)), const((1, GLA_W)),
           pl.BlockSpec((rows, d), lambda i: (done(i), 0)),
           pl.BlockSpec((None, 1, N_MOD * d), lambda i: (done(i) // per_seq, 0, 0)),
           const(wout.shape)],
        out_specs=pl.BlockSpec((rows, d), lambda i: (done(i), 0)),
        out_shape=jax.ShapeDtypeStruct((t, d), F32),
        scratch_shapes=[pltpu.VMEM((2, rows, RET_W + GLA_W), BF16),
                        pltpu.VMEM((RET_HEADS, CHUNK, CHUNK), F32),
                        pltpu.VMEM((2 * RET_HEADS, CHUNK, LANE), F32)],
        compiler_params=pltpu.CompilerParams(dimension_semantics=("arbitrary",),
                                             vmem_limit_bytes=VMEM_LIMIT),
        name="mix",
    )(*mix_in, *states, dec, rnw.reshape(1, RET_W), gnw.reshape(1, GLA_W), x1, m3, wout)


def _rope_tables(n_tok):
    freqs = ROPE_BASE ** (-jnp.arange(RET_DK // 4, dtype=F32) / (RET_DK // 4))

    def table(n_pos, first_half):
        ang = jnp.arange(n_pos, dtype=F32)[:, None] * freqs
        zero = jnp.zeros((n_pos, LANE // 2), F32)
        cos = jnp.concatenate([jnp.cos(ang)] * 2, axis=-1)
        sin = jnp.concatenate([-jnp.sin(ang), jnp.sin(ang)], axis=-1)
        halves = (cos, zero, sin, zero) if first_half else (zero, cos, zero, sin)
        return jnp.concatenate(halves, axis=-1)

    return table(n_tok // GRID_W, True), table(GRID_W, False)


def _pack_gate(w_f, b_f, w_b, b_b):
    gw = jnp.zeros((LANE, 2 * GLA_QK), F32)
    gw = gw.at[:GLA_RANK, :GLA_QK].set(w_f).at[GLA_RANK:2 * GLA_RANK, GLA_QK:].set(w_b)
    return gw.astype(BF16), jnp.concatenate([b_f, b_b]).reshape(1, 2 * GLA_QK)


def kernel(x, c, ctx, c_ctx, ada_w, ada_b, norm1_w, ffn1_w1, ffn1_w3, ffn1_w2, norm2_w, w_in,
           ret_decay_f, ret_decay_b, ret_norm_w, gla_gate_w_f, gla_gate_b_f, gla_gate_w_b, gla_gate_b_b,
           gla_norm_w, w_out, norm3_w, ffn2_w1, ffn2_w3, ffn2_w2, final_norm_w):
    batch, n_tok, d = x.shape
    n_ctx = ctx.shape[1]
    depth = ada_w.shape[0]
    assert depth == 1 and batch + 1 <= SUBLANES
    assert n_tok % FFN_TILE == 0 and n_tok % CHUNK == 0 and n_ctx % CHUNK == 0
    assert (batch * n_ctx) % FFN_TILE == 0

    cvec = jnp.zeros((SUBLANES, d), F32).at[:batch].set(c).at[batch].set(c_ctx)
    m, w1a, w3a, w2a, w_in_b, w_low_b = _modulation(cvec, ada_w[0], ada_b[0],
                                                    (ffn1_w1[0], ffn1_w3[0], ffn1_w2[0]), w_in[0].T)
    m3 = m.reshape(SUBLANES, 1, N_MOD * d)

    gw, gbias = _pack_gate(gla_gate_w_f[0], gla_gate_b_f[0], gla_gate_w_b[0], gla_gate_b_b[0])
    proj = (norm2_w[0], w_in_b, w_low_b, gw, gbias)
    f1 = (norm1_w[0], w1a, w3a, w2a)
    rowtab, coltab = _rope_tables(n_tok)
    dec = jnp.zeros((SUBLANES, RET_W), F32)
    dec = dec.at[0].set(jnp.repeat(ret_decay_f[0], LANE)).at[1].set(jnp.repeat(ret_decay_b[0], LANE))

    tiles_per_seq = n_tok // FFN_TILE
    first = _ffn_call(x.reshape(batch * n_tok, d), m3, tiles_per_seq, *f1, mod_off=0, tm=FFN_TILE,
                      ctx=ctx.reshape(batch * n_ctx, d), ctx_row=batch, proj=proj, rope=(rowtab, coltab),
                      cast=(ffn2_w1[0], ffn2_w3[0], ffn2_w2[0], w_out[0]), name="ffn_in")
    x1, mix_in, (w1b, w3b, w2b, woutb) = first[0], first[1:11], first[11:]
    scan_in = (mix_in[1], mix_in[2], mix_in[5], mix_in[6], mix_in[8], mix_in[9])
    zero = jnp.zeros((batch, RET_HEADS, LANE, LANE), F32)
    nc, nc_ctx = n_tok // CHUNK, n_ctx // CHUNK
    ctx_states = _states(*scan_in, dec, (zero,) * 4, batch, batch * nc, nc_ctx)[4:]
    states = _states(*scan_in, dec, ctx_states, batch, 0, nc)[:4]
    x2 = _mix_call(mix_in, states, dec, ret_norm_w[0], gla_norm_w[0], x1, m3, woutb, batch, nc)
    out = _ffn_call(x2, m3, tiles_per_seq, norm3_w[0], w1b, w3b, w2b, mod_off=6, tm=FFN_TILE,
                    final_w=final_norm_w, name="ffn_out")[0]
    return out.reshape(batch, n_tok, d)
```

```python
import functools

import jax
import jax.numpy as jnp
from jax import lax
from jax.experimental import pallas as pl
from jax.experimental.pallas import tpu as pltpu

F32 = jnp.float32
BF16 = jnp.bfloat16

EPS = 1e-6
LOG2E = 1.4426950408889634
N_MOD = 9
GRID_W = 64
ROPE_BASE = 10000.0
RET_HEADS = 4
RET_DK = 128
RET_DV = 128
GLA_HEADS = 4
GLA_DK = 64
GLA_DV = 128
GLA_RANK = 16
GLA_TAU = 16.0
RET_W = RET_HEADS * RET_DV
GLA_W = GLA_HEADS * GLA_DV
GLA_QK = GLA_HEADS * GLA_DK

LANE = 128
SUBLANES = 8
BF16_ROWS = 16
MOD_STEPS = 8
CHUNK = 256
MIX_CHUNKS = 2
FFN_TILE = 512
ROW_BLOCK = 256
FF_CHUNK = 256
VMEM_LIMIT = 60 * 1024 * 1024

C_RQ, C_RK, C_RV, C_RG = 0, 512, 1024, 1536
C_GQ, C_GK, C_GV, C_GG = 2048, 2304, 2560, 3072
C_LOW = 3584


def _silu(x):
    return x * (1.0 / (1.0 + jnp.exp(-x)))


def _log_sigmoid(z):
    return jnp.minimum(z, 0.0) - jnp.log(1.0 + jnp.exp(-jnp.abs(z)))


def _rms(x, w):
    return x * lax.rsqrt(jnp.mean(x * x, axis=-1, keepdims=True) + EPS) * w


def _dot(a, b):
    return jnp.dot(a, b, preferred_element_type=F32)


def _dot_nt(a, b):
    return lax.dot_general(a, b, (((1,), (1,)), ((), ())), preferred_element_type=F32)


def _dot_tn(a, b):
    return lax.dot_general(a, b, (((0,), (0,)), ((), ())), preferred_element_type=F32)


def _dot_select(sel, x):
    hi = x.astype(BF16)
    lo = (x - hi.astype(F32)).astype(BF16)
    return _dot(sel, hi) + _dot(sel, lo)


def _iota(shape, dim):
    return lax.broadcasted_iota(jnp.int32, shape, dim)


def _resident(shape):
    nd = len(shape)
    return pl.BlockSpec(shape, lambda *_: (0,) * nd, pipeline_mode=pl.Buffered(1))


def _cast_jobs(arrays, n_steps):
    specs, shapes = [], []
    for w in arrays:
        rows = next(r for r in range(BF16_ROWS, w.shape[0] + 1, BF16_ROWS)
                    if w.shape[0] % r == 0 and w.shape[0] // r <= n_steps)
        specs.append(pl.BlockSpec((rows, w.shape[1]),
                                  lambda i, last=w.shape[0] // rows - 1: (jnp.minimum(i, last), 0)))
        shapes.append(jax.ShapeDtypeStruct(w.shape, BF16))
    return specs, shapes


def _run_cast_jobs(src_refs, dst_refs):
    for src_ref, dst_ref in zip(src_refs, dst_refs):
        dst_ref[...] = src_ref[...].astype(BF16)


W_IN_BLOCK = 512


def _mod_kernel(*refs, n_cast):
    c_ref, w_ref, b_ref = refs[:3]
    cast_in, (wt_ref, wt_low_ref) = refs[3:3 + n_cast], refs[3 + n_cast:5 + n_cast]
    o_ref = refs[5 + n_cast]
    cast_out, (win_ref, wlow_ref) = refs[6 + n_cast:6 + 2 * n_cast], refs[6 + 2 * n_cast:]
    cond = _silu(c_ref[...]).astype(BF16)
    o_ref[...] = _dot(cond, w_ref[...].astype(BF16)) + b_ref[...]
    _run_cast_jobs(cast_in, cast_out)
    win_ref[...] = wt_ref[...].T.astype(BF16)
    low = wt_low_ref[...]
    low = jnp.concatenate([low, jnp.zeros((LANE - low.shape[0], low.shape[1]), F32)], axis=0)
    wlow_ref[...] = low.T.astype(BF16)


def _modulation(cvec, ada_w, ada_b, cast, w_in_t):
    d, n = ada_w.shape
    bn = n // MOD_STEPS if n % (MOD_STEPS * LANE) == 0 else d
    steps = n // bn
    assert C_LOW % W_IN_BLOCK == 0 and C_LOW // W_IN_BLOCK <= steps
    n_low = w_in_t.shape[0] - C_LOW
    last = C_LOW // W_IN_BLOCK - 1
    cast_specs, cast_shapes = _cast_jobs(cast, steps)
    return pl.pallas_call(
        functools.partial(_mod_kernel, n_cast=len(cast)),
        grid=(steps,),
        in_specs=[pl.BlockSpec((SUBLANES, d), lambda j: (0, 0)),
                  pl.BlockSpec((d, bn), lambda j: (0, j)),
                  pl.BlockSpec((1, bn), lambda j: (0, j))] + cast_specs
        + [pl.BlockSpec((W_IN_BLOCK, d), lambda j: (jnp.minimum(j, last), 0)),
           pl.BlockSpec((n_low, d), lambda j: (C_LOW // n_low, 0))],
        out_specs=[pl.BlockSpec((SUBLANES, bn), lambda j: (0, j))] + cast_specs
        + [pl.BlockSpec((d, W_IN_BLOCK), lambda j: (0, jnp.minimum(j, last))),
           pl.BlockSpec((d, LANE), lambda j: (0, 0))],
        out_shape=[jax.ShapeDtypeStruct((SUBLANES, n), F32)] + cast_shapes
        + [jax.ShapeDtypeStruct((d, C_LOW), BF16), jax.ShapeDtypeStruct((d, LANE), BF16)],
        compiler_params=pltpu.CompilerParams(dimension_semantics=("arbitrary",),
                                             vmem_limit_bytes=VMEM_LIMIT),
        name="mod",
    )(cvec, ada_w, ada_b.reshape(1, n), *cast, w_in_t, w_in_t)


def _swap32(x):
    lane = _iota(x.shape, 1)
    quarter = RET_DK // 4
    first = (lane & (2 * quarter - 1)) < quarter
    return jnp.where(first, pltpu.roll(x, LANE - quarter, 1), pltpu.roll(x, quarter, 1))


def _ffn_kernel(*refs, mod_off, proj, rope, final, n_main, n_cast, d, f):
    it = iter(refs)
    x_ref = next(it)
    if n_main is not None:
        xc_ref = next(it)
        is_ctx = pl.program_id(0) >= n_main
    m_ref, nw_ref, w1_ref, w3_ref, w2_ref = (next(it) for _ in range(5))
    if proj:
        n2w_ref, win_ref, wlow_ref, gw_ref, gbias_ref = (next(it) for _ in range(5))
        if rope:
            rowtab_ref, coltab_ref = next(it), next(it)
    if final:
        fnw_ref = next(it)
    cast_in = [next(it) for _ in range(n_cast)]
    xo_ref = next(it)
    if proj:
        (rq_ref, rk_ref, rv_ref, rg_ref, gq_ref, gk_ref, gv_ref, gg_ref,
         gf_ref, gb_ref) = (next(it) for _ in range(10))
    cast_out = [next(it) for _ in range(n_cast)]
    u_ref = next(it)
    _run_cast_jobs(cast_in, cast_out)

    def mod(i):
        return m_ref[:, (mod_off + i) * d:(mod_off + i + 1) * d]

    h2s = []
    for rb in range(x_ref.shape[0] // ROW_BLOCK):
        rows = slice(rb * ROW_BLOCK, (rb + 1) * ROW_BLOCK)
        x = x_ref[rows, :]
        if n_main is not None:
            x = jnp.where(is_ctx, xc_ref[rows, :], x)
        h = (_rms(x, nw_ref[...]) * (1.0 + mod(1)) + mod(0)).astype(BF16)
        for k in range(f // FF_CHUNK):
            sl = slice(k * FF_CHUNK, (k + 1) * FF_CHUNK)
            a = _dot(h, w1_ref[:, sl])
            g = _dot(h, w3_ref[:, sl])
            u_ref[rows, sl] = (_silu(a) * g).astype(BF16)
        y = _dot(u_ref[rows, :], w2_ref[...])
        x1 = x + (0.5 * mod(2)) * y

        if final:
            xo_ref[rows, :] = _rms(x1, fnw_ref[...])
        else:
            xo_ref[rows, :] = x1

        if proj:
            h2s.append((_rms(x1, n2w_ref[...]) * (1.0 + mod(4)) + mod(3)).astype(BF16))

    for rb, h2 in enumerate(h2s):
        rows = slice(rb * ROW_BLOCK, (rb + 1) * ROW_BLOCK)

        def p(lo, hi):
            return _dot(h2, win_ref[:, lo:hi])

        if rope:
            by_row = _iota((GRID_W, 2 * LANE), 1) % LANE < LANE // 2
            g0 = rb * (ROW_BLOCK // GRID_W)
            tab = jnp.concatenate(
                [jnp.where(by_row, jnp.broadcast_to(rowtab_ref[g0 + g:g0 + g + 1, :], (GRID_W, 2 * LANE)),
                           coltab_ref[...]) for g in range(ROW_BLOCK // GRID_W)], axis=0)
            cos, sin = tab[:, :LANE], tab[:, LANE:]
            if n_main is not None:
                cos = jnp.where(is_ctx, 1.0, cos)
                sin = jnp.where(is_ctx, 0.0, sin)

        low = _dot(h2, wlow_ref[...]).astype(BF16)
        z = _dot(low, gw_ref[...]) + gbias_ref[...]
        ls = _log_sigmoid(z) * (1.0 / GLA_TAU)
        gf_ref[rows, :] = ls[:, :GLA_QK]
        gb_ref[rows, :] = ls[:, GLA_QK:]
        for base, scale, o_ref in ((C_RQ, RET_DK ** -0.5, rq_ref), (C_RK, 1.0, rk_ref)):
            t = p(base, base + RET_W)
            for hd in range(RET_HEADS):
                th = t[:, hd * LANE:(hd + 1) * LANE] * scale
                if rope:
                    th = th * cos + _swap32(th) * sin
                o_ref[rows, hd * LANE:(hd + 1) * LANE] = th.astype(BF16)
        rg_ref[rows, :] = _silu(p(C_RG, C_RG + RET_W)).astype(BF16)
        gg_ref[rows, :] = _silu(p(C_GG, C_GG + GLA_W)).astype(BF16)
        gq_ref[rows, :] = (p(C_GQ, C_GQ + GLA_QK) * GLA_DK ** -0.5).astype(BF16)
        gk_ref[rows, :] = p(C_GK, C_GK + GLA_QK).astype(BF16)
        rv_ref[rows, :] = p(C_RV, C_RV + RET_W).astype(BF16)
        gv_ref[rows, :] = p(C_GV, C_GV + GLA_W).astype(BF16)


def _ffn_call(x, m3, tiles_per_seq, nw, w1, w3, w2, *, mod_off, tm, ctx=None, ctx_row=None, proj=None,
              rope=None, final_w=None, cast=(), name):
    t, d = x.shape
    f = w1.shape[1]
    n_main = t // tm
    n_tiles = n_main + (0 if ctx is None else ctx.shape[0] // tm)
    t = n_tiles * tm
    main = lambda i: jnp.minimum(i, n_main - 1)
    tile = lambda w: pl.BlockSpec((tm, w), lambda i: (i, 0))
    in_specs = [pl.BlockSpec((tm, d), lambda i: (main(i), 0))]
    args = [x]
    if ctx is None:
        row_of_tile = lambda i: i // tiles_per_seq
    else:
        row_of_tile = lambda i: jnp.where(i >= n_main, ctx_row, i // tiles_per_seq)
        in_specs.append(pl.BlockSpec((tm, d), lambda i: (jnp.maximum(i - n_main, 0), 0)))
        args.append(ctx)
    in_specs += [pl.BlockSpec((None, 1, N_MOD * d), lambda i: (row_of_tile(i), 0, 0)),
                 _resident((1, d)), _resident((d, f)), _resident((d, f)), _resident((f, d))]
    args += [m3, nw.reshape(1, d), w1, w3, w2]
    if proj is not None:
        n2w, win, wlow, gw, gbias = proj
        in_specs += [_resident((1, d))] + [_resident(a.shape) for a in (win, wlow, gw, gbias)]
        args += [n2w.reshape(1, d), win, wlow, gw, gbias]
        if rope is not None:
            rowtab, coltab = rope
            in_specs += [pl.BlockSpec((tm // GRID_W, 2 * LANE), lambda i: (main(i) % tiles_per_seq, 0)),
                         _resident(coltab.shape)]
            args += [rowtab, coltab]
    if final_w is not None:
        in_specs.append(_resident((1, d)))
        args.append(final_w.reshape(1, d))
    out_specs = [tile(d)]
    out_shape = [jax.ShapeDtypeStruct((t, d), F32)]
    if proj is not None:
        for w, dt in ((RET_W, BF16),) * 4 + ((GLA_QK, BF16),) * 2 + ((GLA_W, BF16),) * 2 + ((GLA_QK, F32),) * 2:
            out_specs.append(tile(w))
            out_shape.append(jax.ShapeDtypeStruct((t, w), dt))
    cast_specs, cast_shapes = _cast_jobs(cast, n_main)
    in_specs += cast_specs
    args += list(cast)
    out_specs += cast_specs
    out_shape += cast_shapes
    kern = functools.partial(_ffn_kernel, mod_off=mod_off, proj=proj is not None,
                             rope=rope is not None, final=final_w is not None,
                             n_main=None if ctx is None else n_main, n_cast=len(cast), d=d, f=f)
    return pl.pallas_call(
        kern, grid=(n_tiles,), in_specs=in_specs, out_specs=out_specs, out_shape=out_shape,
        scratch_shapes=[pltpu.VMEM((tm, f), BF16)],
        compiler_params=pltpu.CompilerParams(dimension_semantics=("arbitrary",),
                                             vmem_limit_bytes=VMEM_LIMIT),
        name=name,
    )(*args)


def _head_lanes(shape, hd):
    lane = _iota(shape, len(shape) - 1)
    return (lane >= GLA_DK) if hd % 2 else (lane < GLA_DK)


def _state_kernel(*refs, batch, cps, steps, bcps, bsteps):
    it = iter(refs)
    per_batch = lambda: [next(it) for _ in range(batch)]
    rk_ref, rv_ref, gk_ref, gv_ref, gf_ref, gb_ref = (per_batch() for _ in range(6))
    dec_ref = next(it)
    irf_ref, irb_ref, igf_ref, igb_ref = (next(it) for _ in range(4))
    orf_ref, orb_ref, ogf_ref, ogb_ref = (next(it) for _ in range(4))
    frf_ref, frb_ref, fgf_ref, fgb_ref = (next(it) for _ in range(4))
    srf, srb, sgf, sgb, ub_ret, ub_gla, ub_dec = (next(it) for _ in range(7))
    n = CHUNK
    step = pl.program_id(0)
    lg = _log_sigmoid(dec_ref[...])

    @pl.when(step == 0)
    def _():
        srf[...] = irf_ref[...]
        srb[...] = irb_ref[...]
        sgf[...] = igf_ref[...]
        sgb[...] = igb_ref[...]

    @pl.when(step < steps)
    def _():
        row = _iota((n, LANE), 0).astype(F32)
        ri = _iota((n, n), 0)
        ci = _iota((n, n), 1)
        after = (ci > ri).astype(BF16)
        before = (ci < ri).astype(BF16)
        heads = [slice(hd * LANE, (hd + 1) * LANE) for hd in range(RET_HEADS)]
        to_end = [jnp.exp((n - 1.0 - row) * lg[0:1, sl]).astype(BF16) for sl in heads]
        to_start = [jnp.exp(row * lg[1:2, sl]).astype(BF16) for sl in heads]

        def gla_keys(b, rows, g_ref, tri, edge):
            g = g_ref[b][rows, :]
            e = _dot_select(tri, g)
            tot = e[edge:edge + 1, :] + g[edge:edge + 1, :]
            return gk_ref[b][rows, :] * jnp.exp(e).astype(BF16), jnp.exp(tot)

        def gla(b, rows, keys):
            kd, dec = keys
            upd = []
            for hd in range(GLA_HEADS):
                pr = slice((hd // 2) * LANE, (hd // 2 + 1) * LANE)
                u = _dot_tn(gv_ref[b][rows, hd * LANE:(hd + 1) * LANE], kd[:, pr])
                upd.append(jnp.where(_head_lanes(u.shape, hd), u, 0.0))
            return upd, dec

        for j in range(cps):
            chunk = step * cps + j
            rows = slice(j * n, (j + 1) * n)
            orf_ref[:, j] = srf[...].astype(BF16)
            ogf_ref[:, j] = sgf[...].astype(BF16)
            keys_f = [gla_keys(b, rows, gf_ref, after, 0) for b in range(batch)]
            keys_b = [gla_keys(b, rows, gb_ref, before, n - 1) for b in range(batch)]
            for b in range(batch):
                for hd in range(RET_HEADS):
                    sl = slice(hd * LANE, (hd + 1) * LANE)
                    k = rk_ref[b][rows, sl]
                    v = rv_ref[b][rows, sl]
                    srf[b, hd] = srf[b, hd] * jnp.exp(n * lg[0:1, sl]) + _dot_tn(k * to_end[hd], v)
                    ub_ret[chunk, b, hd] = _dot_tn(k * to_start[hd], v).astype(BF16)
                upd, dec = gla(b, rows, keys_f[b])
                for hd in range(GLA_HEADS):
                    pr = slice((hd // 2) * LANE, (hd // 2 + 1) * LANE)
                    sgf[b, hd] = sgf[b, hd] * dec[:, pr] + upd[hd]
                upd, dec = gla(b, rows, keys_b[b])
                for hd in range(GLA_HEADS):
                    ub_gla[chunk, b, hd] = upd[hd].astype(BF16)
                ub_dec[chunk, b] = jnp.broadcast_to(dec, ub_dec.shape[2:])
        frf_ref[...] = srf[...]
        fgf_ref[...] = sgf[...]

    @pl.when(step >= steps)
    def _():
        for j in range(bcps):
            jb = bcps - 1 - j
            chunk = (steps + bsteps - 1 - step) * bcps + jb
            orb_ref[:, jb] = srb[...].astype(BF16)
            ogb_ref[:, jb] = sgb[...].astype(BF16)
            for b in range(batch):
                for hd in range(RET_HEADS):
                    lgb = lg[1:2, hd * LANE:(hd + 1) * LANE]
                    srb[b, hd] = srb[b, hd] * jnp.exp(n * lgb) + ub_ret[chunk, b, hd].astype(F32)
                for hd in range(GLA_HEADS):
                    pr = slice((hd // 2) * LANE, (hd // 2 + 1) * LANE)
                    sgb[b, hd] = sgb[b, hd] * ub_dec[chunk, b, 0:1, pr] + ub_gla[chunk, b, hd].astype(F32)
        frb_ref[...] = srb[...]
        fgb_ref[...] = sgb[...]


def _states(rk, rv, gk, gv, gf, gb, dec, init, batch, first_chunk, nc):
    cps = 2 if nc % 2 == 0 and first_chunk % 2 == 0 else 1
    bcps = next(c for c in (8, 4, 2, 1) if nc % c == 0)
    steps, bsteps = nc // cps, nc // bcps
    reading = lambda c: jnp.minimum(c, steps - 1)
    scanning = lambda c: bsteps - 1 - jnp.maximum(c - steps, 0)

    def tokens(w):
        return [pl.BlockSpec((cps * CHUNK, w), lambda c, b=b: ((first_chunk + b * nc) // cps + reading(c), 0))
                for b in range(batch)]

    st_shape = (batch, RET_HEADS, LANE, LANE)
    init_spec = pl.BlockSpec(st_shape, lambda c: (0, 0, 0, 0))
    of_spec = pl.BlockSpec((batch, cps, RET_HEADS, LANE, LANE), lambda c: (0, reading(c), 0, 0, 0))
    ob_spec = pl.BlockSpec((batch, bcps, RET_HEADS, LANE, LANE), lambda c: (0, scanning(c), 0, 0, 0))
    per_chunk = jax.ShapeDtypeStruct((batch, nc, RET_HEADS, LANE, LANE), BF16)
    final = jax.ShapeDtypeStruct(st_shape, F32)
    kept = (nc, batch, RET_HEADS, LANE, LANE)
    rep = lambda a: [a] * batch
    return pl.pallas_call(
        functools.partial(_state_kernel, batch=batch, cps=cps, steps=steps, bcps=bcps, bsteps=bsteps),
        grid=(steps + bsteps,),
        in_specs=tokens(RET_W) + tokens(RET_W) + tokens(GLA_QK) + tokens(GLA_W) + tokens(GLA_QK) + tokens(GLA_QK)
        + [pl.BlockSpec(dec.shape, lambda c: (0, 0))] + [init_spec] * 4,
        out_specs=[of_spec, ob_spec, of_spec, ob_spec] + [init_spec] * 4,
        out_shape=[per_chunk] * 4 + [final] * 4,
        scratch_shapes=[pltpu.VMEM(st_shape, F32)] * 4
        + [pltpu.VMEM(kept, BF16), pltpu.VMEM(kept, BF16), pltpu.VMEM((nc, batch, SUBLANES, GLA_QK), F32)],
        compiler_params=pltpu.CompilerParams(dimension_semantics=("arbitrary",),
                                             vmem_limit_bytes=VMEM_LIMIT),
        name="states",
    )(*rep(rk), *rep(rv), *rep(gk), *rep(gv), *rep(gf), *rep(gb), dec, *init)


def _block_row(x, parent, r):
    n, w = x.shape
    if parent == n:
        return jnp.broadcast_to(x[r:r + 1, :], (n, w))
    x3 = x.reshape(n // parent, parent, w)
    return jnp.broadcast_to(x3[:, r:r + 1, :], x3.shape).reshape(n, w)


def _gla_levels(q, k_even, k_odd, gf, gb, bf, bb):
    n = q.shape[0]
    row = _iota(q.shape, 0)

    def factors(u_exp, w_exp):
        ew = jnp.exp2(w_exp).astype(BF16)
        return (q * jnp.exp2(u_exp).astype(BF16), k_even * ew, k_odd * ew)

    levels = []
    s = n // 2
    while s >= 2:
        if s >= 4:
            last_of_first = _block_row(bf, 2 * s, s - 1)
            first_of_second = _block_row(bb, 2 * s, s)
        else:
            upper = (row & 4) != 0
            last_of_first = jnp.where(upper, _block_row(bf, SUBLANES, 5), _block_row(bf, SUBLANES, 1))
            first_of_second = jnp.where(upper, _block_row(bb, SUBLANES, 6), _block_row(bb, SUBLANES, 2))
        df = bf - last_of_first
        db = bb - first_of_second
        levels.append((2 * s, [factors(jnp.minimum(df, db), -jnp.maximum(df, db))]))
        s //= 2
    odd = (row & 1) == 1
    qa = q * jnp.where(odd, jnp.exp2(gf), 2.0).astype(BF16)
    qb = q * jnp.where(odd, 2.0, jnp.exp2(gb)).astype(BF16)
    zero = jnp.zeros_like(k_even)
    ev = lambda x: jnp.where(odd, zero, x)
    od = lambda x: jnp.where(odd, x, zero)
    levels.append((2, [(qa, ev(k_even), ev(k_odd)), (qb, od(k_even), od(k_odd))]))
    return levels


def _mix_kernel(rq_ref, rk_ref, rv_ref, rg_ref, gq_ref, gk_ref, gv_ref, gg_ref, gf_ref, gb_ref,
                srf_ref, srb_ref, sgf_ref, sgb_ref, dec_ref, rnw_ref, gnw_ref,
                x_ref, m_ref, wout_ref, o_ref, mix_ref, decay_ref, qdec_ref, *, d):
    n = CHUNK
    half = n // 2
    ri = _iota((n, n), 0)
    ci = _iota((n, n), 1)
    step = pl.program_id(0)
    last_step = pl.num_programs(0) - 1
    cur = step % 2
    code = _iota((half, n), 0) ^ (_iota((half, n), 1) & (half - 1))

    @pl.when(step == 0)
    def _():
        mix_ref[1] = jnp.zeros(mix_ref.shape[1:], BF16)
        lg = _log_sigmoid(dec_ref[...])
        dist = (ri - ci).astype(F32)
        row = _iota((n, LANE), 0).astype(F32)
        for hd in range(RET_HEADS):
            sl = slice(hd * LANE, (hd + 1) * LANE)
            lgf = lg[0:1, sl]
            lgb = lg[1:2, sl]
            decay_ref[hd] = jnp.exp(jnp.where(dist > 0, dist * lgf[:, 0:1],
                                              jnp.where(dist < 0, -dist * lgb[:, 0:1], jnp.log(2.0))))
            qdec_ref[hd] = jnp.exp((row + 1.0) * lgf)
            qdec_ref[RET_HEADS + hd] = jnp.exp((n - row) * lgb)

    def retention(c):
        tok = slice(c * n, (c + 1) * n)
        for hd in range(RET_HEADS):
            sl = slice(hd * LANE, (hd + 1) * LANE)
            q = rq_ref[tok, sl]
            p = (_dot_nt(q, rk_ref[tok, sl]) * decay_ref[hd]).astype(BF16)
            qf = q.astype(F32)
            qs = jnp.concatenate([(qf * qdec_ref[hd]).astype(BF16),
                                  (qf * qdec_ref[RET_HEADS + hd]).astype(BF16)], axis=1)
            st = jnp.concatenate([srf_ref[c, hd], srb_ref[c, hd]], axis=0)
            o = _dot(p, rv_ref[tok, sl]) + _dot(qs, st)
            mu = jnp.mean(o, axis=-1, keepdims=True)
            oc = o - mu
            var = jnp.mean(oc * oc, axis=-1, keepdims=True)
            r = oc * lax.rsqrt(var + EPS) * rnw_ref[:, sl] * rg_ref[tok, sl].astype(F32)
            mix_ref[cur, tok, sl] = r.astype(BF16)

    def gla_factors(c):
        tok = slice(c * n, (c + 1) * n)
        gf = gf_ref[tok, :] * LOG2E
        gb = gb_ref[tok, :] * LOG2E
        bf = _dot_select((ci <= ri).astype(BF16), gf)
        bb = _dot_select((ci >= ri).astype(BF16), gb)
        q = gq_ref[tok, :]
        k = gk_ref[tok, :]
        even_head = (_iota(k.shape, 1) & GLA_DK) == 0
        no_k = jnp.zeros_like(k)
        levels = _gla_levels(q, jnp.where(even_head, k, no_k), jnp.where(even_head, no_k, k), gf, gb, bf, bb)
        return levels, q * jnp.exp2(bf).astype(BF16), q * jnp.exp2(bb).astype(BF16)

    def finish_previous():
        o_ref[...] = x_ref[...] + m_ref[:, 5 * d:6 * d] * _dot(mix_ref[1 - cur], wout_ref[...])

    def gla_outputs(c, levels, qsf, qsb):
        tok = slice(c * n, (c + 1) * n)
        for pair in range(GLA_HEADS // 2):
            pr = slice(pair * LANE, (pair + 1) * LANE)
            near = [None, None]
            far = [None, None]
            for size, blocks in levels:
                for rb in range(2):
                    kb = rb if size < n else 1 - rb
                    rows = slice(rb * half, (rb + 1) * half)
                    keys = slice(kb * half, (kb + 1) * half)
                    u = jnp.concatenate([blk[0][rows, pr] for blk in blocks], axis=1)
                    w = jnp.concatenate([jnp.concatenate([blk[j][keys, pr] for blk in blocks], axis=1)
                                         for j in (1, 2)], axis=0)
                    t = _dot_nt(u, w)
                    if size == n:
                        far[rb] = t
                    else:
                        near[rb] = t if near[rb] is None else jnp.where(code < size, t, near[rb])
            qcat = jnp.concatenate([qsf[:, pr], qsb[:, pr]], axis=1)
            for j in range(2):
                hd = 2 * pair + j
                sl = slice(hd * LANE, (hd + 1) * LANE)
                mine = slice(j * half, (j + 1) * half)
                p = jnp.concatenate([jnp.concatenate([near[0][:, mine], far[0][:, mine]], axis=1),
                                     jnp.concatenate([far[1][:, mine], near[1][:, mine]], axis=1)], axis=0)
                st = jnp.concatenate([sgf_ref[c, hd], sgb_ref[c, hd]], axis=1)
                o = _dot(p.astype(BF16), gv_ref[tok, sl]) + _dot_nt(qcat, st)
                ms = jnp.mean(o * o, axis=-1, keepdims=True)
                r = o * lax.rsqrt(ms + EPS) * gnw_ref[:, sl] * gg_ref[tok, sl].astype(F32)
                mix_ref[cur, tok, RET_W + hd * LANE:RET_W + (hd + 1) * LANE] = r.astype(BF16)

    @pl.when(step < last_step)
    def _():
        factors = []
        for c in range(MIX_CHUNKS):
            factors.append(gla_factors(c))
            if c == 0:
                finish_previous()
            retention(c)
        for c in range(MIX_CHUNKS):
            gla_outputs(c, *factors[c])

    @pl.when(step == last_step)
    def _():
        finish_previous()


def _mix_call(mix_in, states, dec, rnw, gnw, x1, m3, wout, batch, nc):
    assert nc % MIX_CHUNKS == 0
    per_seq = nc // MIX_CHUNKS
    n_steps = batch * per_seq
    rows = MIX_CHUNKS * CHUNK
    t, d = n_steps * rows, x1.shape[1]
    mixed = lambda i: jnp.minimum(i, n_steps - 1)
    done = lambda i: jnp.maximum(i - 1, 0)
    tile = lambda w: pl.BlockSpec((rows, w), lambda i: (mixed(i), 0))
    st_spec = pl.BlockSpec((None, MIX_CHUNKS, RET_HEADS, LANE, LANE),
                           lambda i: (mixed(i) // per_seq, mixed(i) % per_seq, 0, 0, 0))
    const = lambda shape: pl.BlockSpec(shape, lambda i: (0,) * len(shape))
    widths = (RET_W,) * 4 + (GLA_QK,) * 2 + (GLA_W,) * 2 + (GLA_QK,) * 2
    return pl.pallas_call(
        functools.partial(_mix_kernel, d=d),
        grid=(n_steps + 1,),
        in_specs=[tile(w) for w in widths] + [st_spec] * 4
        + [const(dec.shape), const((1, RET_W)), const((1, GLA_W)),
           pl.BlockSpec((rows, d), lambda i: (done(i), 0)),
           pl.BlockSpec((None, 1, N_MOD * d), lambda i: (done(i) // per_seq, 0, 0)),
           const(wout.shape)],
        out_specs=pl.BlockSpec((rows, d), lambda i: (done(i), 0)),
        out_shape=jax.ShapeDtypeStruct((t, d), F32),
        scratch_shapes=[pltpu.VMEM((2, rows, RET_W + GLA_W), BF16),
                        pltpu.VMEM((RET_HEADS, CHUNK, CHUNK), F32),
                        pltpu.VMEM((2 * RET_HEADS, CHUNK, LANE), F32)],
        compiler_params=pltpu.CompilerParams(dimension_semantics=("arbitrary",),
                                             vmem_limit_bytes=VMEM_LIMIT),
        name="mix",
    )(*mix_in, *states, dec, rnw.reshape(1, RET_W), gnw.reshape(1, GLA_W), x1, m3, wout)


def _rope_tables(n_tok):
    freqs = ROPE_BASE ** (-jnp.arange(RET_DK // 4, dtype=F32) / (RET_DK // 4))

    def table(n_pos, first_half):
        ang = jnp.arange(n_pos, dtype=F32)[:, None] * freqs
        zero = jnp.zeros((n_pos, LANE // 2), F32)
        cos = jnp.concatenate([jnp.cos(ang)] * 2, axis=-1)
        sin = jnp.concatenate([-jnp.sin(ang), jnp.sin(ang)], axis=-1)
        halves = (cos, zero, sin, zero) if first_half else (zero, cos, zero, sin)
        return jnp.concatenate(halves, axis=-1)

    return table(n_tok // GRID_W, True), table(GRID_W, False)


def _pack_gate(w_f, b_f, w_b, b_b):
    gw = jnp.zeros((LANE, 2 * GLA_QK), F32)
    gw = gw.at[:GLA_RANK, :GLA_QK].set(w_f).at[GLA_RANK:2 * GLA_RANK, GLA_QK:].set(w_b)
    return gw.astype(BF16), jnp.concatenate([b_f, b_b]).reshape(1, 2 * GLA_QK)


def kernel(x, c, ctx, c_ctx, ada_w, ada_b, norm1_w, ffn1_w1, ffn1_w3, ffn1_w2, norm2_w, w_in,
           ret_decay_f, ret_decay_b, ret_norm_w, gla_gate_w_f, gla_gate_b_f, gla_gate_w_b, gla_gate_b_b,
           gla_norm_w, w_out, norm3_w, ffn2_w1, ffn2_w3, ffn2_w2, final_norm_w):
    batch, n_tok, d = x.shape
    n_ctx = ctx.shape[1]
    depth = ada_w.shape[0]
    assert depth == 1 and batch + 1 <= SUBLANES
    assert n_tok % FFN_TILE == 0 and n_tok % CHUNK == 0 and n_ctx % CHUNK == 0
    assert (batch * n_ctx) % FFN_TILE == 0

    cvec = jnp.zeros((SUBLANES, d), F32).at[:batch].set(c).at[batch].set(c_ctx)
    m, w1a, w3a, w2a, w_in_b, w_low_b = _modulation(cvec, ada_w[0], ada_b[0],
                                                    (ffn1_w1[0], ffn1_w3[0], ffn1_w2[0]), w_in[0].T)
    m3 = m.reshape(SUBLANES, 1, N_MOD * d)

    gw, gbias = _pack_gate(gla_gate_w_f[0], gla_gate_b_f[0], gla_gate_w_b[0], gla_gate_b_b[0])
    proj = (norm2_w[0], w_in_b, w_low_b, gw, gbias)
    f1 = (norm1_w[0], w1a, w3a, w2a)
    rowtab, coltab = _rope_tables(n_tok)
    dec = jnp.zeros((SUBLANES, RET_W), F32)
    dec = dec.at[0].set(jnp.repeat(ret_decay_f[0], LANE)).at[1].set(jnp.repeat(ret_decay_b[0], LANE))

    tiles_per_seq = n_tok // FFN_TILE
    first = _ffn_call(x.reshape(batch * n_tok, d), m3, tiles_per_seq, *f1, mod_off=0, tm=FFN_TILE,
                      ctx=ctx.reshape(batch * n_ctx, d), ctx_row=batch, proj=proj, rope=(rowtab, coltab),
                      cast=(ffn2_w1[0], ffn2_w3[0], ffn2_w2[0], w_out[0]), name="ffn_in")
    x1, mix_in, (w1b, w3b, w2b, woutb) = first[0], first[1:11], first[11:]
    scan_in = (mix_in[1], mix_in[2], mix_in[5], mix_in[6], mix_in[8], mix_in[9])
    zero = jnp.zeros((batch, RET_HEADS, LANE, LANE), F32)
    nc, nc_ctx = n_tok // CHUNK, n_ctx // CHUNK
    ctx_states = _states(*scan_in, dec, (zero,) * 4, batch, batch * nc, nc_ctx)[4:]
    states = _states(*scan_in, dec, ctx_states, batch, 0, nc)[:4]
    x2 = _mix_call(mix_in, states, dec, ret_norm_w[0], gla_norm_w[0], x1, m3, woutb, batch, nc)
    out = _ffn_call(x2, m3, tiles_per_seq, norm3_w[0], w1b, w3b, w2b, mod_off=6, tm=FFN_TILE,
                    final_w=final_norm_w, name="ffn_out")[0]
    return out.reshape(batch, n_tok, d)
```

```python
import functools

import jax
import jax.numpy as jnp
from jax import lax
from jax.experimental import pallas as pl
from jax.experimental.pallas import tpu as pltpu

F32 = jnp.float32
BF16 = jnp.bfloat16

EPS = 1e-6
LOG2E = 1.4426950408889634
N_MOD = 9
GRID_W = 64
ROPE_BASE = 10000.0
RET_HEADS = 4
RET_DK = 128
RET_DV = 128
GLA_HEADS = 4
GLA_DK = 64
GLA_DV = 128
GLA_RANK = 16
GLA_TAU = 16.0
RET_W = RET_HEADS * RET_DV
GLA_W = GLA_HEADS * GLA_DV
GLA_QK = GLA_HEADS * GLA_DK

LANE = 128
SUBLANES = 8
BF16_ROWS = 16
MOD_STEPS = 8
CHUNK = 256
MIX_CHUNKS = 2
FFN_TILE = 512
ROW_BLOCK = 256
FF_CHUNK = 256
VMEM_LIMIT = 60 * 1024 * 1024

C_RQ, C_RK, C_RV, C_RG = 0, 512, 1024, 1536
C_GQ, C_GK, C_GV, C_GG = 2048, 2304, 2560, 3072
C_LOW = 3584


def _silu(x):
    return x * (1.0 / (1.0 + jnp.exp(-x)))


def _log_sigmoid(z):
    return jnp.minimum(z, 0.0) - jnp.log(1.0 + jnp.exp(-jnp.abs(z)))


def _rms(x, w):
    return x * lax.rsqrt(jnp.mean(x * x, axis=-1, keepdims=True) + EPS) * w


def _dot(a, b):
    return jnp.dot(a, b, preferred_element_type=F32)


def _dot_nt(a, b):
    return lax.dot_general(a, b, (((1,), (1,)), ((), ())), preferred_element_type=F32)


def _dot_tn(a, b):
    return lax.dot_general(a, b, (((0,), (0,)), ((), ())), preferred_element_type=F32)


def _dot_select(sel, x):
    hi = x.astype(BF16)
    lo = (x - hi.astype(F32)).astype(BF16)
    return _dot(sel, hi) + _dot(sel, lo)


def _iota(shape, dim):
    return lax.broadcasted_iota(jnp.int32, shape, dim)


def _resident(shape):
    nd = len(shape)
    return pl.BlockSpec(shape, lambda *_: (0,) * nd, pipeline_mode=pl.Buffered(1))


def _cast_jobs(arrays, n_steps):
    specs, shapes = [], []
    for w in arrays:
        rows = next(r for r in range(BF16_ROWS, w.shape[0] + 1, BF16_ROWS)
                    if w.shape[0] % r == 0 and w.shape[0] // r <= n_steps)
        specs.append(pl.BlockSpec((rows, w.shape[1]),
                                  lambda i, last=w.shape[0] // rows - 1: (jnp.minimum(i, last), 0)))
        shapes.append(jax.ShapeDtypeStruct(w.shape, BF16))
    return specs, shapes


def _run_cast_jobs(src_refs, dst_refs):
    for src_ref, dst_ref in zip(src_refs, dst_refs):
        dst_ref[...] = src_ref[...].astype(BF16)


W_IN_BLOCK = 512


def _mod_kernel(*refs, n_cast):
    c_ref, w_ref, b_ref = refs[:3]
    cast_in, (wt_ref, wt_low_ref) = refs[3:3 + n_cast], refs[3 + n_cast:5 + n_cast]
    o_ref = refs[5 + n_cast]
    cast_out, (win_ref, wlow_ref) = refs[6 + n_cast:6 + 2 * n_cast], refs[6 + 2 * n_cast:]
    cond = _silu(c_ref[...]).astype(BF16)
    o_ref[...] = _dot(cond, w_ref[...].astype(BF16)) + b_ref[...]
    _run_cast_jobs(cast_in, cast_out)
    win_ref[...] = wt_ref[...].T.astype(BF16)
    low = wt_low_ref[...]
    low = jnp.concatenate([low, jnp.zeros((LANE - low.shape[0], low.shape[1]), F32)], axis=0)
    wlow_ref[...] = low.T.astype(BF16)


def _modulation(cvec, ada_w, ada_b, cast, w_in_t):
    d, n = ada_w.shape
    bn = n // MOD_STEPS if n % (MOD_STEPS * LANE) == 0 else d
    steps = n // bn
    assert C_LOW % W_IN_BLOCK == 0 and C_LOW // W_IN_BLOCK <= steps
    n_low = w_in_t.shape[0] - C_LOW
    last = C_LOW // W_IN_BLOCK - 1
    cast_specs, cast_shapes = _cast_jobs(cast, steps)
    return pl.pallas_call(
        functools.partial(_mod_kernel, n_cast=len(cast)),
        grid=(steps,),
        in_specs=[pl.BlockSpec((SUBLANES, d), lambda j: (0, 0)),
                  pl.BlockSpec((d, bn), lambda j: (0, j)),
                  pl.BlockSpec((1, bn), lambda j: (0, j))] + cast_specs
        + [pl.BlockSpec((W_IN_BLOCK, d), lambda j: (jnp.minimum(j, last), 0)),
           pl.BlockSpec((n_low, d), lambda j: (C_LOW // n_low, 0))],
        out_specs=[pl.BlockSpec((SUBLANES, bn), lambda j: (0, j))] + cast_specs
        + [pl.BlockSpec((d, W_IN_BLOCK), lambda j: (0, jnp.minimum(j, last))),
           pl.BlockSpec((d, LANE), lambda j: (0, 0))],
        out_shape=[jax.ShapeDtypeStruct((SUBLANES, n), F32)] + cast_shapes
        + [jax.ShapeDtypeStruct((d, C_LOW), BF16), jax.ShapeDtypeStruct((d, LANE), BF16)],
        compiler_params=pltpu.CompilerParams(dimension_semantics=("arbitrary",),
                                             vmem_limit_bytes=VMEM_LIMIT),
        name="mod",
    )(cvec, ada_w, ada_b.reshape(1, n), *cast, w_in_t, w_in_t)


def _swap32(x):
    lane = _iota(x.shape, 1)
    quarter = RET_DK // 4
    first = (lane & (2 * quarter - 1)) < quarter
    return jnp.where(first, pltpu.roll(x, LANE - quarter, 1), pltpu.roll(x, quarter, 1))


def _ffn_kernel(*refs, mod_off, proj, rope, final, n_main, n_cast, d, f):
    it = iter(refs)
    x_ref = next(it)
    if n_main is not None:
        xc_ref = next(it)
        is_ctx = pl.program_id(0) >= n_main
    m_ref, nw_ref, w1_ref, w3_ref, w2_ref = (next(it) for _ in range(5))
    if proj:
        n2w_ref, win_ref, wlow_ref, gw_ref, gbias_ref = (next(it) for _ in range(5))
        if rope:
            rowtab_ref, coltab_ref = next(it), next(it)
    if final:
        fnw_ref = next(it)
    cast_in = [next(it) for _ in range(n_cast)]
    xo_ref = next(it)
    if proj:
        (rq_ref, rk_ref, rv_ref, rg_ref, gq_ref, gk_ref, gv_ref, gg_ref,
         gf_ref, gb_ref) = (next(it) for _ in range(10))
    cast_out = [next(it) for _ in range(n_cast)]
    u_ref = next(it)
    _run_cast_jobs(cast_in, cast_out)

    def mod(i):
        return m_ref[:, (mod_off + i) * d:(mod_off + i + 1) * d]

    h2s = []
    for rb in range(x_ref.shape[0] // ROW_BLOCK):
        rows = slice(rb * ROW_BLOCK, (rb + 1) * ROW_BLOCK)
        x = x_ref[rows, :]
        if n_main is not None:
            x = jnp.where(is_ctx, xc_ref[rows, :], x)
        h = (_rms(x, nw_ref[...]) * (1.0 + mod(1)) + mod(0)).astype(BF16)
        for k in range(f // FF_CHUNK):
            sl = slice(k * FF_CHUNK, (k + 1) * FF_CHUNK)
            a = _dot(h, w1_ref[:, sl])
            g = _dot(h, w3_ref[:, sl])
            u_ref[rows, sl] = (_silu(a) * g).astype(BF16)
        y = _dot(u_ref[rows, :], w2_ref[...])
        x1 = x + (0.5 * mod(2)) * y

        if final:
            xo_ref[rows, :] = _rms(x1, fnw_ref[...])
        else:
            xo_ref[rows, :] = x1

        if proj:
            h2s.append((_rms(x1, n2w_ref[...]) * (1.0 + mod(4)) + mod(3)).astype(BF16))

    for rb, h2 in enumerate(h2s):
        rows = slice(rb * ROW_BLOCK, (rb + 1) * ROW_BLOCK)

        def p(lo, hi):
            return _dot(h2, win_ref[:, lo:hi])

        if rope:
            by_row = _iota((GRID_W, 2 * LANE), 1) % LANE < LANE // 2
            g0 = rb * (ROW_BLOCK // GRID_W)
            tab = jnp.concatenate(
                [jnp.where(by_row, jnp.broadcast_to(rowtab_ref[g0 + g:g0 + g + 1, :], (GRID_W, 2 * LANE)),
                           coltab_ref[...]) for g in range(ROW_BLOCK // GRID_W)], axis=0)
            cos, sin = tab[:, :LANE], tab[:, LANE:]
            if n_main is not None:
                cos = jnp.where(is_ctx, 1.0, cos)
                sin = jnp.where(is_ctx, 0.0, sin)

        low = _dot(h2, wlow_ref[...]).astype(BF16)
        z = _dot(low, gw_ref[...]) + gbias_ref[...]
        ls = _log_sigmoid(z) * (1.0 / GLA_TAU)
        gf_ref[rows, :] = ls[:, :GLA_QK]
        gb_ref[rows, :] = ls[:, GLA_QK:]
        for base, scale, o_ref in ((C_RQ, RET_DK ** -0.5, rq_ref), (C_RK, 1.0, rk_ref)):
            t = p(base, base + RET_W)
            for hd in range(RET_HEADS):
                th = t[:, hd * LANE:(hd + 1) * LANE] * scale
                if rope:
                    th = th * cos + _swap32(th) * sin
                o_ref[rows, hd * LANE:(hd + 1) * LANE] = th.astype(BF16)
        rg_ref[rows, :] = _silu(p(C_RG, C_RG + RET_W)).astype(BF16)
        gg_ref[rows, :] = _silu(p(C_GG, C_GG + GLA_W)).astype(BF16)
        gq_ref[rows, :] = (p(C_GQ, C_GQ + GLA_QK) * GLA_DK ** -0.5).astype(BF16)
        gk_ref[rows, :] = p(C_GK, C_GK + GLA_QK).astype(BF16)
        rv_ref[rows, :] = p(C_RV, C_RV + RET_W).astype(BF16)
        gv_ref[rows, :] = p(C_GV, C_GV + GLA_W).astype(BF16)


def _ffn_call(x, m3, tiles_per_seq, nw, w1, w3, w2, *, mod_off, tm, ctx=None, ctx_row=None, proj=None,
              rope=None, final_w=None, cast=(), name):
    t, d = x.shape
    f = w1.shape[1]
    n_main = t // tm
    n_tiles = n_main + (0 if ctx is None else ctx.shape[0] // tm)
    t = n_tiles * tm
    main = lambda i: jnp.minimum(i, n_main - 1)
    tile = lambda w: pl.BlockSpec((tm, w), lambda i: (i, 0))
    in_specs = [pl.BlockSpec((tm, d), lambda i: (main(i), 0))]
    args = [x]
    if ctx is None:
        row_of_tile = lambda i: i // tiles_per_seq
    else:
        row_of_tile = lambda i: jnp.where(i >= n_main, ctx_row, i // tiles_per_seq)
        in_specs.append(pl.BlockSpec((tm, d), lambda i: (jnp.maximum(i - n_main, 0), 0)))
        args.append(ctx)
    in_specs += [pl.BlockSpec((None, 1, N_MOD * d), lambda i: (row_of_tile(i), 0, 0)),
                 _resident((1, d)), _resident((d, f)), _resident((d, f)), _resident((f, d))]
    args += [m3, nw.reshape(1, d), w1, w3, w2]
    if proj is not None:
        n2w, win, wlow, gw, gbias = proj
        in_specs += [_resident((1, d))] + [_resident(a.shape) for a in (win, wlow, gw, gbias)]
        args += [n2w.reshape(1, d), win, wlow, gw, gbias]
        if rope is not None:
            rowtab, coltab = rope
            in_specs += [pl.BlockSpec((tm // GRID_W, 2 * LANE), lambda i: (main(i) % tiles_per_seq, 0)),
                         _resident(coltab.shape)]
            args += [rowtab, coltab]
    if final_w is not None:
        in_specs.append(_resident((1, d)))
        args.append(final_w.reshape(1, d))
    out_specs = [tile(d)]
    out_shape = [jax.ShapeDtypeStruct((t, d), F32)]
    if proj is not None:
        for w, dt in ((RET_W, BF16),) * 4 + ((GLA_QK, BF16),) * 2 + ((GLA_W, BF16),) * 2 + ((GLA_QK, F32),) * 2:
            out_specs.append(tile(w))
            out_shape.append(jax.ShapeDtypeStruct((t, w), dt))
    cast_specs, cast_shapes = _cast_jobs(cast, n_main)
    in_specs += cast_specs
    args += list(cast)
    out_specs += cast_specs
    out_shape += cast_shapes
    kern = functools.partial(_ffn_kernel, mod_off=mod_off, proj=proj is not None,
                             rope=rope is not None, final=final_w is not None,
                             n_main=None if ctx is None else n_main, n_cast=len(cast), d=d, f=f)
    return pl.pallas_call(
        kern, grid=(n_tiles,), in_specs=in_specs, out_specs=out_specs, out_shape=out_shape,
        scratch_shapes=[pltpu.VMEM((tm, f), BF16)],
        compiler_params=pltpu.CompilerParams(dimension_semantics=("arbitrary",),
                                             vmem_limit_bytes=VMEM_LIMIT),
        name=name,
    )(*args)


def _head_lanes(shape, hd):
    lane = _iota(shape, len(shape) - 1)
    return (lane >= GLA_DK) if hd % 2 else (lane < GLA_DK)


STATE_RING = 3


def _state_kernel(*refs, batch, cps, steps, bcps, bsteps, first_chunk, nc):
    it = iter(refs)
    token_hbm = [next(it) for _ in range(6)]
    dec_ref = next(it)
    irf_ref, irb_ref, igf_ref, igb_ref = (next(it) for _ in range(4))
    orf_ref, orb_ref, ogf_ref, ogb_ref = (next(it) for _ in range(4))
    frf_ref, frb_ref, fgf_ref, fgb_ref = (next(it) for _ in range(4))
    srf, srb, sgf, sgb, ub_ret, ub_gla, ub_dec = (next(it) for _ in range(7))
    token_ring = [next(it) for _ in range(6)]
    sem = next(it)
    n = CHUNK
    step = pl.program_id(0)
    lg = _log_sigmoid(dec_ref[...])

    def token_copies(s, slot):
        out = []
        for a, (hbm, ring) in enumerate(zip(token_hbm, token_ring)):
            for b in range(batch):
                row0 = pl.multiple_of((first_chunk + b * nc + s * cps) * n, n)
                out.append(pltpu.make_async_copy(hbm.at[pl.ds(row0, cps * n), :], ring.at[slot, b],
                                                 sem.at[slot, a * batch + b]))
        return out

    @pl.when(step == 0)
    def _():
        srf[...] = irf_ref[...]
        srb[...] = irb_ref[...]
        sgf[...] = igf_ref[...]
        sgb[...] = igb_ref[...]
        for s in range(min(STATE_RING - 1, steps)):
            for cp in token_copies(s, s):
                cp.start()

    def reading_step(slot):
        rk_ref, rv_ref, gk_ref, gv_ref, gf_ref, gb_ref = ([ring.at[slot, b] for b in range(batch)]
                                                          for ring in token_ring)

        @pl.when(step + (STATE_RING - 1) < steps)
        def _():
            for cp in token_copies(step + (STATE_RING - 1), (slot + STATE_RING - 1) % STATE_RING):
                cp.start()

        for cp in token_copies(step, slot):
            cp.wait()
        row = _iota((n, LANE), 0).astype(F32)
        ri = _iota((n, n), 0)
        ci = _iota((n, n), 1)
        after = (ci > ri).astype(BF16)
        before = (ci < ri).astype(BF16)
        heads = [slice(hd * LANE, (hd + 1) * LANE) for hd in range(RET_HEADS)]
        to_end = [jnp.exp((n - 1.0 - row) * lg[0:1, sl]).astype(BF16) for sl in heads]
        to_start = [jnp.exp(row * lg[1:2, sl]).astype(BF16) for sl in heads]

        def gla_keys(b, rows, g_ref, tri, edge):
            g = g_ref[b][rows, :]
            e = _dot_select(tri, g)
            tot = e[edge:edge + 1, :] + g[edge:edge + 1, :]
            return gk_ref[b][rows, :] * jnp.exp(e).astype(BF16), jnp.exp(tot)

        def gla(b, rows, keys):
            kd, dec = keys
            upd = []
            for hd in range(GLA_HEADS):
                pr = slice((hd // 2) * LANE, (hd // 2 + 1) * LANE)
                u = _dot_tn(gv_ref[b][rows, hd * LANE:(hd + 1) * LANE], kd[:, pr])
                upd.append(jnp.where(_head_lanes(u.shape, hd), u, 0.0))
            return upd, dec

        for j in range(cps):
            chunk = step * cps + j
            rows = slice(j * n, (j + 1) * n)
            orf_ref[:, j] = srf[...].astype(BF16)
            ogf_ref[:, j] = sgf[...].astype(BF16)
            keys_f = [gla_keys(b, rows, gf_ref, after, 0) for b in range(batch)]
            keys_b = [gla_keys(b, rows, gb_ref, before, n - 1) for b in range(batch)]
            for b in range(batch):
                for hd in range(RET_HEADS):
                    sl = slice(hd * LANE, (hd + 1) * LANE)
                    k = rk_ref[b][rows, sl]
                    v = rv_ref[b][rows, sl]
                    srf[b, hd] = srf[b, hd] * jnp.exp(n * lg[0:1, sl]) + _dot_tn(k * to_end[hd], v)
                    ub_ret[chunk, b, hd] = _dot_tn(k * to_start[hd], v).astype(BF16)
                upd, dec = gla(b, rows, keys_f[b])
                for hd in range(GLA_HEADS):
                    pr = slice((hd // 2) * LANE, (hd // 2 + 1) * LANE)
                    sgf[b, hd] = sgf[b, hd] * dec[:, pr] + upd[hd]
                upd, dec = gla(b, rows, keys_b[b])
                for hd in range(GLA_HEADS):
                    ub_gla[chunk, b, hd] = upd[hd].astype(BF16)
                ub_dec[chunk, b] = jnp.broadcast_to(dec, ub_dec.shape[2:])
        frf_ref[...] = srf[...]
        fgf_ref[...] = sgf[...]

    for slot in range(min(STATE_RING, steps)):
        pl.when((step < steps) & (step % STATE_RING == slot))(functools.partial(reading_step, slot))

    @pl.when(step >= steps)
    def _():
        for j in range(bcps):
            jb = bcps - 1 - j
            chunk = (steps + bsteps - 1 - step) * bcps + jb
            orb_ref[:, jb] = srb[...].astype(BF16)
            ogb_ref[:, jb] = sgb[...].astype(BF16)
            for b in range(batch):
                for hd in range(RET_HEADS):
                    lgb = lg[1:2, hd * LANE:(hd + 1) * LANE]
                    srb[b, hd] = srb[b, hd] * jnp.exp(n * lgb) + ub_ret[chunk, b, hd].astype(F32)
                for hd in range(GLA_HEADS):
                    pr = slice((hd // 2) * LANE, (hd // 2 + 1) * LANE)
                    sgb[b, hd] = sgb[b, hd] * ub_dec[chunk, b, 0:1, pr] + ub_gla[chunk, b, hd].astype(F32)
        frb_ref[...] = srb[...]
        fgb_ref[...] = sgb[...]


def _states(rk, rv, gk, gv, gf, gb, dec, init, batch, first_chunk, nc):
    cps = 2 if nc % 2 == 0 and first_chunk % 2 == 0 else 1
    bcps = next(c for c in (8, 4, 2, 1) if nc % c == 0)
    steps, bsteps = nc // cps, nc // bcps
    reading = lambda c: jnp.minimum(c, steps - 1)
    scanning = lambda c: bsteps - 1 - jnp.maximum(c - steps, 0)

    tokens = (rk, rv, gk, gv, gf, gb)
    st_shape = (batch, RET_HEADS, LANE, LANE)
    init_spec = pl.BlockSpec(st_shape, lambda c: (0, 0, 0, 0))
    of_spec = pl.BlockSpec((batch, cps, RET_HEADS, LANE, LANE), lambda c: (0, reading(c), 0, 0, 0))
    ob_spec = pl.BlockSpec((batch, bcps, RET_HEADS, LANE, LANE), lambda c: (0, scanning(c), 0, 0, 0))
    per_chunk = jax.ShapeDtypeStruct((batch, nc, RET_HEADS, LANE, LANE), BF16)
    final = jax.ShapeDtypeStruct(st_shape, F32)
    kept = (nc, batch, RET_HEADS, LANE, LANE)
    return pl.pallas_call(
        functools.partial(_state_kernel, batch=batch, cps=cps, steps=steps, bcps=bcps, bsteps=bsteps,
                          first_chunk=first_chunk, nc=nc),
        grid=(steps + bsteps,),
        in_specs=[pl.BlockSpec(memory_space=pl.ANY)] * len(tokens)
        + [pl.BlockSpec(dec.shape, lambda c: (0, 0))] + [init_spec] * 4,
        out_specs=[of_spec, ob_spec, of_spec, ob_spec] + [init_spec] * 4,
        out_shape=[per_chunk] * 4 + [final] * 4,
        scratch_shapes=[pltpu.VMEM(st_shape, F32)] * 4
        + [pltpu.VMEM(kept, BF16), pltpu.VMEM(kept, BF16), pltpu.VMEM((nc, batch, SUBLANES, GLA_QK), F32)]
        + [pltpu.VMEM((STATE_RING, batch, cps * CHUNK, a.shape[1]), a.dtype) for a in tokens]
        + [pltpu.SemaphoreType.DMA((STATE_RING, len(tokens) * batch))],
        compiler_params=pltpu.CompilerParams(dimension_semantics=("arbitrary",),
                                             vmem_limit_bytes=VMEM_LIMIT),
        name="states",
    )(*tokens, dec, *init)


def _block_row(x, parent, r):
    n, w = x.shape
    if parent == n:
        return jnp.broadcast_to(x[r:r + 1, :], (n, w))
    x3 = x.reshape(n // parent, parent, w)
    return jnp.broadcast_to(x3[:, r:r + 1, :], x3.shape).reshape(n, w)


def _gla_levels(q, k_even, k_odd, gf, gb, bf, bb):
    n = q.shape[0]
    row = _iota(q.shape, 0)

    def factors(u_exp, w_exp):
        ew = jnp.exp2(w_exp).astype(BF16)
        return (q * jnp.exp2(u_exp).astype(BF16), k_even * ew, k_odd * ew)

    levels = []
    s = n // 2
    while s >= 2:
        if s >= 4:
            last_of_first = _block_row(bf, 2 * s, s - 1)
            first_of_second = _block_row(bb, 2 * s, s)
        else:
            upper = (row & 4) != 0
            last_of_first = jnp.where(upper, _block_row(bf, SUBLANES, 5), _block_row(bf, SUBLANES, 1))
            first_of_second = jnp.where(upper, _block_row(bb, SUBLANES, 6), _block_row(bb, SUBLANES, 2))
        df = bf - last_of_first
        db = bb - first_of_second
        levels.append((2 * s, [factors(jnp.minimum(df, db), -jnp.maximum(df, db))]))
        s //= 2
    odd = (row & 1) == 1
    qa = q * jnp.where(odd, jnp.exp2(gf), 2.0).astype(BF16)
    qb = q * jnp.where(odd, 2.0, jnp.exp2(gb)).astype(BF16)
    zero = jnp.zeros_like(k_even)
    ev = lambda x: jnp.where(odd, zero, x)
    od = lambda x: jnp.where(odd, x, zero)
    levels.append((2, [(qa, ev(k_even), ev(k_odd)), (qb, od(k_even), od(k_odd))]))
    return levels


def _mix_kernel(rq_ref, rk_ref, rv_ref, rg_ref, gq_ref, gk_ref, gv_ref, gg_ref, gf_ref, gb_ref,
                srf_ref, srb_ref, sgf_ref, sgb_ref, dec_ref, rnw_ref, gnw_ref,
                x_ref, m_ref, wout_ref, o_ref, mix_ref, decay_ref, qdec_ref, *, d):
    n = CHUNK
    half = n // 2
    ri = _iota((n, n), 0)
    ci = _iota((n, n), 1)
    step = pl.program_id(0)
    last_step = pl.num_programs(0) - 1
    cur = step % 2
    code = _iota((half, n), 0) ^ (_iota((half, n), 1) & (half - 1))

    @pl.when(step == 0)
    def _():
        mix_ref[1] = jnp.zeros(mix_ref.shape[1:], BF16)
        lg = _log_sigmoid(dec_ref[...])
        dist = (ri - ci).astype(F32)
        row = _iota((n, LANE), 0).astype(F32)
        for hd in range(RET_HEADS):
            sl = slice(hd * LANE, (hd + 1) * LANE)
            lgf = lg[0:1, sl]
            lgb = lg[1:2, sl]
            decay_ref[hd] = jnp.exp(jnp.where(dist > 0, dist * lgf[:, 0:1],
                                              jnp.where(dist < 0, -dist * lgb[:, 0:1], jnp.log(2.0))))
            qdec_ref[hd] = jnp.exp((row + 1.0) * lgf)
            qdec_ref[RET_HEADS + hd] = jnp.exp((n - row) * lgb)

    def retention(c):
        tok = slice(c * n, (c + 1) * n)
        for hd in range(RET_HEADS):
            sl = slice(hd * LANE, (hd + 1) * LANE)
            q = rq_ref[tok, sl]
            p = (_dot_nt(q, rk_ref[tok, sl]) * decay_ref[hd]).astype(BF16)
            qf = q.astype(F32)
            qs = jnp.concatenate([(qf * qdec_ref[hd]).astype(BF16),
                                  (qf * qdec_ref[RET_HEADS + hd]).astype(BF16)], axis=1)
            st = jnp.concatenate([srf_ref[c, hd], srb_ref[c, hd]], axis=0)
            o = _dot(p, rv_ref[tok, sl]) + _dot(qs, st)
            mu = jnp.mean(o, axis=-1, keepdims=True)
            oc = o - mu
            var = jnp.mean(oc * oc, axis=-1, keepdims=True)
            r = oc * lax.rsqrt(var + EPS) * rnw_ref[:, sl] * rg_ref[tok, sl].astype(F32)
            mix_ref[cur, tok, sl] = r.astype(BF16)

    def gla_factors(c):
        tok = slice(c * n, (c + 1) * n)
        gf = gf_ref[tok, :] * LOG2E
        gb = gb_ref[tok, :] * LOG2E
        bf = _dot_select((ci <= ri).astype(BF16), gf)
        bb = _dot_select((ci >= ri).astype(BF16), gb)
        q = gq_ref[tok, :]
        k = gk_ref[tok, :]
        even_head = (_iota(k.shape, 1) & GLA_DK) == 0
        no_k = jnp.zeros_like(k)
        levels = _gla_levels(q, jnp.where(even_head, k, no_k), jnp.where(even_head, no_k, k), gf, gb, bf, bb)
        return levels, q * jnp.exp2(bf).astype(BF16), q * jnp.exp2(bb).astype(BF16)

    def finish_previous():
        o_ref[...] = x_ref[...] + m_ref[:, 5 * d:6 * d] * _dot(mix_ref[1 - cur], wout_ref[...])

    def gla_outputs(c, levels, qsf, qsb):
        tok = slice(c * n, (c + 1) * n)
        for pair in range(GLA_HEADS // 2):
            pr = slice(pair * LANE, (pair + 1) * LANE)
            near = [None, None]
            far = [None, None]
            for size, blocks in levels:
                for rb in range(2):
                    kb = rb if size < n else 1 - rb
                    rows = slice(rb * half, (rb + 1) * half)
                    keys = slice(kb * half, (kb + 1) * half)
                    u = jnp.concatenate([blk[0][rows, pr] for blk in blocks], axis=1)
                    w = jnp.concatenate([jnp.concatenate([blk[j][keys, pr] for blk in blocks], axis=1)
                                         for j in (1, 2)], axis=0)
                    t = _dot_nt(u, w)
                    if size == n:
                        far[rb] = t
                    else:
                        near[rb] = t if near[rb] is None else jnp.where(code < size, t, near[rb])
            qcat = jnp.concatenate([qsf[:, pr], qsb[:, pr]], axis=1)
            for j in range(2):
                hd = 2 * pair + j
                sl = slice(hd * LANE, (hd + 1) * LANE)
                mine = slice(j * half, (j + 1) * half)
                p = jnp.concatenate([jnp.concatenate([near[0][:, mine], far[0][:, mine]], axis=1),
                                     jnp.concatenate([far[1][:, mine], near[1][:, mine]], axis=1)], axis=0)
                st = jnp.concatenate([sgf_ref[c, hd], sgb_ref[c, hd]], axis=1)
                o = _dot(p.astype(BF16), gv_ref[tok, sl]) + _dot_nt(qcat, st)
                ms = jnp.mean(o * o, axis=-1, keepdims=True)
                r = o * lax.rsqrt(ms + EPS) * gnw_ref[:, sl] * gg_ref[tok, sl].astype(F32)
                mix_ref[cur, tok, RET_W + hd * LANE:RET_W + (hd + 1) * LANE] = r.astype(BF16)

    @pl.when(step < last_step)
    def _():
        factors = []
        for c in range(MIX_CHUNKS):
            factors.append(gla_factors(c))
            if c == 0:
                finish_previous()
            retention(c)
        for c in range(MIX_CHUNKS):
            gla_outputs(c, *factors[c])

    @pl.when(step == last_step)
    def _():
        finish_previous()


def _mix_call(mix_in, states, dec, rnw, gnw, x1, m3, wout, batch, nc):
    assert nc % MIX_CHUNKS == 0
    per_seq = nc // MIX_CHUNKS
    n_steps = batch * per_seq
    rows = MIX_CHUNKS * CHUNK
    t, d = n_steps * rows, x1.shape[1]
    mixed = lambda i: jnp.minimum(i, n_steps - 1)
    done = lambda i: jnp.maximum(i - 1, 0)
    tile = lambda w: pl.BlockSpec((rows, w), lambda i: (mixed(i), 0))
    st_spec = pl.BlockSpec((None, MIX_CHUNKS, RET_HEADS, LANE, LANE),
                           lambda i: (mixed(i) // per_seq, mixed(i) % per_seq, 0, 0, 0))
    const = lambda shape: pl.BlockSpec(shape, lambda i: (0,) * len(shape))
    widths = (RET_W,) * 4 + (GLA_QK,) * 2 + (GLA_W,) * 2 + (GLA_QK,) * 2
    return pl.pallas_call(
        functools.partial(_mix_kernel, d=d),
        grid=(n_steps + 1,),
        in_specs=[tile(w) for w in widths] + [st_spec] * 4
        + [const(dec.shape), const((1, RET_W)), const((1, GLA_W)),
           pl.BlockSpec((rows, d), lambda i: (done(i), 0)),
           pl.BlockSpec((None, 1, N_MOD * d), lambda i: (done(i) // per_seq, 0, 0)),
           const(wout.shape)],
        out_specs=pl.BlockSpec((rows, d), lambda i: (done(i), 0)),
        out_shape=jax.ShapeDtypeStruct((t, d), F32),
        scratch_shapes=[pltpu.VMEM((2, rows, RET_W + GLA_W), BF16),
                        pltpu.VMEM((RET_HEADS, CHUNK, CHUNK), F32),
                        pltpu.VMEM((2 * RET_HEADS, CHUNK, LANE), F32)],
        compiler_params=pltpu.CompilerParams(dimension_semantics=("arbitrary",),
                                             vmem_limit_bytes=VMEM_LIMIT),
        name="mix",
    )(*mix_in, *states, dec, rnw.reshape(1, RET_W), gnw.reshape(1, GLA_W), x1, m3, wout)


def _rope_tables(n_tok):
    freqs = ROPE_BASE ** (-jnp.arange(RET_DK // 4, dtype=F32) / (RET_DK // 4))

    def table(n_pos, first_half):
        ang = jnp.arange(n_pos, dtype=F32)[:, None] * freqs
        zero = jnp.zeros((n_pos, LANE // 2), F32)
        cos = jnp.concatenate([jnp.cos(ang)] * 2, axis=-1)
        sin = jnp.concatenate([-jnp.sin(ang), jnp.sin(ang)], axis=-1)
        halves = (cos, zero, sin, zero) if first_half else (zero, cos, zero, sin)
        return jnp.concatenate(halves, axis=-1)

    return table(n_tok // GRID_W, True), table(GRID_W, False)


def _pack_gate(w_f, b_f, w_b, b_b):
    gw = jnp.zeros((LANE, 2 * GLA_QK), F32)
    gw = gw.at[:GLA_RANK, :GLA_QK].set(w_f).at[GLA_RANK:2 * GLA_RANK, GLA_QK:].set(w_b)
    return gw.astype(BF16), jnp.concatenate([b_f, b_b]).reshape(1, 2 * GLA_QK)


def kernel(x, c, ctx, c_ctx, ada_w, ada_b, norm1_w, ffn1_w1, ffn1_w3, ffn1_w2, norm2_w, w_in,
           ret_decay_f, ret_decay_b, ret_norm_w, gla_gate_w_f, gla_gate_b_f, gla_gate_w_b, gla_gate_b_b,
           gla_norm_w, w_out, norm3_w, ffn2_w1, ffn2_w3, ffn2_w2, final_norm_w):
    batch, n_tok, d = x.shape
    n_ctx = ctx.shape[1]
    depth = ada_w.shape[0]
    assert depth == 1 and batch + 1 <= SUBLANES
    assert n_tok % FFN_TILE == 0 and n_tok % CHUNK == 0 and n_ctx % CHUNK == 0
    assert (batch * n_ctx) % FFN_TILE == 0

    cvec = jnp.zeros((SUBLANES, d), F32).at[:batch].set(c).at[batch].set(c_ctx)
    m, w1a, w3a, w2a, w_in_b, w_low_b = _modulation(cvec, ada_w[0], ada_b[0],
                                                    (ffn1_w1[0], ffn1_w3[0], ffn1_w2[0]), w_in[0].T)
    m3 = m.reshape(SUBLANES, 1, N_MOD * d)

    gw, gbias = _pack_gate(gla_gate_w_f[0], gla_gate_b_f[0], gla_gate_w_b[0], gla_gate_b_b[0])
    proj = (norm2_w[0], w_in_b, w_low_b, gw, gbias)
    f1 = (norm1_w[0], w1a, w3a, w2a)
    rowtab, coltab = _rope_tables(n_tok)
    dec = jnp.zeros((SUBLANES, RET_W), F32)
    dec = dec.at[0].set(jnp.repeat(ret_decay_f[0], LANE)).at[1].set(jnp.repeat(ret_decay_b[0], LANE))

    tiles_per_seq = n_tok // FFN_TILE
    first = _ffn_call(x.reshape(batch * n_tok, d), m3, tiles_per_seq, *f1, mod_off=0, tm=FFN_TILE,
                      ctx=ctx.reshape(batch * n_ctx, d), ctx_row=batch, proj=proj, rope=(rowtab, coltab),
                      cast=(ffn2_w1[0], ffn2_w3[0], ffn2_w2[0], w_out[0]), name="ffn_in")
    x1, mix_in, (w1b, w3b, w2b, woutb) = first[0], first[1:11], first[11:]
    scan_in = (mix_in[1], mix_in[2], mix_in[5], mix_in[6], mix_in[8], mix_in[9])
    zero = jnp.zeros((batch, RET_HEADS, LANE, LANE), F32)
    nc, nc_ctx = n_tok // CHUNK, n_ctx // CHUNK
    ctx_states = _states(*scan_in, dec, (zero,) * 4, batch, batch * nc, nc_ctx)[4:]
    states = _states(*scan_in, dec, ctx_states, batch, 0, nc)[:4]
    x2 = _mix_call(mix_in, states, dec, ret_norm_w[0], gla_norm_w[0], x1, m3, woutb, batch, nc)
    out = _ffn_call(x2, m3, tiles_per_seq, norm3_w[0], w1b, w3b, w2b, mod_off=6, tm=FFN_TILE,
                    final_w=final_norm_w, name="ffn_out")[0]
    return out.reshape(batch, n_tok, d)
```

```python
import functools

import jax
import jax.numpy as jnp
from jax import lax
from jax.experimental import pallas as pl
from jax.experimental.pallas import tpu as pltpu

F32 = jnp.float32
BF16 = jnp.bfloat16

EPS = 1e-6
LOG2E = 1.4426950408889634
N_MOD = 9
GRID_W = 64
ROPE_BASE = 10000.0
RET_HEADS = 4
RET_DK = 128
RET_DV = 128
GLA_HEADS = 4
GLA_DK = 64
GLA_DV = 128
GLA_RANK = 16
GLA_TAU = 16.0
RET_W = RET_HEADS * RET_DV
GLA_W = GLA_HEADS * GLA_DV
GLA_QK = GLA_HEADS * GLA_DK

LANE = 128
SUBLANES = 8
BF16_ROWS = 16
MOD_STEPS = 8
CHUNK = 256
MIX_CHUNKS = 2
FFN_TILE = 512
ROW_BLOCK = 256
FF_CHUNK = 256
VMEM_LIMIT = 60 * 1024 * 1024

C_RQ, C_RK, C_RV, C_RG = 0, 512, 1024, 1536
C_GQ, C_GK, C_GV, C_GG = 2048, 2304, 2560, 3072
C_LOW = 3584


def _silu(x):
    return x * (1.0 / (1.0 + jnp.exp(-x)))


def _log_sigmoid(z):
    return jnp.minimum(z, 0.0) - jnp.log(1.0 + jnp.exp(-jnp.abs(z)))


def _rms(x, w):
    return x * lax.rsqrt(jnp.mean(x * x, axis=-1, keepdims=True) + EPS) * w


def _dot(a, b):
    return jnp.dot(a, b, preferred_element_type=F32)


def _dot_nt(a, b):
    return lax.dot_general(a, b, (((1,), (1,)), ((), ())), preferred_element_type=F32)


def _dot_tn(a, b):
    return lax.dot_general(a, b, (((0,), (0,)), ((), ())), preferred_element_type=F32)


def _dot_select(sel, x):
    hi = x.astype(BF16)
    lo = (x - hi.astype(F32)).astype(BF16)
    return _dot(sel, hi) + _dot(sel, lo)


def _iota(shape, dim):
    return lax.broadcasted_iota(jnp.int32, shape, dim)


def _resident(shape):
    nd = len(shape)
    return pl.BlockSpec(shape, lambda *_: (0,) * nd, pipeline_mode=pl.Buffered(1))


def _cast_jobs(arrays, n_steps):
    specs, shapes = [], []
    for w in arrays:
        rows = next(r for r in range(BF16_ROWS, w.shape[0] + 1, BF16_ROWS)
                    if w.shape[0] % r == 0 and w.shape[0] // r <= n_steps)
        specs.append(pl.BlockSpec((rows, w.shape[1]),
                                  lambda i, last=w.shape[0] // rows - 1: (jnp.minimum(i, last), 0)))
        shapes.append(jax.ShapeDtypeStruct(w.shape, BF16))
    return specs, shapes


def _run_cast_jobs(src_refs, dst_refs):
    for src_ref, dst_ref in zip(src_refs, dst_refs):
        dst_ref[...] = src_ref[...].astype(BF16)


W_IN_BLOCK = 512


def _mod_kernel(*refs, n_cast):
    c_ref, w_ref, b_ref = refs[:3]
    cast_in, (wt_ref, wt_low_ref) = refs[3:3 + n_cast], refs[3 + n_cast:5 + n_cast]
    o_ref = refs[5 + n_cast]
    cast_out, (win_ref, wlow_ref) = refs[6 + n_cast:6 + 2 * n_cast], refs[6 + 2 * n_cast:]
    cond = _silu(c_ref[...]).astype(BF16)
    o_ref[...] = _dot(cond, w_ref[...].astype(BF16)) + b_ref[...]
    _run_cast_jobs(cast_in, cast_out)
    win_ref[...] = wt_ref[...].T.astype(BF16)
    low = wt_low_ref[...]
    low = jnp.concatenate([low, jnp.zeros((LANE - low.shape[0], low.shape[1]), F32)], axis=0)
    wlow_ref[...] = low.T.astype(BF16)


def _modulation(cvec, ada_w, ada_b, cast, w_in_t):
    d, n = ada_w.shape
    bn = n // MOD_STEPS if n % (MOD_STEPS * LANE) == 0 else d
    steps = n // bn
    assert C_LOW % W_IN_BLOCK == 0 and C_LOW // W_IN_BLOCK <= steps
    n_low = w_in_t.shape[0] - C_LOW
    last = C_LOW // W_IN_BLOCK - 1
    cast_specs, cast_shapes = _cast_jobs(cast, steps)
    return pl.pallas_call(
        functools.partial(_mod_kernel, n_cast=len(cast)),
        grid=(steps,),
        in_specs=[pl.BlockSpec((SUBLANES, d), lambda j: (0, 0)),
                  pl.BlockSpec((d, bn), lambda j: (0, j)),
                  pl.BlockSpec((1, bn), lambda j: (0, j))] + cast_specs
        + [pl.BlockSpec((W_IN_BLOCK, d), lambda j: (jnp.minimum(j, last), 0)),
           pl.BlockSpec((n_low, d), lambda j: (C_LOW // n_low, 0))],
        out_specs=[pl.BlockSpec((SUBLANES, bn), lambda j: (0, j))] + cast_specs
        + [pl.BlockSpec((d, W_IN_BLOCK), lambda j: (0, jnp.minimum(j, last))),
           pl.BlockSpec((d, LANE), lambda j: (0, 0))],
        out_shape=[jax.ShapeDtypeStruct((SUBLANES, n), F32)] + cast_shapes
        + [jax.ShapeDtypeStruct((d, C_LOW), BF16), jax.ShapeDtypeStruct((d, LANE), BF16)],
        compiler_params=pltpu.CompilerParams(dimension_semantics=("arbitrary",),
                                             vmem_limit_bytes=VMEM_LIMIT),
        name="mod",
    )(cvec, ada_w, ada_b.reshape(1, n), *cast, w_in_t, w_in_t)


def _swap32(x):
    lane = _iota(x.shape, 1)
    quarter = RET_DK // 4
    first = (lane & (2 * quarter - 1)) < quarter
    return jnp.where(first, pltpu.roll(x, LANE - quarter, 1), pltpu.roll(x, quarter, 1))


def _ffn_kernel(*refs, mod_off, proj, rope, final, n_main, n_cast, d, f):
    it = iter(refs)
    x_ref = next(it)
    if n_main is not None:
        xc_ref = next(it)
        is_ctx = pl.program_id(0) >= n_main
    m_ref, nw_ref, w1_ref, w3_ref, w2_ref = (next(it) for _ in range(5))
    if proj:
        n2w_ref, win_ref, wlow_ref, gw_ref, gbias_ref = (next(it) for _ in range(5))
        if rope:
            rowtab_ref, coltab_ref = next(it), next(it)
    if final:
        fnw_ref = next(it)
    cast_in = [next(it) for _ in range(n_cast)]
    xo_ref = next(it)
    if proj:
        (rq_ref, rk_ref, rv_ref, rg_ref, gq_ref, gk_ref, gv_ref, gg_ref,
         gf_ref, gb_ref) = (next(it) for _ in range(10))
    cast_out = [next(it) for _ in range(n_cast)]
    u_ref = next(it)
    _run_cast_jobs(cast_in, cast_out)

    def mod(i):
        return m_ref[:, (mod_off + i) * d:(mod_off + i + 1) * d]

    h2s = []
    for rb in range(x_ref.shape[0] // ROW_BLOCK):
        rows = slice(rb * ROW_BLOCK, (rb + 1) * ROW_BLOCK)
        x = x_ref[rows, :]
        if n_main is not None:
            x = jnp.where(is_ctx, xc_ref[rows, :], x)
        h = (_rms(x, nw_ref[...]) * (1.0 + mod(1)) + mod(0)).astype(BF16)
        for k in range(f // FF_CHUNK):
            sl = slice(k * FF_CHUNK, (k + 1) * FF_CHUNK)
            a = _dot(h, w1_ref[:, sl])
            g = _dot(h, w3_ref[:, sl])
            u_ref[rows, sl] = (_silu(a) * g).astype(BF16)
        y = _dot(u_ref[rows, :], w2_ref[...])
        x1 = x + (0.5 * mod(2)) * y

        if final:
            xo_ref[rows, :] = _rms(x1, fnw_ref[...])
        else:
            xo_ref[rows, :] = x1

        if proj:
            h2s.append((_rms(x1, n2w_ref[...]) * (1.0 + mod(4)) + mod(3)).astype(BF16))

    for rb, h2 in enumerate(h2s):
        rows = slice(rb * ROW_BLOCK, (rb + 1) * ROW_BLOCK)

        def p(lo, hi):
            return _dot(h2, win_ref[:, lo:hi])

        if rope:
            by_row = _iota((GRID_W, 2 * LANE), 1) % LANE < LANE // 2
            g0 = rb * (ROW_BLOCK // GRID_W)
            tab = jnp.concatenate(
                [jnp.where(by_row, jnp.broadcast_to(rowtab_ref[g0 + g:g0 + g + 1, :], (GRID_W, 2 * LANE)),
                           coltab_ref[...]) for g in range(ROW_BLOCK // GRID_W)], axis=0)
            cos, sin = tab[:, :LANE], tab[:, LANE:]
            if n_main is not None:
                cos = jnp.where(is_ctx, 1.0, cos)
                sin = jnp.where(is_ctx, 0.0, sin)

        low = _dot(h2, wlow_ref[...]).astype(BF16)
        z = _dot(low, gw_ref[...]) + gbias_ref[...]
        ls = _log_sigmoid(z) * (1.0 / GLA_TAU)
        gf_ref[rows, :] = ls[:, :GLA_QK]
        gb_ref[rows, :] = ls[:, GLA_QK:]
        for base, scale, o_ref in ((C_RQ, RET_DK ** -0.5, rq_ref), (C_RK, 1.0, rk_ref)):
            t = p(base, base + RET_W)
            for hd in range(RET_HEADS):
                th = t[:, hd * LANE:(hd + 1) * LANE] * scale
                if rope:
                    th = th * cos + _swap32(th) * sin
                o_ref[rows, hd * LANE:(hd + 1) * LANE] = th.astype(BF16)
        rg_ref[rows, :] = _silu(p(C_RG, C_RG + RET_W)).astype(BF16)
        gg_ref[rows, :] = _silu(p(C_GG, C_GG + GLA_W)).astype(BF16)
        gq_ref[rows, :] = (p(C_GQ, C_GQ + GLA_QK) * GLA_DK ** -0.5).astype(BF16)
        gk_ref[rows, :] = p(C_GK, C_GK + GLA_QK).astype(BF16)
        rv_ref[rows, :] = p(C_RV, C_RV + RET_W).astype(BF16)
        gv_ref[rows, :] = p(C_GV, C_GV + GLA_W).astype(BF16)


def _ffn_call(x, m3, tiles_per_seq, nw, w1, w3, w2, *, mod_off, tm, ctx=None, ctx_row=None, proj=None,
              rope=None, final_w=None, cast=(), name):
    t, d = x.shape
    f = w1.shape[1]
    n_main = t // tm
    n_tiles = n_main + (0 if ctx is None else ctx.shape[0] // tm)
    t = n_tiles * tm
    main = lambda i: jnp.minimum(i, n_main - 1)
    tile = lambda w: pl.BlockSpec((tm, w), lambda i: (i, 0))
    in_specs = [pl.BlockSpec((tm, d), lambda i: (main(i), 0))]
    args = [x]
    if ctx is None:
        row_of_tile = lambda i: i // tiles_per_seq
    else:
        row_of_tile = lambda i: jnp.where(i >= n_main, ctx_row, i // tiles_per_seq)
        in_specs.append(pl.BlockSpec((tm, d), lambda i: (jnp.maximum(i - n_main, 0), 0)))
        args.append(ctx)
    in_specs += [pl.BlockSpec((None, 1, N_MOD * d), lambda i: (row_of_tile(i), 0, 0)),
                 _resident((1, d)), _resident((d, f)), _resident((d, f)), _resident((f, d))]
    args += [m3, nw.reshape(1, d), w1, w3, w2]
    if proj is not None:
        n2w, win, wlow, gw, gbias = proj
        in_specs += [_resident((1, d))] + [_resident(a.shape) for a in (win, wlow, gw, gbias)]
        args += [n2w.reshape(1, d), win, wlow, gw, gbias]
        if rope is not None:
            rowtab, coltab = rope
            in_specs += [pl.BlockSpec((tm // GRID_W, 2 * LANE), lambda i: (main(i) % tiles_per_seq, 0)),
                         _resident(coltab.shape)]
            args += [rowtab, coltab]
    if final_w is not None:
        in_specs.append(_resident((1, d)))
        args.append(final_w.reshape(1, d))
    out_specs = [tile(d)]
    out_shape = [jax.ShapeDtypeStruct((t, d), F32)]
    if proj is not None:
        for w, dt in ((RET_W, BF16),) * 4 + ((GLA_QK, BF16),) * 2 + ((GLA_W, BF16),) * 2 + ((GLA_QK, F32),) * 2:
            out_specs.append(tile(w))
            out_shape.append(jax.ShapeDtypeStruct((t, w), dt))
    cast_specs, cast_shapes = _cast_jobs(cast, n_main)
    in_specs += cast_specs
    args += list(cast)
    out_specs += cast_specs
    out_shape += cast_shapes
    kern = functools.partial(_ffn_kernel, mod_off=mod_off, proj=proj is not None,
                             rope=rope is not None, final=final_w is not None,
                             n_main=None if ctx is None else n_main, n_cast=len(cast), d=d, f=f)
    return pl.pallas_call(
        kern, grid=(n_tiles,), in_specs=in_specs, out_specs=out_specs, out_shape=out_shape,
        scratch_shapes=[pltpu.VMEM((tm, f), BF16)],
        compiler_params=pltpu.CompilerParams(dimension_semantics=("arbitrary",),
                                             vmem_limit_bytes=VMEM_LIMIT),
        name=name,
    )(*args)


def _head_lanes(shape, hd):
    lane = _iota(shape, len(shape) - 1)
    return (lane >= GLA_DK) if hd % 2 else (lane < GLA_DK)


STATE_RING = 3


def _state_kernel(*refs, batch, cps, steps, bcps, bsteps, ctx_chunk, nc):
    it = iter(refs)
    token_hbm = [next(it) for _ in range(6)]
    dec_ref = next(it)
    irf_ref, irb_ref, igf_ref, igb_ref = (next(it) for _ in range(4))
    orf_ref, orb_ref, ogf_ref, ogb_ref = (next(it) for _ in range(4))
    frf_ref, frb_ref, fgf_ref, fgb_ref = (next(it) for _ in range(4))
    srf, srb, sgf, sgb, ub_ret, ub_gla, ub_dec = (next(it) for _ in range(7))
    token_ring = [next(it) for _ in range(6)]
    sem = next(it)
    n = CHUNK
    step = pl.program_id(0)
    lg = _log_sigmoid(dec_ref[...])

    def token_copies(s, slot, ctx=False):
        out = []
        for a, (hbm, ring) in enumerate(zip(token_hbm, token_ring)):
            for b in range(batch):
                if ctx:
                    src, dst = hbm.at[pl.ds((ctx_chunk + b) * n, n), :], ring.at[slot, b, pl.ds(0, n), :]
                else:
                    row0 = pl.multiple_of((b * nc + (s - 1) * cps) * n, n)
                    src, dst = hbm.at[pl.ds(row0, cps * n), :], ring.at[slot, b]
                out.append(pltpu.make_async_copy(src, dst, sem.at[slot, a * batch + b]))
        return out

    @pl.when(step == 0)
    def _():
        srf[...] = irf_ref[...]
        srb[...] = irb_ref[...]
        sgf[...] = igf_ref[...]
        sgb[...] = igb_ref[...]
        for cp in token_copies(0, 0, ctx=True):
            cp.start()
        for s in range(1, min(STATE_RING - 1, steps + 1)):
            for cp in token_copies(s, s):
                cp.start()

    def reading_step(slot, ctx=False):
        rk_ref, rv_ref, gk_ref, gv_ref, gf_ref, gb_ref = ([ring.at[slot, b] for b in range(batch)]
                                                          for ring in token_ring)

        @pl.when(step + (STATE_RING - 1) <= steps)
        def _():
            for cp in token_copies(step + (STATE_RING - 1), (slot + STATE_RING - 1) % STATE_RING):
                cp.start()

        for cp in token_copies(step, slot, ctx):
            cp.wait()
        row = _iota((n, LANE), 0).astype(F32)
        ri = _iota((n, n), 0)
        ci = _iota((n, n), 1)
        after = (ci > ri).astype(BF16)
        before = (ci < ri).astype(BF16)
        heads = [slice(hd * LANE, (hd + 1) * LANE) for hd in range(RET_HEADS)]
        to_end = [jnp.exp((n - 1.0 - row) * lg[0:1, sl]).astype(BF16) for sl in heads]
        to_start = [jnp.exp(row * lg[1:2, sl]).astype(BF16) for sl in heads]

        def gla_keys(b, rows, g_ref, tri, edge):
            g = g_ref[b][rows, :]
            e = _dot_select(tri, g)
            tot = e[edge:edge + 1, :] + g[edge:edge + 1, :]
            return gk_ref[b][rows, :] * jnp.exp(e).astype(BF16), jnp.exp(tot)

        def gla(b, rows, keys):
            kd, dec = keys
            upd = []
            for hd in range(GLA_HEADS):
                pr = slice((hd // 2) * LANE, (hd // 2 + 1) * LANE)
                u = _dot_tn(gv_ref[b][rows, hd * LANE:(hd + 1) * LANE], kd[:, pr])
                upd.append(jnp.where(_head_lanes(u.shape, hd), u, 0.0))
            return upd, dec

        for j in range(1 if ctx else cps):
            chunk = (step - 1) * cps + j
            rows = slice(j * n, (j + 1) * n)
            if not ctx:
                orf_ref[:, j] = srf[...].astype(BF16)
                ogf_ref[:, j] = sgf[...].astype(BF16)
            keys_f = [gla_keys(b, rows, gf_ref, after, 0) for b in range(batch)]
            keys_b = [gla_keys(b, rows, gb_ref, before, n - 1) for b in range(batch)]
            for b in range(batch):
                for hd in range(RET_HEADS):
                    sl = slice(hd * LANE, (hd + 1) * LANE)
                    k = rk_ref[b][rows, sl]
                    v = rv_ref[b][rows, sl]
                    srf[b, hd] = srf[b, hd] * jnp.exp(n * lg[0:1, sl]) + _dot_tn(k * to_end[hd], v)
                    back = _dot_tn(k * to_start[hd], v)
                    if ctx:
                        srb[b, hd] = srb[b, hd] * jnp.exp(n * lg[1:2, sl]) + back
                    else:
                        ub_ret[chunk, b, hd] = back.astype(BF16)
                upd, dec = gla(b, rows, keys_f[b])
                for hd in range(GLA_HEADS):
                    pr = slice((hd // 2) * LANE, (hd // 2 + 1) * LANE)
                    sgf[b, hd] = sgf[b, hd] * dec[:, pr] + upd[hd]
                upd, dec = gla(b, rows, keys_b[b])
                for hd in range(GLA_HEADS):
                    pr = slice((hd // 2) * LANE, (hd // 2 + 1) * LANE)
                    if ctx:
                        sgb[b, hd] = sgb[b, hd] * dec[:, pr] + upd[hd]
                    else:
                        ub_gla[chunk, b, hd] = upd[hd].astype(BF16)
                if not ctx:
                    ub_dec[chunk, b] = jnp.broadcast_to(dec, ub_dec.shape[2:])
        frf_ref[...] = srf[...]
        fgf_ref[...] = sgf[...]

    pl.when(step == 0)(functools.partial(reading_step, 0, ctx=True))
    for slot in range(STATE_RING):
        pl.when((step >= 1) & (step <= steps) & (step % STATE_RING == slot))(functools.partial(reading_step, slot))

    @pl.when(step > steps)
    def _():
        for j in range(bcps):
            jb = bcps - 1 - j
            chunk = (steps + bsteps - step) * bcps + jb
            orb_ref[:, jb] = srb[...].astype(BF16)
            ogb_ref[:, jb] = sgb[...].astype(BF16)
            for b in range(batch):
                for hd in range(RET_HEADS):
                    lgb = lg[1:2, hd * LANE:(hd + 1) * LANE]
                    srb[b, hd] = srb[b, hd] * jnp.exp(n * lgb) + ub_ret[chunk, b, hd].astype(F32)
                for hd in range(GLA_HEADS):
                    pr = slice((hd // 2) * LANE, (hd // 2 + 1) * LANE)
                    sgb[b, hd] = sgb[b, hd] * ub_dec[chunk, b, 0:1, pr] + ub_gla[chunk, b, hd].astype(F32)
        frb_ref[...] = srb[...]
        fgb_ref[...] = sgb[...]


def _states(rk, rv, gk, gv, gf, gb, dec, init, batch, nc, ctx_chunk):
    cps = 2 if nc % 2 == 0 else 1
    bcps = next(c for c in (8, 4, 2, 1) if nc % c == 0)
    steps, bsteps = nc // cps, nc // bcps
    reading = lambda c: jnp.clip(c - 1, 0, steps - 1)
    scanning = lambda c: bsteps - 1 - jnp.maximum(c - steps - 1, 0)

    tokens = (rk, rv, gk, gv, gf, gb)
    st_shape = (batch, RET_HEADS, LANE, LANE)
    init_spec = pl.BlockSpec(st_shape, lambda c: (0, 0, 0, 0))
    of_spec = pl.BlockSpec((batch, cps, RET_HEADS, LANE, LANE), lambda c: (0, reading(c), 0, 0, 0))
    ob_spec = pl.BlockSpec((batch, bcps, RET_HEADS, LANE, LANE), lambda c: (0, scanning(c), 0, 0, 0))
    per_chunk = jax.ShapeDtypeStruct((batch, nc, RET_HEADS, LANE, LANE), BF16)
    final = jax.ShapeDtypeStruct(st_shape, F32)
    kept = (nc, batch, RET_HEADS, LANE, LANE)
    return pl.pallas_call(
        functools.partial(_state_kernel, batch=batch, cps=cps, steps=steps, bcps=bcps, bsteps=bsteps,
                          ctx_chunk=ctx_chunk, nc=nc),
        grid=(1 + steps + bsteps,),
        in_specs=[pl.BlockSpec(memory_space=pl.ANY)] * len(tokens)
        + [pl.BlockSpec(dec.shape, lambda c: (0, 0))] + [init_spec] * 4,
        out_specs=[of_spec, ob_spec, of_spec, ob_spec] + [init_spec] * 4,
        out_shape=[per_chunk] * 4 + [final] * 4,
        scratch_shapes=[pltpu.VMEM(st_shape, F32)] * 4
        + [pltpu.VMEM(kept, BF16), pltpu.VMEM(kept, BF16), pltpu.VMEM((nc, batch, SUBLANES, GLA_QK), F32)]
        + [pltpu.VMEM((STATE_RING, batch, cps * CHUNK, a.shape[1]), a.dtype) for a in tokens]
        + [pltpu.SemaphoreType.DMA((STATE_RING, len(tokens) * batch))],
        compiler_params=pltpu.CompilerParams(dimension_semantics=("arbitrary",),
                                             vmem_limit_bytes=VMEM_LIMIT),
        name="states",
    )(*tokens, dec, *init)


def _block_row(x, parent, r):
    n, w = x.shape
    if parent == n:
        return jnp.broadcast_to(x[r:r + 1, :], (n, w))
    x3 = x.reshape(n // parent, parent, w)
    return jnp.broadcast_to(x3[:, r:r + 1, :], x3.shape).reshape(n, w)


def _gla_levels(q, k_even, k_odd, gf, gb, bf, bb):
    n = q.shape[0]
    row = _iota(q.shape, 0)

    def factors(u_exp, w_exp):
        ew = jnp.exp2(w_exp).astype(BF16)
        return (q * jnp.exp2(u_exp).astype(BF16), k_even * ew, k_odd * ew)

    levels = []
    s = n // 2
    while s >= 2:
        if s >= 4:
            last_of_first = _block_row(bf, 2 * s, s - 1)
            first_of_second = _block_row(bb, 2 * s, s)
        else:
            upper = (row & 4) != 0
            last_of_first = jnp.where(upper, _block_row(bf, SUBLANES, 5), _block_row(bf, SUBLANES, 1))
            first_of_second = jnp.where(upper, _block_row(bb, SUBLANES, 6), _block_row(bb, SUBLANES, 2))
        df = bf - last_of_first
        db = bb - first_of_second
        levels.append((2 * s, [factors(jnp.minimum(df, db), -jnp.maximum(df, db))]))
        s //= 2
    odd = (row & 1) == 1
    qa = q * jnp.where(odd, jnp.exp2(gf), 2.0).astype(BF16)
    qb = q * jnp.where(odd, 2.0, jnp.exp2(gb)).astype(BF16)
    zero = jnp.zeros_like(k_even)
    ev = lambda x: jnp.where(odd, zero, x)
    od = lambda x: jnp.where(odd, x, zero)
    levels.append((2, [(qa, ev(k_even), ev(k_odd)), (qb, od(k_even), od(k_odd))]))
    return levels


def _mix_kernel(rq_ref, rk_ref, rv_ref, rg_ref, gq_ref, gk_ref, gv_ref, gg_ref, gf_ref, gb_ref,
                srf_ref, srb_ref, sgf_ref, sgb_ref, dec_ref, rnw_ref, gnw_ref,
                x_ref, m_ref, wout_ref, o_ref, mix_ref, decay_ref, qdec_ref, *, d):
    n = CHUNK
    half = n // 2
    ri = _iota((n, n), 0)
    ci = _iota((n, n), 1)
    step = pl.program_id(0)
    last_step = pl.num_programs(0) - 1
    cur = step % 2
    code = _iota((half, n), 0) ^ (_iota((half, n), 1) & (half - 1))

    @pl.when(step == 0)
    def _():
        mix_ref[1] = jnp.zeros(mix_ref.shape[1:], BF16)
        lg = _log_sigmoid(dec_ref[...])
        dist = (ri - ci).astype(F32)
        row = _iota((n, LANE), 0).astype(F32)
        for hd in range(RET_HEADS):
            sl = slice(hd * LANE, (hd + 1) * LANE)
            lgf = lg[0:1, sl]
            lgb = lg[1:2, sl]
            decay_ref[hd] = jnp.exp(jnp.where(dist > 0, dist * lgf[:, 0:1],
                                              jnp.where(dist < 0, -dist * lgb[:, 0:1], jnp.log(2.0))))
            qdec_ref[hd] = jnp.exp((row + 1.0) * lgf)
            qdec_ref[RET_HEADS + hd] = jnp.exp((n - row) * lgb)

    def retention(c):
        tok = slice(c * n, (c + 1) * n)
        for hd in range(RET_HEADS):
            sl = slice(hd * LANE, (hd + 1) * LANE)
            q = rq_ref[tok, sl]
            p = (_dot_nt(q, rk_ref[tok, sl]) * decay_ref[hd]).astype(BF16)
            qf = q.astype(F32)
            qs = jnp.concatenate([(qf * qdec_ref[hd]).astype(BF16),
                                  (qf * qdec_ref[RET_HEADS + hd]).astype(BF16)], axis=1)
            st = jnp.concatenate([srf_ref[c, hd], srb_ref[c, hd]], axis=0)
            o = _dot(p, rv_ref[tok, sl]) + _dot(qs, st)
            mu = jnp.mean(o, axis=-1, keepdims=True)
            oc = o - mu
            var = jnp.mean(oc * oc, axis=-1, keepdims=True)
            r = oc * lax.rsqrt(var + EPS) * rnw_ref[:, sl] * rg_ref[tok, sl].astype(F32)
            mix_ref[cur, tok, sl] = r.astype(BF16)

    def gla_factors(c):
        tok = slice(c * n, (c + 1) * n)
        gf = gf_ref[tok, :] * LOG2E
        gb = gb_ref[tok, :] * LOG2E
        bf = _dot_select((ci <= ri).astype(BF16), gf)
        bb = _dot_select((ci >= ri).astype(BF16), gb)
        q = gq_ref[tok, :]
        k = gk_ref[tok, :]
        even_head = (_iota(k.shape, 1) & GLA_DK) == 0
        no_k = jnp.zeros_like(k)
        levels = _gla_levels(q, jnp.where(even_head, k, no_k), jnp.where(even_head, no_k, k), gf, gb, bf, bb)
        return levels, q * jnp.exp2(bf).astype(BF16), q * jnp.exp2(bb).astype(BF16)

    def finish_previous():
        o_ref[...] = x_ref[...] + m_ref[:, 5 * d:6 * d] * _dot(mix_ref[1 - cur], wout_ref[...])

    def gla_outputs(c, levels, qsf, qsb):
        tok = slice(c * n, (c + 1) * n)
        for pair in range(GLA_HEADS // 2):
            pr = slice(pair * LANE, (pair + 1) * LANE)
            near = [None, None]
            far = [None, None]
            for size, blocks in levels:
                for rb in range(2):
                    kb = rb if size < n else 1 - rb
                    rows = slice(rb * half, (rb + 1) * half)
                    keys = slice(kb * half, (kb + 1) * half)
                    u = jnp.concatenate([blk[0][rows, pr] for blk in blocks], axis=1)
                    w = jnp.concatenate([jnp.concatenate([blk[j][keys, pr] for blk in blocks], axis=1)
                                         for j in (1, 2)], axis=0)
                    t = _dot_nt(u, w)
                    if size == n:
                        far[rb] = t
                    else:
                        near[rb] = t if near[rb] is None else jnp.where(code < size, t, near[rb])
            qcat = jnp.concatenate([qsf[:, pr], qsb[:, pr]], axis=1)
            for j in range(2):
                hd = 2 * pair + j
                sl = slice(hd * LANE, (hd + 1) * LANE)
                mine = slice(j * half, (j + 1) * half)
                p = jnp.concatenate([jnp.concatenate([near[0][:, mine], far[0][:, mine]], axis=1),
                                     jnp.concatenate([far[1][:, mine], near[1][:, mine]], axis=1)], axis=0)
                st = jnp.concatenate([sgf_ref[c, hd], sgb_ref[c, hd]], axis=1)
                o = _dot(p.astype(BF16), gv_ref[tok, sl]) + _dot_nt(qcat, st)
                ms = jnp.mean(o * o, axis=-1, keepdims=True)
                r = o * lax.rsqrt(ms + EPS) * gnw_ref[:, sl] * gg_ref[tok, sl].astype(F32)
                mix_ref[cur, tok, RET_W + hd * LANE:RET_W + (hd + 1) * LANE] = r.astype(BF16)

    @pl.when(step < last_step)
    def _():
        factors = []
        for c in range(MIX_CHUNKS):
            factors.append(gla_factors(c))
            if c == 0:
                finish_previous()
            retention(c)
        for c in range(MIX_CHUNKS):
            gla_outputs(c, *factors[c])

    @pl.when(step == last_step)
    def _():
        finish_previous()


def _mix_call(mix_in, states, dec, rnw, gnw, x1, m3, wout, batch, nc):
    assert nc % MIX_CHUNKS == 0
    per_seq = nc // MIX_CHUNKS
    n_steps = batch * per_seq
    rows = MIX_CHUNKS * CHUNK
    t, d = n_steps * rows, x1.shape[1]
    mixed = lambda i: jnp.minimum(i, n_steps - 1)
    done = lambda i: jnp.maximum(i - 1, 0)
    tile = lambda w: pl.BlockSpec((rows, w), lambda i: (mixed(i), 0))
    st_spec = pl.BlockSpec((None, MIX_CHUNKS, RET_HEADS, LANE, LANE),
                           lambda i: (mixed(i) // per_seq, mixed(i) % per_seq, 0, 0, 0))
    const = lambda shape: pl.BlockSpec(shape, lambda i: (0,) * len(shape))
    widths = (RET_W,) * 4 + (GLA_QK,) * 2 + (GLA_W,) * 2 + (GLA_QK,) * 2
    return pl.pallas_call(
        functools.partial(_mix_kernel, d=d),
        grid=(n_steps + 1,),
        in_specs=[tile(w) for w in widths] + [st_spec] * 4
        + [const(dec.shape), const((1, RET_W)), const((1, GLA_W)),
           pl.BlockSpec((rows, d), lambda i: (done(i), 0)),
           pl.BlockSpec((None, 1, N_MOD * d), lambda i: (done(i) // per_seq, 0, 0)),
           const(wout.shape)],
        out_specs=pl.BlockSpec((rows, d), lambda i: (done(i), 0)),
        out_shape=jax.ShapeDtypeStruct((t, d), F32),
        scratch_shapes=[pltpu.VMEM((2, rows, RET_W + GLA_W), BF16),
                        pltpu.VMEM((RET_HEADS, CHUNK, CHUNK), F32),
                        pltpu.VMEM((2 * RET_HEADS, CHUNK, LANE), F32)],
        compiler_params=pltpu.CompilerParams(dimension_semantics=("arbitrary",),
                                             vmem_limit_bytes=VMEM_LIMIT),
        name="mix",
    )(*mix_in, *states, dec, rnw.reshape(1, RET_W), gnw.reshape(1, GLA_W), x1, m3, wout)


def _rope_tables(n_tok):
    freqs = ROPE_BASE ** (-jnp.arange(RET_DK // 4, dtype=F32) / (RET_DK // 4))

    def table(n_pos, first_half):
        ang = jnp.arange(n_pos, dtype=F32)[:, None] * freqs
        zero = jnp.zeros((n_pos, LANE // 2), F32)
        cos = jnp.concatenate([jnp.cos(ang)] * 2, axis=-1)
        sin = jnp.concatenate([-jnp.sin(ang), jnp.sin(ang)], axis=-1)
        halves = (cos, zero, sin, zero) if first_half else (zero, cos, zero, sin)
        return jnp.concatenate(halves, axis=-1)

    return table(n_tok // GRID_W, True), table(GRID_W, False)


def _pack_gate(w_f, b_f, w_b, b_b):
    gw = jnp.zeros((LANE, 2 * GLA_QK), F32)
    gw = gw.at[:GLA_RANK, :GLA_QK].set(w_f).at[GLA_RANK:2 * GLA_RANK, GLA_QK:].set(w_b)
    return gw.astype(BF16), jnp.concatenate([b_f, b_b]).reshape(1, 2 * GLA_QK)


def kernel(x, c, ctx, c_ctx, ada_w, ada_b, norm1_w, ffn1_w1, ffn1_w3, ffn1_w2, norm2_w, w_in,
           ret_decay_f, ret_decay_b, ret_norm_w, gla_gate_w_f, gla_gate_b_f, gla_gate_w_b, gla_gate_b_b,
           gla_norm_w, w_out, norm3_w, ffn2_w1, ffn2_w3, ffn2_w2, final_norm_w):
    batch, n_tok, d = x.shape
    n_ctx = ctx.shape[1]
    depth = ada_w.shape[0]
    assert depth == 1 and batch + 1 <= SUBLANES
    assert n_tok % FFN_TILE == 0 and n_tok % CHUNK == 0 and n_ctx == CHUNK
    assert (batch * n_ctx) % FFN_TILE == 0

    cvec = jnp.zeros((SUBLANES, d), F32).at[:batch].set(c).at[batch].set(c_ctx)
    m, w1a, w3a, w2a, w_in_b, w_low_b = _modulation(cvec, ada_w[0], ada_b[0],
                                                    (ffn1_w1[0], ffn1_w3[0], ffn1_w2[0]), w_in[0].T)
    m3 = m.reshape(SUBLANES, 1, N_MOD * d)

    gw, gbias = _pack_gate(gla_gate_w_f[0], gla_gate_b_f[0], gla_gate_w_b[0], gla_gate_b_b[0])
    proj = (norm2_w[0], w_in_b, w_low_b, gw, gbias)
    f1 = (norm1_w[0], w1a, w3a, w2a)
    rowtab, coltab = _rope_tables(n_tok)
    dec = jnp.zeros((SUBLANES, RET_W), F32)
    dec = dec.at[0].set(jnp.repeat(ret_decay_f[0], LANE)).at[1].set(jnp.repeat(ret_decay_b[0], LANE))

    tiles_per_seq = n_tok // FFN_TILE
    first = _ffn_call(x.reshape(batch * n_tok, d), m3, tiles_per_seq, *f1, mod_off=0, tm=FFN_TILE,
                      ctx=ctx.reshape(batch * n_ctx, d), ctx_row=batch, proj=proj, rope=(rowtab, coltab),
                      cast=(ffn2_w1[0], ffn2_w3[0], ffn2_w2[0], w_out[0]), name="ffn_in")
    x1, mix_in, (w1b, w3b, w2b, woutb) = first[0], first[1:11], first[11:]
    scan_in = (mix_in[1], mix_in[2], mix_in[5], mix_in[6], mix_in[8], mix_in[9])
    zero = jnp.zeros((batch, RET_HEADS, LANE, LANE), F32)
    nc = n_tok // CHUNK
    states = _states(*scan_in, dec, (zero,) * 4, batch, nc, batch * nc)[:4]
    x2 = _mix_call(mix_in, states, dec, ret_norm_w[0], gla_norm_w[0], x1, m3, woutb, batch, nc)
    out = _ffn_call(x2, m3, tiles_per_seq, norm3_w[0], w1b, w3b, w2b, mod_off=6, tm=FFN_TILE,
                    final_w=final_norm_w, name="ffn_out")[0]
    return out.reshape(batch, n_tok, d)
```

```python
import functools

import jax
import jax.numpy as jnp
from jax import lax
from jax.experimental import pallas as pl
from jax.experimental.pallas import tpu as pltpu

F32 = jnp.float32
BF16 = jnp.bfloat16

EPS = 1e-6
LOG2E = 1.4426950408889634
N_MOD = 9
GRID_W = 64
ROPE_BASE = 10000.0
RET_HEADS = 4
RET_DK = 128
RET_DV = 128
GLA_HEADS = 4
GLA_DK = 64
GLA_DV = 128
GLA_RANK = 16
GLA_TAU = 16.0
RET_W = RET_HEADS * RET_DV
GLA_W = GLA_HEADS * GLA_DV
GLA_QK = GLA_HEADS * GLA_DK

LANE = 128
SUBLANES = 8
BF16_ROWS = 16
MOD_STEPS = 8
CHUNK = 256
MIX_CHUNKS = 2
FFN_TILE = 512
ROW_BLOCK = 256
FF_CHUNK = 256
VMEM_LIMIT = 60 * 1024 * 1024

C_RQ, C_RK, C_RV, C_RG = 0, 512, 1024, 1536
C_GQ, C_GK, C_GV, C_GG = 2048, 2304, 2560, 3072
C_LOW = 3584


def _silu(x):
    return x * (1.0 / (1.0 + jnp.exp(-x)))


def _log_sigmoid(z):
    return jnp.minimum(z, 0.0) - jnp.log(1.0 + jnp.exp(-jnp.abs(z)))


def _rms(x, w):
    return x * lax.rsqrt(jnp.mean(x * x, axis=-1, keepdims=True) + EPS) * w


def _dot(a, b):
    return jnp.dot(a, b, preferred_element_type=F32)


def _dot_nt(a, b):
    return lax.dot_general(a, b, (((1,), (1,)), ((), ())), preferred_element_type=F32)


def _dot_tn(a, b):
    return lax.dot_general(a, b, (((0,), (0,)), ((), ())), preferred_element_type=F32)


def _dot_select(sel, x):
    hi = x.astype(BF16)
    lo = (x - hi.astype(F32)).astype(BF16)
    return _dot(sel, hi) + _dot(sel, lo)


def _iota(shape, dim):
    return lax.broadcasted_iota(jnp.int32, shape, dim)


def _resident(shape):
    nd = len(shape)
    return pl.BlockSpec(shape, lambda *_: (0,) * nd, pipeline_mode=pl.Buffered(1))


def _cast_jobs(arrays, n_steps):
    specs, shapes = [], []
    for w in arrays:
        rows = next(r for r in range(BF16_ROWS, w.shape[0] + 1, BF16_ROWS)
                    if w.shape[0] % r == 0 and w.shape[0] // r <= n_steps)
        specs.append(pl.BlockSpec((rows, w.shape[1]),
                                  lambda i, last=w.shape[0] // rows - 1: (jnp.minimum(i, last), 0)))
        shapes.append(jax.ShapeDtypeStruct(w.shape, BF16))
    return specs, shapes


def _run_cast_jobs(src_refs, dst_refs):
    for src_ref, dst_ref in zip(src_refs, dst_refs):
        dst_ref[...] = src_ref[...].astype(BF16)


W_IN_BLOCK = 512


def _mod_kernel(*refs, n_cast):
    c_ref, w_ref, b_ref = refs[:3]
    cast_in, (wt_ref, wt_low_ref) = refs[3:3 + n_cast], refs[3 + n_cast:5 + n_cast]
    o_ref = refs[5 + n_cast]
    cast_out, (win_ref, wlow_ref) = refs[6 + n_cast:6 + 2 * n_cast], refs[6 + 2 * n_cast:]
    cond = _silu(c_ref[...]).astype(BF16)
    o_ref[...] = _dot(cond, w_ref[...].astype(BF16)) + b_ref[...]
    _run_cast_jobs(cast_in, cast_out)
    win_ref[...] = wt_ref[...].T.astype(BF16)
    low = wt_low_ref[...]
    low = jnp.concatenate([low, jnp.zeros((LANE - low.shape[0], low.shape[1]), F32)], axis=0)
    wlow_ref[...] = low.T.astype(BF16)


def _modulation(cvec, ada_w, ada_b, cast, w_in_t):
    d, n = ada_w.shape
    bn = n // MOD_STEPS if n % (MOD_STEPS * LANE) == 0 else d
    steps = n // bn
    assert C_LOW % W_IN_BLOCK == 0 and C_LOW // W_IN_BLOCK <= steps
    n_low = w_in_t.shape[0] - C_LOW
    last = C_LOW // W_IN_BLOCK - 1
    cast_specs, cast_shapes = _cast_jobs(cast, steps)
    return pl.pallas_call(
        functools.partial(_mod_kernel, n_cast=len(cast)),
        grid=(steps,),
        in_specs=[pl.BlockSpec((SUBLANES, d), lambda j: (0, 0)),
                  pl.BlockSpec((d, bn), lambda j: (0, j)),
                  pl.BlockSpec((1, bn), lambda j: (0, j))] + cast_specs
        + [pl.BlockSpec((W_IN_BLOCK, d), lambda j: (jnp.minimum(j, last), 0)),
           pl.BlockSpec((n_low, d), lambda j: (C_LOW // n_low, 0))],
        out_specs=[pl.BlockSpec((SUBLANES, bn), lambda j: (0, j))] + cast_specs
        + [pl.BlockSpec((d, W_IN_BLOCK), lambda j: (0, jnp.minimum(j, last))),
           pl.BlockSpec((d, LANE), lambda j: (0, 0))],
        out_shape=[jax.ShapeDtypeStruct((SUBLANES, n), F32)] + cast_shapes
        + [jax.ShapeDtypeStruct((d, C_LOW), BF16), jax.ShapeDtypeStruct((d, LANE), BF16)],
        compiler_params=pltpu.CompilerParams(dimension_semantics=("arbitrary",),
                                             vmem_limit_bytes=VMEM_LIMIT),
        name="mod",
    )(cvec, ada_w, ada_b.reshape(1, n), *cast, w_in_t, w_in_t)


def _swap32(x):
    lane = _iota(x.shape, 1)
    quarter = RET_DK // 4
    first = (lane & (2 * quarter - 1)) < quarter
    return jnp.where(first, pltpu.roll(x, LANE - quarter, 1), pltpu.roll(x, quarter, 1))


def _ffn_kernel(*refs, mod_off, proj, rope, final, n_main, n_cast, d, f):
    it = iter(refs)
    x_ref = next(it)
    if n_main is not None:
        xc_ref = next(it)
        is_ctx = pl.program_id(0) >= n_main
    m_ref, nw_ref, w1_ref, w3_ref, w2_ref = (next(it) for _ in range(5))
    if proj:
        n2w_ref, win_ref, wlow_ref, gw_ref, gbias_ref = (next(it) for _ in range(5))
        if rope:
            rowtab_ref, coltab_ref = next(it), next(it)
    if final:
        fnw_ref = next(it)
    cast_in = [next(it) for _ in range(n_cast)]
    xo_ref = next(it)
    if proj:
        (rq_ref, rk_ref, rv_ref, rg_ref, gq_ref, gk_ref, gv_ref, gg_ref,
         gf_ref, gb_ref) = (next(it) for _ in range(10))
    cast_out = [next(it) for _ in range(n_cast)]
    u_ref = next(it)
    _run_cast_jobs(cast_in, cast_out)

    def mod(i):
        return m_ref[:, (mod_off + i) * d:(mod_off + i + 1) * d]

    h2s = []
    for rb in range(x_ref.shape[0] // ROW_BLOCK):
        rows = slice(rb * ROW_BLOCK, (rb + 1) * ROW_BLOCK)
        x = x_ref[rows, :]
        if n_main is not None:
            x = jnp.where(is_ctx, xc_ref[rows, :], x)
        h = (_rms(x, nw_ref[...]) * (1.0 + mod(1)) + mod(0)).astype(BF16)
        for k in range(f // FF_CHUNK):
            sl = slice(k * FF_CHUNK, (k + 1) * FF_CHUNK)
            a = _dot(h, w1_ref[:, sl])
            g = _dot(h, w3_ref[:, sl])
            u_ref[rows, sl] = (_silu(a) * g).astype(BF16)
        y = _dot(u_ref[rows, :], w2_ref[...])
        x1 = x + (0.5 * mod(2)) * y

        if final:
            xo_ref[rows, :] = _rms(x1, fnw_ref[...])
        else:
            xo_ref[rows, :] = x1

        if proj:
            h2s.append((_rms(x1, n2w_ref[...]) * (1.0 + mod(4)) + mod(3)).astype(BF16))

    for rb, h2 in enumerate(h2s):
        rows = slice(rb * ROW_BLOCK, (rb + 1) * ROW_BLOCK)

        def p(lo, hi):
            return _dot(h2, win_ref[:, lo:hi])

        if rope:
            by_row = _iota((GRID_W, 2 * LANE), 1) % LANE < LANE // 2
            g0 = rb * (ROW_BLOCK // GRID_W)
            tab = jnp.concatenate(
                [jnp.where(by_row, jnp.broadcast_to(rowtab_ref[g0 + g:g0 + g + 1, :], (GRID_W, 2 * LANE)),
                           coltab_ref[...]) for g in range(ROW_BLOCK // GRID_W)], axis=0)
            cos, sin = tab[:, :LANE], tab[:, LANE:]
            if n_main is not None:
                cos = jnp.where(is_ctx, 1.0, cos)
                sin = jnp.where(is_ctx, 0.0, sin)

        low = _dot(h2, wlow_ref[...]).astype(BF16)
        z = _dot(low, gw_ref[...]) + gbias_ref[...]
        ls = _log_sigmoid(z) * (1.0 / GLA_TAU)
        gf_ref[rows, :] = ls[:, :GLA_QK]
        gb_ref[rows, :] = ls[:, GLA_QK:]
        for base, scale, o_ref in ((C_RQ, RET_DK ** -0.5, rq_ref), (C_RK, 1.0, rk_ref)):
            t = p(base, base + RET_W)
            for hd in range(RET_HEADS):
                th = t[:, hd * LANE:(hd + 1) * LANE] * scale
                if rope:
                    th = th * cos + _swap32(th) * sin
                o_ref[rows, hd * LANE:(hd + 1) * LANE] = th.astype(BF16)
        rg_ref[rows, :] = _silu(p(C_RG, C_RG + RET_W)).astype(BF16)
        gg_ref[rows, :] = _silu(p(C_GG, C_GG + GLA_W)).astype(BF16)
        gq_ref[rows, :] = (p(C_GQ, C_GQ + GLA_QK) * GLA_DK ** -0.5).astype(BF16)
        gk_ref[rows, :] = p(C_GK, C_GK + GLA_QK).astype(BF16)
        rv_ref[rows, :] = p(C_RV, C_RV + RET_W).astype(BF16)
        gv_ref[rows, :] = p(C_GV, C_GV + GLA_W).astype(BF16)


def _ffn_call(x, m3, tiles_per_seq, nw, w1, w3, w2, *, mod_off, tm, ctx=None, ctx_row=None, proj=None,
              rope=None, final_w=None, cast=(), name):
    t, d = x.shape
    f = w1.shape[1]
    n_main = t // tm
    n_tiles = n_main + (0 if ctx is None else ctx.shape[0] // tm)
    t = n_tiles * tm
    main = lambda i: jnp.minimum(i, n_main - 1)
    tile = lambda w: pl.BlockSpec((tm, w), lambda i: (i, 0))
    in_specs = [pl.BlockSpec((tm, d), lambda i: (main(i), 0))]
    args = [x]
    if ctx is None:
        row_of_tile = lambda i: i // tiles_per_seq
    else:
        row_of_tile = lambda i: jnp.where(i >= n_main, ctx_row, i // tiles_per_seq)
        in_specs.append(pl.BlockSpec((tm, d), lambda i: (jnp.maximum(i - n_main, 0), 0)))
        args.append(ctx)
    in_specs += [pl.BlockSpec((None, 1, N_MOD * d), lambda i: (row_of_tile(i), 0, 0)),
                 _resident((1, d)), _resident((d, f)), _resident((d, f)), _resident((f, d))]
    args += [m3, nw.reshape(1, d), w1, w3, w2]
    if proj is not None:
        n2w, win, wlow, gw, gbias = proj
        in_specs += [_resident((1, d))] + [_resident(a.shape) for a in (win, wlow, gw, gbias)]
        args += [n2w.reshape(1, d), win, wlow, gw, gbias]
        if rope is not None:
            rowtab, coltab = rope
            in_specs += [pl.BlockSpec((tm // GRID_W, 2 * LANE), lambda i: (main(i) % tiles_per_seq, 0)),
                         _resident(coltab.shape)]
            args += [rowtab, coltab]
    if final_w is not None:
        in_specs.append(_resident((1, d)))
        args.append(final_w.reshape(1, d))
    out_specs = [tile(d)]
    out_shape = [jax.ShapeDtypeStruct((t, d), F32)]
    if proj is not None:
        for w, dt in ((RET_W, BF16),) * 4 + ((GLA_QK, BF16),) * 2 + ((GLA_W, BF16),) * 2 + ((GLA_QK, F32),) * 2:
            out_specs.append(tile(w))
            out_shape.append(jax.ShapeDtypeStruct((t, w), dt))
    cast_specs, cast_shapes = _cast_jobs(cast, n_main)
    in_specs += cast_specs
    args += list(cast)
    out_specs += cast_specs
    out_shape += cast_shapes
    kern = functools.partial(_ffn_kernel, mod_off=mod_off, proj=proj is not None,
                             rope=rope is not None, final=final_w is not None,
                             n_main=None if ctx is None else n_main, n_cast=len(cast), d=d, f=f)
    return pl.pallas_call(
        kern, grid=(n_tiles,), in_specs=in_specs, out_specs=out_specs, out_shape=out_shape,
        scratch_shapes=[pltpu.VMEM((tm, f), BF16)],
        compiler_params=pltpu.CompilerParams(dimension_semantics=("arbitrary",),
                                             vmem_limit_bytes=VMEM_LIMIT),
        name=name,
    )(*args)


def _head_lanes(shape, hd):
    lane = _iota(shape, len(shape) - 1)
    return (lane >= GLA_DK) if hd % 2 else (lane < GLA_DK)


STATE_RING = 3


def _state_kernel(*refs, batch, cps, steps, bcps, bsteps, ctx_chunk, nc):
    it = iter(refs)
    token_hbm = [next(it) for _ in range(6)]
    dec_ref = next(it)
    orf_ref, orb_ref, ogf_ref, ogb_ref = (next(it) for _ in range(4))
    srf, srb, sgf, sgb, ub_ret, ub_gla, ub_dec = (next(it) for _ in range(7))
    token_ring = [next(it) for _ in range(6)]
    sem = next(it)
    n = CHUNK
    step = pl.program_id(0)
    lg = _log_sigmoid(dec_ref[...])

    def token_copies(s, slot, ctx=False):
        out = []
        for a, (hbm, ring) in enumerate(zip(token_hbm, token_ring)):
            for b in range(batch):
                if ctx:
                    src, dst = hbm.at[pl.ds((ctx_chunk + b) * n, n), :], ring.at[slot, b, pl.ds(0, n), :]
                else:
                    row0 = pl.multiple_of((b * nc + (s - 1) * cps) * n, n)
                    src, dst = hbm.at[pl.ds(row0, cps * n), :], ring.at[slot, b]
                out.append(pltpu.make_async_copy(src, dst, sem.at[slot, a * batch + b]))
        return out

    @pl.when(step == 0)
    def _():
        for carry in (srf, srb, sgf, sgb):
            carry[...] = jnp.zeros(carry.shape, F32)
        for cp in token_copies(0, 0, ctx=True):
            cp.start()
        for s in range(1, min(STATE_RING - 1, steps + 1)):
            for cp in token_copies(s, s):
                cp.start()

    def reading_step(slot, ctx=False):
        rk_ref, rv_ref, gk_ref, gv_ref, gf_ref, gb_ref = ([ring.at[slot, b] for b in range(batch)]
                                                          for ring in token_ring)

        @pl.when(step + (STATE_RING - 1) <= steps)
        def _():
            for cp in token_copies(step + (STATE_RING - 1), (slot + STATE_RING - 1) % STATE_RING):
                cp.start()

        for cp in token_copies(step, slot, ctx):
            cp.wait()
        row = _iota((n, LANE), 0).astype(F32)
        ri = _iota((n, n), 0)
        ci = _iota((n, n), 1)
        after = (ci > ri).astype(BF16)
        before = (ci < ri).astype(BF16)
        heads = [slice(hd * LANE, (hd + 1) * LANE) for hd in range(RET_HEADS)]
        to_end = [jnp.exp((n - 1.0 - row) * lg[0:1, sl]).astype(BF16) for sl in heads]
        to_start = [jnp.exp(row * lg[1:2, sl]).astype(BF16) for sl in heads]

        def gla_keys(b, rows, g_ref, tri, edge):
            g = g_ref[b][rows, :]
            e = _dot_select(tri, g)
            tot = e[edge:edge + 1, :] + g[edge:edge + 1, :]
            return gk_ref[b][rows, :] * jnp.exp(e).astype(BF16), jnp.exp(tot)

        def gla(b, rows, keys):
            kd, dec = keys
            upd = []
            for hd in range(GLA_HEADS):
                pr = slice((hd // 2) * LANE, (hd // 2 + 1) * LANE)
                u = _dot_tn(gv_ref[b][rows, hd * LANE:(hd + 1) * LANE], kd[:, pr])
                upd.append(jnp.where(_head_lanes(u.shape, hd), u, 0.0))
            return upd, dec

        for j in range(1 if ctx else cps):
            chunk = (step - 1) * cps + j
            rows = slice(j * n, (j + 1) * n)
            if not ctx:
                orf_ref[:, j] = srf[...].astype(BF16)
                ogf_ref[:, j] = sgf[...].astype(BF16)
            keys_f = [gla_keys(b, rows, gf_ref, after, 0) for b in range(batch)]
            keys_b = [gla_keys(b, rows, gb_ref, before, n - 1) for b in range(batch)]
            for b in range(batch):
                for hd in range(RET_HEADS):
                    sl = slice(hd * LANE, (hd + 1) * LANE)
                    k = rk_ref[b][rows, sl]
                    v = rv_ref[b][rows, sl]
                    srf[b, hd] = srf[b, hd] * jnp.exp(n * lg[0:1, sl]) + _dot_tn(k * to_end[hd], v)
                    back = _dot_tn(k * to_start[hd], v)
                    if ctx:
                        srb[b, hd] = srb[b, hd] * jnp.exp(n * lg[1:2, sl]) + back
                    else:
                        ub_ret[chunk, b, hd] = back.astype(BF16)
                upd, dec = gla(b, rows, keys_f[b])
                for hd in range(GLA_HEADS):
                    pr = slice((hd // 2) * LANE, (hd // 2 + 1) * LANE)
                    sgf[b, hd] = sgf[b, hd] * dec[:, pr] + upd[hd]
                upd, dec = gla(b, rows, keys_b[b])
                for hd in range(GLA_HEADS):
                    pr = slice((hd // 2) * LANE, (hd // 2 + 1) * LANE)
                    if ctx:
                        sgb[b, hd] = sgb[b, hd] * dec[:, pr] + upd[hd]
                    else:
                        ub_gla[chunk, b, hd] = upd[hd].astype(BF16)
                if not ctx:
                    ub_dec[chunk, b] = jnp.broadcast_to(dec, ub_dec.shape[2:])

    pl.when(step == 0)(functools.partial(reading_step, 0, ctx=True))
    for slot in range(STATE_RING):
        pl.when((step >= 1) & (step <= steps) & (step % STATE_RING == slot))(functools.partial(reading_step, slot))

    @pl.when(step > steps)
    def _():
        for j in range(bcps):
            jb = bcps - 1 - j
            chunk = (steps + bsteps - step) * bcps + jb
            orb_ref[:, jb] = srb[...].astype(BF16)
            ogb_ref[:, jb] = sgb[...].astype(BF16)
            for b in range(batch):
                for hd in range(RET_HEADS):
                    lgb = lg[1:2, hd * LANE:(hd + 1) * LANE]
                    srb[b, hd] = srb[b, hd] * jnp.exp(n * lgb) + ub_ret[chunk, b, hd].astype(F32)
                for hd in range(GLA_HEADS):
                    pr = slice((hd // 2) * LANE, (hd // 2 + 1) * LANE)
                    sgb[b, hd] = sgb[b, hd] * ub_dec[chunk, b, 0:1, pr] + ub_gla[chunk, b, hd].astype(F32)


def _states(rk, rv, gk, gv, gf, gb, dec, batch, nc, ctx_chunk):
    cps = 2 if nc % 2 == 0 else 1
    bcps = next(c for c in (8, 4, 2, 1) if nc % c == 0)
    steps, bsteps = nc // cps, nc // bcps
    reading = lambda c: jnp.clip(c - 1, 0, steps - 1)
    scanning = lambda c: bsteps - 1 - jnp.maximum(c - steps - 1, 0)

    tokens = (rk, rv, gk, gv, gf, gb)
    st_shape = (batch, RET_HEADS, LANE, LANE)
    of_spec = pl.BlockSpec((batch, cps, RET_HEADS, LANE, LANE), lambda c: (0, reading(c), 0, 0, 0))
    ob_spec = pl.BlockSpec((batch, bcps, RET_HEADS, LANE, LANE), lambda c: (0, scanning(c), 0, 0, 0))
    per_chunk = jax.ShapeDtypeStruct((batch, nc, RET_HEADS, LANE, LANE), BF16)
    kept = (nc, batch, RET_HEADS, LANE, LANE)
    return pl.pallas_call(
        functools.partial(_state_kernel, batch=batch, cps=cps, steps=steps, bcps=bcps, bsteps=bsteps,
                          ctx_chunk=ctx_chunk, nc=nc),
        grid=(1 + steps + bsteps,),
        in_specs=[pl.BlockSpec(memory_space=pl.ANY)] * len(tokens)
        + [pl.BlockSpec(dec.shape, lambda c: (0, 0))],
        out_specs=[of_spec, ob_spec, of_spec, ob_spec],
        out_shape=[per_chunk] * 4,
        scratch_shapes=[pltpu.VMEM(st_shape, F32)] * 4
        + [pltpu.VMEM(kept, BF16), pltpu.VMEM(kept, BF16), pltpu.VMEM((nc, batch, SUBLANES, GLA_QK), F32)]
        + [pltpu.VMEM((STATE_RING, batch, cps * CHUNK, a.shape[1]), a.dtype) for a in tokens]
        + [pltpu.SemaphoreType.DMA((STATE_RING, len(tokens) * batch))],
        compiler_params=pltpu.CompilerParams(dimension_semantics=("arbitrary",),
                                             vmem_limit_bytes=VMEM_LIMIT),
        name="states",
    )(*tokens, dec)


def _block_row(x, parent, r):
    n, w = x.shape
    if parent == n:
        return jnp.broadcast_to(x[r:r + 1, :], (n, w))
    x3 = x.reshape(n // parent, parent, w)
    return jnp.broadcast_to(x3[:, r:r + 1, :], x3.shape).reshape(n, w)


def _gla_levels(q, k_even, k_odd, gf, gb, bf, bb):
    n = q.shape[0]
    row = _iota(q.shape, 0)

    def factors(u_exp, w_exp):
        ew = jnp.exp2(w_exp).astype(BF16)
        return (q * jnp.exp2(u_exp).astype(BF16), k_even * ew, k_odd * ew)

    levels = []
    s = n // 2
    while s >= 2:
        if s >= 4:
            last_of_first = _block_row(bf, 2 * s, s - 1)
            first_of_second = _block_row(bb, 2 * s, s)
        else:
            upper = (row & 4) != 0
            last_of_first = jnp.where(upper, _block_row(bf, SUBLANES, 5), _block_row(bf, SUBLANES, 1))
            first_of_second = jnp.where(upper, _block_row(bb, SUBLANES, 6), _block_row(bb, SUBLANES, 2))
        df = bf - last_of_first
        db = bb - first_of_second
        levels.append((2 * s, [factors(jnp.minimum(df, db), -jnp.maximum(df, db))]))
        s //= 2
    odd = (row & 1) == 1
    qa = q * jnp.where(odd, jnp.exp2(gf), 2.0).astype(BF16)
    qb = q * jnp.where(odd, 2.0, jnp.exp2(gb)).astype(BF16)
    zero = jnp.zeros_like(k_even)
    ev = lambda x: jnp.where(odd, zero, x)
    od = lambda x: jnp.where(odd, x, zero)
    levels.append((2, [(qa, ev(k_even), ev(k_odd)), (qb, od(k_even), od(k_odd))]))
    return levels


def _mix_kernel(rq_ref, rk_ref, rv_ref, rg_ref, gq_ref, gk_ref, gv_ref, gg_ref, gf_ref, gb_ref,
                srf_ref, srb_ref, sgf_ref, sgb_ref, dec_ref, rnw_ref, gnw_ref,
                x_ref, m_ref, wout_ref, o_ref, mix_ref, decay_ref, qdec_ref, *, d):
    n = CHUNK
    half = n // 2
    ri = _iota((n, n), 0)
    ci = _iota((n, n), 1)
    step = pl.program_id(0)
    last_step = pl.num_programs(0) - 1
    cur = step % 2
    code = _iota((half, n), 0) ^ (_iota((half, n), 1) & (half - 1))

    @pl.when(step == 0)
    def _():
        mix_ref[1] = jnp.zeros(mix_ref.shape[1:], BF16)
        lg = _log_sigmoid(dec_ref[...])
        dist = (ri - ci).astype(F32)
        row = _iota((n, LANE), 0).astype(F32)
        for hd in range(RET_HEADS):
            sl = slice(hd * LANE, (hd + 1) * LANE)
            lgf = lg[0:1, sl]
            lgb = lg[1:2, sl]
            decay_ref[hd] = jnp.exp(jnp.where(dist > 0, dist * lgf[:, 0:1],
                                              jnp.where(dist < 0, -dist * lgb[:, 0:1], jnp.log(2.0))))
            qdec_ref[hd] = jnp.exp((row + 1.0) * lgf)
            qdec_ref[RET_HEADS + hd] = jnp.exp((n - row) * lgb)

    def retention(c):
        tok = slice(c * n, (c + 1) * n)
        for hd in range(RET_HEADS):
            sl = slice(hd * LANE, (hd + 1) * LANE)
            q = rq_ref[tok, sl]
            p = (_dot_nt(q, rk_ref[tok, sl]) * decay_ref[hd]).astype(BF16)
            qf = q.astype(F32)
            qs = jnp.concatenate([(qf * qdec_ref[hd]).astype(BF16),
                                  (qf * qdec_ref[RET_HEADS + hd]).astype(BF16)], axis=1)
            st = jnp.concatenate([srf_ref[c, hd], srb_ref[c, hd]], axis=0)
            o = _dot(p, rv_ref[tok, sl]) + _dot(qs, st)
            mu = jnp.mean(o, axis=-1, keepdims=True)
            oc = o - mu
            var = jnp.mean(oc * oc, axis=-1, keepdims=True)
            r = oc * lax.rsqrt(var + EPS) * rnw_ref[:, sl] * rg_ref[tok, sl].astype(F32)
            mix_ref[cur, tok, sl] = r.astype(BF16)

    def gla_factors(c):
        tok = slice(c * n, (c + 1) * n)
        gf = gf_ref[tok, :] * LOG2E
        gb = gb_ref[tok, :] * LOG2E
        bf = _dot_select((ci <= ri).astype(BF16), gf)
        bb = _dot_select((ci >= ri).astype(BF16), gb)
        q = gq_ref[tok, :]
        k = gk_ref[tok, :]
        even_head = (_iota(k.shape, 1) & GLA_DK) == 0
        no_k = jnp.zeros_like(k)
        levels = _gla_levels(q, jnp.where(even_head, k, no_k), jnp.where(even_head, no_k, k), gf, gb, bf, bb)
        return levels, q * jnp.exp2(bf).astype(BF16), q * jnp.exp2(bb).astype(BF16)

    def finish_previous():
        o_ref[...] = x_ref[...] + m_ref[:, 5 * d:6 * d] * _dot(mix_ref[1 - cur], wout_ref[...])

    def gla_outputs(c, levels, qsf, qsb):
        tok = slice(c * n, (c + 1) * n)
        for pair in range(GLA_HEADS // 2):
            pr = slice(pair * LANE, (pair + 1) * LANE)
            near = [None, None]
            far = [None, None]
            for size, blocks in levels:
                for rb in range(2):
                    kb = rb if size < n else 1 - rb
                    rows = slice(rb * half, (rb + 1) * half)
                    keys = slice(kb * half, (kb + 1) * half)
                    u = jnp.concatenate([blk[0][rows, pr] for blk in blocks], axis=1)
                    w = jnp.concatenate([jnp.concatenate([blk[j][keys, pr] for blk in blocks], axis=1)
                                         for j in (1, 2)], axis=0)
                    t = _dot_nt(u, w)
                    if size == n:
                        far[rb] = t
                    else:
                        near[rb] = t if near[rb] is None else jnp.where(code < size, t, near[rb])
            qcat = jnp.concatenate([qsf[:, pr], qsb[:, pr]], axis=1)
            for j in range(2):
                hd = 2 * pair + j
                sl = slice(hd * LANE, (hd + 1) * LANE)
                mine = slice(j * half, (j + 1) * half)
                p = jnp.concatenate([jnp.concatenate([near[0][:, mine], far[0][:, mine]], axis=1),
                                     jnp.concatenate([far[1][:, mine], near[1][:, mine]], axis=1)], axis=0)
                st = jnp.concatenate([sgf_ref[c, hd], sgb_ref[c, hd]], axis=1)
                o = _dot(p.astype(BF16), gv_ref[tok, sl]) + _dot_nt(qcat, st)
                ms = jnp.mean(o * o, axis=-1, keepdims=True)
                r = o * lax.rsqrt(ms + EPS) * gnw_ref[:, sl] * gg_ref[tok, sl].astype(F32)
                mix_ref[cur, tok, RET_W + hd * LANE:RET_W + (hd + 1) * LANE] = r.astype(BF16)

    @pl.when(step < last_step)
    def _():
        factors = []
        for c in range(MIX_CHUNKS):
            factors.append(gla_factors(c))
            if c == 0:
                finish_previous()
            retention(c)
        for c in range(MIX_CHUNKS):
            gla_outputs(c, *factors[c])

    @pl.when(step == last_step)
    def _():
        finish_previous()


def _mix_call(mix_in, states, dec, rnw, gnw, x1, m3, wout, batch, nc):
    assert nc % MIX_CHUNKS == 0
    per_seq = nc // MIX_CHUNKS
    n_steps = batch * per_seq
    rows = MIX_CHUNKS * CHUNK
    t, d = n_steps * rows, x1.shape[1]
    mixed = lambda i: jnp.minimum(i, n_steps - 1)
    done = lambda i: jnp.maximum(i - 1, 0)
    tile = lambda w: pl.BlockSpec((rows, w), lambda i: (mixed(i), 0))
    st_spec = pl.BlockSpec((None, MIX_CHUNKS, RET_HEADS, LANE, LANE),
                           lambda i: (mixed(i) // per_seq, mixed(i) % per_seq, 0, 0, 0))
    const = lambda shape: pl.BlockSpec(shape, lambda i: (0,) * len(shape))
    widths = (RET_W,) * 4 + (GLA_QK,) * 2 + (GLA_W,) * 2 + (GLA_QK,) * 2
    return pl.pallas_call(
        functools.partial(_mix_kernel, d=d),
        grid=(n_steps + 1,),
        in_specs=[tile(w) for w in widths] + [st_spec] * 4
        + [const(dec.shape), const((1, RET_W)), const((1, GLA_W)),
           pl.BlockSpec((rows, d), lambda i: (done(i), 0)),
           pl.BlockSpec((None, 1, N_MOD * d), lambda i: (done(i) // per_seq, 0, 0)),
           const(wout.shape)],
        out_specs=pl.BlockSpec((rows, d), lambda i: (done(i), 0)),
        out_shape=jax.ShapeDtypeStruct((t, d), F32),
        scratch_shapes=[pltpu.VMEM((2, rows, RET_W + GLA_W), BF16),
                        pltpu.VMEM((RET_HEADS, CHUNK, CHUNK), F32),
                        pltpu.VMEM((2 * RET_HEADS, CHUNK, LANE), F32)],
        compiler_params=pltpu.CompilerParams(dimension_semantics=("arbitrary",),
                                             vmem_limit_bytes=VMEM_LIMIT),
        name="mix",
    )(*mix_in, *states, dec, rnw.reshape(1, RET_W), gnw.reshape(1, GLA_W), x1, m3, wout)


def _rope_tables(n_tok):
    freqs = ROPE_BASE ** (-jnp.arange(RET_DK // 4, dtype=F32) / (RET_DK // 4))

    def table(n_pos, first_half):
        ang = jnp.arange(n_pos, dtype=F32)[:, None] * freqs
        zero = jnp.zeros((n_pos, LANE // 2), F32)
        cos = jnp.concatenate([jnp.cos(ang)] * 2, axis=-1)
        sin = jnp.concatenate([-jnp.sin(ang), jnp.sin(ang)], axis=-1)
        halves = (cos, zero, sin, zero) if first_half else (zero, cos, zero, sin)
        return jnp.concatenate(halves, axis=-1)

    return table(n_tok // GRID_W, True), table(GRID_W, False)


def _pack_gate(w_f, b_f, w_b, b_b):
    gw = jnp.zeros((LANE, 2 * GLA_QK), F32)
    gw = gw.at[:GLA_RANK, :GLA_QK].set(w_f).at[GLA_RANK:2 * GLA_RANK, GLA_QK:].set(w_b)
    return gw.astype(BF16), jnp.concatenate([b_f, b_b]).reshape(1, 2 * GLA_QK)


def kernel(x, c, ctx, c_ctx, ada_w, ada_b, norm1_w, ffn1_w1, ffn1_w3, ffn1_w2, norm2_w, w_in,
           ret_decay_f, ret_decay_b, ret_norm_w, gla_gate_w_f, gla_gate_b_f, gla_gate_w_b, gla_gate_b_b,
           gla_norm_w, w_out, norm3_w, ffn2_w1, ffn2_w3, ffn2_w2, final_norm_w):
    batch, n_tok, d = x.shape
    n_ctx = ctx.shape[1]
    depth = ada_w.shape[0]
    assert depth == 1 and batch + 1 <= SUBLANES
    assert n_tok % FFN_TILE == 0 and n_tok % CHUNK == 0 and n_ctx == CHUNK
    assert (batch * n_ctx) % FFN_TILE == 0

    cvec = jnp.zeros((SUBLANES, d), F32).at[:batch].set(c).at[batch].set(c_ctx)
    m, w1a, w3a, w2a, w_in_b, w_low_b = _modulation(cvec, ada_w[0], ada_b[0],
                                                    (ffn1_w1[0], ffn1_w3[0], ffn1_w2[0]), w_in[0].T)
    m3 = m.reshape(SUBLANES, 1, N_MOD * d)

    gw, gbias = _pack_gate(gla_gate_w_f[0], gla_gate_b_f[0], gla_gate_w_b[0], gla_gate_b_b[0])
    proj = (norm2_w[0], w_in_b, w_low_b, gw, gbias)
    f1 = (norm1_w[0], w1a, w3a, w2a)
    rowtab, coltab = _rope_tables(n_tok)
    dec = jnp.zeros((SUBLANES, RET_W), F32)
    dec = dec.at[0].set(jnp.repeat(ret_decay_f[0], LANE)).at[1].set(jnp.repeat(ret_decay_b[0], LANE))

    tiles_per_seq = n_tok // FFN_TILE
    first = _ffn_call(x.reshape(batch * n_tok, d), m3, tiles_per_seq, *f1, mod_off=0, tm=FFN_TILE,
                      ctx=ctx.reshape(batch * n_ctx, d), ctx_row=batch, proj=proj, rope=(rowtab, coltab),
                      cast=(ffn2_w1[0], ffn2_w3[0], ffn2_w2[0], w_out[0]), name="ffn_in")
    x1, mix_in, (w1b, w3b, w2b, woutb) = first[0], first[1:11], first[11:]
    scan_in = (mix_in[1], mix_in[2], mix_in[5], mix_in[6], mix_in[8], mix_in[9])
    nc = n_tok // CHUNK
    states = _states(*scan_in, dec, batch, nc, batch * nc)
    x2 = _mix_call(mix_in, states, dec, ret_norm_w[0], gla_norm_w[0], x1, m3, woutb, batch, nc)
    out = _ffn_call(x2, m3, tiles_per_seq, norm3_w[0], w1b, w3b, w2b, mod_off=6, tm=FFN_TILE,
                    final_w=final_norm_w, name="ffn_out")[0]
    return out.reshape(batch, n_tok, d)
```

```python
import functools

import jax
import jax.numpy as jnp
from jax import lax
from jax.experimental import pallas as pl
from jax.experimental.pallas import tpu as pltpu

F32 = jnp.float32
BF16 = jnp.bfloat16

EPS = 1e-6
LOG2E = 1.4426950408889634
N_MOD = 9
GRID_W = 64
ROPE_BASE = 10000.0
RET_HEADS = 4
RET_DK = 128
RET_DV = 128
GLA_HEADS = 4
GLA_DK = 64
GLA_DV = 128
GLA_RANK = 16
GLA_TAU = 16.0
RET_W = RET_HEADS * RET_DV
GLA_W = GLA_HEADS * GLA_DV
GLA_QK = GLA_HEADS * GLA_DK

LANE = 128
SUBLANES = 8
BF16_ROWS = 16
MOD_STEPS = 8
CHUNK = 256
MIX_CHUNKS = 2
FFN_TILE = 512
ROW_BLOCK = 256
FF_CHUNK = 256
VMEM_LIMIT = 60 * 1024 * 1024

C_RQ, C_RK, C_RV, C_RG = 0, 512, 1024, 1536
C_GQ, C_GK, C_GV, C_GG = 2048, 2304, 2560, 3072
C_LOW = 3584


def _silu(x):
    return x * (1.0 / (1.0 + jnp.exp(-x)))


def _log_sigmoid(z):
    return jnp.minimum(z, 0.0) - jnp.log(1.0 + jnp.exp(-jnp.abs(z)))


def _rms(x, w):
    return x * lax.rsqrt(jnp.mean(x * x, axis=-1, keepdims=True) + EPS) * w


def _dot(a, b):
    return jnp.dot(a, b, preferred_element_type=F32)


def _dot_nt(a, b):
    return lax.dot_general(a, b, (((1,), (1,)), ((), ())), preferred_element_type=F32)


def _dot_tn(a, b):
    return lax.dot_general(a, b, (((0,), (0,)), ((), ())), preferred_element_type=F32)


def _dot_select(sel, x):
    hi = x.astype(BF16)
    lo = (x - hi.astype(F32)).astype(BF16)
    return _dot(sel, hi) + _dot(sel, lo)


def _iota(shape, dim):
    return lax.broadcasted_iota(jnp.int32, shape, dim)


def _resident(shape):
    nd = len(shape)
    return pl.BlockSpec(shape, lambda *_: (0,) * nd, pipeline_mode=pl.Buffered(1))


def _cast_jobs(arrays, n_steps):
    specs, shapes = [], []
    for w in arrays:
        rows = next(r for r in range(BF16_ROWS, w.shape[0] + 1, BF16_ROWS)
                    if w.shape[0] % r == 0 and w.shape[0] // r <= n_steps)
        specs.append(pl.BlockSpec((rows, w.shape[1]),
                                  lambda i, last=w.shape[0] // rows - 1: (jnp.minimum(i, last), 0)))
        shapes.append(jax.ShapeDtypeStruct(w.shape, BF16))
    return specs, shapes


def _run_cast_jobs(src_refs, dst_refs):
    for src_ref, dst_ref in zip(src_refs, dst_refs):
        dst_ref[...] = src_ref[...].astype(BF16)


W_IN_BLOCK = 512


def _mod_kernel(*refs, n_cast):
    c_ref, w_ref, b_ref = refs[:3]
    cast_in, (wt_ref, wt_low_ref) = refs[3:3 + n_cast], refs[3 + n_cast:5 + n_cast]
    o_ref = refs[5 + n_cast]
    cast_out, (win_ref, wlow_ref) = refs[6 + n_cast:6 + 2 * n_cast], refs[6 + 2 * n_cast:]
    cond = _silu(c_ref[...]).astype(BF16)
    o_ref[...] = _dot(cond, w_ref[...].astype(BF16)) + b_ref[...]
    _run_cast_jobs(cast_in, cast_out)
    win_ref[...] = wt_ref[...].T.astype(BF16)
    low = wt_low_ref[...]
    low = jnp.concatenate([low, jnp.zeros((LANE - low.shape[0], low.shape[1]), F32)], axis=0)
    wlow_ref[...] = low.T.astype(BF16)


def _modulation(cvec, ada_w, ada_b, cast, w_in_t):
    d, n = ada_w.shape
    bn = n // MOD_STEPS if n % (MOD_STEPS * LANE) == 0 else d
    steps = n // bn
    assert C_LOW % W_IN_BLOCK == 0 and C_LOW // W_IN_BLOCK <= steps
    n_low = w_in_t.shape[0] - C_LOW
    last = C_LOW // W_IN_BLOCK - 1
    cast_specs, cast_shapes = _cast_jobs(cast, steps)
    return pl.pallas_call(
        functools.partial(_mod_kernel, n_cast=len(cast)),
        grid=(steps,),
        in_specs=[pl.BlockSpec((SUBLANES, d), lambda j: (0, 0)),
                  pl.BlockSpec((d, bn), lambda j: (0, j)),
                  pl.BlockSpec((1, bn), lambda j: (0, j))] + cast_specs
        + [pl.BlockSpec((W_IN_BLOCK, d), lambda j: (jnp.minimum(j, last), 0)),
           pl.BlockSpec((n_low, d), lambda j: (C_LOW // n_low, 0))],
        out_specs=[pl.BlockSpec((SUBLANES, bn), lambda j: (0, j))] + cast_specs
        + [pl.BlockSpec((d, W_IN_BLOCK), lambda j: (0, jnp.minimum(j, last))),
           pl.BlockSpec((d, LANE), lambda j: (0, 0))],
        out_shape=[jax.ShapeDtypeStruct((SUBLANES, n), F32)] + cast_shapes
        + [jax.ShapeDtypeStruct((d, C_LOW), BF16), jax.ShapeDtypeStruct((d, LANE), BF16)],
        compiler_params=pltpu.CompilerParams(dimension_semantics=("arbitrary",),
                                             vmem_limit_bytes=VMEM_LIMIT),
        name="mod",
    )(cvec, ada_w, ada_b.reshape(1, n), *cast, w_in_t, w_in_t)


def _swap32(x):
    lane = _iota(x.shape, 1)
    quarter = RET_DK // 4
    first = (lane & (2 * quarter - 1)) < quarter
    return jnp.where(first, pltpu.roll(x, LANE - quarter, 1), pltpu.roll(x, quarter, 1))


def _ffn_kernel(*refs, mod_off, proj, rope, final, n_main, n_cast, d, f):
    it = iter(refs)
    x_ref = next(it)
    if n_main is not None:
        xc_ref = next(it)
        is_ctx = pl.program_id(0) >= n_main
    m_ref, nw_ref, w1_ref, w3_ref, w2_ref = (next(it) for _ in range(5))
    if proj:
        n2w_ref, win_ref, wlow_ref, gw_ref, gbias_ref = (next(it) for _ in range(5))
        if rope:
            rowtab_ref, coltab_ref = next(it), next(it)
    if final:
        fnw_ref = next(it)
    cast_in = [next(it) for _ in range(n_cast)]
    xo_ref = next(it)
    if proj:
        (rq_ref, rk_ref, rv_ref, rg_ref, gq_ref, gk_ref, gv_ref, gg_ref,
         gf_ref, gb_ref) = (next(it) for _ in range(10))
    cast_out = [next(it) for _ in range(n_cast)]
    u_ref = next(it)
    _run_cast_jobs(cast_in, cast_out)

    def mod(i):
        return m_ref[:, (mod_off + i) * d:(mod_off + i + 1) * d]

    h2s = []
    for rb in range(x_ref.shape[0] // ROW_BLOCK):
        rows = slice(rb * ROW_BLOCK, (rb + 1) * ROW_BLOCK)
        x = x_ref[rows, :]
        if n_main is not None:
            x = jnp.where(is_ctx, xc_ref[rows, :], x)
        h = (_rms(x, nw_ref[...]) * (1.0 + mod(1)) + mod(0)).astype(BF16)
        for k in range(f // FF_CHUNK):
            sl = slice(k * FF_CHUNK, (k + 1) * FF_CHUNK)
            a = _dot(h, w1_ref[:, sl])
            g = _dot(h, w3_ref[:, sl])
            u_ref[rows, sl] = (_silu(a) * g).astype(BF16)
        y = _dot(u_ref[rows, :], w2_ref[...])
        x1 = x + (0.5 * mod(2)) * y

        if final:
            xo_ref[rows, :] = _rms(x1, fnw_ref[...])
        else:
            xo_ref[rows, :] = x1

        if proj:
            h2s.append((_rms(x1, n2w_ref[...]) * (1.0 + mod(4)) + mod(3)).astype(BF16))

    for rb, h2 in enumerate(h2s):
        rows = slice(rb * ROW_BLOCK, (rb + 1) * ROW_BLOCK)

        def p(lo, hi):
            return _dot(h2, win_ref[:, lo:hi])

        if rope:
            by_row = _iota((GRID_W, 2 * LANE), 1) % LANE < LANE // 2
            g0 = rb * (ROW_BLOCK // GRID_W)
            tab = jnp.concatenate(
                [jnp.where(by_row, jnp.broadcast_to(rowtab_ref[g0 + g:g0 + g + 1, :], (GRID_W, 2 * LANE)),
                           coltab_ref[...]) for g in range(ROW_BLOCK // GRID_W)], axis=0)
            cos, sin = tab[:, :LANE], tab[:, LANE:]
            if n_main is not None:
                cos = jnp.where(is_ctx, 1.0, cos)
                sin = jnp.where(is_ctx, 0.0, sin)

        low = _dot(h2, wlow_ref[...]).astype(BF16)
        z = _dot(low, gw_ref[...]) + gbias_ref[...]
        ls = _log_sigmoid(z) * (1.0 / GLA_TAU)
        gf_ref[rows, :] = ls[:, :GLA_QK]
        gb_ref[rows, :] = ls[:, GLA_QK:]
        for base, scale, o_ref in ((C_RQ, RET_DK ** -0.5, rq_ref), (C_RK, 1.0, rk_ref)):
            t = p(base, base + RET_W)
            for hd in range(RET_HEADS):
                th = t[:, hd * LANE:(hd + 1) * LANE] * scale
                if rope:
                    th = th * cos + _swap32(th) * sin
                o_ref[rows, hd * LANE:(hd + 1) * LANE] = th.astype(BF16)
        rg_ref[rows, :] = _silu(p(C_RG, C_RG + RET_W)).astype(BF16)
        gg_ref[rows, :] = _silu(p(C_GG, C_GG + GLA_W)).astype(BF16)
        gq_ref[rows, :] = (p(C_GQ, C_GQ + GLA_QK) * GLA_DK ** -0.5).astype(BF16)
        gk_ref[rows, :] = p(C_GK, C_GK + GLA_QK).astype(BF16)
        rv_ref[rows, :] = p(C_RV, C_RV + RET_W).astype(BF16)
        gv_ref[rows, :] = p(C_GV, C_GV + GLA_W).astype(BF16)


def _ffn_call(x, m3, tiles_per_seq, nw, w1, w3, w2, *, mod_off, tm, ctx=None, ctx_row=None, proj=None,
              rope=None, final_w=None, cast=(), name):
    t, d = x.shape
    f = w1.shape[1]
    n_main = t // tm
    n_tiles = n_main + (0 if ctx is None else ctx.shape[0] // tm)
    t = n_tiles * tm
    main = lambda i: jnp.minimum(i, n_main - 1)
    tile = lambda w: pl.BlockSpec((tm, w), lambda i: (i, 0))
    in_specs = [pl.BlockSpec((tm, d), lambda i: (main(i), 0))]
    args = [x]
    if ctx is None:
        row_of_tile = lambda i: i // tiles_per_seq
    else:
        row_of_tile = lambda i: jnp.where(i >= n_main, ctx_row, i // tiles_per_seq)
        in_specs.append(pl.BlockSpec((tm, d), lambda i: (jnp.maximum(i - n_main, 0), 0)))
        args.append(ctx)
    in_specs += [pl.BlockSpec((None, 1, N_MOD * d), lambda i: (row_of_tile(i), 0, 0)),
                 _resident((1, d)), _resident((d, f)), _resident((d, f)), _resident((f, d))]
    args += [m3, nw.reshape(1, d), w1, w3, w2]
    if proj is not None:
        n2w, win, wlow, gw, gbias = proj
        in_specs += [_resident((1, d))] + [_resident(a.shape) for a in (win, wlow, gw, gbias)]
        args += [n2w.reshape(1, d), win, wlow, gw, gbias]
        if rope is not None:
            rowtab, coltab = rope
            in_specs += [pl.BlockSpec((tm // GRID_W, 2 * LANE), lambda i: (main(i) % tiles_per_seq, 0)),
                         _resident(coltab.shape)]
            args += [rowtab, coltab]
    if final_w is not None:
        in_specs.append(_resident((1, d)))
        args.append(final_w.reshape(1, d))
    out_specs = [tile(d)]
    out_shape = [jax.ShapeDtypeStruct((t, d), F32)]
    if proj is not None:
        for w, dt in ((RET_W, BF16),) * 4 + ((GLA_QK, BF16),) * 2 + ((GLA_W, BF16),) * 2 + ((GLA_QK, F32),) * 2:
            out_specs.append(tile(w))
            out_shape.append(jax.ShapeDtypeStruct((t, w), dt))
    cast_specs, cast_shapes = _cast_jobs(cast, n_main)
    in_specs += cast_specs
    args += list(cast)
    out_specs += cast_specs
    out_shape += cast_shapes
    kern = functools.partial(_ffn_kernel, mod_off=mod_off, proj=proj is not None,
                             rope=rope is not None, final=final_w is not None,
                             n_main=None if ctx is None else n_main, n_cast=len(cast), d=d, f=f)
    return pl.pallas_call(
        kern, grid=(n_tiles,), in_specs=in_specs, out_specs=out_specs, out_shape=out_shape,
        scratch_shapes=[pltpu.VMEM((tm, f), BF16)],
        compiler_params=pltpu.CompilerParams(dimension_semantics=("arbitrary",),
                                             vmem_limit_bytes=VMEM_LIMIT),
        name=name,
    )(*args)


def _head_lanes(shape, hd):
    lane = _iota(shape, len(shape) - 1)
    return (lane >= GLA_DK) if hd % 2 else (lane < GLA_DK)


STATE_RING = 3


def _state_kernel(*refs, batch, cps, steps, bcps, bsteps, ctx_chunk, nc):
    it = iter(refs)
    token_hbm = [next(it) for _ in range(6)]
    dec_ref = next(it)
    orf_ref, orb_ref, ogf_ref, ogb_ref = (next(it) for _ in range(4))
    srf, srb, sgf, sgb, ub_ret, ub_gla, ub_dec = (next(it) for _ in range(7))
    token_ring = [next(it) for _ in range(6)]
    sem = next(it)
    n = CHUNK
    step = pl.program_id(0)
    lg = _log_sigmoid(dec_ref[...])

    def token_copies(s, slot, ctx=False):
        out = []
        for a, (hbm, ring) in enumerate(zip(token_hbm, token_ring)):
            for b in range(batch):
                if ctx:
                    src, dst = hbm.at[pl.ds((ctx_chunk + b) * n, n), :], ring.at[slot, b, pl.ds(0, n), :]
                else:
                    row0 = pl.multiple_of((b * nc + (s - 1) * cps) * n, n)
                    src, dst = hbm.at[pl.ds(row0, cps * n), :], ring.at[slot, b]
                out.append(pltpu.make_async_copy(src, dst, sem.at[slot, a * batch + b]))
        return out

    @pl.when(step == 0)
    def _():
        for carry in (srf, srb, sgf, sgb):
            carry[...] = jnp.zeros(carry.shape, F32)
        for cp in token_copies(0, 0, ctx=True):
            cp.start()
        for s in range(1, min(STATE_RING - 1, steps + 1)):
            for cp in token_copies(s, s):
                cp.start()

    def reading_step(slot, ctx=False):
        rk_ref, rv_ref, gk_ref, gv_ref, gf_ref, gb_ref = ([ring.at[slot, b] for b in range(batch)]
                                                          for ring in token_ring)

        @pl.when(step + (STATE_RING - 1) <= steps)
        def _():
            for cp in token_copies(step + (STATE_RING - 1), (slot + STATE_RING - 1) % STATE_RING):
                cp.start()

        for cp in token_copies(step, slot, ctx):
            cp.wait()
        row = _iota((n, LANE), 0).astype(F32)
        ri = _iota((n, n), 0)
        ci = _iota((n, n), 1)
        after = (ci > ri).astype(BF16)
        before = (ci < ri).astype(BF16)
        heads = [slice(hd * LANE, (hd + 1) * LANE) for hd in range(RET_HEADS)]
        to_end = [jnp.exp((n - 1.0 - row) * lg[0:1, sl]).astype(BF16) for sl in heads]
        to_start = [jnp.exp(row * lg[1:2, sl]).astype(BF16) for sl in heads]

        def gla_keys(b, rows, g_ref, tri, edge):
            g = g_ref[b][rows, :]
            e = _dot_select(tri, g)
            tot = e[edge:edge + 1, :] + g[edge:edge + 1, :]
            return gk_ref[b][rows, :] * jnp.exp(e).astype(BF16), jnp.exp(tot)

        def gla(b, rows, keys_fwd, keys_bwd):
            fwd, bwd = [], []
            for hd in range(GLA_HEADS):
                pr = slice((hd // 2) * LANE, (hd // 2 + 1) * LANE)
                kd = jnp.concatenate([keys_fwd[0][:, pr], keys_bwd[0][:, pr]], axis=1)
                u = _dot_tn(gv_ref[b][rows, hd * LANE:(hd + 1) * LANE], kd)
                own = _head_lanes((LANE, LANE), hd)
                fwd.append(jnp.where(own, u[:, :LANE], 0.0))
                bwd.append(jnp.where(own, u[:, LANE:], 0.0))
            return fwd, bwd

        for j in range(1 if ctx else cps):
            chunk = (step - 1) * cps + j
            rows = slice(j * n, (j + 1) * n)
            if not ctx:
                orf_ref[:, j] = srf[...].astype(BF16)
                ogf_ref[:, j] = sgf[...].astype(BF16)
            keys_f = [gla_keys(b, rows, gf_ref, after, 0) for b in range(batch)]
            keys_b = [gla_keys(b, rows, gb_ref, before, n - 1) for b in range(batch)]
            for b in range(batch):
                for hd in range(RET_HEADS):
                    sl = slice(hd * LANE, (hd + 1) * LANE)
                    k = rk_ref[b][rows, sl]
                    v = rv_ref[b][rows, sl]
                    both = _dot_tn(v, jnp.concatenate([k * to_end[hd], k * to_start[hd]], axis=1))
                    srf[b, hd] = srf[b, hd] * jnp.exp(n * lg[0:1, sl]) + both[:, :LANE]
                    back = both[:, LANE:]
                    if ctx:
                        srb[b, hd] = srb[b, hd] * jnp.exp(n * lg[1:2, sl]) + back
                    else:
                        ub_ret[chunk, b, hd] = back.astype(BF16)
                fwd, bwd = gla(b, rows, keys_f[b], keys_b[b])
                dec = keys_f[b][1]
                for hd in range(GLA_HEADS):
                    pr = slice((hd // 2) * LANE, (hd // 2 + 1) * LANE)
                    sgf[b, hd] = sgf[b, hd] * dec[:, pr] + fwd[hd]
                dec = keys_b[b][1]
                for hd in range(GLA_HEADS):
                    pr = slice((hd // 2) * LANE, (hd // 2 + 1) * LANE)
                    if ctx:
                        sgb[b, hd] = sgb[b, hd] * dec[:, pr] + bwd[hd]
                    else:
                        ub_gla[chunk, b, hd] = bwd[hd].astype(BF16)
                if not ctx:
                    ub_dec[chunk, b] = jnp.broadcast_to(dec, ub_dec.shape[2:])

    pl.when(step == 0)(functools.partial(reading_step, 0, ctx=True))
    for slot in range(STATE_RING):
        pl.when((step >= 1) & (step <= steps) & (step % STATE_RING == slot))(functools.partial(reading_step, slot))

    @pl.when(step > steps)
    def _():
        for j in range(bcps):
            jb = bcps - 1 - j
            chunk = (steps + bsteps - step) * bcps + jb
            orb_ref[:, jb] = srb[...].astype(BF16)
            ogb_ref[:, jb] = sgb[...].astype(BF16)
            for b in range(batch):
                for hd in range(RET_HEADS):
                    lgb = lg[1:2, hd * LANE:(hd + 1) * LANE]
                    srb[b, hd] = srb[b, hd] * jnp.exp(n * lgb) + ub_ret[chunk, b, hd].astype(F32)
                for hd in range(GLA_HEADS):
                    pr = slice((hd // 2) * LANE, (hd // 2 + 1) * LANE)
                    sgb[b, hd] = sgb[b, hd] * ub_dec[chunk, b, 0:1, pr] + ub_gla[chunk, b, hd].astype(F32)


def _states(rk, rv, gk, gv, gf, gb, dec, batch, nc, ctx_chunk):
    cps = 2 if nc % 2 == 0 else 1
    bcps = next(c for c in (8, 4, 2, 1) if nc % c == 0)
    steps, bsteps = nc // cps, nc // bcps
    reading = lambda c: jnp.clip(c - 1, 0, steps - 1)
    scanning = lambda c: bsteps - 1 - jnp.maximum(c - steps - 1, 0)

    tokens = (rk, rv, gk, gv, gf, gb)
    st_shape = (batch, RET_HEADS, LANE, LANE)
    of_spec = pl.BlockSpec((batch, cps, RET_HEADS, LANE, LANE), lambda c: (0, reading(c), 0, 0, 0))
    ob_spec = pl.BlockSpec((batch, bcps, RET_HEADS, LANE, LANE), lambda c: (0, scanning(c), 0, 0, 0))
    per_chunk = jax.ShapeDtypeStruct((batch, nc, RET_HEADS, LANE, LANE), BF16)
    kept = (nc, batch, RET_HEADS, LANE, LANE)
    return pl.pallas_call(
        functools.partial(_state_kernel, batch=batch, cps=cps, steps=steps, bcps=bcps, bsteps=bsteps,
                          ctx_chunk=ctx_chunk, nc=nc),
        grid=(1 + steps + bsteps,),
        in_specs=[pl.BlockSpec(memory_space=pl.ANY)] * len(tokens)
        + [pl.BlockSpec(dec.shape, lambda c: (0, 0))],
        out_specs=[of_spec, ob_spec, of_spec, ob_spec],
        out_shape=[per_chunk] * 4,
        scratch_shapes=[pltpu.VMEM(st_shape, F32)] * 4
        + [pltpu.VMEM(kept, BF16), pltpu.VMEM(kept, BF16), pltpu.VMEM((nc, batch, SUBLANES, GLA_QK), F32)]
        + [pltpu.VMEM((STATE_RING, batch, cps * CHUNK, a.shape[1]), a.dtype) for a in tokens]
        + [pltpu.SemaphoreType.DMA((STATE_RING, len(tokens) * batch))],
        compiler_params=pltpu.CompilerParams(dimension_semantics=("arbitrary",),
                                             vmem_limit_bytes=VMEM_LIMIT),
        name="states",
    )(*tokens, dec)


def _block_row(x, parent, r):
    n, w = x.shape
    if parent == n:
        return jnp.broadcast_to(x[r:r + 1, :], (n, w))
    x3 = x.reshape(n // parent, parent, w)
    return jnp.broadcast_to(x3[:, r:r + 1, :], x3.shape).reshape(n, w)


def _gla_levels(q, k_even, k_odd, gf, gb, bf, bb):
    n = q.shape[0]
    row = _iota(q.shape, 0)

    def factors(u_exp, w_exp):
        ew = jnp.exp2(w_exp).astype(BF16)
        return (q * jnp.exp2(u_exp).astype(BF16), k_even * ew, k_odd * ew)

    levels = []
    s = n // 2
    while s >= 2:
        if s >= 4:
            last_of_first = _block_row(bf, 2 * s, s - 1)
            first_of_second = _block_row(bb, 2 * s, s)
        else:
            upper = (row & 4) != 0
            last_of_first = jnp.where(upper, _block_row(bf, SUBLANES, 5), _block_row(bf, SUBLANES, 1))
            first_of_second = jnp.where(upper, _block_row(bb, SUBLANES, 6), _block_row(bb, SUBLANES, 2))
        df = bf - last_of_first
        db = bb - first_of_second
        levels.append((2 * s, [factors(jnp.minimum(df, db), -jnp.maximum(df, db))]))
        s //= 2
    odd = (row & 1) == 1
    qa = q * jnp.where(odd, jnp.exp2(gf), 2.0).astype(BF16)
    qb = q * jnp.where(odd, 2.0, jnp.exp2(gb)).astype(BF16)
    zero = jnp.zeros_like(k_even)
    ev = lambda x: jnp.where(odd, zero, x)
    od = lambda x: jnp.where(odd, x, zero)
    levels.append((2, [(qa, ev(k_even), ev(k_odd)), (qb, od(k_even), od(k_odd))]))
    return levels


def _mix_kernel(rq_ref, rk_ref, rv_ref, rg_ref, gq_ref, gk_ref, gv_ref, gg_ref, gf_ref, gb_ref,
                srf_ref, srb_ref, sgf_ref, sgb_ref, dec_ref, rnw_ref, gnw_ref,
                x_ref, m_ref, wout_ref, o_ref, mix_ref, decay_ref, qdec_ref, *, d):
    n = CHUNK
    half = n // 2
    ri = _iota((n, n), 0)
    ci = _iota((n, n), 1)
    step = pl.program_id(0)
    last_step = pl.num_programs(0) - 1
    cur = step % 2
    code = _iota((half, n), 0) ^ (_iota((half, n), 1) & (half - 1))

    @pl.when(step == 0)
    def _():
        mix_ref[1] = jnp.zeros(mix_ref.shape[1:], BF16)
        lg = _log_sigmoid(dec_ref[...])
        dist = (ri - ci).astype(F32)
        row = _iota((n, LANE), 0).astype(F32)
        for hd in range(RET_HEADS):
            sl = slice(hd * LANE, (hd + 1) * LANE)
            lgf = lg[0:1, sl]
            lgb = lg[1:2, sl]
            decay_ref[hd] = jnp.exp(jnp.where(dist > 0, dist * lgf[:, 0:1],
                                              jnp.where(dist < 0, -dist * lgb[:, 0:1], jnp.log(2.0))))
            qdec_ref[hd] = jnp.exp((row + 1.0) * lgf)
            qdec_ref[RET_HEADS + hd] = jnp.exp((n - row) * lgb)

    def retention(c):
        tok = slice(c * n, (c + 1) * n)
        for hd in range(RET_HEADS):
            sl = slice(hd * LANE, (hd + 1) * LANE)
            q = rq_ref[tok, sl]
            p = (_dot_nt(q, rk_ref[tok, sl]) * decay_ref[hd]).astype(BF16)
            qf = q.astype(F32)
            qs = jnp.concatenate([(qf * qdec_ref[hd]).astype(BF16),
                                  (qf * qdec_ref[RET_HEADS + hd]).astype(BF16)], axis=1)
            st = jnp.concatenate([srf_ref[c, hd], srb_ref[c, hd]], axis=1)
            o = _dot(p, rv_ref[tok, sl]) + _dot_nt(qs, st)
            mu = jnp.mean(o, axis=-1, keepdims=True)
            oc = o - mu
            var = jnp.mean(oc * oc, axis=-1, keepdims=True)
            r = oc * lax.rsqrt(var + EPS) * rnw_ref[:, sl] * rg_ref[tok, sl].astype(F32)
            mix_ref[cur, tok, sl] = r.astype(BF16)

    def gla_factors(c):
        tok = slice(c * n, (c + 1) * n)
        gf = gf_ref[tok, :] * LOG2E
        gb = gb_ref[tok, :] * LOG2E
        bf = _dot_select((ci <= ri).astype(BF16), gf)
        bb = _dot_select((ci >= ri).astype(BF16), gb)
        q = gq_ref[tok, :]
        k = gk_ref[tok, :]
        even_head = (_iota(k.shape, 1) & GLA_DK) == 0
        no_k = jnp.zeros_like(k)
        levels = _gla_levels(q, jnp.where(even_head, k, no_k), jnp.where(even_head, no_k, k), gf, gb, bf, bb)
        return levels, q * jnp.exp2(bf).astype(BF16), q * jnp.exp2(bb).astype(BF16)

    def finish_previous():
        o_ref[...] = x_ref[...] + m_ref[:, 5 * d:6 * d] * _dot(mix_ref[1 - cur], wout_ref[...])

    def gla_outputs(c, levels, qsf, qsb):
        tok = slice(c * n, (c + 1) * n)
        for pair in range(GLA_HEADS // 2):
            pr = slice(pair * LANE, (pair + 1) * LANE)
            near = [None, None]
            far = [None, None]
            for size, blocks in levels:
                for rb in range(2):
                    kb = rb if size < n else 1 - rb
                    rows = slice(rb * half, (rb + 1) * half)
                    keys = slice(kb * half, (kb + 1) * half)
                    u = jnp.concatenate([blk[0][rows, pr] for blk in blocks], axis=1)
                    w = jnp.concatenate([jnp.concatenate([blk[j][keys, pr] for blk in blocks], axis=1)
                                         for j in (1, 2)], axis=0)
                    t = _dot_nt(u, w)
                    if size == n:
                        far[rb] = t
                    else:
                        near[rb] = t if near[rb] is None else jnp.where(code < size, t, near[rb])
            qcat = jnp.concatenate([qsf[:, pr], qsb[:, pr]], axis=1)
            for j in range(2):
                hd = 2 * pair + j
                sl = slice(hd * LANE, (hd + 1) * LANE)
                mine = slice(j * half, (j + 1) * half)
                p = jnp.concatenate([jnp.concatenate([near[0][:, mine], far[0][:, mine]], axis=1),
                                     jnp.concatenate([far[1][:, mine], near[1][:, mine]], axis=1)], axis=0)
                st = jnp.concatenate([sgf_ref[c, hd], sgb_ref[c, hd]], axis=1)
                o = _dot(p.astype(BF16), gv_ref[tok, sl]) + _dot_nt(qcat, st)
                ms = jnp.mean(o * o, axis=-1, keepdims=True)
                r = o * lax.rsqrt(ms + EPS) * gnw_ref[:, sl] * gg_ref[tok, sl].astype(F32)
                mix_ref[cur, tok, RET_W + hd * LANE:RET_W + (hd + 1) * LANE] = r.astype(BF16)

    @pl.when(step < last_step)
    def _():
        factors = []
        for c in range(MIX_CHUNKS):
            factors.append(gla_factors(c))
            if c == 0:
                finish_previous()
            retention(c)
        for c in range(MIX_CHUNKS):
            gla_outputs(c, *factors[c])

    @pl.when(step == last_step)
    def _():
        finish_previous()


def _mix_call(mix_in, states, dec, rnw, gnw, x1, m3, wout, batch, nc):
    assert nc % MIX_CHUNKS == 0
    per_seq = nc // MIX_CHUNKS
    n_steps = batch * per_seq
    rows = MIX_CHUNKS * CHUNK
    t, d = n_steps * rows, x1.shape[1]
    mixed = lambda i: jnp.minimum(i, n_steps - 1)
    done = lambda i: jnp.maximum(i - 1, 0)
    tile = lambda w: pl.BlockSpec((rows, w), lambda i: (mixed(i), 0))
    st_spec = pl.BlockSpec((None, MIX_CHUNKS, RET_HEADS, LANE, LANE),
                           lambda i: (mixed(i) // per_seq, mixed(i) % per_seq, 0, 0, 0))
    const = lambda shape: pl.BlockSpec(shape, lambda i: (0,) * len(shape))
    widths = (RET_W,) * 4 + (GLA_QK,) * 2 + (GLA_W,) * 2 + (GLA_QK,) * 2
    return pl.pallas_call(
        functools.partial(_mix_kernel, d=d),
        grid=(n_steps + 1,),
        in_specs=[tile(w) for w in widths] + [st_spec] * 4
        + [const(dec.shape), const((1, RET_W)), const((1, GLA_W)),
           pl.BlockSpec((rows, d), lambda i: (done(i), 0)),
           pl.BlockSpec((None, 1, N_MOD * d), lambda i: (done(i) // per_seq, 0, 0)),
           const(wout.shape)],
        out_specs=pl.BlockSpec((rows, d), lambda i: (done(i), 0)),
        out_shape=jax.ShapeDtypeStruct((t, d), F32),
        scratch_shapes=[pltpu.VMEM((2, rows, RET_W + GLA_W), BF16),
                        pltpu.VMEM((RET_HEADS, CHUNK, CHUNK), F32),
                        pltpu.VMEM((2 * RET_HEADS, CHUNK, LANE), F32)],
        compiler_params=pltpu.CompilerParams(dimension_semantics=("arbitrary",),
                                             vmem_limit_bytes=VMEM_LIMIT),
        name="mix",
    )(*mix_in, *states, dec, rnw.reshape(1, RET_W), gnw.reshape(1, GLA_W), x1, m3, wout)


def _rope_tables(n_tok):
    freqs = ROPE_BASE ** (-jnp.arange(RET_DK // 4, dtype=F32) / (RET_DK // 4))

    def table(n_pos, first_half):
        ang = jnp.arange(n_pos, dtype=F32)[:, None] * freqs
        zero = jnp.zeros((n_pos, LANE // 2), F32)
        cos = jnp.concatenate([jnp.cos(ang)] * 2, axis=-1)
        sin = jnp.concatenate([-jnp.sin(ang), jnp.sin(ang)], axis=-1)
        halves = (cos, zero, sin, zero) if first_half else (zero, cos, zero, sin)
        return jnp.concatenate(halves, axis=-1)

    return table(n_tok // GRID_W, True), table(GRID_W, False)


def _pack_gate(w_f, b_f, w_b, b_b):
    gw = jnp.zeros((LANE, 2 * GLA_QK), F32)
    gw = gw.at[:GLA_RANK, :GLA_QK].set(w_f).at[GLA_RANK:2 * GLA_RANK, GLA_QK:].set(w_b)
    return gw.astype(BF16), jnp.concatenate([b_f, b_b]).reshape(1, 2 * GLA_QK)


def kernel(x, c, ctx, c_ctx, ada_w, ada_b, norm1_w, ffn1_w1, ffn1_w3, ffn1_w2, norm2_w, w_in,
           ret_decay_f, ret_decay_b, ret_norm_w, gla_gate_w_f, gla_gate_b_f, gla_gate_w_b, gla_gate_b_b,
           gla_norm_w, w_out, norm3_w, ffn2_w1, ffn2_w3, ffn2_w2, final_norm_w):
    batch, n_tok, d = x.shape
    n_ctx = ctx.shape[1]
    depth = ada_w.shape[0]
    assert depth == 1 and batch + 1 <= SUBLANES
    assert n_tok % FFN_TILE == 0 and n_tok % CHUNK == 0 and n_ctx == CHUNK
    assert (batch * n_ctx) % FFN_TILE == 0

    cvec = jnp.zeros((SUBLANES, d), F32).at[:batch].set(c).at[batch].set(c_ctx)
    m, w1a, w3a, w2a, w_in_b, w_low_b = _modulation(cvec, ada_w[0], ada_b[0],
                                                    (ffn1_w1[0], ffn1_w3[0], ffn1_w2[0]), w_in[0].T)
    m3 = m.reshape(SUBLANES, 1, N_MOD * d)

    gw, gbias = _pack_gate(gla_gate_w_f[0], gla_gate_b_f[0], gla_gate_w_b[0], gla_gate_b_b[0])
    proj = (norm2_w[0], w_in_b, w_low_b, gw, gbias)
    f1 = (norm1_w[0], w1a, w3a, w2a)
    rowtab, coltab = _rope_tables(n_tok)
    dec = jnp.zeros((SUBLANES, RET_W), F32)
    dec = dec.at[0].set(jnp.repeat(ret_decay_f[0], LANE)).at[1].set(jnp.repeat(ret_decay_b[0], LANE))

    tiles_per_seq = n_tok // FFN_TILE
    first = _ffn_call(x.reshape(batch * n_tok, d), m3, tiles_per_seq, *f1, mod_off=0, tm=FFN_TILE,
                      ctx=ctx.reshape(batch * n_ctx, d), ctx_row=batch, proj=proj, rope=(rowtab, coltab),
                      cast=(ffn2_w1[0], ffn2_w3[0], ffn2_w2[0], w_out[0]), name="ffn_in")
    x1, mix_in, (w1b, w3b, w2b, woutb) = first[0], first[1:11], first[11:]
    scan_in = (mix_in[1], mix_in[2], mix_in[5], mix_in[6], mix_in[8], mix_in[9])
    nc = n_tok // CHUNK
    states = _states(*scan_in, dec, batch, nc, batch * nc)
    x2 = _mix_call(mix_in, states, dec, ret_norm_w[0], gla_norm_w[0], x1, m3, woutb, batch, nc)
    out = _ffn_call(x2, m3, tiles_per_seq, norm3_w[0], w1b, w3b, w2b, mod_off=6, tm=FFN_TILE,
                    final_w=final_norm_w, name="ffn_out")[0]
    return out.reshape(batch, n_tok, d)
```
